```python
import jax
import jax.numpy as jnp
from jax import lax
import numpy as np

D_MODEL = 1024
BATCH = 16
SEQ = 4096
DEPTH = 4

CTX_LEN = 256
GRID_W = 64

DN_HEADS = 4
DN_DK = 128
DN_DV = 128
SSM_HEADS = 8
SSM_P = 64
SSM_N = 128
SSM_G = 2
CHUNK = 64
CONV_K = 3
N_EXPERTS = 16
EC_CAPACITY_FACTOR = 2
D_EXPERT = 512
EPS = 1e-6

DN_QK = DN_HEADS * DN_DK
DN_VW = DN_HEADS * DN_DV
SSM_DI = SSM_HEADS * SSM_P
SSM_BC = SSM_G * SSM_N
D_MIX = DN_VW + SSM_DI
CONV_SPLITS = (DN_QK, DN_QK, DN_VW, SSM_DI, SSM_BC, SSM_BC)
CONV_CH = sum(CONV_SPLITS)
REST_SPLITS = (DN_VW, SSM_DI, 2 * DN_HEADS, 2 * DN_HEADS, 2 * SSM_HEADS)
D_IN_PROJ = CONV_CH + sum(REST_SPLITS)

kernel_name = "hybrid_dit_deltanet_ssd_ecmoe"


def _split(a, sizes):
    return jnp.split(a, np.cumsum(sizes)[:-1].tolist(), axis=-1)


def _rev(a):
    return a[:, ::-1]


def rms_norm(x, g):
    xf = x.astype(jnp.float32)
    y = xf * lax.rsqrt(jnp.mean(xf * xf, axis=-1, keepdims=True) + EPS)
    return (y * g.astype(jnp.float32)).astype(x.dtype)


def l2_normalize(x):
    xf = x.astype(jnp.float32)
    return xf * lax.rsqrt(jnp.sum(xf * xf, axis=-1, keepdims=True) + EPS)


def modulate(h, shift, scale):
    return h * (1.0 + scale) + shift


def conv_grid(u, w, b):
    bsz, t, ch = u.shape
    rows = t // GRID_W
    img = u.reshape(bsz, rows, GRID_W, ch)
    kern = jnp.transpose(w, (1, 2, 0))[:, :, None, :].astype(u.dtype)
    out = lax.conv_general_dilated(img, kern, window_strides=(1, 1), padding="SAME",
                                   dimension_numbers=("NHWC", "HWIO", "NHWC"),
                                   feature_group_count=ch)
    return out.reshape(bsz, t, ch) + b.astype(u.dtype)


def conv_seq(u, w, b):
    taps = w[:, CONV_K // 2, :].astype(u.dtype)
    t = u.shape[1]
    pad = CONV_K // 2
    up = jnp.pad(u, ((0, 0), (pad, pad), (0, 0)))
    out = sum(up[:, i:i + t] * taps[:, i] for i in range(CONV_K))
    return out + b.astype(u.dtype)


def stream_features(h, w_in, conv_w, conv_b, on_grid):
    bsz, t = h.shape[:2]
    proj = jnp.einsum("btd,de->bte", h, w_in)
    conv_in, rest = proj[..., :CONV_CH], proj[..., CONV_CH:]
    conv = conv_grid if on_grid else conv_seq
    u = jax.nn.silu(conv(conv_in, conv_w, conv_b)).astype(jnp.float32)
    q, k, v, xs, bs, cs = _split(u, CONV_SPLITS)
    dn_z, ssm_z, dn_b, dn_a, ssm_dt = _split(rest.astype(jnp.float32), REST_SPLITS)
    return dict(
        q=l2_normalize(q.reshape(bsz, t, DN_HEADS, DN_DK)) * (DN_DK ** -0.5),
        k=l2_normalize(k.reshape(bsz, t, DN_HEADS, DN_DK)),
        v=v.reshape(bsz, t, DN_HEADS, DN_DV),
        x=xs.reshape(bsz, t, SSM_HEADS, SSM_P),
        B=bs.reshape(bsz, t, SSM_G, SSM_N),
        C=cs.reshape(bsz, t, SSM_G, SSM_N),
        dn_z=dn_z,
        ssm_z=ssm_z,
        dn_b=dn_b.reshape(bsz, t, 2, DN_HEADS),
        dn_a=dn_a.reshape(bsz, t, 2, DN_HEADS),
        ssm_dt=ssm_dt.reshape(bsz, t, 2, SSM_HEADS),
    )


def gated_delta_chunked(q, k, v, beta, logg, S0, with_output):
    bsz, t, nh, dk = k.shape
    dv = v.shape[-1]
    nc = t // CHUNK

    def to_chunks(a):
        a = a.reshape(bsz, nc, CHUNK, nh, *a.shape[3:])
        return jnp.moveaxis(a, (1, 3), (0, 2))

    qc, kc, vc, bc, gc = map(to_chunks, (q, k, v, beta, logg))
    G = jnp.cumsum(gc, axis=-1)
    tri = jnp.tril(jnp.ones((CHUNK, CHUNK), dtype=bool))
    strict = jnp.tril(jnp.ones((CHUNK, CHUNK), dtype=bool), -1)
    decay = jnp.where(tri, jnp.exp(jnp.where(tri, G[..., :, None] - G[..., None, :], 0.0)), 0.0)
    kb = kc * bc[..., None]
    lmat = jnp.where(strict, jnp.einsum("cbhid,cbhjd->cbhij", kb, kc) * decay, 0.0)
    rhs = jnp.concatenate([vc * bc[..., None], kb * jnp.exp(G)[..., None]], axis=-1)
    uw = lax.linalg.triangular_solve(lmat + jnp.eye(CHUNK, dtype=lmat.dtype), rhs,
                                     left_side=True, lower=True, unit_diagonal=True)
    u, w = uw[..., :dv], uw[..., dv:]
    g_last = G[..., -1]
    k_tail = kc * jnp.exp(g_last[..., None] - G)[..., None]

    def step(S, inp):
        u_i, w_i, kt_i, gl_i = inp[:4]
        v_new = u_i - jnp.einsum("bhld,bhdv->bhlv", w_i, S)
        if with_output:
            qg_i, qk_i = inp[4:]
            o = jnp.einsum("bhld,bhdv->bhlv", qg_i, S) + jnp.einsum("bhij,bhjv->bhiv", qk_i, v_new)
        else:
            o = None
        S = S * jnp.exp(gl_i)[..., None, None] + jnp.einsum("bhld,bhlv->bhdv", kt_i, v_new)
        return S, o

    xs = (u, w, k_tail, g_last)
    if with_output:
        qk = jnp.where(tri, jnp.einsum("cbhid,cbhjd->cbhij", qc, kc) * decay, 0.0)
        qg = qc * jnp.exp(G)[..., None]
        xs = xs + (qg, qk)
    S_T, o = lax.scan(step, S0, xs)
    if not with_output:
        return None, S_T
    o = jnp.moveaxis(o, (0, 2), (1, 3)).reshape(bsz, t, nh, dv)
    return o, S_T


def ssd_chunked(x, dt, A, Bm, Cm, h0, with_output):
    bsz, t, nh, hp = x.shape
    ng, ns = Bm.shape[2], Bm.shape[3]
    hpg = nh // ng
    nc = t // CHUNK
    xc = jnp.transpose(x.reshape(bsz, nc, CHUNK, ng, hpg, hp), (1, 0, 3, 4, 2, 5))
    dtc = jnp.transpose(dt.reshape(bsz, nc, CHUNK, ng, hpg), (1, 0, 3, 4, 2))
    Bc = jnp.transpose(Bm.reshape(bsz, nc, CHUNK, ng, ns), (1, 0, 3, 2, 4))
    Cc = jnp.transpose(Cm.reshape(bsz, nc, CHUNK, ng, ns), (1, 0, 3, 2, 4))
    Acs = jnp.cumsum(dtc * A.reshape(ng, hpg, 1), axis=-1)
    xdt = xc * dtc[..., None]

    def step(h, inp):
        B_i, C_i, xdt_i, a_i = inp
        a_last = a_i[..., -1:]
        if with_output:
            y_off = jnp.einsum("bgin,bghnp->bghip", C_i, h) * jnp.exp(a_i)[..., None]
        else:
            y_off = None
        st = jnp.einsum("bgjn,bghjp->bghnp", B_i, xdt_i * jnp.exp(a_last - a_i)[..., None])
        h = h * jnp.exp(a_last)[..., None] + st
        return h, y_off

    hT, y_off = lax.scan(step, h0.reshape(bsz, ng, hpg, ns, hp), (Bc, Cc, xdt, Acs))
    hT = hT.reshape(bsz, nh, ns, hp)
    if not with_output:
        return None, hT
    tri = jnp.tril(jnp.ones((CHUNK, CHUNK), dtype=bool))
    decay = jnp.where(tri, jnp.exp(jnp.where(tri, Acs[..., :, None] - Acs[..., None, :], 0.0)), 0.0)
    scores = jnp.einsum("cbgin,cbgjn->cbgij", Cc, Bc)[:, :, :, None] * decay
    y_diag = jnp.einsum("cbghij,cbghjp->cbghip", scores, xdt)
    y = jnp.transpose(y_diag + y_off, (1, 0, 4, 2, 3, 5)).reshape(bsz, t, nh, hp)
    return y, hT


def dn_inputs(f, d, A_log, dt_bias):
    beta = jax.nn.sigmoid(f["dn_b"][:, :, d])
    logg = -jnp.exp(A_log[d]) * jax.nn.softplus(f["dn_a"][:, :, d] + dt_bias[d])
    args = (f["q"], f["k"], f["v"], beta, logg)
    return tuple(_rev(a) for a in args) if d == 1 else args


def ssd_inputs(f, d, A_log, dt_bias):
    dt = jax.nn.softplus(f["ssm_dt"][:, :, d] + dt_bias[d])
    seq_args = (f["x"], dt, f["B"], f["C"])
    if d == 1:
        seq_args = tuple(_rev(a) for a in seq_args)
    x, dt, Bm, Cm = seq_args
    return x, dt, -jnp.exp(A_log[d]).astype(jnp.float32), Bm, Cm


def merge_heads(f, o_dirs, y_dirs, dn_norm_g, ssm_D, ssm_norm_g, w_out, out_dtype):
    o = o_dirs[0] + o_dirs[1]
    bsz, t = o.shape[:2]
    dn = rms_norm(o, dn_norm_g) * jax.nn.silu(f["dn_z"].reshape(o.shape))
    y = y_dirs[0] + y_dirs[1] + ssm_D[:, None] * f["x"]
    yz = y.reshape(bsz, t, SSM_DI) * jax.nn.silu(f["ssm_z"])
    ssm = rms_norm(yz.reshape(bsz, t, SSM_G, SSM_DI // SSM_G),
                   ssm_norm_g.reshape(SSM_G, SSM_DI // SSM_G)).reshape(bsz, t, SSM_DI)
    mixed = jnp.concatenate([dn.reshape(bsz, t, DN_VW), ssm], axis=-1).astype(out_dtype)
    return jnp.einsum("bte,ed->btd", mixed, w_out)


def hybrid_mixer(h_ctx, h_lat, w_in, conv_w, conv_b, dn_A_log, dn_dt_bias, dn_norm_g,
                 ssm_A_log, ssm_dt_bias, ssm_D, ssm_norm_g, w_out, ctx_out):
    fc = stream_features(h_ctx, w_in, conv_w, conv_b, on_grid=False)
    fl = stream_features(h_lat, w_in, conv_w, conv_b, on_grid=True)
    bsz = h_lat.shape[0]
    dn_c, dn_l, ssm_c, ssm_l = [], [], [], []
    for d in range(2):
        unrev = _rev if d == 1 else (lambda a: a)
        S0 = jnp.zeros((bsz, DN_HEADS, DN_DK, DN_DV), jnp.float32)
        o_c, S_c = gated_delta_chunked(*dn_inputs(fc, d, dn_A_log, dn_dt_bias), S0, ctx_out)
        o_l, _ = gated_delta_chunked(*dn_inputs(fl, d, dn_A_log, dn_dt_bias), S_c, True)
        h0 = jnp.zeros((bsz, SSM_HEADS, SSM_N, SSM_P), jnp.float32)
        y_c, H_c = ssd_chunked(*ssd_inputs(fc, d, ssm_A_log, ssm_dt_bias), h0, ctx_out)
        y_l, _ = ssd_chunked(*ssd_inputs(fl, d, ssm_A_log, ssm_dt_bias), H_c, True)
        dn_l.append(unrev(o_l))
        ssm_l.append(unrev(y_l))
        if ctx_out:
            dn_c.append(unrev(o_c))
            ssm_c.append(unrev(y_c))
    m_lat = merge_heads(fl, dn_l, ssm_l, dn_norm_g, ssm_D, ssm_norm_g, w_out, h_lat.dtype)
    m_ctx = merge_heads(fc, dn_c, ssm_c, dn_norm_g, ssm_D, ssm_norm_g, w_out, h_ctx.dtype) if ctx_out else None
    return m_ctx, m_lat


def ec_moe(h, router_w, w_gate, w_up, w_down):
    bsz, t, _ = h.shape
    cap = EC_CAPACITY_FACTOR * t // N_EXPERTS
    aff = jax.nn.softmax(jnp.einsum("btd,de->bte", h, router_w).astype(jnp.float32), axis=-1)
    gate, idx = lax.top_k(jnp.swapaxes(aff, 1, 2), cap)
    bidx = jnp.arange(bsz)[:, None, None]
    xs = h[bidx, idx]
    hid = jax.nn.silu(jnp.einsum("becd,edf->becf", xs, w_gate)) * jnp.einsum("becd,edf->becf", xs, w_up)
    ye = jnp.einsum("becf,efd->becd", hid, w_down) * gate[..., None].astype(h.dtype)
    return jnp.zeros_like(h).at[bidx, idx].add(ye)


def setup_inputs(seed: int = 0) -> dict:
    key = jax.random.key(seed)
    ks = jax.random.split(key, 24)
    f32 = jnp.float32

    def nrm(k, shape, scale):
        return jax.random.normal(k, shape, f32) * scale

    def a_log(k, shape):
        return jnp.log(jax.random.uniform(k, shape, f32, 1.0, 16.0))

    def dt_bias(k, shape):
        dt = jnp.exp(jax.random.uniform(k, shape, f32, float(np.log(1e-3)), float(np.log(1e-1))))
        return dt + jnp.log(-jnp.expm1(-dt))

    return {
        "x": nrm(ks[0], (BATCH, SEQ, D_MODEL), 1.0),
        "c": nrm(ks[1], (BATCH, D_MODEL), 1.0),
        "ctx": nrm(ks[2], (BATCH, CTX_LEN, D_MODEL), 1.0),
        "c_ctx": nrm(ks[3], (D_MODEL,), 1.0),
        "ada_w": nrm(ks[4], (DEPTH, D_MODEL, 6 * D_MODEL), 0.3 * D_MODEL ** -0.5),
        "ada_b": nrm(ks[5], (DEPTH, 6 * D_MODEL), 0.02),
        "norm_g": 1.0 + nrm(ks[6], (DEPTH, 4, D_MODEL), 0.1),
        "w_in": nrm(ks[7], (DEPTH, D_MODEL, D_IN_PROJ), D_MODEL ** -0.5),
        "conv_w": nrm(ks[8], (DEPTH, CONV_CH, CONV_K, CONV_K), 1.0 / CONV_K),
        "conv_b": nrm(ks[9], (DEPTH, CONV_CH), 0.02),
        "dn_A_log": a_log(ks[10], (DEPTH, 2, DN_HEADS)),
        "dn_dt_bias": dt_bias(ks[11], (DEPTH, 2, DN_HEADS)),
        "dn_norm_g": 1.0 + nrm(ks[12], (DEPTH, DN_DV), 0.1),
        "ssm_A_log": a_log(ks[13], (DEPTH, 2, SSM_HEADS)),
        "ssm_dt_bias": dt_bias(ks[14], (DEPTH, 2, SSM_HEADS)),
        "ssm_D": 1.0 + nrm(ks[15], (DEPTH, SSM_HEADS), 0.1),
        "ssm_norm_g": 1.0 + nrm(ks[16], (DEPTH, SSM_DI), 0.1),
        "w_out": nrm(ks[17], (DEPTH, D_MIX, D_MODEL), D_MIX ** -0.5),
        "router_w": nrm(ks[18], (DEPTH, D_MODEL, N_EXPERTS), D_MODEL ** -0.5),
        "exp_w_gate": nrm(ks[19], (DEPTH, N_EXPERTS, D_MODEL, D_EXPERT), D_MODEL ** -0.5),
        "exp_w_up": nrm(ks[20], (DEPTH, N_EXPERTS, D_MODEL, D_EXPERT), D_MODEL ** -0.5),
        "exp_w_down": nrm(ks[21], (DEPTH, N_EXPERTS, D_EXPERT, D_MODEL), D_EXPERT ** -0.5),
    }


def reference(x, c, ctx, c_ctx, ada_w, ada_b, norm_g, w_in, conv_w, conv_b, dn_A_log, dn_dt_bias,
              dn_norm_g, ssm_A_log, ssm_dt_bias, ssm_D, ssm_norm_g, w_out, router_w,
              exp_w_gate, exp_w_up, exp_w_down):
    s_lat = jax.nn.silu(c)
    s_ctx = jax.nn.silu(c_ctx)
    for l in range(DEPTH):
        last = l == DEPTH - 1
        mod_lat = jnp.split((s_lat @ ada_w[l] + ada_b[l])[:, None, :], 6, axis=-1)
        n_ctx_mod = 2 if last else 6
        mod_ctx = jnp.split((s_ctx @ ada_w[l][:, :n_ctx_mod * D_MODEL]
                             + ada_b[l][:n_ctx_mod * D_MODEL])[None, None, :], n_ctx_mod, axis=-1)

        h_lat = modulate(rms_norm(x, norm_g[l, 0]), mod_lat[0], mod_lat[1])
        h_ctx = modulate(rms_norm(ctx, norm_g[l, 0]), mod_ctx[0], mod_ctx[1])
        m_ctx, m_lat = hybrid_mixer(h_ctx, h_lat, w_in[l], conv_w[l], conv_b[l], dn_A_log[l],
                                    dn_dt_bias[l], dn_norm_g[l], ssm_A_log[l], ssm_dt_bias[l],
                                    ssm_D[l], ssm_norm_g[l], w_out[l], ctx_out=not last)
        x = x + mod_lat[2] * rms_norm(m_lat, norm_g[l, 1])

        h_lat = modulate(rms_norm(x, norm_g[l, 2]), mod_lat[3], mod_lat[4])
        y_lat = ec_moe(h_lat, router_w[l], exp_w_gate[l], exp_w_up[l], exp_w_down[l])
        x = x + mod_lat[5] * rms_norm(y_lat, norm_g[l, 3])

        if not last:
            ctx = ctx + mod_ctx[2] * rms_norm(m_ctx, norm_g[l, 1])
            h_ctx = modulate(rms_norm(ctx, norm_g[l, 2]), mod_ctx[3], mod_ctx[4])
            y_ctx = ec_moe(h_ctx, router_w[l], exp_w_gate[l], exp_w_up[l], exp_w_down[l])
            ctx = ctx + mod_ctx[5] * rms_norm(y_ctx, norm_g[l, 3])
    return x
```

```python
import functools

import jax
import jax.numpy as jnp
import numpy as np
from jax import lax
from jax.experimental import pallas as pl
from jax.experimental.pallas import tpu as pltpu

D_MODEL = 1024
DEPTH = 4
GRID_W = 64
DN_HEADS = 4
DN_DK = 128
DN_DV = 128
SSM_HEADS = 8
SSM_P = 64
SSM_N = 128
SSM_G = 2
CHUNK = 64
CONV_K = 3
N_EXPERTS = 16
EC_CAPACITY_FACTOR = 2
D_EXPERT = 512
EPS = 1e-6

DN_QK = DN_HEADS * DN_DK
DN_VW = DN_HEADS * DN_DV
SSM_DI = SSM_HEADS * SSM_P
SSM_BC = SSM_G * SSM_N
D_MIX = DN_VW + SSM_DI
CONV_SPLITS = (DN_QK, DN_QK, DN_VW, SSM_DI, SSM_BC, SSM_BC)
CONV_CH = sum(CONV_SPLITS)
REST_SPLITS = (DN_VW, SSM_DI, 2 * DN_HEADS, 2 * DN_HEADS, 2 * SSM_HEADS)
D_IN_PROJ = CONV_CH + sum(REST_SPLITS)

VMEM_LIMIT_BYTES = 56 * 1024 * 1024

F32 = jnp.float32
BF16 = jnp.bfloat16


def _compiler_params(semantics):
    return pltpu.CompilerParams(dimension_semantics=semantics, vmem_limit_bytes=VMEM_LIMIT_BYTES)


def _rms(x):
    return x * lax.rsqrt(jnp.mean(x * x, axis=-1, keepdims=True) + EPS)


def _norm_mod_matmul_kernel(x_ref, g_ref, shift_ref, scale_ref, w_ref, o_ref):
    h = _rms(x_ref[0]) * g_ref[...]
    h = h * (1.0 + scale_ref[0]) + shift_ref[0]
    o_ref[0] = jnp.dot(h.astype(BF16), w_ref[...], preferred_element_type=F32)


def norm_mod_matmul(x, g, shift, scale, w_bf16, tm):
    bsz, t, d = x.shape
    n = w_bf16.shape[1]
    return pl.pallas_call(
        _norm_mod_matmul_kernel,
        grid=(bsz, t // tm),
        in_specs=[
            pl.BlockSpec((1, tm, d), lambda b, i: (b, i, 0)),
            pl.BlockSpec((1, d), lambda b, i: (0, 0)),
            pl.BlockSpec((1, 1, d), lambda b, i: (b, 0, 0)),
            pl.BlockSpec((1, 1, d), lambda b, i: (b, 0, 0)),
            pl.BlockSpec((d, n), lambda b, i: (0, 0)),
        ],
        out_specs=pl.BlockSpec((1, tm, n), lambda b, i: (b, i, 0)),
        out_shape=jax.ShapeDtypeStruct((bsz, t, n), F32),
        compiler_params=_compiler_params(("parallel", "parallel")),
        name="norm_mod_matmul",
    )(x, g.reshape(1, d), shift.reshape(bsz, 1, d), scale.reshape(bsz, 1, d), w_bf16)


def _matmul_norm_residual_kernel(m_ref, w_ref, g_ref, gate_ref, x_ref, o_ref):
    y = jnp.dot(m_ref[0].astype(BF16), w_ref[...], preferred_element_type=F32)
    o_ref[0] = x_ref[0] + gate_ref[0] * (_rms(y) * g_ref[...])


def matmul_norm_residual(m, w_bf16, g, gate, x, tm):
    bsz, t, k = m.shape
    d = w_bf16.shape[1]
    return pl.pallas_call(
        _matmul_norm_residual_kernel,
        grid=(bsz, t // tm),
        in_specs=[
            pl.BlockSpec((1, tm, k), lambda b, i: (b, i, 0)),
            pl.BlockSpec((k, d), lambda b, i: (0, 0)),
            pl.BlockSpec((1, d), lambda b, i: (0, 0)),
            pl.BlockSpec((1, 1, d), lambda b, i: (b, 0, 0)),
            pl.BlockSpec((1, tm, d), lambda b, i: (b, i, 0)),
        ],
        out_specs=pl.BlockSpec((1, tm, d), lambda b, i: (b, i, 0)),
        out_shape=jax.ShapeDtypeStruct((bsz, t, d), F32),
        compiler_params=_compiler_params(("parallel", "parallel")),
        name="matmul_norm_residual",
    )(m, w_bf16, g.reshape(1, d), gate.reshape(bsz, 1, d), x)


def _expert_ffn_kernel(xs_ref, wg_ref, wu_ref, wd_ref, o_ref):
    xs = xs_ref[0, 0].astype(BF16)
    a = jnp.dot(xs, wg_ref[0], preferred_element_type=F32)
    u = jnp.dot(xs, wu_ref[0], preferred_element_type=F32)
    hid = (a * jax.nn.sigmoid(a)) * u
    o_ref[0, 0] = jnp.dot(hid.astype(BF16), wd_ref[0], preferred_element_type=F32)


def expert_ffn(xs, wg_bf16, wu_bf16, wd_bf16):
    bsz, ne, cap, d = xs.shape
    f = wg_bf16.shape[2]
    return pl.pallas_call(
        _expert_ffn_kernel,
        grid=(ne, bsz),
        in_specs=[
            pl.BlockSpec((1, 1, cap, d), lambda e, b: (b, e, 0, 0)),
            pl.BlockSpec((1, d, f), lambda e, b: (e, 0, 0)),
            pl.BlockSpec((1, d, f), lambda e, b: (e, 0, 0)),
            pl.BlockSpec((1, f, d), lambda e, b: (e, 0, 0)),
        ],
        out_specs=pl.BlockSpec((1, 1, cap, d), lambda e, b: (b, e, 0, 0)),
        out_shape=jax.ShapeDtypeStruct((bsz, ne, cap, d), F32),
        compiler_params=_compiler_params(("parallel", "parallel")),
        name="expert_ffn",
    )(xs, wg_bf16, wu_bf16, wd_bf16)


def _split(a, sizes):
    return jnp.split(a, np.cumsum(sizes)[:-1].tolist(), axis=-1)


def _rev(a):
    return a[:, ::-1]


def rms_norm(x, g):
    return _rms(x) * g


def l2_normalize(x):
    return x * lax.rsqrt(jnp.sum(x * x, axis=-1, keepdims=True) + EPS)


def conv_grid(u, w, b):
    bsz, t, ch = u.shape
    rows = t // GRID_W
    img = u.reshape(bsz, rows, GRID_W, ch)
    kern = jnp.transpose(w, (1, 2, 0))[:, :, None, :]
    out = lax.conv_general_dilated(img, kern, window_strides=(1, 1), padding="SAME",
                                   dimension_numbers=("NHWC", "HWIO", "NHWC"),
                                   feature_group_count=ch)
    return out.reshape(bsz, t, ch) + b


def conv_seq(u, w, b):
    taps = w[:, CONV_K // 2, :]
    t = u.shape[1]
    pad = CONV_K // 2
    up = jnp.pad(u, ((0, 0), (pad, pad), (0, 0)))
    out = sum(up[:, i:i + t] * taps[:, i] for i in range(CONV_K))
    return out + b


def stream_features(proj, conv_w, conv_b, on_grid):
    bsz, t = proj.shape[:2]
    conv_in, rest = proj[..., :CONV_CH], proj[..., CONV_CH:]
    conv = conv_grid if on_grid else conv_seq
    u = jax.nn.silu(conv(conv_in, conv_w, conv_b))
    q, k, v, xs, bs, cs = _split(u, CONV_SPLITS)
    dn_z, ssm_z, dn_b, dn_a, ssm_dt = _split(rest, REST_SPLITS)
    return dict(
        q=l2_normalize(q.reshape(bsz, t, DN_HEADS, DN_DK)) * (DN_DK ** -0.5),
        k=l2_normalize(k.reshape(bsz, t, DN_HEADS, DN_DK)),
        v=v.reshape(bsz, t, DN_HEADS, DN_DV),
        x=xs.reshape(bsz, t, SSM_HEADS, SSM_P),
        B=bs.reshape(bsz, t, SSM_G, SSM_N),
        C=cs.reshape(bsz, t, SSM_G, SSM_N),
        dn_z=dn_z,
        ssm_z=ssm_z,
        dn_b=dn_b.reshape(bsz, t, 2, DN_HEADS),
        dn_a=dn_a.reshape(bsz, t, 2, DN_HEADS),
        ssm_dt=ssm_dt.reshape(bsz, t, 2, SSM_HEADS),
    )


def gated_delta_chunked(q, k, v, beta, logg, S0, with_output):
    bsz, t, nh, dk = k.shape
    dv = v.shape[-1]
    nc = t // CHUNK

    def to_chunks(a):
        a = a.reshape(bsz, nc, CHUNK, nh, *a.shape[3:])
        return jnp.moveaxis(a, (1, 3), (0, 2))

    qc, kc, vc, bc, gc = map(to_chunks, (q, k, v, beta, logg))
    G = jnp.cumsum(gc, axis=-1)
    tri = jnp.tril(jnp.ones((CHUNK, CHUNK), dtype=bool))
    strict = jnp.tril(jnp.ones((CHUNK, CHUNK), dtype=bool), -1)
    decay = jnp.where(tri, jnp.exp(jnp.where(tri, G[..., :, None] - G[..., None, :], 0.0)), 0.0)
    kb = kc * bc[..., None]
    lmat = jnp.where(strict, jnp.einsum("cbhid,cbhjd->cbhij", kb, kc) * decay, 0.0)
    rhs = jnp.concatenate([vc * bc[..., None], kb * jnp.exp(G)[..., None]], axis=-1)
    uw = lax.linalg.triangular_solve(lmat + jnp.eye(CHUNK, dtype=lmat.dtype), rhs,
                                     left_side=True, lower=True, unit_diagonal=True)
    u, w = uw[..., :dv], uw[..., dv:]
    g_last = G[..., -1]
    k_tail = kc * jnp.exp(g_last[..., None] - G)[..., None]

    def step(S, inp):
        u_i, w_i, kt_i, gl_i = inp[:4]
        v_new = u_i - jnp.einsum("bhld,bhdv->bhlv", w_i, S)
        if with_output:
            qg_i, qk_i = inp[4:]
            o = jnp.einsum("bhld,bhdv->bhlv", qg_i, S) + jnp.einsum("bhij,bhjv->bhiv", qk_i, v_new)
        else:
            o = None
        S = S * jnp.exp(gl_i)[..., None, None] + jnp.einsum("bhld,bhlv->bhdv", kt_i, v_new)
        return S, o

    xs = (u, w, k_tail, g_last)
    if with_output:
        qk = jnp.where(tri, jnp.einsum("cbhid,cbhjd->cbhij", qc, kc) * decay, 0.0)
        qg = qc * jnp.exp(G)[..., None]
        xs = xs + (qg, qk)
    S_T, o = lax.scan(step, S0, xs)
    if not with_output:
        return None, S_T
    o = jnp.moveaxis(o, (0, 2), (1, 3)).reshape(bsz, t, nh, dv)
    return o, S_T


def ssd_chunked(x, dt, A, Bm, Cm, h0, with_output):
    bsz, t, nh, hp = x.shape
    ng, ns = Bm.shape[2], Bm.shape[3]
    hpg = nh // ng
    nc = t // CHUNK
    xc = jnp.transpose(x.reshape(bsz, nc, CHUNK, ng, hpg, hp), (1, 0, 3, 4, 2, 5))
    dtc = jnp.transpose(dt.reshape(bsz, nc, CHUNK, ng, hpg), (1, 0, 3, 4, 2))
    Bc = jnp.transpose(Bm.reshape(bsz, nc, CHUNK, ng, ns), (1, 0, 3, 2, 4))
    Cc = jnp.transpose(Cm.reshape(bsz, nc, CHUNK, ng, ns), (1, 0, 3, 2, 4))
    Acs = jnp.cumsum(dtc * A.reshape(ng, hpg, 1), axis=-1)
    xdt = xc * dtc[..., None]

    def step(h, inp):
        B_i, C_i, xdt_i, a_i = inp
        a_last = a_i[..., -1:]
        if with_output:
            y_off = jnp.einsum("bgin,bghnp->bghip", C_i, h) * jnp.exp(a_i)[..., None]
        else:
            y_off = None
        st = jnp.einsum("bgjn,bghjp->bghnp", B_i, xdt_i * jnp.exp(a_last - a_i)[..., None])
        h = h * jnp.exp(a_last)[..., None] + st
        return h, y_off

    hT, y_off = lax.scan(step, h0.reshape(bsz, ng, hpg, ns, hp), (Bc, Cc, xdt, Acs))
    hT = hT.reshape(bsz, nh, ns, hp)
    if not with_output:
        return None, hT
    tri = jnp.tril(jnp.ones((CHUNK, CHUNK), dtype=bool))
    decay = jnp.where(tri, jnp.exp(jnp.where(tri, Acs[..., :, None] - Acs[..., None, :], 0.0)), 0.0)
    scores = jnp.einsum("cbgin,cbgjn->cbgij", Cc, Bc)[:, :, :, None] * decay
    y_diag = jnp.einsum("cbghij,cbghjp->cbghip", scores, xdt)
    y = jnp.transpose(y_diag + y_off, (1, 0, 4, 2, 3, 5)).reshape(bsz, t, nh, hp)
    return y, hT


def dn_inputs(f, d, A_log, dt_bias):
    beta = jax.nn.sigmoid(f["dn_b"][:, :, d])
    logg = -jnp.exp(A_log[d]) * jax.nn.softplus(f["dn_a"][:, :, d] + dt_bias[d])
    args = (f["q"], f["k"], f["v"], beta, logg)
    return tuple(_rev(a) for a in args) if d == 1 else args


def ssd_inputs(f, d, A_log, dt_bias):
    dt = jax.nn.softplus(f["ssm_dt"][:, :, d] + dt_bias[d])
    seq_args = (f["x"], dt, f["B"], f["C"])
    if d == 1:
        seq_args = tuple(_rev(a) for a in seq_args)
    x, dt, Bm, Cm = seq_args
    return x, dt, -jnp.exp(A_log[d]), Bm, Cm


def merge_heads(f, o_dirs, y_dirs, dn_norm_g, ssm_D, ssm_norm_g):
    o = o_dirs[0] + o_dirs[1]
    bsz, t = o.shape[:2]
    dn = rms_norm(o, dn_norm_g) * jax.nn.silu(f["dn_z"].reshape(o.shape))
    y = y_dirs[0] + y_dirs[1] + ssm_D[:, None] * f["x"]
    yz = y.reshape(bsz, t, SSM_DI) * jax.nn.silu(f["ssm_z"])
    ssm = rms_norm(yz.reshape(bsz, t, SSM_G, SSM_DI // SSM_G),
                   ssm_norm_g.reshape(SSM_G, SSM_DI // SSM_G)).reshape(bsz, t, SSM_DI)
    return jnp.concatenate([dn.reshape(bsz, t, DN_VW), ssm], axis=-1)


def hybrid_mixer(proj_ctx, proj_lat, conv_w, conv_b, dn_A_log, dn_dt_bias, dn_norm_g,
                 ssm_A_log, ssm_dt_bias, ssm_D, ssm_norm_g, ctx_out):
    fc = stream_features(proj_ctx, conv_w, conv_b, on_grid=False)
    fl = stream_features(proj_lat, conv_w, conv_b, on_grid=True)
    bsz = proj_lat.shape[0]
    dn_c, dn_l, ssm_c, ssm_l = [], [], [], []
    for d in range(2):
        unrev = _rev if d == 1 else (lambda a: a)
        S0 = jnp.zeros((bsz, DN_HEADS, DN_DK, DN_DV), F32)
        o_c, S_c = gated_delta_chunked(*dn_inputs(fc, d, dn_A_log, dn_dt_bias), S0, ctx_out)
        o_l, _ = gated_delta_chunked(*dn_inputs(fl, d, dn_A_log, dn_dt_bias), S_c, True)
        h0 = jnp.zeros((bsz, SSM_HEADS, SSM_N, SSM_P), F32)
        y_c, H_c = ssd_chunked(*ssd_inputs(fc, d, ssm_A_log, ssm_dt_bias), h0, ctx_out)
        y_l, _ = ssd_chunked(*ssd_inputs(fl, d, ssm_A_log, ssm_dt_bias), H_c, True)
        dn_l.append(unrev(o_l))
        ssm_l.append(unrev(y_l))
        if ctx_out:
            dn_c.append(unrev(o_c))
            ssm_c.append(unrev(y_c))
    m_lat = merge_heads(fl, dn_l, ssm_l, dn_norm_g, ssm_D, ssm_norm_g)
    m_ctx = merge_heads(fc, dn_c, ssm_c, dn_norm_g, ssm_D, ssm_norm_g) if ctx_out else None
    return m_ctx, m_lat


def ec_moe(x, g, shift, scale, router_w, wg, wu, wd):
    bsz, t, d = x.shape
    cap = EC_CAPACITY_FACTOR * t // N_EXPERTS
    h = rms_norm(x, g) * (1.0 + scale[:, None, :]) + shift[:, None, :]
    aff = jax.nn.softmax(jnp.einsum("btd,de->bte", h, router_w), axis=-1)
    gate, idx = lax.top_k(jnp.swapaxes(aff, 1, 2), cap)
    bidx = jnp.arange(bsz)[:, None, None]
    xs = h[bidx, idx]
    ye = expert_ffn(xs, wg, wu, wd) * gate[..., None]
    return jnp.zeros_like(h).at[bidx, idx].add(ye)


def kernel(x, c, ctx, c_ctx, ada_w, ada_b, norm_g, w_in, conv_w, conv_b, dn_A_log, dn_dt_bias,
           dn_norm_g, ssm_A_log, ssm_dt_bias, ssm_D, ssm_norm_g, w_out, router_w,
           exp_w_gate, exp_w_up, exp_w_down):
    bsz = x.shape[0]
    s_lat = jax.nn.silu(c)
    s_ctx = jax.nn.silu(c_ctx)
    for l in range(DEPTH):
        last = l == DEPTH - 1
        mod_lat = jnp.split(s_lat @ ada_w[l] + ada_b[l], 6, axis=-1)
        mod_ctx_row = s_ctx @ ada_w[l] + ada_b[l]
        mod_ctx = [jnp.broadcast_to(m[None, :], (bsz, D_MODEL)) for m in jnp.split(mod_ctx_row, 6)]

        w_in_l = w_in[l].astype(BF16)
        w_out_l = w_out[l].astype(BF16)
        wg, wu, wd = (exp_w_gate[l].astype(BF16), exp_w_up[l].astype(BF16), exp_w_down[l].astype(BF16))

        proj_lat = norm_mod_matmul(x, norm_g[l, 0], mod_lat[0], mod_lat[1], w_in_l, tm=512)
        proj_ctx = norm_mod_matmul(ctx, norm_g[l, 0], mod_ctx[0], mod_ctx[1], w_in_l, tm=256)
        m_ctx, m_lat = hybrid_mixer(proj_ctx, proj_lat, conv_w[l], conv_b[l], dn_A_log[l],
                                    dn_dt_bias[l], dn_norm_g[l], ssm_A_log[l], ssm_dt_bias[l],
                                    ssm_D[l], ssm_norm_g[l], ctx_out=not last)
        x = matmul_norm_residual(m_lat, w_out_l, norm_g[l, 1], mod_lat[2], x, tm=512)

        y_lat = ec_moe(x, norm_g[l, 2], mod_lat[3], mod_lat[4], router_w[l], wg, wu, wd)
        x = x + mod_lat[5][:, None, :] * rms_norm(y_lat, norm_g[l, 3])

        if not last:
            ctx = matmul_norm_residual(m_ctx, w_out_l, norm_g[l, 1], mod_ctx[2], ctx, tm=256)
            y_ctx = ec_moe(ctx, norm_g[l, 2], mod_ctx[3], mod_ctx[4], router_w[l], wg, wu, wd)
            ctx = ctx + mod_ctx[5][:, None, :] * rms_norm(y_ctx, norm_g[l, 3])
    return x
```

```python
import functools

import jax
import jax.numpy as jnp
import numpy as np
from jax import lax
from jax.experimental import pallas as pl
from jax.experimental.pallas import tpu as pltpu

D_MODEL = 1024
DEPTH = 4
GRID_W = 64
DN_HEADS = 4
DN_DK = 128
DN_DV = 128
SSM_HEADS = 8
SSM_P = 64
SSM_N = 128
SSM_G = 2
CHUNK = 64
CONV_K = 3
N_EXPERTS = 16
EC_CAPACITY_FACTOR = 2
D_EXPERT = 512
EPS = 1e-6

DN_QK = DN_HEADS * DN_DK
DN_VW = DN_HEADS * DN_DV
SSM_DI = SSM_HEADS * SSM_P
SSM_BC = SSM_G * SSM_N
SSM_HPG = SSM_HEADS // SSM_G
SSM_GW = SSM_HPG * SSM_P
D_MIX = DN_VW + SSM_DI
CONV_SPLITS = (DN_QK, DN_QK, DN_VW, SSM_DI, SSM_BC, SSM_BC)
CONV_CH = sum(CONV_SPLITS)
D_Z = DN_VW + SSM_DI
N_GATE_COLS = 2 * DN_HEADS + 2 * DN_HEADS + 2 * SSM_HEADS
D_IN_PROJ = CONV_CH + D_Z + N_GATE_COLS

LANES = 128
SUBLANES = 8
VMEM_LIMIT_BYTES = 56 * 1024 * 1024

OFF_Q, OFF_K, OFF_V = 0, DN_QK, 2 * DN_QK
OFF_X = 2 * DN_QK + DN_VW
OFF_B = OFF_X + SSM_DI
OFF_C = OFF_B + SSM_BC

NDH = 2 * DN_HEADS
NSH = 2 * SSM_HEADS
COL_BETA, COL_G, COL_EG, COL_EGL, COL_EGT = (i * NDH for i in range(5))
COL_DT, COL_A, COL_EA, COL_DTEAL, COL_EAT = (5 * NDH + i * NSH for i in range(5))

F32 = jnp.float32
BF16 = jnp.bfloat16


def _compiler_params(semantics):
    return pltpu.CompilerParams(dimension_semantics=semantics, vmem_limit_bytes=VMEM_LIMIT_BYTES)


def _rms(x):
    return x * lax.rsqrt(jnp.mean(x * x, axis=-1, keepdims=True) + EPS)


def _silu(x):
    return x * jax.nn.sigmoid(x)


def _softplus(x):
    return jnp.maximum(x, 0.0) + jnp.log(1.0 + jnp.exp(-jnp.abs(x)))


def _dot(a, b):
    return jnp.dot(a.astype(BF16), b.astype(BF16), preferred_element_type=F32)


def _dot_nt(a, b):
    return lax.dot_general(a.astype(BF16), b.astype(BF16), (((1,), (1,)), ((), ())),
                           preferred_element_type=F32)


def _dot_tn(a, b):
    return lax.dot_general(a.astype(BF16), b.astype(BF16), (((0,), (0,)), ((), ())),
                           preferred_element_type=F32)


def _dot_exact01(a, m01):
    a1 = a.astype(BF16)
    r1 = a - a1.astype(F32)
    a2 = r1.astype(BF16)
    a3 = (r1 - a2.astype(F32)).astype(BF16)
    m = m01.astype(BF16)
    out = jnp.dot(a3, m, preferred_element_type=F32)
    out = out + jnp.dot(a2, m, preferred_element_type=F32)
    return out + jnp.dot(a1, m, preferred_element_type=F32)


def _in_proj_kernel(x_ref, g_ref, shift_ref, scale_ref, wc_ref, wz_ref, ws_ref, oc_ref, oz_ref, os_ref):
    h = _rms(x_ref[0]) * g_ref[...]
    h = (h * (1.0 + scale_ref[0]) + shift_ref[0]).astype(BF16)
    oc_ref[0] = jnp.dot(h, wc_ref[...], preferred_element_type=F32)
    oz_ref[0] = jnp.dot(h, wz_ref[...], preferred_element_type=F32)
    os_ref[0] = jnp.dot(h, ws_ref[...], preferred_element_type=F32)


def in_proj(x, g, shift, scale, wc, wz, ws, tm):
    bsz, t, d = x.shape
    row = lambda b, i: (b, i, 0)
    const = lambda b, i: (0, 0)
    per_b = lambda b, i: (b, 0, 0)
    return pl.pallas_call(
        _in_proj_kernel,
        grid=(bsz, t // tm),
        in_specs=[
            pl.BlockSpec((1, tm, d), row),
            pl.BlockSpec((1, d), const),
            pl.BlockSpec((1, 1, d), per_b),
            pl.BlockSpec((1, 1, d), per_b),
            pl.BlockSpec((d, CONV_CH), const),
            pl.BlockSpec((d, D_Z), const),
            pl.BlockSpec((d, LANES), const),
        ],
        out_specs=[
            pl.BlockSpec((1, tm, CONV_CH), row),
            pl.BlockSpec((1, tm, D_Z), row),
            pl.BlockSpec((1, tm, LANES), row),
        ],
        out_shape=[
            jax.ShapeDtypeStruct((bsz, t, CONV_CH), F32),
            jax.ShapeDtypeStruct((bsz, t, D_Z), F32),
            jax.ShapeDtypeStruct((bsz, t, LANES), F32),
        ],
        compiler_params=_compiler_params(("parallel", "parallel")),
        name="in_proj",
    )(x, g.reshape(1, d), shift.reshape(bsz, 1, d), scale.reshape(bsz, 1, d), wc, wz, ws)


CONV_PAD = GRID_W + SUBLANES
CONV_TILE = 256


def _conv_feat_kernel(x_ref, w_ref, b_ref, o_ref, xp_ref, *, t, on_grid):
    j = pl.program_id(1)
    zeros = jnp.zeros((CONV_PAD, LANES), F32)
    xp_ref[0:CONV_PAD, :] = zeros
    xp_ref[CONV_PAD + t:CONV_PAD + t + CONV_PAD, :] = zeros
    xp_ref[CONV_PAD:CONV_PAD + t, :] = x_ref[0]

    tt = min(CONV_TILE, t)
    dys = (-1, 0, 1) if on_grid else (0,)
    col = lax.broadcasted_iota(jnp.int32, (tt, LANES), 0) % GRID_W
    bias = b_ref[...]
    w = w_ref[...]

    def conv_tile(i):
        base = pl.multiple_of(i * tt, tt)
        acc_c = acc_m = acc_p = None
        for dy in dys:
            win = xp_ref[pl.ds(base + CONV_PAD + dy * GRID_W - SUBLANES, tt + 2 * SUBLANES), :]
            c = win[SUBLANES:SUBLANES + tt]
            m = pltpu.roll(win, 1, 0)[SUBLANES:SUBLANES + tt]
            p = pltpu.roll(win, tt + 2 * SUBLANES - 1, 0)[SUBLANES:SUBLANES + tt]
            wr = 3 * (dy + 1)
            tc, tm_, tp = c * w[wr + 1:wr + 2], m * w[wr:wr + 1], p * w[wr + 2:wr + 3]
            acc_c = tc if acc_c is None else acc_c + tc
            acc_m = tm_ if acc_m is None else acc_m + tm_
            acc_p = tp if acc_p is None else acc_p + tp
        if on_grid:
            acc_m = jnp.where(col != 0, acc_m, 0.0)
            acc_p = jnp.where(col != GRID_W - 1, acc_p, 0.0)
        return base, _silu(acc_c + acc_m + acc_p + bias)

    n_qk_blocks = 2 * DN_QK // LANES

    @pl.when(j < n_qk_blocks)
    def _():
        qscale = jnp.where(j < DN_QK // LANES, DN_DK ** -0.5, 1.0).astype(F32)

        def body(i, carry):
            base, u = conv_tile(i)
            nrm = lax.rsqrt(jnp.sum(u * u, axis=-1, keepdims=True) + EPS)
            o_ref[0, pl.ds(base, tt), :] = u * nrm * qscale
            return carry
        lax.fori_loop(0, t // tt, body, 0)

    @pl.when(j >= n_qk_blocks)
    def _():
        def body(i, carry):
            base, u = conv_tile(i)
            o_ref[0, pl.ds(base, tt), :] = u
            return carry
        lax.fori_loop(0, t // tt, body, 0)


def conv_features(conv_in, w9, bias, on_grid):
    bsz, t, ch = conv_in.shape
    blk = lambda b, j: (b, 0, j)
    return pl.pallas_call(
        functools.partial(_conv_feat_kernel, t=t, on_grid=on_grid),
        grid=(bsz, ch // LANES),
        in_specs=[
            pl.BlockSpec((1, t, LANES), blk),
            pl.BlockSpec((9, LANES), lambda b, j: (0, j)),
            pl.BlockSpec((1, LANES), lambda b, j: (0, j)),
        ],
        out_specs=pl.BlockSpec((1, t, LANES), blk),
        out_shape=jax.ShapeDtypeStruct((bsz, t, ch), F32),
        scratch_shapes=[pltpu.VMEM((t + 2 * CONV_PAD, LANES), F32)],
        compiler_params=_compiler_params(("parallel", "parallel")),
        name="conv_features",
    )(conv_in, w9, bias)


def _gates_kernel(s_ref, bias_ref, nega_ref, col_ref, grow_ref, arow_ref, *, tg):
    tok = lax.broadcasted_iota(jnp.int32, (LANES, LANES), 0)
    out = lax.broadcasted_iota(jnp.int32, (LANES, LANES), 1)
    same = (tok // CHUNK) == (out // CHUNK)
    m_fwd = jnp.where(same & (tok <= out), 1.0, 0.0)
    m_bwd = jnp.where(same & (tok >= out), 1.0, 0.0)
    m_all = jnp.where(same, 1.0, 0.0)
    bias = bias_ref[...]
    nega = nega_ref[...]

    def dir_cumsum(v, heads):
        f = _dot_exact01(v, m_fwd)
        b = _dot_exact01(v, m_bwd)
        is_fwd = lax.broadcasted_iota(jnp.int32, v.shape, 0) < heads
        return jnp.where(is_fwd, f, b), _dot_exact01(v, m_all)

    for s in range(tg // LANES):
        st = s_ref[0, s * LANES:(s + 1) * LANES, :].T
        beta = jax.nn.sigmoid(st[0:NDH])
        sp = _softplus(st[NDH:2 * NDH + NSH] + bias[NDH:2 * NDH + NSH])
        logg = sp[0:NDH] * nega[NDH:2 * NDH]
        dt = sp[NDH:]
        a = dt * nega[2 * NDH:2 * NDH + NSH]
        g_cs, g_tot = dir_cumsum(logg, DN_HEADS)
        a_cs, a_tot = dir_cumsum(a, SSM_HEADS)
        rows = jnp.concatenate([
            beta, g_cs, jnp.exp(g_cs), jnp.exp(g_tot - g_cs), jnp.exp(g_tot),
            dt, a_cs, jnp.exp(a_cs), dt * jnp.exp(a_tot - a_cs), jnp.exp(a_tot),
            jnp.zeros((LANES - 5 * NDH - 5 * NSH, LANES), F32)], axis=0)
        col_ref[0, s * LANES:(s + 1) * LANES, :] = rows.T
        for half in range(LANES // CHUNK):
            c = s * (LANES // CHUNK) + half
            lo = half * CHUNK
            grow_ref[0, c] = g_cs[:, lo:lo + CHUNK]
            for dg in range(2 * SSM_G):
                r0 = dg * SSM_HPG
                arow_ref[0, c, dg:dg + 1, :] = jnp.concatenate(
                    [a_cs[r0 + hh:r0 + hh + 1, lo:lo + CHUNK] for hh in range(SSM_HPG)], axis=1)


def gates(small, bias_rows, nega_rows, tg):
    bsz, t, _ = small.shape
    nc = t // CHUNK
    ncg = tg // CHUNK
    const = lambda b, i: (0, 0)
    return pl.pallas_call(
        functools.partial(_gates_kernel, tg=tg),
        grid=(bsz, t // tg),
        in_specs=[
            pl.BlockSpec((1, tg, LANES), lambda b, i: (b, i, 0)),
            pl.BlockSpec((LANES, LANES), const),
            pl.BlockSpec((LANES, LANES), const),
        ],
        out_specs=[
            pl.BlockSpec((1, tg, LANES), lambda b, i: (b, i, 0)),
            pl.BlockSpec((1, ncg, NDH, CHUNK), lambda b, i: (b, i, 0, 0)),
            pl.BlockSpec((1, ncg, 2 * SSM_G, SSM_GW), lambda b, i: (b, i, 0, 0)),
        ],
        out_shape=[
            jax.ShapeDtypeStruct((bsz, t, LANES), F32),
            jax.ShapeDtypeStruct((bsz, nc, NDH, CHUNK), F32),
            jax.ShapeDtypeStruct((bsz, nc, 2 * SSM_G, SSM_GW), F32),
        ],
        compiler_params=_compiler_params(("parallel", "parallel")),
        name="gates",
    )(small, bias_rows, nega_rows)


def _chunk_masks(rev):
    r = lax.broadcasted_iota(jnp.int32, (CHUNK, CHUNK), 0)
    c = lax.broadcasted_iota(jnp.int32, (CHUNK, CHUNK), 1)
    return ((r <= c), (r < c)) if rev else ((r >= c), (r > c))


def _unit_tri_inverse(m):
    eye = (lax.broadcasted_iota(jnp.int32, (CHUNK, CHUNK), 0)
           == lax.broadcasted_iota(jnp.int32, (CHUNK, CHUNK), 1)).astype(F32)
    p = _dot(m, m)
    t = eye + m
    n_sq = int(np.log2(CHUNK)) - 1
    for _ in range(n_sq - 1):
        r = _dot(jnp.concatenate([t, p], axis=0), p)
        t = t + r[:CHUNK]
        p = r[CHUNK:]
    return t + _dot(t, p)


def _dn_kernel(qf_ref, kf_ref, vf_ref, cf_ref, rf_ref, qb_ref, kb_ref, vb_ref, cb_ref, rb_ref, s0_ref,
               of_ref, ob_ref, st_ref, s_scr, *, nct):
    j = pl.program_id(1)

    @pl.when(j == 0)
    def _():
        s_scr[...] = s0_ref[0]

    masks = (_chunk_masks(False), _chunk_masks(True))
    dir_refs = ((qf_ref, kf_ref, vf_ref, cf_ref, rf_ref, of_ref),
                (qb_ref, kb_ref, vb_ref, cb_ref, rb_ref, ob_ref))

    def body(c, carry):
        for d in range(2):
            q_ref, k_ref, v_ref, c_ref, r_ref, o_ref = dir_refs[d]
            incl, strict = masks[d]
            cc = c if d == 0 else nct - 1 - c
            r0 = pl.multiple_of(cc * CHUNK, CHUNK)
            ct = c_ref[0, pl.ds(r0, CHUNK), :]
            for h in range(DN_HEADS):
                dh = d * DN_HEADS + h
                lanes = slice(h * DN_DK, (h + 1) * DN_DK)
                q = q_ref[0, pl.ds(r0, CHUNK), lanes]
                k = k_ref[0, pl.ds(r0, CHUNK), lanes]
                v = v_ref[0, pl.ds(r0, CHUNK), lanes]
                beta = ct[:, COL_BETA + dh:COL_BETA + dh + 1]
                g_col = ct[:, COL_G + dh:COL_G + dh + 1]
                eg = ct[:, COL_EG + dh:COL_EG + dh + 1]
                egl = ct[:, COL_EGL + dh:COL_EGL + dh + 1]
                egt = ct[0:1, COL_EGT + dh:COL_EGT + dh + 1]
                g_row = r_ref[0, cc, dh:dh + 1, :]
                decay = jnp.where(incl, jnp.exp(jnp.where(incl, g_col - g_row, 0.0)), 0.0)
                kbeta = k * beta
                a = _dot_nt(jnp.concatenate([kbeta, q], axis=0), k)
                neg_l = jnp.where(strict, -(a[:CHUNK] * decay), 0.0)
                qk = a[CHUNK:] * decay
                t_inv = _unit_tri_inverse(neg_l)
                uw = _dot(t_inv, jnp.concatenate([v * beta, kbeta * eg], axis=1))
                s = s_scr[d, h]
                ws_qs = _dot(jnp.concatenate([uw[:, DN_DV:], q * eg], axis=0), s)
                v_new = uw[:, :DN_DV] - ws_qs[:CHUNK]
                o_ref[0, pl.ds(r0, CHUNK), lanes] = ws_qs[CHUNK:] + _dot(qk, v_new)
                s_scr[d, h] = s * egt + _dot_tn(k * egl, v_new)
        return carry

    lax.fori_loop(0, nct, body, 0)

    @pl.when(j == pl.num_programs(1) - 1)
    def _():
        st_ref[0] = s_scr[...]


def delta_scan(feat, gcol, grow, s0, tt):
    bsz, t, _ = feat.shape
    nt = t // tt
    nct = tt // CHUNK
    fwd = lambda lane_blk: (lambda b, j: (b, j, lane_blk))
    bwd = lambda lane_blk: (lambda b, j: (b, nt - 1 - j, lane_blk))
    qkv_blk = (1, tt, DN_QK)

    def side(im):
        return [pl.BlockSpec(qkv_blk, im(OFF_Q // DN_QK)), pl.BlockSpec(qkv_blk, im(OFF_K // DN_QK)),
                pl.BlockSpec(qkv_blk, im(OFF_V // DN_QK)), pl.BlockSpec((1, tt, LANES), im(0))]

    row_spec = lambda rev: pl.BlockSpec((1, nct, NDH, CHUNK),
                                        (lambda b, j: (b, nt - 1 - j, 0, 0)) if rev else (lambda b, j: (b, j, 0, 0)))
    state_spec = pl.BlockSpec((1, 2, DN_HEADS, DN_DK, DN_DV), lambda b, j: (b, 0, 0, 0, 0))
    return pl.pallas_call(
        functools.partial(_dn_kernel, nct=nct),
        grid=(bsz, nt),
        in_specs=side(fwd) + [row_spec(False)] + side(bwd) + [row_spec(True)] + [state_spec],
        out_specs=[pl.BlockSpec((1, tt, DN_VW), fwd(0)), pl.BlockSpec((1, tt, DN_VW), bwd(0)), state_spec],
        out_shape=[
            jax.ShapeDtypeStruct((bsz, t, DN_VW), F32),
            jax.ShapeDtypeStruct((bsz, t, DN_VW), F32),
            jax.ShapeDtypeStruct((bsz, 2, DN_HEADS, DN_DK, DN_DV), F32),
        ],
        scratch_shapes=[pltpu.VMEM((2, DN_HEADS, DN_DK, DN_DV), F32)],
        compiler_params=_compiler_params(("parallel", "arbitrary")),
        name="delta_scan",
    )(feat, feat, feat, gcol, grow, feat, feat, feat, gcol, grow, s0)


def _group_lane_select(cols):
    shape = (cols[0].shape[0], SSM_GW)
    head = lax.broadcasted_iota(jnp.int32, shape, 1) // SSM_P
    out = jnp.broadcast_to(cols[SSM_HPG - 1], shape)
    for hh in range(SSM_HPG - 2, -1, -1):
        out = jnp.where(head == hh, jnp.broadcast_to(cols[hh], shape), out)
    return out


def _ssd_kernel(xf_ref, bf_ref, cf_ref, gf_ref, rf_ref, xb_ref, bb_ref, cb_ref, gb_ref, rb_ref, h0_ref,
                yf_ref, yb_ref, ht_ref, h_scr, *, nct):
    j = pl.program_id(1)

    @pl.when(j == 0)
    def _():
        h_scr[...] = h0_ref[0]

    row_i = lax.broadcasted_iota(jnp.int32, (CHUNK, SSM_GW), 0)
    col_j = lax.broadcasted_iota(jnp.int32, (CHUNK, SSM_GW), 1) % SSM_P
    incl_dir = (row_i >= col_j, row_i <= col_j)
    bd_rows = lax.broadcasted_iota(jnp.int32, (SSM_HPG * CHUNK, SSM_GW), 0) // CHUNK
    bd_cols = lax.broadcasted_iota(jnp.int32, (SSM_HPG * CHUNK, SSM_GW), 1) // SSM_P
    block_diag = bd_rows == bd_cols
    dir_refs = ((xf_ref, bf_ref, cf_ref, gf_ref, rf_ref, yf_ref),
                (xb_ref, bb_ref, cb_ref, gb_ref, rb_ref, yb_ref))

    def body(c, carry):
        for d in range(2):
            x_ref, b_ref, c_ref, g_ref, r_ref, y_ref = dir_refs[d]
            incl = incl_dir[d]
            cc = c if d == 0 else nct - 1 - c
            r0 = pl.multiple_of(cc * CHUNK, CHUNK)
            ct = g_ref[0, pl.ds(r0, CHUNK), :]
            for g in range(SSM_G):
                dg = d * SSM_G + g
                h0 = d * SSM_HEADS + g * SSM_HPG
                xg = x_ref[0, pl.ds(r0, CHUNK), g * SSM_GW:(g + 1) * SSM_GW]
                bm = b_ref[0, pl.ds(r0, CHUNK), g * SSM_N:(g + 1) * SSM_N]
                cm = c_ref[0, pl.ds(r0, CHUNK), g * SSM_N:(g + 1) * SSM_N]
                pick = lambda off, rows=slice(None): _group_lane_select(
                    [ct[rows, off + h0 + hh:off + h0 + hh + 1] for hh in range(SSM_HPG)])
                dt = pick(COL_DT)
                a_col = pick(COL_A)
                ea = pick(COL_EA)
                dteal = pick(COL_DTEAL)
                eat = pick(COL_EAT, slice(0, 1))
                a_row = r_ref[0, cc, dg:dg + 1, :]
                decay = jnp.where(incl, jnp.exp(jnp.where(incl, a_col - a_row, 0.0)), 0.0)
                cb4 = _dot_nt(cm, jnp.concatenate([bm] * SSM_HPG, axis=0))
                xdt = xg * dt
                xdt_bd = jnp.where(block_diag, jnp.concatenate([xdt] * SSM_HPG, axis=0), 0.0)
                hg = h_scr[d, g]
                y = _dot(cb4 * decay, xdt_bd) + _dot(cm, hg) * ea
                y_ref[0, pl.ds(r0, CHUNK), g * SSM_GW:(g + 1) * SSM_GW] = y
                h_scr[d, g] = hg * eat + _dot_tn(bm, xg * dteal)
        return carry

    lax.fori_loop(0, nct, body, 0)

    @pl.when(j == pl.num_programs(1) - 1)
    def _():
        ht_ref[0] = h_scr[...]


def ssd_scan(feat, gcol, arow, h0, tt):
    bsz, t, _ = feat.shape
    nt = t // tt
    nct = tt // CHUNK
    fwd = lambda lane_blk: (lambda b, j: (b, j, lane_blk))
    bwd = lambda lane_blk: (lambda b, j: (b, nt - 1 - j, lane_blk))

    def side(im):
        return [pl.BlockSpec((1, tt, SSM_DI), im(OFF_X // SSM_DI)),
                pl.BlockSpec((1, tt, SSM_BC), im(OFF_B // SSM_BC)),
                pl.BlockSpec((1, tt, SSM_BC), im(OFF_C // SSM_BC)),
                pl.BlockSpec((1, tt, LANES), im(0))]

    row_spec = lambda rev: pl.BlockSpec((1, nct, 2 * SSM_G, SSM_GW),
                                        (lambda b, j: (b, nt - 1 - j, 0, 0)) if rev else (lambda b, j: (b, j, 0, 0)))
    state_spec = pl.BlockSpec((1, 2, SSM_G, SSM_N, SSM_GW), lambda b, j: (b, 0, 0, 0, 0))
    return pl.pallas_call(
        functools.partial(_ssd_kernel, nct=nct),
        grid=(bsz, nt),
        in_specs=side(fwd) + [row_spec(False)] + side(bwd) + [row_spec(True)] + [state_spec],
        out_specs=[pl.BlockSpec((1, tt, SSM_DI), fwd(0)), pl.BlockSpec((1, tt, SSM_DI), bwd(0)), state_spec],
        out_shape=[
            jax.ShapeDtypeStruct((bsz, t, SSM_DI), F32),
            jax.ShapeDtypeStruct((bsz, t, SSM_DI), F32),
            jax.ShapeDtypeStruct((bsz, 2, SSM_G, SSM_N, SSM_GW), F32),
        ],
        scratch_shapes=[pltpu.VMEM((2, SSM_G, SSM_N, SSM_GW), F32)],
        compiler_params=_compiler_params(("parallel", "arbitrary")),
        name="ssd_scan",
    )(feat, feat, feat, gcol, arow, feat, feat, feat, gcol, arow, h0)


def _merge_out_kernel(of_ref, ob_ref, yf_ref, yb_ref, xs_ref, z_ref, dng_ref, dskip_ref, ssg_ref,
                      w_ref, g_ref, gate_ref, res_ref, o_ref):
    o = of_ref[0] + ob_ref[0]
    z = z_ref[0]
    parts = []
    for h in range(DN_HEADS):
        lanes = slice(h * DN_DV, (h + 1) * DN_DV)
        parts.append(_rms(o[:, lanes]) * dng_ref[...] * _silu(z[:, lanes]))
    y = yf_ref[0] + yb_ref[0] + dskip_ref[...] * xs_ref[0]
    yz = y * _silu(z[:, DN_VW:])
    gw = SSM_DI // SSM_G
    for g in range(SSM_G):
        lanes = slice(g * gw, (g + 1) * gw)
        parts.append(_rms(yz[:, lanes]) * ssg_ref[:, lanes])
    mixed = jnp.concatenate(parts, axis=1).astype(BF16)
    m = jnp.dot(mixed, w_ref[...], preferred_element_type=F32)
    o_ref[0] = res_ref[0] + gate_ref[0] * (_rms(m) * g_ref[...])


def merge_out(o_f, o_b, y_f, y_b, feat, z, dn_norm_g, dskip, ssm_norm_g, w_out, g, gate, res, tm):
    bsz, t, d = res.shape
    row = lambda b, i: (b, i, 0)
    const = lambda b, i: (0, 0)
    half = lambda: pl.BlockSpec((1, tm, DN_VW), row)
    return pl.pallas_call(
        _merge_out_kernel,
        grid=(bsz, t // tm),
        in_specs=[
            half(), half(), half(), half(),
            pl.BlockSpec((1, tm, SSM_DI), lambda b, i: (b, i, OFF_X // SSM_DI)),
            pl.BlockSpec((1, tm, D_Z), row),
            pl.BlockSpec((1, DN_DV), const),
            pl.BlockSpec((1, SSM_DI), const),
            pl.BlockSpec((1, SSM_DI), const),
            pl.BlockSpec((D_MIX, d), const),
            pl.BlockSpec((1, d), const),
            pl.BlockSpec((1, 1, d), lambda b, i: (b, 0, 0)),
            pl.BlockSpec((1, tm, d), row),
        ],
        out_specs=pl.BlockSpec((1, tm, d), row),
        out_shape=jax.ShapeDtypeStruct((bsz, t, d), F32),
        compiler_params=_compiler_params(("parallel", "parallel")),
        name="merge_out",
    )(o_f, o_b, y_f, y_b, feat, z, dn_norm_g.reshape(1, DN_DV), dskip, ssm_norm_g.reshape(1, SSM_DI),
      w_out, g.reshape(1, d), gate.reshape(bsz, 1, d), res)


def _expert_ffn_kernel(xs_ref, wg_ref, wu_ref, wd_ref, o_ref):
    xs = xs_ref[0, 0].astype(BF16)
    a = jnp.dot(xs, wg_ref[0], preferred_element_type=F32)
    u = jnp.dot(xs, wu_ref[0], preferred_element_type=F32)
    hid = _silu(a) * u
    o_ref[0, 0] = jnp.dot(hid.astype(BF16), wd_ref[0], preferred_element_type=F32)


def expert_ffn(xs, wg_bf16, wu_bf16, wd_bf16):
    bsz, ne, cap, d = xs.shape
    f = wg_bf16.shape[2]
    return pl.pallas_call(
        _expert_ffn_kernel,
        grid=(ne, bsz),
        in_specs=[
            pl.BlockSpec((1, 1, cap, d), lambda e, b: (b, e, 0, 0)),
            pl.BlockSpec((1, d, f), lambda e, b: (e, 0, 0)),
            pl.BlockSpec((1, d, f), lambda e, b: (e, 0, 0)),
            pl.BlockSpec((1, f, d), lambda e, b: (e, 0, 0)),
        ],
        out_specs=pl.BlockSpec((1, 1, cap, d), lambda e, b: (b, e, 0, 0)),
        out_shape=jax.ShapeDtypeStruct((bsz, ne, cap, d), F32),
        compiler_params=_compiler_params(("parallel", "parallel")),
        name="expert_ffn",
    )(xs, wg_bf16, wu_bf16, wd_bf16)


def _token_tile(t):
    return min(512, t)


def mixer_stream(x, g0, shift, scale, wc, wz, ws, w9, conv_b, bias_rows, nega_rows, dn_state, ssm_state, on_grid):
    t = x.shape[1]
    tm = _token_tile(t)
    conv_in, z, small = in_proj(x, g0, shift, scale, wc, wz, ws, tm)
    feat = conv_features(conv_in, w9, conv_b, on_grid)
    gcol, grow, arow = gates(small, bias_rows, nega_rows, tm)
    o_f, o_b, dn_state = delta_scan(feat, gcol, grow, dn_state, tm)
    y_f, y_b, ssm_state = ssd_scan(feat, gcol, arow, ssm_state, tm)
    return (o_f, o_b, y_f, y_b, feat, z), dn_state, ssm_state


def rms_norm(x, g):
    return _rms(x) * g


def ec_moe(x, g, shift, scale, router_w, wg, wu, wd):
    bsz, t, d = x.shape
    cap = EC_CAPACITY_FACTOR * t // N_EXPERTS
    h = rms_norm(x, g) * (1.0 + scale[:, None, :]) + shift[:, None, :]
    aff = jax.nn.softmax(jnp.einsum("btd,de->bte", h, router_w), axis=-1)
    gate, idx = lax.top_k(jnp.swapaxes(aff, 1, 2), cap)
    bidx = jnp.arange(bsz)[:, None, None]
    xs = h[bidx, idx]
    ye = expert_ffn(xs, wg, wu, wd) * gate[..., None]
    return jnp.zeros_like(h).at[bidx, idx].add(ye)


def _gate_param_rows(dn_bias, dn_a_log, ssm_bias, ssm_a_log):
    zeros = jnp.zeros((NDH,), F32)
    bias = jnp.concatenate([zeros, dn_bias.reshape(-1), ssm_bias.reshape(-1)])
    nega = jnp.concatenate([zeros, -jnp.exp(dn_a_log.reshape(-1)), -jnp.exp(ssm_a_log.reshape(-1))])
    pad = LANES - bias.shape[0]
    expand = lambda v: jnp.broadcast_to(jnp.pad(v, (0, pad))[:, None], (LANES, LANES))
    return expand(bias), expand(nega)


def kernel(x, c, ctx, c_ctx, ada_w, ada_b, norm_g, w_in, conv_w, conv_b, dn_A_log, dn_dt_bias,
           dn_norm_g, ssm_A_log, ssm_dt_bias, ssm_D, ssm_norm_g, w_out, router_w,
           exp_w_gate, exp_w_up, exp_w_down):
    bsz = x.shape[0]
    s_lat = jax.nn.silu(c)
    s_ctx = jax.nn.silu(c_ctx)
    for l in range(DEPTH):
        last = l == DEPTH - 1
        mod_lat = jnp.split(s_lat @ ada_w[l] + ada_b[l], 6, axis=-1)
        mod_ctx_row = s_ctx @ ada_w[l] + ada_b[l]
        mod_ctx = [jnp.broadcast_to(m[None, :], (bsz, D_MODEL)) for m in jnp.split(mod_ctx_row, 6)]

        w_in_l = w_in[l]
        wc = w_in_l[:, :CONV_CH].astype(BF16)
        wz = w_in_l[:, CONV_CH:CONV_CH + D_Z].astype(BF16)
        ws = jnp.pad(w_in_l[:, CONV_CH + D_Z:], ((0, 0), (0, LANES - N_GATE_COLS))).astype(BF16)
        w9 = conv_w[l].reshape(CONV_CH, CONV_K * CONV_K).T
        cb = conv_b[l].reshape(1, CONV_CH)
        bias_rows, nega_rows = _gate_param_rows(dn_dt_bias[l], dn_A_log[l], ssm_dt_bias[l], ssm_A_log[l])
        dskip = jnp.repeat(ssm_D[l], SSM_P).reshape(1, SSM_DI)
        w_out_l = w_out[l].astype(BF16)
        wg, wu, wd = (exp_w_gate[l].astype(BF16), exp_w_up[l].astype(BF16), exp_w_down[l].astype(BF16))

        dn0 = jnp.zeros((bsz, 2, DN_HEADS, DN_DK, DN_DV), F32)
        ssm0 = jnp.zeros((bsz, 2, SSM_G, SSM_N, SSM_GW), F32)
        shared = (wc, wz, ws, w9, cb, bias_rows, nega_rows)
        mix_ctx, dn_c, ssm_c = mixer_stream(ctx, norm_g[l, 0], mod_ctx[0], mod_ctx[1], *shared, dn0, ssm0, False)
        mix_lat, _, _ = mixer_stream(x, norm_g[l, 0], mod_lat[0], mod_lat[1], *shared, dn_c, ssm_c, True)
        merge_w = (dn_norm_g[l], dskip, ssm_norm_g[l], w_out_l, norm_g[l, 1])
        x = merge_out(*mix_lat, *merge_w, mod_lat[2], x, _token_tile(x.shape[1]))

        y_lat = ec_moe(x, norm_g[l, 2], mod_lat[3], mod_lat[4], router_w[l], wg, wu, wd)
        x = x + mod_lat[5][:, None, :] * rms_norm(y_lat, norm_g[l, 3])

        if not last:
            ctx = merge_out(*mix_ctx, *merge_w, mod_ctx[2], ctx, _token_tile(ctx.shape[1]))
            y_ctx = ec_moe(ctx, norm_g[l, 2], mod_ctx[3], mod_ctx[4], router_w[l], wg, wu, wd)
            ctx = ctx + mod_ctx[5][:, None, :] * rms_norm(y_ctx, norm_g[l, 3])
    return x
```

```python
import functools

import jax
import jax.numpy as jnp
import numpy as np
from jax import lax
from jax.experimental import pallas as pl
from jax.experimental.pallas import tpu as pltpu

D_MODEL = 1024
DEPTH = 4
GRID_W = 64
DN_HEADS = 4
DN_DK = 128
DN_DV = 128
SSM_HEADS = 8
SSM_P = 64
SSM_N = 128
SSM_G = 2
CHUNK = 64
CONV_K = 3
N_EXPERTS = 16
EC_CAPACITY_FACTOR = 2
D_EXPERT = 512
EPS = 1e-6

DN_QK = DN_HEADS * DN_DK
DN_VW = DN_HEADS * DN_DV
SSM_DI = SSM_HEADS * SSM_P
SSM_BC = SSM_G * SSM_N
SSM_HPG = SSM_HEADS // SSM_G
SSM_GW = SSM_HPG * SSM_P
D_MIX = DN_VW + SSM_DI
CONV_SPLITS = (DN_QK, DN_QK, DN_VW, SSM_DI, SSM_BC, SSM_BC)
CONV_CH = sum(CONV_SPLITS)
D_Z = DN_VW + SSM_DI
N_GATE_COLS = 2 * DN_HEADS + 2 * DN_HEADS + 2 * SSM_HEADS
D_IN_PROJ = CONV_CH + D_Z + N_GATE_COLS

LANES = 128
SUBLANES = 8
VMEM_LIMIT_BYTES = 56 * 1024 * 1024

OFF_Q, OFF_K, OFF_V = 0, DN_QK, 2 * DN_QK
OFF_X = 2 * DN_QK + DN_VW
OFF_B = OFF_X + SSM_DI
OFF_C = OFF_B + SSM_BC

NDH = 2 * DN_HEADS
NSH = 2 * SSM_HEADS
COL_BETA, COL_G, COL_EG, COL_EGL, COL_EGT = (i * NDH for i in range(5))
COL_DT, COL_A, COL_EA, COL_DTEAL, COL_EAT = (5 * NDH + i * NSH for i in range(5))

F32 = jnp.float32
BF16 = jnp.bfloat16


def _compiler_params(semantics):
    return pltpu.CompilerParams(dimension_semantics=semantics, vmem_limit_bytes=VMEM_LIMIT_BYTES)


def _rms(x):
    return x * lax.rsqrt(jnp.mean(x * x, axis=-1, keepdims=True) + EPS)


def _silu(x):
    return x * jax.nn.sigmoid(x)


def _softplus(x):
    return jnp.maximum(x, 0.0) + jnp.log(1.0 + jnp.exp(-jnp.abs(x)))


def _dot(a, b):
    return jnp.dot(a.astype(BF16), b.astype(BF16), preferred_element_type=F32)


def _dot_nt(a, b):
    return lax.dot_general(a.astype(BF16), b.astype(BF16), (((1,), (1,)), ((), ())),
                           preferred_element_type=F32)


def _dot_tn(a, b):
    return lax.dot_general(a.astype(BF16), b.astype(BF16), (((0,), (0,)), ((), ())),
                           preferred_element_type=F32)


def _dot_exact01(a, m01):
    a1 = a.astype(BF16)
    r1 = a - a1.astype(F32)
    a2 = r1.astype(BF16)
    a3 = (r1 - a2.astype(F32)).astype(BF16)
    m = m01.astype(BF16)
    out = jnp.dot(a3, m, preferred_element_type=F32)
    out = out + jnp.dot(a2, m, preferred_element_type=F32)
    return out + jnp.dot(a1, m, preferred_element_type=F32)


def _in_proj_kernel(x_ref, g_ref, shift_ref, scale_ref, wc_ref, wz_ref, ws_ref, oc_ref, oz_ref, os_ref):
    h = _rms(x_ref[0]) * g_ref[...]
    h = (h * (1.0 + scale_ref[0]) + shift_ref[0]).astype(BF16)
    oc_ref[0] = jnp.dot(h, wc_ref[...], preferred_element_type=F32)
    oz_ref[0] = jnp.dot(h, wz_ref[...], preferred_element_type=F32)
    os_ref[0] = jnp.dot(h, ws_ref[...], preferred_element_type=F32)


def in_proj(x, g, shift, scale, wc, wz, ws, tm):
    bsz, t, d = x.shape
    row = lambda b, i: (b, i, 0)
    const = lambda b, i: (0, 0)
    per_b = lambda b, i: (b, 0, 0)
    return pl.pallas_call(
        _in_proj_kernel,
        grid=(bsz, t // tm),
        in_specs=[
            pl.BlockSpec((1, tm, d), row),
            pl.BlockSpec((1, d), const),
            pl.BlockSpec((1, 1, d), per_b),
            pl.BlockSpec((1, 1, d), per_b),
            pl.BlockSpec((d, CONV_CH), const),
            pl.BlockSpec((d, D_Z), const),
            pl.BlockSpec((d, LANES), const),
        ],
        out_specs=[
            pl.BlockSpec((1, tm, CONV_CH), row),
            pl.BlockSpec((1, tm, D_Z), row),
            pl.BlockSpec((1, tm, LANES), row),
        ],
        out_shape=[
            jax.ShapeDtypeStruct((bsz, t, CONV_CH), F32),
            jax.ShapeDtypeStruct((bsz, t, D_Z), F32),
            jax.ShapeDtypeStruct((bsz, t, LANES), F32),
        ],
        compiler_params=_compiler_params(("parallel", "parallel")),
        name="in_proj",
    )(x, g.reshape(1, d), shift.reshape(bsz, 1, d), scale.reshape(bsz, 1, d), wc, wz, ws)


CONV_PAD = GRID_W + SUBLANES
CONV_TILE = 256


def _conv_feat_kernel(x_ref, w_ref, b_ref, o_ref, xp_ref, *, t, on_grid):
    j = pl.program_id(1)
    zeros = jnp.zeros((CONV_PAD, LANES), F32)
    xp_ref[0:CONV_PAD, :] = zeros
    xp_ref[CONV_PAD + t:CONV_PAD + t + CONV_PAD, :] = zeros
    xp_ref[CONV_PAD:CONV_PAD + t, :] = x_ref[0]

    tt = min(CONV_TILE, t)
    dys = (-1, 0, 1) if on_grid else (0,)
    col = lax.broadcasted_iota(jnp.int32, (tt, LANES), 0) % GRID_W
    bias = b_ref[...]
    w = w_ref[...]

    def conv_tile(i):
        base = pl.multiple_of(i * tt, tt)
        acc_c = acc_m = acc_p = None
        for dy in dys:
            win = xp_ref[pl.ds(base + CONV_PAD + dy * GRID_W - SUBLANES, tt + 2 * SUBLANES), :]
            c = win[SUBLANES:SUBLANES + tt]
            m = pltpu.roll(win, 1, 0)[SUBLANES:SUBLANES + tt]
            p = pltpu.roll(win, tt + 2 * SUBLANES - 1, 0)[SUBLANES:SUBLANES + tt]
            wr = 3 * (dy + 1)
            tc, tm_, tp = c * w[wr + 1:wr + 2], m * w[wr:wr + 1], p * w[wr + 2:wr + 3]
            acc_c = tc if acc_c is None else acc_c + tc
            acc_m = tm_ if acc_m is None else acc_m + tm_
            acc_p = tp if acc_p is None else acc_p + tp
        if on_grid:
            acc_m = jnp.where(col != 0, acc_m, 0.0)
            acc_p = jnp.where(col != GRID_W - 1, acc_p, 0.0)
        return base, _silu(acc_c + acc_m + acc_p + bias)

    n_qk_blocks = 2 * DN_QK // LANES

    @pl.when(j < n_qk_blocks)
    def _():
        qscale = jnp.where(j < DN_QK // LANES, DN_DK ** -0.5, 1.0).astype(F32)

        def body(i, carry):
            base, u = conv_tile(i)
            nrm = lax.rsqrt(jnp.sum(u * u, axis=-1, keepdims=True) + EPS)
            o_ref[0, pl.ds(base, tt), :] = u * nrm * qscale
            return carry
        lax.fori_loop(0, t // tt, body, 0)

    @pl.when(j >= n_qk_blocks)
    def _():
        def body(i, carry):
            base, u = conv_tile(i)
            o_ref[0, pl.ds(base, tt), :] = u
            return carry
        lax.fori_loop(0, t // tt, body, 0)


def conv_features(conv_in, w9, bias, on_grid):
    bsz, t, ch = conv_in.shape
    blk = lambda b, j: (b, 0, j)
    return pl.pallas_call(
        functools.partial(_conv_feat_kernel, t=t, on_grid=on_grid),
        grid=(bsz, ch // LANES),
        in_specs=[
            pl.BlockSpec((1, t, LANES), blk),
            pl.BlockSpec((9, LANES), lambda b, j: (0, j)),
            pl.BlockSpec((1, LANES), lambda b, j: (0, j)),
        ],
        out_specs=pl.BlockSpec((1, t, LANES), blk),
        out_shape=jax.ShapeDtypeStruct((bsz, t, ch), F32),
        scratch_shapes=[pltpu.VMEM((t + 2 * CONV_PAD, LANES), F32)],
        compiler_params=_compiler_params(("parallel", "parallel")),
        name="conv_features",
    )(conv_in, w9, bias)


def _gates_kernel(s_ref, bias_ref, nega_ref, col_ref, grow_ref, arow_ref, *, tg):
    tok = lax.broadcasted_iota(jnp.int32, (LANES, LANES), 0)
    out = lax.broadcasted_iota(jnp.int32, (LANES, LANES), 1)
    same = (tok // CHUNK) == (out // CHUNK)
    m_fwd = jnp.where(same & (tok <= out), 1.0, 0.0)
    m_bwd = jnp.where(same & (tok >= out), 1.0, 0.0)
    m_all = jnp.where(same, 1.0, 0.0)
    bias = bias_ref[...]
    nega = nega_ref[...]

    def dir_cumsum(v, heads):
        f = _dot_exact01(v, m_fwd)
        b = _dot_exact01(v, m_bwd)
        is_fwd = lax.broadcasted_iota(jnp.int32, v.shape, 0) < heads
        return jnp.where(is_fwd, f, b), _dot_exact01(v, m_all)

    for s in range(tg // LANES):
        st = s_ref[0, s * LANES:(s + 1) * LANES, :].T
        beta = jax.nn.sigmoid(st[0:NDH])
        sp = _softplus(st[NDH:2 * NDH + NSH] + bias[NDH:2 * NDH + NSH])
        logg = sp[0:NDH] * nega[NDH:2 * NDH]
        dt = sp[NDH:]
        a = dt * nega[2 * NDH:2 * NDH + NSH]
        g_cs, g_tot = dir_cumsum(logg, DN_HEADS)
        a_cs, a_tot = dir_cumsum(a, SSM_HEADS)
        rows = jnp.concatenate([
            beta, g_cs, jnp.exp(g_cs), jnp.exp(g_tot - g_cs), jnp.exp(g_tot),
            dt, a_cs, jnp.exp(a_cs), dt * jnp.exp(a_tot - a_cs), jnp.exp(a_tot),
            jnp.zeros((LANES - 5 * NDH - 5 * NSH, LANES), F32)], axis=0)
        col_ref[0, s * LANES:(s + 1) * LANES, :] = rows.T
        for half in range(LANES // CHUNK):
            c = s * (LANES // CHUNK) + half
            lo = half * CHUNK
            grow_ref[0, c] = g_cs[:, lo:lo + CHUNK]
            for dg in range(2 * SSM_G):
                r0 = dg * SSM_HPG
                arow_ref[0, c, dg:dg + 1, :] = jnp.concatenate(
                    [a_cs[r0 + hh:r0 + hh + 1, lo:lo + CHUNK] for hh in range(SSM_HPG)], axis=1)


def gates(small, bias_rows, nega_rows, tg):
    bsz, t, _ = small.shape
    nc = t // CHUNK
    ncg = tg // CHUNK
    const = lambda b, i: (0, 0)
    return pl.pallas_call(
        functools.partial(_gates_kernel, tg=tg),
        grid=(bsz, t // tg),
        in_specs=[
            pl.BlockSpec((1, tg, LANES), lambda b, i: (b, i, 0)),
            pl.BlockSpec((LANES, LANES), const),
            pl.BlockSpec((LANES, LANES), const),
        ],
        out_specs=[
            pl.BlockSpec((1, tg, LANES), lambda b, i: (b, i, 0)),
            pl.BlockSpec((1, ncg, NDH, CHUNK), lambda b, i: (b, i, 0, 0)),
            pl.BlockSpec((1, ncg, 2 * SSM_G, SSM_GW), lambda b, i: (b, i, 0, 0)),
        ],
        out_shape=[
            jax.ShapeDtypeStruct((bsz, t, LANES), F32),
            jax.ShapeDtypeStruct((bsz, nc, NDH, CHUNK), F32),
            jax.ShapeDtypeStruct((bsz, nc, 2 * SSM_G, SSM_GW), F32),
        ],
        compiler_params=_compiler_params(("parallel", "parallel")),
        name="gates",
    )(small, bias_rows, nega_rows)


def _chunk_masks(rev):
    r = lax.broadcasted_iota(jnp.int32, (CHUNK, CHUNK), 0)
    c = lax.broadcasted_iota(jnp.int32, (CHUNK, CHUNK), 1)
    return ((r <= c), (r < c)) if rev else ((r >= c), (r > c))


def _unit_tri_inverse(ms):
    eye = (lax.broadcasted_iota(jnp.int32, (CHUNK, CHUNK), 0)
           == lax.broadcasted_iota(jnp.int32, (CHUNK, CHUNK), 1)).astype(F32)
    ps = [_dot(m, m) for m in ms]
    ts = [eye + m for m in ms]
    n_sq = int(np.log2(CHUNK)) - 1
    for _ in range(n_sq - 1):
        rs = [_dot(jnp.concatenate([t, p], axis=0), p) for t, p in zip(ts, ps)]
        ts = [t + r[:CHUNK] for t, r in zip(ts, rs)]
        ps = [r[CHUNK:] for r in rs]
    return [t + _dot(t, p) for t, p in zip(ts, ps)]


def _dn_kernel(qf_ref, kf_ref, vf_ref, cf_ref, rf_ref, qb_ref, kb_ref, vb_ref, cb_ref, rb_ref, s0_ref,
               of_ref, ob_ref, st_ref, s_scr, *, nct):
    j = pl.program_id(1)

    @pl.when(j == 0)
    def _():
        s_scr[...] = s0_ref[0]

    masks = (_chunk_masks(False), _chunk_masks(True))
    dir_refs = ((qf_ref, kf_ref, vf_ref, cf_ref, rf_ref, of_ref),
                (qb_ref, kb_ref, vb_ref, cb_ref, rb_ref, ob_ref))

    def body(c, carry):
        chains = []
        for d in range(2):
            cc = c if d == 0 else nct - 1 - c
            r0 = pl.multiple_of(cc * CHUNK, CHUNK)
            ct = dir_refs[d][3][0, pl.ds(r0, CHUNK), :]
            for h in range(DN_HEADS):
                chains.append((d, h, cc, r0, ct))

        def load(chain, which):
            d, h, _, r0, _ = chain
            return dir_refs[d][which][0, pl.ds(r0, CHUNK), h * DN_DK:(h + 1) * DN_DK]

        def col(chain, off, rows=slice(None)):
            d, h, _, _, ct = chain
            lane = off + d * DN_HEADS + h
            return ct[rows, lane:lane + 1]

        def decay_of(chain):
            d, h, cc, _, _ = chain
            incl = masks[d][0]
            g_row = dir_refs[d][4][0, cc, d * DN_HEADS + h:d * DN_HEADS + h + 1, :]
            return jnp.where(incl, jnp.exp(jnp.where(incl, col(chain, COL_G) - g_row, 0.0)), 0.0)

        kbeta = [load(ch, 1) * col(ch, COL_BETA) for ch in chains]
        a = [_dot_nt(jnp.concatenate([kb, load(ch, 0)], axis=0), load(ch, 1)) for ch, kb in zip(chains, kbeta)]
        decay = [decay_of(ch) for ch in chains]
        neg_l = [jnp.where(masks[ch[0]][1], -(ai[:CHUNK] * dc), 0.0) for ch, ai, dc in zip(chains, a, decay)]
        qk = [ai[CHUNK:] * dc for ai, dc in zip(a, decay)]
        t_inv = _unit_tri_inverse(neg_l)
        uw = [_dot(ti, jnp.concatenate([load(ch, 2) * col(ch, COL_BETA), kb * col(ch, COL_EG)], axis=1))
              for ch, ti, kb in zip(chains, t_inv, kbeta)]
        s = [s_scr[ch[0], ch[1]] for ch in chains]
        ws_qs = [_dot(jnp.concatenate([uwi[:, DN_DV:], load(ch, 0) * col(ch, COL_EG)], axis=0), si)
                 for ch, uwi, si in zip(chains, uw, s)]
        v_new = [uwi[:, :DN_DV] - wq[:CHUNK] for uwi, wq in zip(uw, ws_qs)]
        o = [wq[CHUNK:] + _dot(qki, vn) for wq, qki, vn in zip(ws_qs, qk, v_new)]
        s_new = [si * col(ch, COL_EGT, slice(0, 1)) + _dot_tn(load(ch, 1) * col(ch, COL_EGL), vn)
                 for ch, si, vn in zip(chains, s, v_new)]
        for ch, oi, sn in zip(chains, o, s_new):
            d, h, _, r0, _ = ch
            dir_refs[d][5][0, pl.ds(r0, CHUNK), h * DN_DV:(h + 1) * DN_DV] = oi
            s_scr[d, h] = sn
        return carry

    lax.fori_loop(0, nct, body, 0)

    @pl.when(j == pl.num_programs(1) - 1)
    def _():
        st_ref[0] = s_scr[...]


def delta_scan(feat, gcol, grow, s0, tt):
    bsz, t, _ = feat.shape
    nt = t // tt
    nct = tt // CHUNK
    fwd = lambda lane_blk: (lambda b, j: (b, j, lane_blk))
    bwd = lambda lane_blk: (lambda b, j: (b, nt - 1 - j, lane_blk))
    qkv_blk = (1, tt, DN_QK)

    def side(im):
        return [pl.BlockSpec(qkv_blk, im(OFF_Q // DN_QK)), pl.BlockSpec(qkv_blk, im(OFF_K // DN_QK)),
                pl.BlockSpec(qkv_blk, im(OFF_V // DN_QK)), pl.BlockSpec((1, tt, LANES), im(0))]

    row_spec = lambda rev: pl.BlockSpec((1, nct, NDH, CHUNK),
                                        (lambda b, j: (b, nt - 1 - j, 0, 0)) if rev else (lambda b, j: (b, j, 0, 0)))
    state_spec = pl.BlockSpec((1, 2, DN_HEADS, DN_DK, DN_DV), lambda b, j: (b, 0, 0, 0, 0))
    return pl.pallas_call(
        functools.partial(_dn_kernel, nct=nct),
        grid=(bsz, nt),
        in_specs=side(fwd) + [row_spec(False)] + side(bwd) + [row_spec(True)] + [state_spec],
        out_specs=[pl.BlockSpec((1, tt, DN_VW), fwd(0)), pl.BlockSpec((1, tt, DN_VW), bwd(0)), state_spec],
        out_shape=[
            jax.ShapeDtypeStruct((bsz, t, DN_VW), F32),
            jax.ShapeDtypeStruct((bsz, t, DN_VW), F32),
            jax.ShapeDtypeStruct((bsz, 2, DN_HEADS, DN_DK, DN_DV), F32),
        ],
        scratch_shapes=[pltpu.VMEM((2, DN_HEADS, DN_DK, DN_DV), F32)],
        compiler_params=_compiler_params(("parallel", "arbitrary")),
        name="delta_scan",
    )(feat, feat, feat, gcol, grow, feat, feat, feat, gcol, grow, s0)


def _group_lane_select(cols):
    shape = (cols[0].shape[0], SSM_GW)
    head = lax.broadcasted_iota(jnp.int32, shape, 1) // SSM_P
    out = jnp.broadcast_to(cols[SSM_HPG - 1], shape)
    for hh in range(SSM_HPG - 2, -1, -1):
        out = jnp.where(head == hh, jnp.broadcast_to(cols[hh], shape), out)
    return out


def _ssd_kernel(xf_ref, bf_ref, cf_ref, gf_ref, rf_ref, xb_ref, bb_ref, cb_ref, gb_ref, rb_ref, h0_ref,
                yf_ref, yb_ref, ht_ref, h_scr, *, nct):
    j = pl.program_id(1)

    @pl.when(j == 0)
    def _():
        h_scr[...] = h0_ref[0]

    row_i = lax.broadcasted_iota(jnp.int32, (CHUNK, SSM_GW), 0)
    col_j = lax.broadcasted_iota(jnp.int32, (CHUNK, SSM_GW), 1) % SSM_P
    incl_dir = (row_i >= col_j, row_i <= col_j)
    bd_rows = lax.broadcasted_iota(jnp.int32, (SSM_HPG * CHUNK, SSM_GW), 0) // CHUNK
    bd_cols = lax.broadcasted_iota(jnp.int32, (SSM_HPG * CHUNK, SSM_GW), 1) // SSM_P
    block_diag = bd_rows == bd_cols
    dir_refs = ((xf_ref, bf_ref, cf_ref, gf_ref, rf_ref, yf_ref),
                (xb_ref, bb_ref, cb_ref, gb_ref, rb_ref, yb_ref))

    def body(c, carry):
        chains = []
        for d in range(2):
            cc = c if d == 0 else nct - 1 - c
            r0 = pl.multiple_of(cc * CHUNK, CHUNK)
            ct = dir_refs[d][3][0, pl.ds(r0, CHUNK), :]
            for g in range(SSM_G):
                chains.append((d, g, cc, r0, ct))

        def load(chain, which, width):
            d, g, _, r0, _ = chain
            return dir_refs[d][which][0, pl.ds(r0, CHUNK), g * width:(g + 1) * width]

        def pick(chain, off, rows=slice(None)):
            d, g, _, _, ct = chain
            h0 = off + d * SSM_HEADS + g * SSM_HPG
            return _group_lane_select([ct[rows, h0 + hh:h0 + hh + 1] for hh in range(SSM_HPG)])

        def decay_of(chain):
            d, g, cc, _, _ = chain
            incl = incl_dir[d]
            a_row = dir_refs[d][4][0, cc, d * SSM_G + g:d * SSM_G + g + 1, :]
            return jnp.where(incl, jnp.exp(jnp.where(incl, pick(chain, COL_A) - a_row, 0.0)), 0.0)

        cb4 = [_dot_nt(load(ch, 2, SSM_N), jnp.concatenate([load(ch, 1, SSM_N)] * SSM_HPG, axis=0)) for ch in chains]
        hg = [h_scr[ch[0], ch[1]] for ch in chains]
        y_off = [_dot(load(ch, 2, SSM_N), h) for ch, h in zip(chains, hg)]
        h_in = [_dot_tn(load(ch, 1, SSM_N), load(ch, 0, SSM_GW) * pick(ch, COL_DTEAL)) for ch in chains]
        scores = [cb * decay_of(ch) for ch, cb in zip(chains, cb4)]
        xdt_bd = [jnp.where(block_diag,
                            jnp.concatenate([load(ch, 0, SSM_GW) * pick(ch, COL_DT)] * SSM_HPG, axis=0), 0.0)
                  for ch in chains]
        y_diag = [_dot(sc, xb) for sc, xb in zip(scores, xdt_bd)]
        for ch, yd, yo, h, hi in zip(chains, y_diag, y_off, hg, h_in):
            d, g, _, r0, _ = ch
            dir_refs[d][5][0, pl.ds(r0, CHUNK), g * SSM_GW:(g + 1) * SSM_GW] = yd + yo * pick(ch, COL_EA)
            h_scr[d, g] = h * pick(ch, COL_EAT, slice(0, 1)) + hi
        return carry

    lax.fori_loop(0, nct, body, 0)

    @pl.when(j == pl.num_programs(1) - 1)
    def _():
        ht_ref[0] = h_scr[...]


def ssd_scan(feat, gcol, arow, h0, tt):
    bsz, t, _ = feat.shape
    nt = t // tt
    nct = tt // CHUNK
    fwd = lambda lane_blk: (lambda b, j: (b, j, lane_blk))
    bwd = lambda lane_blk: (lambda b, j: (b, nt - 1 - j, lane_blk))

    def side(im):
        return [pl.BlockSpec((1, tt, SSM_DI), im(OFF_X // SSM_DI)),
                pl.BlockSpec((1, tt, SSM_BC), im(OFF_B // SSM_BC)),
                pl.BlockSpec((1, tt, SSM_BC), im(OFF_C // SSM_BC)),
                pl.BlockSpec((1, tt, LANES), im(0))]

    row_spec = lambda rev: pl.BlockSpec((1, nct, 2 * SSM_G, SSM_GW),
                                        (lambda b, j: (b, nt - 1 - j, 0, 0)) if rev else (lambda b, j: (b, j, 0, 0)))
    state_spec = pl.BlockSpec((1, 2, SSM_G, SSM_N, SSM_GW), lambda b, j: (b, 0, 0, 0, 0))
    return pl.pallas_call(
        functools.partial(_ssd_kernel, nct=nct),
        grid=(bsz, nt),
        in_specs=side(fwd) + [row_spec(False)] + side(bwd) + [row_spec(True)] + [state_spec],
        out_specs=[pl.BlockSpec((1, tt, SSM_DI), fwd(0)), pl.BlockSpec((1, tt, SSM_DI), bwd(0)), state_spec],
        out_shape=[
            jax.ShapeDtypeStruct((bsz, t, SSM_DI), F32),
            jax.ShapeDtypeStruct((bsz, t, SSM_DI), F32),
            jax.ShapeDtypeStruct((bsz, 2, SSM_G, SSM_N, SSM_GW), F32),
        ],
        scratch_shapes=[pltpu.VMEM((2, SSM_G, SSM_N, SSM_GW), F32)],
        compiler_params=_compiler_params(("parallel", "arbitrary")),
        name="ssd_scan",
    )(feat, feat, feat, gcol, arow, feat, feat, feat, gcol, arow, h0)


def _merge_out_kernel(of_ref, ob_ref, yf_ref, yb_ref, xs_ref, z_ref, dng_ref, dskip_ref, ssg_ref,
                      w_ref, g_ref, gate_ref, res_ref, o_ref):
    o = of_ref[0] + ob_ref[0]
    z = z_ref[0]
    parts = []
    for h in range(DN_HEADS):
        lanes = slice(h * DN_DV, (h + 1) * DN_DV)
        parts.append(_rms(o[:, lanes]) * dng_ref[...] * _silu(z[:, lanes]))
    y = yf_ref[0] + yb_ref[0] + dskip_ref[...] * xs_ref[0]
    yz = y * _silu(z[:, DN_VW:])
    gw = SSM_DI // SSM_G
    for g in range(SSM_G):
        lanes = slice(g * gw, (g + 1) * gw)
        parts.append(_rms(yz[:, lanes]) * ssg_ref[:, lanes])
    mixed = jnp.concatenate(parts, axis=1).astype(BF16)
    m = jnp.dot(mixed, w_ref[...], preferred_element_type=F32)
    o_ref[0] = res_ref[0] + gate_ref[0] * (_rms(m) * g_ref[...])


def merge_out(o_f, o_b, y_f, y_b, feat, z, dn_norm_g, dskip, ssm_norm_g, w_out, g, gate, res, tm):
    bsz, t, d = res.shape
    row = lambda b, i: (b, i, 0)
    const = lambda b, i: (0, 0)
    half = lambda: pl.BlockSpec((1, tm, DN_VW), row)
    return pl.pallas_call(
        _merge_out_kernel,
        grid=(bsz, t // tm),
        in_specs=[
            half(), half(), half(), half(),
            pl.BlockSpec((1, tm, SSM_DI), lambda b, i: (b, i, OFF_X // SSM_DI)),
            pl.BlockSpec((1, tm, D_Z), row),
            pl.BlockSpec((1, DN_DV), const),
            pl.BlockSpec((1, SSM_DI), const),
            pl.BlockSpec((1, SSM_DI), const),
            pl.BlockSpec((D_MIX, d), const),
            pl.BlockSpec((1, d), const),
            pl.BlockSpec((1, 1, d), lambda b, i: (b, 0, 0)),
            pl.BlockSpec((1, tm, d), row),
        ],
        out_specs=pl.BlockSpec((1, tm, d), row),
        out_shape=jax.ShapeDtypeStruct((bsz, t, d), F32),
        compiler_params=_compiler_params(("parallel", "parallel")),
        name="merge_out",
    )(o_f, o_b, y_f, y_b, feat, z, dn_norm_g.reshape(1, DN_DV), dskip, ssm_norm_g.reshape(1, SSM_DI),
      w_out, g.reshape(1, d), gate.reshape(bsz, 1, d), res)


def _expert_ffn_kernel(xs_ref, wg_ref, wu_ref, wd_ref, o_ref):
    xs = xs_ref[0, 0].astype(BF16)
    a = jnp.dot(xs, wg_ref[0], preferred_element_type=F32)
    u = jnp.dot(xs, wu_ref[0], preferred_element_type=F32)
    hid = _silu(a) * u
    o_ref[0, 0] = jnp.dot(hid.astype(BF16), wd_ref[0], preferred_element_type=F32)


def expert_ffn(xs, wg_bf16, wu_bf16, wd_bf16):
    bsz, ne, cap, d = xs.shape
    f = wg_bf16.shape[2]
    return pl.pallas_call(
        _expert_ffn_kernel,
        grid=(ne, bsz),
        in_specs=[
            pl.BlockSpec((1, 1, cap, d), lambda e, b: (b, e, 0, 0)),
            pl.BlockSpec((1, d, f), lambda e, b: (e, 0, 0)),
            pl.BlockSpec((1, d, f), lambda e, b: (e, 0, 0)),
            pl.BlockSpec((1, f, d), lambda e, b: (e, 0, 0)),
        ],
        out_specs=pl.BlockSpec((1, 1, cap, d), lambda e, b: (b, e, 0, 0)),
        out_shape=jax.ShapeDtypeStruct((bsz, ne, cap, d), F32),
        compiler_params=_compiler_params(("parallel", "parallel")),
        name="expert_ffn",
    )(xs, wg_bf16, wu_bf16, wd_bf16)


def _token_tile(t):
    return min(512, t)


def mixer_stream(x, g0, shift, scale, wc, wz, ws, w9, conv_b, bias_rows, nega_rows, dn_state, ssm_state, on_grid):
    t = x.shape[1]
    tm = _token_tile(t)
    conv_in, z, small = in_proj(x, g0, shift, scale, wc, wz, ws, tm)
    feat = conv_features(conv_in, w9, conv_b, on_grid)
    gcol, grow, arow = gates(small, bias_rows, nega_rows, tm)
    o_f, o_b, dn_state = delta_scan(feat, gcol, grow, dn_state, tm)
    y_f, y_b, ssm_state = ssd_scan(feat, gcol, arow, ssm_state, tm)
    return (o_f, o_b, y_f, y_b, feat, z), dn_state, ssm_state


def rms_norm(x, g):
    return _rms(x) * g


def ec_moe(x, g, shift, scale, router_w, wg, wu, wd):
    bsz, t, d = x.shape
    cap = EC_CAPACITY_FACTOR * t // N_EXPERTS
    h = rms_norm(x, g) * (1.0 + scale[:, None, :]) + shift[:, None, :]
    aff = jax.nn.softmax(jnp.einsum("btd,de->bte", h, router_w), axis=-1)
    gate, idx = lax.top_k(jnp.swapaxes(aff, 1, 2), cap)
    bidx = jnp.arange(bsz)[:, None, None]
    xs = h[bidx, idx]
    ye = expert_ffn(xs, wg, wu, wd) * gate[..., None]
    return jnp.zeros_like(h).at[bidx, idx].add(ye)


def _gate_param_rows(dn_bias, dn_a_log, ssm_bias, ssm_a_log):
    zeros = jnp.zeros((NDH,), F32)
    bias = jnp.concatenate([zeros, dn_bias.reshape(-1), ssm_bias.reshape(-1)])
    nega = jnp.concatenate([zeros, -jnp.exp(dn_a_log.reshape(-1)), -jnp.exp(ssm_a_log.reshape(-1))])
    pad = LANES - bias.shape[0]
    expand = lambda v: jnp.broadcast_to(jnp.pad(v, (0, pad))[:, None], (LANES, LANES))
    return expand(bias), expand(nega)


def kernel(x, c, ctx, c_ctx, ada_w, ada_b, norm_g, w_in, conv_w, conv_b, dn_A_log, dn_dt_bias,
           dn_norm_g, ssm_A_log, ssm_dt_bias, ssm_D, ssm_norm_g, w_out, router_w,
           exp_w_gate, exp_w_up, exp_w_down):
    bsz = x.shape[0]
    s_lat = jax.nn.silu(c)
    s_ctx = jax.nn.silu(c_ctx)
    for l in range(DEPTH):
        last = l == DEPTH - 1
        mod_lat = jnp.split(s_lat @ ada_w[l] + ada_b[l], 6, axis=-1)
        mod_ctx_row = s_ctx @ ada_w[l] + ada_b[l]
        mod_ctx = [jnp.broadcast_to(m[None, :], (bsz, D_MODEL)) for m in jnp.split(mod_ctx_row, 6)]

        w_in_l = w_in[l]
        wc = w_in_l[:, :CONV_CH].astype(BF16)
        wz = w_in_l[:, CONV_CH:CONV_CH + D_Z].astype(BF16)
        ws = jnp.pad(w_in_l[:, CONV_CH + D_Z:], ((0, 0), (0, LANES - N_GATE_COLS))).astype(BF16)
        w9 = conv_w[l].reshape(CONV_CH, CONV_K * CONV_K).T
        cb = conv_b[l].reshape(1, CONV_CH)
        bias_rows, nega_rows = _gate_param_rows(dn_dt_bias[l], dn_A_log[l], ssm_dt_bias[l], ssm_A_log[l])
        dskip = jnp.repeat(ssm_D[l], SSM_P).reshape(1, SSM_DI)
        w_out_l = w_out[l].astype(BF16)
        wg, wu, wd = (exp_w_gate[l].astype(BF16), exp_w_up[l].astype(BF16), exp_w_down[l].astype(BF16))

        dn0 = jnp.zeros((bsz, 2, DN_HEADS, DN_DK, DN_DV), F32)
        ssm0 = jnp.zeros((bsz, 2, SSM_G, SSM_N, SSM_GW), F32)
        shared = (wc, wz, ws, w9, cb, bias_rows, nega_rows)
        mix_ctx, dn_c, ssm_c = mixer_stream(ctx, norm_g[l, 0], mod_ctx[0], mod_ctx[1], *shared, dn0, ssm0, False)
        mix_lat, _, _ = mixer_stream(x, norm_g[l, 0], mod_lat[0], mod_lat[1], *shared, dn_c, ssm_c, True)
        merge_w = (dn_norm_g[l], dskip, ssm_norm_g[l], w_out_l, norm_g[l, 1])
        x = merge_out(*mix_lat, *merge_w, mod_lat[2], x, _token_tile(x.shape[1]))

        y_lat = ec_moe(x, norm_g[l, 2], mod_lat[3], mod_lat[4], router_w[l], wg, wu, wd)
        x = x + mod_lat[5][:, None, :] * rms_norm(y_lat, norm_g[l, 3])

        if not last:
            ctx = merge_out(*mix_ctx, *merge_w, mod_ctx[2], ctx, _token_tile(ctx.shape[1]))
            y_ctx = ec_moe(ctx, norm_g[l, 2], mod_ctx[3], mod_ctx[4], router_w[l], wg, wu, wd)
            ctx = ctx + mod_ctx[5][:, None, :] * rms_norm(y_ctx, norm_g[l, 3])
    return x
```

```python
import functools

import jax
import jax.numpy as jnp
import numpy as np
from jax import lax
from jax.experimental import pallas as pl
from jax.experimental.pallas import tpu as pltpu

D_MODEL = 1024
DEPTH = 4
GRID_W = 64
DN_HEADS = 4
DN_DK = 128
DN_DV = 128
SSM_HEADS = 8
SSM_P = 64
SSM_N = 128
SSM_G = 2
CHUNK = 64
CONV_K = 3
N_EXPERTS = 16
EC_CAPACITY_FACTOR = 2
D_EXPERT = 512
EPS = 1e-6

DN_QK = DN_HEADS * DN_DK
DN_VW = DN_HEADS * DN_DV
SSM_DI = SSM_HEADS * SSM_P
SSM_BC = SSM_G * SSM_N
SSM_HPG = SSM_HEADS // SSM_G
SSM_GW = SSM_HPG * SSM_P
D_MIX = DN_VW + SSM_DI
CONV_SPLITS = (DN_QK, DN_QK, DN_VW, SSM_DI, SSM_BC, SSM_BC)
CONV_CH = sum(CONV_SPLITS)
D_Z = DN_VW + SSM_DI
N_GATE_COLS = 2 * DN_HEADS + 2 * DN_HEADS + 2 * SSM_HEADS
D_IN_PROJ = CONV_CH + D_Z + N_GATE_COLS

LANES = 128
SUBLANES = 8
VMEM_LIMIT_BYTES = 56 * 1024 * 1024

OFF_Q, OFF_K, OFF_V = 0, DN_QK, 2 * DN_QK
OFF_X = 2 * DN_QK + DN_VW
OFF_B = OFF_X + SSM_DI
OFF_C = OFF_B + SSM_BC

NDH = 2 * DN_HEADS
NSH = 2 * SSM_HEADS
COL_BETA, COL_G, COL_EG, COL_EGL, COL_EGT = (i * NDH for i in range(5))
COL_DT, COL_A, COL_EA, COL_DTEAL, COL_EAT = (5 * NDH + i * NSH for i in range(5))

F32 = jnp.float32
BF16 = jnp.bfloat16


def _compiler_params(semantics):
    return pltpu.CompilerParams(dimension_semantics=semantics, vmem_limit_bytes=VMEM_LIMIT_BYTES)


def _rms(x):
    return x * lax.rsqrt(jnp.mean(x * x, axis=-1, keepdims=True) + EPS)


def _silu(x):
    return x * jax.nn.sigmoid(x)


def _softplus(x):
    return jnp.maximum(x, 0.0) + jnp.log(1.0 + jnp.exp(-jnp.abs(x)))


def _dot(a, b):
    return jnp.dot(a.astype(BF16), b.astype(BF16), preferred_element_type=F32)


def _dot_nt(a, b):
    return lax.dot_general(a.astype(BF16), b.astype(BF16), (((1,), (1,)), ((), ())),
                           preferred_element_type=F32)


def _dot_tn(a, b):
    return lax.dot_general(a.astype(BF16), b.astype(BF16), (((0,), (0,)), ((), ())),
                           preferred_element_type=F32)


def _dot_exact01(a, m01):
    a1 = a.astype(BF16)
    r1 = a - a1.astype(F32)
    a2 = r1.astype(BF16)
    a3 = (r1 - a2.astype(F32)).astype(BF16)
    m = m01.astype(BF16)
    out = jnp.dot(a3, m, preferred_element_type=F32)
    out = out + jnp.dot(a2, m, preferred_element_type=F32)
    return out + jnp.dot(a1, m, preferred_element_type=F32)


def _in_proj_kernel(x_ref, g_ref, shift_ref, scale_ref, wc_ref, wz_ref, ws_ref, oc_ref, oz_ref, os_ref):
    h = _rms(x_ref[0]) * g_ref[...]
    h = (h * (1.0 + scale_ref[0]) + shift_ref[0]).astype(BF16)
    oc_ref[0] = jnp.dot(h, wc_ref[...], preferred_element_type=F32)
    oz_ref[0] = jnp.dot(h, wz_ref[...], preferred_element_type=F32)
    os_ref[0] = jnp.dot(h, ws_ref[...], preferred_element_type=F32)


def in_proj(x, g, shift, scale, wc, wz, ws, tm):
    bsz, t, d = x.shape
    row = lambda b, i: (b, i, 0)
    const = lambda b, i: (0, 0)
    per_b = lambda b, i: (b, 0, 0)
    return pl.pallas_call(
        _in_proj_kernel,
        grid=(bsz, t // tm),
        in_specs=[
            pl.BlockSpec((1, tm, d), row),
            pl.BlockSpec((1, d), const),
            pl.BlockSpec((1, 1, d), per_b),
            pl.BlockSpec((1, 1, d), per_b),
            pl.BlockSpec((d, CONV_CH), const),
            pl.BlockSpec((d, D_Z), const),
            pl.BlockSpec((d, LANES), const),
        ],
        out_specs=[
            pl.BlockSpec((1, tm, CONV_CH), row),
            pl.BlockSpec((1, tm, D_Z), row),
            pl.BlockSpec((1, tm, LANES), row),
        ],
        out_shape=[
            jax.ShapeDtypeStruct((bsz, t, CONV_CH), F32),
            jax.ShapeDtypeStruct((bsz, t, D_Z), F32),
            jax.ShapeDtypeStruct((bsz, t, LANES), F32),
        ],
        compiler_params=_compiler_params(("parallel", "parallel")),
        name="in_proj",
    )(x, g.reshape(1, d), shift.reshape(bsz, 1, d), scale.reshape(bsz, 1, d), wc, wz, ws)


CONV_PAD = GRID_W + SUBLANES
CONV_TILE = 256


def _conv_feat_kernel(x_ref, w_ref, b_ref, o_ref, xp_ref, *, t, on_grid):
    j = pl.program_id(1)
    zeros = jnp.zeros((CONV_PAD, LANES), F32)
    xp_ref[0:CONV_PAD, :] = zeros
    xp_ref[CONV_PAD + t:CONV_PAD + t + CONV_PAD, :] = zeros
    xp_ref[CONV_PAD:CONV_PAD + t, :] = x_ref[0]

    tt = min(CONV_TILE, t)
    dys = (-1, 0, 1) if on_grid else (0,)
    col = lax.broadcasted_iota(jnp.int32, (tt, LANES), 0) % GRID_W
    bias = b_ref[...]
    w = w_ref[...]

    def conv_tile(i):
        base = pl.multiple_of(i * tt, tt)
        acc_c = acc_m = acc_p = None
        for dy in dys:
            win = xp_ref[pl.ds(base + CONV_PAD + dy * GRID_W - SUBLANES, tt + 2 * SUBLANES), :]
            c = win[SUBLANES:SUBLANES + tt]
            m = pltpu.roll(win, 1, 0)[SUBLANES:SUBLANES + tt]
            p = pltpu.roll(win, tt + 2 * SUBLANES - 1, 0)[SUBLANES:SUBLANES + tt]
            wr = 3 * (dy + 1)
            tc, tm_, tp = c * w[wr + 1:wr + 2], m * w[wr:wr + 1], p * w[wr + 2:wr + 3]
            acc_c = tc if acc_c is None else acc_c + tc
            acc_m = tm_ if acc_m is None else acc_m + tm_
            acc_p = tp if acc_p is None else acc_p + tp
        if on_grid:
            acc_m = jnp.where(col != 0, acc_m, 0.0)
            acc_p = jnp.where(col != GRID_W - 1, acc_p, 0.0)
        return base, _silu(acc_c + acc_m + acc_p + bias)

    n_qk_blocks = 2 * DN_QK // LANES

    @pl.when(j < n_qk_blocks)
    def _():
        qscale = jnp.where(j < DN_QK // LANES, DN_DK ** -0.5, 1.0).astype(F32)

        def body(i, carry):
            base, u = conv_tile(i)
            nrm = lax.rsqrt(jnp.sum(u * u, axis=-1, keepdims=True) + EPS)
            o_ref[0, pl.ds(base, tt), :] = u * nrm * qscale
            return carry
        lax.fori_loop(0, t // tt, body, 0)

    @pl.when(j >= n_qk_blocks)
    def _():
        def body(i, carry):
            base, u = conv_tile(i)
            o_ref[0, pl.ds(base, tt), :] = u
            return carry
        lax.fori_loop(0, t // tt, body, 0)


def conv_features(conv_in, w9, bias, on_grid):
    bsz, t, ch = conv_in.shape
    blk = lambda b, j: (b, 0, j)
    return pl.pallas_call(
        functools.partial(_conv_feat_kernel, t=t, on_grid=on_grid),
        grid=(bsz, ch // LANES),
        in_specs=[
            pl.BlockSpec((1, t, LANES), blk),
            pl.BlockSpec((9, LANES), lambda b, j: (0, j)),
            pl.BlockSpec((1, LANES), lambda b, j: (0, j)),
        ],
        out_specs=pl.BlockSpec((1, t, LANES), blk),
        out_shape=jax.ShapeDtypeStruct((bsz, t, ch), F32),
        scratch_shapes=[pltpu.VMEM((t + 2 * CONV_PAD, LANES), F32)],
        compiler_params=_compiler_params(("parallel", "parallel")),
        name="conv_features",
    )(conv_in, w9, bias)


def _gates_kernel(s_ref, bias_ref, nega_ref, col_ref, grow_ref, arow_ref, *, tg):
    tok = lax.broadcasted_iota(jnp.int32, (LANES, LANES), 0)
    out = lax.broadcasted_iota(jnp.int32, (LANES, LANES), 1)
    same = (tok // CHUNK) == (out // CHUNK)
    m_fwd = jnp.where(same & (tok <= out), 1.0, 0.0)
    m_bwd = jnp.where(same & (tok >= out), 1.0, 0.0)
    m_all = jnp.where(same, 1.0, 0.0)
    bias = bias_ref[...]
    nega = nega_ref[...]

    def dir_cumsum(v, heads):
        f = _dot_exact01(v, m_fwd)
        b = _dot_exact01(v, m_bwd)
        is_fwd = lax.broadcasted_iota(jnp.int32, v.shape, 0) < heads
        return jnp.where(is_fwd, f, b), _dot_exact01(v, m_all)

    for s in range(tg // LANES):
        st = s_ref[0, s * LANES:(s + 1) * LANES, :].T
        beta = jax.nn.sigmoid(st[0:NDH])
        sp = _softplus(st[NDH:2 * NDH + NSH] + bias[NDH:2 * NDH + NSH])
        logg = sp[0:NDH] * nega[NDH:2 * NDH]
        dt = sp[NDH:]
        a = dt * nega[2 * NDH:2 * NDH + NSH]
        g_cs, g_tot = dir_cumsum(logg, DN_HEADS)
        a_cs, a_tot = dir_cumsum(a, SSM_HEADS)
        rows = jnp.concatenate([
            beta, g_cs, jnp.exp(g_cs), jnp.exp(g_tot - g_cs), jnp.exp(g_tot),
            dt, a_cs, jnp.exp(a_cs), dt * jnp.exp(a_tot - a_cs), jnp.exp(a_tot),
            jnp.zeros((LANES - 5 * NDH - 5 * NSH, LANES), F32)], axis=0)
        col_ref[0, s * LANES:(s + 1) * LANES, :] = rows.T
        for half in range(LANES // CHUNK):
            c = s * (LANES // CHUNK) + half
            lo = half * CHUNK
            grow_ref[0, c] = g_cs[:, lo:lo + CHUNK]
            for dg in range(2 * SSM_G):
                r0 = dg * SSM_HPG
                arow_ref[0, c, dg:dg + 1, :] = jnp.concatenate(
                    [a_cs[r0 + hh:r0 + hh + 1, lo:lo + CHUNK] for hh in range(SSM_HPG)], axis=1)


def gates(small, bias_rows, nega_rows, tg):
    bsz, t, _ = small.shape
    nc = t // CHUNK
    ncg = tg // CHUNK
    const = lambda b, i: (0, 0)
    return pl.pallas_call(
        functools.partial(_gates_kernel, tg=tg),
        grid=(bsz, t // tg),
        in_specs=[
            pl.BlockSpec((1, tg, LANES), lambda b, i: (b, i, 0)),
            pl.BlockSpec((LANES, LANES), const),
            pl.BlockSpec((LANES, LANES), const),
        ],
        out_specs=[
            pl.BlockSpec((1, tg, LANES), lambda b, i: (b, i, 0)),
            pl.BlockSpec((1, ncg, NDH, CHUNK), lambda b, i: (b, i, 0, 0)),
            pl.BlockSpec((1, ncg, 2 * SSM_G, SSM_GW), lambda b, i: (b, i, 0, 0)),
        ],
        out_shape=[
            jax.ShapeDtypeStruct((bsz, t, LANES), F32),
            jax.ShapeDtypeStruct((bsz, nc, NDH, CHUNK), F32),
            jax.ShapeDtypeStruct((bsz, nc, 2 * SSM_G, SSM_GW), F32),
        ],
        compiler_params=_compiler_params(("parallel", "parallel")),
        name="gates",
    )(small, bias_rows, nega_rows)


def _chunk_masks(rev):
    r = lax.broadcasted_iota(jnp.int32, (CHUNK, CHUNK), 0)
    c = lax.broadcasted_iota(jnp.int32, (CHUNK, CHUNK), 1)
    return ((r <= c), (r < c)) if rev else ((r >= c), (r > c))


def _unit_tri_inverse(ms):
    eye = (lax.broadcasted_iota(jnp.int32, (CHUNK, CHUNK), 0)
           == lax.broadcasted_iota(jnp.int32, (CHUNK, CHUNK), 1)).astype(F32)
    ps = [_dot(m, m) for m in ms]
    ts = [eye + m for m in ms]
    n_sq = int(np.log2(CHUNK)) - 1
    for _ in range(n_sq - 1):
        rs = [_dot(jnp.concatenate([t, p], axis=0), p) for t, p in zip(ts, ps)]
        ts = [t + r[:CHUNK] for t, r in zip(ts, rs)]
        ps = [r[CHUNK:] for r in rs]
    return [t + _dot(t, p) for t, p in zip(ts, ps)]


def _dn_kernel(qf_ref, kf_ref, vf_ref, cf_ref, rf_ref, qb_ref, kb_ref, vb_ref, cb_ref, rb_ref, s0_ref,
               of_ref, ob_ref, st_ref, s_scr, *, nct):
    j = pl.program_id(1)

    @pl.when(j == 0)
    def _():
        s_scr[...] = s0_ref[0]

    masks = (_chunk_masks(False), _chunk_masks(True))
    dir_refs = ((qf_ref, kf_ref, vf_ref, cf_ref, rf_ref, of_ref),
                (qb_ref, kb_ref, vb_ref, cb_ref, rb_ref, ob_ref))

    def body(c, carry):
        chains = []
        for d in range(2):
            cc = c if d == 0 else nct - 1 - c
            r0 = pl.multiple_of(cc * CHUNK, CHUNK)
            ct = dir_refs[d][3][0, pl.ds(r0, CHUNK), :]
            for h in range(DN_HEADS):
                chains.append((d, h, cc, r0, ct))

        def load(chain, which):
            d, h, _, r0, _ = chain
            return dir_refs[d][which][0, pl.ds(r0, CHUNK), h * DN_DK:(h + 1) * DN_DK]

        def col(chain, off, rows=slice(None)):
            d, h, _, _, ct = chain
            lane = off + d * DN_HEADS + h
            return ct[rows, lane:lane + 1]

        def decay_of(chain):
            d, h, cc, _, _ = chain
            incl = masks[d][0]
            g_row = dir_refs[d][4][0, cc, d * DN_HEADS + h:d * DN_HEADS + h + 1, :]
            return jnp.where(incl, jnp.exp(jnp.where(incl, col(chain, COL_G) - g_row, 0.0)), 0.0)

        kbeta = [load(ch, 1) * col(ch, COL_BETA) for ch in chains]
        a = [_dot_nt(jnp.concatenate([kb, load(ch, 0)], axis=0), load(ch, 1)) for ch, kb in zip(chains, kbeta)]
        decay = [decay_of(ch) for ch in chains]
        neg_l = [jnp.where(masks[ch[0]][1], -(ai[:CHUNK] * dc), 0.0) for ch, ai, dc in zip(chains, a, decay)]
        qk = [ai[CHUNK:] * dc for ai, dc in zip(a, decay)]
        t_inv = _unit_tri_inverse(neg_l)
        uw = [_dot(ti, jnp.concatenate([load(ch, 2) * col(ch, COL_BETA), kb * col(ch, COL_EG)], axis=1))
              for ch, ti, kb in zip(chains, t_inv, kbeta)]
        s = [s_scr[ch[0], ch[1]] for ch in chains]
        ws_qs = [_dot(jnp.concatenate([uwi[:, DN_DV:], load(ch, 0) * col(ch, COL_EG)], axis=0), si)
                 for ch, uwi, si in zip(chains, uw, s)]
        v_new = [uwi[:, :DN_DV] - wq[:CHUNK] for uwi, wq in zip(uw, ws_qs)]
        o = [wq[CHUNK:] + _dot(qki, vn) for wq, qki, vn in zip(ws_qs, qk, v_new)]
        s_new = [si * col(ch, COL_EGT, slice(0, 1)) + _dot_tn(load(ch, 1) * col(ch, COL_EGL), vn)
                 for ch, si, vn in zip(chains, s, v_new)]
        for ch, oi, sn in zip(chains, o, s_new):
            d, h, _, r0, _ = ch
            dir_refs[d][5][0, pl.ds(r0, CHUNK), h * DN_DV:(h + 1) * DN_DV] = oi
            s_scr[d, h] = sn
        return carry

    lax.fori_loop(0, nct, body, 0)

    @pl.when(j == pl.num_programs(1) - 1)
    def _():
        st_ref[0] = s_scr[...]


def delta_scan(feat, gcol, grow, s0, tt):
    bsz, t, _ = feat.shape
    nt = t // tt
    nct = tt // CHUNK
    fwd = lambda lane_blk: (lambda b, j: (b, j, lane_blk))
    bwd = lambda lane_blk: (lambda b, j: (b, nt - 1 - j, lane_blk))
    qkv_blk = (1, tt, DN_QK)

    def side(im):
        return [pl.BlockSpec(qkv_blk, im(OFF_Q // DN_QK)), pl.BlockSpec(qkv_blk, im(OFF_K // DN_QK)),
                pl.BlockSpec(qkv_blk, im(OFF_V // DN_QK)), pl.BlockSpec((1, tt, LANES), im(0))]

    row_spec = lambda rev: pl.BlockSpec((1, nct, NDH, CHUNK),
                                        (lambda b, j: (b, nt - 1 - j, 0, 0)) if rev else (lambda b, j: (b, j, 0, 0)))
    state_spec = pl.BlockSpec((1, 2, DN_HEADS, DN_DK, DN_DV), lambda b, j: (b, 0, 0, 0, 0))
    return pl.pallas_call(
        functools.partial(_dn_kernel, nct=nct),
        grid=(bsz, nt),
        in_specs=side(fwd) + [row_spec(False)] + side(bwd) + [row_spec(True)] + [state_spec],
        out_specs=[pl.BlockSpec((1, tt, DN_VW), fwd(0)), pl.BlockSpec((1, tt, DN_VW), bwd(0)), state_spec],
        out_shape=[
            jax.ShapeDtypeStruct((bsz, t, DN_VW), F32),
            jax.ShapeDtypeStruct((bsz, t, DN_VW), F32),
            jax.ShapeDtypeStruct((bsz, 2, DN_HEADS, DN_DK, DN_DV), F32),
        ],
        scratch_shapes=[pltpu.VMEM((2, DN_HEADS, DN_DK, DN_DV), F32)],
        compiler_params=_compiler_params(("parallel", "arbitrary")),
        name="delta_scan",
    )(feat, feat, feat, gcol, grow, feat, feat, feat, gcol, grow, s0)


def _group_lane_select(cols):
    shape = (cols[0].shape[0], SSM_GW)
    head = lax.broadcasted_iota(jnp.int32, shape, 1) // SSM_P
    out = jnp.broadcast_to(cols[SSM_HPG - 1], shape)
    for hh in range(SSM_HPG - 2, -1, -1):
        out = jnp.where(head == hh, jnp.broadcast_to(cols[hh], shape), out)
    return out


def _ssd_kernel(xf_ref, bf_ref, cf_ref, gf_ref, rf_ref, xb_ref, bb_ref, cb_ref, gb_ref, rb_ref, h0_ref,
                yf_ref, yb_ref, ht_ref, h_scr, *, nct):
    j = pl.program_id(1)

    @pl.when(j == 0)
    def _():
        h_scr[...] = h0_ref[0]

    row_i = lax.broadcasted_iota(jnp.int32, (CHUNK, SSM_GW), 0)
    col_j = lax.broadcasted_iota(jnp.int32, (CHUNK, SSM_GW), 1) % SSM_P
    incl_dir = (row_i >= col_j, row_i <= col_j)
    bd_rows = lax.broadcasted_iota(jnp.int32, (SSM_HPG * CHUNK, SSM_GW), 0) // CHUNK
    bd_cols = lax.broadcasted_iota(jnp.int32, (SSM_HPG * CHUNK, SSM_GW), 1) // SSM_P
    block_diag = bd_rows == bd_cols
    dir_refs = ((xf_ref, bf_ref, cf_ref, gf_ref, rf_ref, yf_ref),
                (xb_ref, bb_ref, cb_ref, gb_ref, rb_ref, yb_ref))

    def body(c, carry):
        chains = []
        for d in range(2):
            cc = c if d == 0 else nct - 1 - c
            r0 = pl.multiple_of(cc * CHUNK, CHUNK)
            ct = dir_refs[d][3][0, pl.ds(r0, CHUNK), :]
            for g in range(SSM_G):
                chains.append((d, g, cc, r0, ct))

        def load(chain, which, width):
            d, g, _, r0, _ = chain
            return dir_refs[d][which][0, pl.ds(r0, CHUNK), g * width:(g + 1) * width]

        def pick(chain, off, rows=slice(None)):
            d, g, _, _, ct = chain
            h0 = off + d * SSM_HEADS + g * SSM_HPG
            return _group_lane_select([ct[rows, h0 + hh:h0 + hh + 1] for hh in range(SSM_HPG)])

        def decay_of(chain):
            d, g, cc, _, _ = chain
            incl = incl_dir[d]
            a_row = dir_refs[d][4][0, cc, d * SSM_G + g:d * SSM_G + g + 1, :]
            return jnp.where(incl, jnp.exp(jnp.where(incl, pick(chain, COL_A) - a_row, 0.0)), 0.0)

        cb4 = [_dot_nt(load(ch, 2, SSM_N), jnp.concatenate([load(ch, 1, SSM_N)] * SSM_HPG, axis=0)) for ch in chains]
        hg = [h_scr[ch[0], ch[1]] for ch in chains]
        y_off = [_dot(load(ch, 2, SSM_N), h) for ch, h in zip(chains, hg)]
        h_in = [_dot_tn(load(ch, 1, SSM_N), load(ch, 0, SSM_GW) * pick(ch, COL_DTEAL)) for ch in chains]
        scores = [cb * decay_of(ch) for ch, cb in zip(chains, cb4)]
        xdt_bd = [jnp.where(block_diag,
                            jnp.concatenate([load(ch, 0, SSM_GW) * pick(ch, COL_DT)] * SSM_HPG, axis=0), 0.0)
                  for ch in chains]
        y_diag = [_dot(sc, xb) for sc, xb in zip(scores, xdt_bd)]
        for ch, yd, yo, h, hi in zip(chains, y_diag, y_off, hg, h_in):
            d, g, _, r0, _ = ch
            dir_refs[d][5][0, pl.ds(r0, CHUNK), g * SSM_GW:(g + 1) * SSM_GW] = yd + yo * pick(ch, COL_EA)
            h_scr[d, g] = h * pick(ch, COL_EAT, slice(0, 1)) + hi
        return carry

    lax.fori_loop(0, nct, body, 0)

    @pl.when(j == pl.num_programs(1) - 1)
    def _():
        ht_ref[0] = h_scr[...]


def ssd_scan(feat, gcol, arow, h0, tt):
    bsz, t, _ = feat.shape
    nt = t // tt
    nct = tt // CHUNK
    fwd = lambda lane_blk: (lambda b, j: (b, j, lane_blk))
    bwd = lambda lane_blk: (lambda b, j: (b, nt - 1 - j, lane_blk))

    def side(im):
        return [pl.BlockSpec((1, tt, SSM_DI), im(OFF_X // SSM_DI)),
                pl.BlockSpec((1, tt, SSM_BC), im(OFF_B // SSM_BC)),
                pl.BlockSpec((1, tt, SSM_BC), im(OFF_C // SSM_BC)),
                pl.BlockSpec((1, tt, LANES), im(0))]

    row_spec = lambda rev: pl.BlockSpec((1, nct, 2 * SSM_G, SSM_GW),
                                        (lambda b, j: (b, nt - 1 - j, 0, 0)) if rev else (lambda b, j: (b, j, 0, 0)))
    state_spec = pl.BlockSpec((1, 2, SSM_G, SSM_N, SSM_GW), lambda b, j: (b, 0, 0, 0, 0))
    return pl.pallas_call(
        functools.partial(_ssd_kernel, nct=nct),
        grid=(bsz, nt),
        in_specs=side(fwd) + [row_spec(False)] + side(bwd) + [row_spec(True)] + [state_spec],
        out_specs=[pl.BlockSpec((1, tt, SSM_DI), fwd(0)), pl.BlockSpec((1, tt, SSM_DI), bwd(0)), state_spec],
        out_shape=[
            jax.ShapeDtypeStruct((bsz, t, SSM_DI), F32),
            jax.ShapeDtypeStruct((bsz, t, SSM_DI), F32),
            jax.ShapeDtypeStruct((bsz, 2, SSM_G, SSM_N, SSM_GW), F32),
        ],
        scratch_shapes=[pltpu.VMEM((2, SSM_G, SSM_N, SSM_GW), F32)],
        compiler_params=_compiler_params(("parallel", "arbitrary")),
        name="ssd_scan",
    )(feat, feat, feat, gcol, arow, feat, feat, feat, gcol, arow, h0)


def _merge_out_kernel(of_ref, ob_ref, yf_ref, yb_ref, xs_ref, z_ref, dng_ref, dskip_ref, ssg_ref,
                      w_ref, g_ref, gate_ref, res_ref, o_ref):
    o = of_ref[0] + ob_ref[0]
    z = z_ref[0]
    parts = []
    for h in range(DN_HEADS):
        lanes = slice(h * DN_DV, (h + 1) * DN_DV)
        parts.append(_rms(o[:, lanes]) * dng_ref[...] * _silu(z[:, lanes]))
    y = yf_ref[0] + yb_ref[0] + dskip_ref[...] * xs_ref[0]
    yz = y * _silu(z[:, DN_VW:])
    gw = SSM_DI // SSM_G
    for g in range(SSM_G):
        lanes = slice(g * gw, (g + 1) * gw)
        parts.append(_rms(yz[:, lanes]) * ssg_ref[:, lanes])
    mixed = jnp.concatenate(parts, axis=1).astype(BF16)
    m = jnp.dot(mixed, w_ref[...], preferred_element_type=F32)
    o_ref[0] = res_ref[0] + gate_ref[0] * (_rms(m) * g_ref[...])


def merge_out(o_f, o_b, y_f, y_b, feat, z, dn_norm_g, dskip, ssm_norm_g, w_out, g, gate, res, tm):
    bsz, t, d = res.shape
    row = lambda b, i: (b, i, 0)
    const = lambda b, i: (0, 0)
    half = lambda: pl.BlockSpec((1, tm, DN_VW), row)
    return pl.pallas_call(
        _merge_out_kernel,
        grid=(bsz, t // tm),
        in_specs=[
            half(), half(), half(), half(),
            pl.BlockSpec((1, tm, SSM_DI), lambda b, i: (b, i, OFF_X // SSM_DI)),
            pl.BlockSpec((1, tm, D_Z), row),
            pl.BlockSpec((1, DN_DV), const),
            pl.BlockSpec((1, SSM_DI), const),
            pl.BlockSpec((1, SSM_DI), const),
            pl.BlockSpec((D_MIX, d), const),
            pl.BlockSpec((1, d), const),
            pl.BlockSpec((1, 1, d), lambda b, i: (b, 0, 0)),
            pl.BlockSpec((1, tm, d), row),
        ],
        out_specs=pl.BlockSpec((1, tm, d), row),
        out_shape=jax.ShapeDtypeStruct((bsz, t, d), F32),
        compiler_params=_compiler_params(("parallel", "parallel")),
        name="merge_out",
    )(o_f, o_b, y_f, y_b, feat, z, dn_norm_g.reshape(1, DN_DV), dskip, ssm_norm_g.reshape(1, SSM_DI),
      w_out, g.reshape(1, d), gate.reshape(bsz, 1, d), res)


MOE_TILE = 256
SLOT_BLOCK = LANES
CNT_LANES = 32


def _router_kernel(x_ref, g_ref, shift_ref, scale_ref, wr_ref, ht_ref, aff_ref):
    h = _rms(x_ref[0]) * g_ref[...]
    h = h * (1.0 + scale_ref[0]) + shift_ref[0]
    ht = h.T.astype(BF16)
    logits = jnp.dot(wr_ref[...], ht, preferred_element_type=F32)
    ex = jnp.exp(logits - jnp.max(logits, axis=0, keepdims=True))
    ht_ref[0, 0] = ht
    aff_ref[0, 0] = ex / jnp.sum(ex, axis=0, keepdims=True)


def router(x, g, shift, scale, wr_t):
    bsz, t, d = x.shape
    nt = t // MOE_TILE
    const = lambda b, j: (0, 0)
    per_b = lambda b, j: (b, 0, 0)
    return pl.pallas_call(
        _router_kernel,
        grid=(bsz, nt),
        in_specs=[
            pl.BlockSpec((1, MOE_TILE, d), lambda b, j: (b, j, 0)),
            pl.BlockSpec((1, d), const),
            pl.BlockSpec((1, 1, d), per_b),
            pl.BlockSpec((1, 1, d), per_b),
            pl.BlockSpec((N_EXPERTS, d), const),
        ],
        out_specs=[
            pl.BlockSpec((1, 1, d, MOE_TILE), lambda b, j: (b, j, 0, 0)),
            pl.BlockSpec((1, 1, N_EXPERTS, MOE_TILE), lambda b, j: (b, j, 0, 0)),
        ],
        out_shape=[
            jax.ShapeDtypeStruct((bsz, nt, d, MOE_TILE), BF16),
            jax.ShapeDtypeStruct((bsz, nt, N_EXPERTS, MOE_TILE), F32),
        ],
        compiler_params=_compiler_params(("parallel", "parallel")),
        name="router",
    )(x, g.reshape(1, d), shift.reshape(bsz, 1, d), scale.reshape(bsz, 1, d), wr_t)


def _lane_prefix(mask, upper):
    carry = jnp.zeros((mask.shape[0], 1), F32)
    blocks = []
    for m in range(mask.shape[1] // LANES):
        p = jnp.dot(mask[:, m * LANES:(m + 1) * LANES].astype(BF16), upper, preferred_element_type=F32) + carry
        carry = p[:, LANES - 1:LANES]
        blocks.append(p)
    return jnp.concatenate(blocks, axis=1)


def _select_kernel(aff_ref, pos_ref, cnt_ref, *, nt, cap):
    aff = jnp.concatenate([aff_ref[0, j] for j in range(nt)], axis=1)
    bits = pltpu.bitcast(aff, jnp.int32)

    def bisect(i, thr):
        cand = thr | lax.shift_left(jnp.int32(1), (30 - i).astype(jnp.int32))
        n_ge = jnp.sum(jnp.where(bits >= cand, 1.0, 0.0), axis=1, keepdims=True)
        return jnp.where(n_ge >= cap, cand, thr)

    thr = lax.fori_loop(0, 31, bisect, jnp.zeros((N_EXPERTS, 1), jnp.int32))
    above = bits > thr
    tied = bits == thr
    room = cap - jnp.sum(jnp.where(above, 1.0, 0.0), axis=1, keepdims=True)
    src = lax.broadcasted_iota(jnp.int32, (LANES, LANES), 0)
    dst = lax.broadcasted_iota(jnp.int32, (LANES, LANES), 1)
    upper = jnp.where(src <= dst, 1.0, 0.0).astype(BF16)
    tie_rank = _lane_prefix(jnp.where(tied, 1.0, 0.0), upper)
    sel = above | (tied & (tie_rank <= room))
    selc = jnp.where(sel, 1.0, 0.0)
    slot = jnp.where(sel, _lane_prefix(selc, upper) - 1.0, -1.0)
    lane = lax.broadcasted_iota(jnp.int32, (N_EXPERTS, LANES), 1)
    before = jnp.zeros((N_EXPERTS, 1), F32)
    counts = jnp.zeros((N_EXPERTS, LANES), F32)
    for j in range(nt):
        pos_ref[0, j] = slot[:, j * MOE_TILE:(j + 1) * MOE_TILE]
        before = before + jnp.sum(selc[:, j * MOE_TILE:(j + 1) * MOE_TILE], axis=1, keepdims=True)
        counts = jnp.where(lane == j + 1, before, counts)
    cnt_ref[0] = counts.astype(jnp.int32)


def select_slots(aff, cap):
    bsz, nt, ne, tile = aff.shape
    blk = lambda b: (b, 0, 0, 0)
    return pl.pallas_call(
        functools.partial(_select_kernel, nt=nt, cap=cap),
        grid=(bsz,),
        in_specs=[pl.BlockSpec((1, nt, ne, tile), blk)],
        out_specs=[pl.BlockSpec((1, nt, ne, tile), blk), pl.BlockSpec((1, ne, LANES), lambda b: (b, 0, 0))],
        out_shape=[jax.ShapeDtypeStruct((bsz, nt, ne, tile), F32),
                   jax.ShapeDtypeStruct((bsz, ne, LANES), jnp.int32)],
        compiler_params=_compiler_params(("parallel",)),
        name="select_slots",
    )(aff)


def _moe_ffn_kernel(cnt_ref, ht_ref, pos_ref, aff_ref, wg_ref, wu_ref, wd_ref, y_ref, xs_scr, og_scr,
                    *, nt, ns):
    e = pl.program_id(1)
    row = pl.program_id(0) * N_EXPERTS + e
    d = ht_ref.shape[2]

    @pl.when(e == 0)
    def _():
        y_ref[...] = jnp.zeros_like(y_ref)

    slot_iota = lax.broadcasted_iota(jnp.int32, (SLOT_BLOCK, MOE_TILE), 0).astype(F32)

    def one_hot(j, k):
        pos = pos_ref[0, j, pl.ds(e, 1), :]
        return jnp.where(pos - (k * SLOT_BLOCK).astype(F32) == slot_iota, 1.0, 0.0).astype(BF16)

    for k in range(ns):
        lo = k * SLOT_BLOCK
        j_lo = sum((cnt_ref[row, j + 1] <= lo).astype(jnp.int32) for j in range(nt))
        j_hi = sum((cnt_ref[row, j] < lo + SLOT_BLOCK).astype(jnp.int32) for j in range(nt))

        def gather_tile(j, acc, k=k):
            return acc + lax.dot_general(ht_ref[0, j], one_hot(j, jnp.int32(k)), (((1,), (1,)), ((), ())),
                                         preferred_element_type=F32)
        acc = lax.fori_loop(j_lo, j_hi, gather_tile, jnp.zeros((d, SLOT_BLOCK), F32))
        xs_scr[k] = acc.astype(BF16)

    xs = jnp.concatenate([xs_scr[k] for k in range(ns)], axis=1)
    a = jnp.dot(wg_ref[0], xs, preferred_element_type=F32)
    u = jnp.dot(wu_ref[0], xs, preferred_element_type=F32)
    hid = (_silu(a) * u).astype(BF16)
    og = jnp.dot(wd_ref[0], hid, preferred_element_type=F32)
    for k in range(ns):
        og_scr[k] = og[:, k * SLOT_BLOCK:(k + 1) * SLOT_BLOCK].astype(BF16)

    def scatter_tile(j, carry):
        c0 = cnt_ref[row, j]
        c1 = cnt_ref[row, j + 1]

        @pl.when(c1 > c0)
        def _():
            k_lo = c0 // SLOT_BLOCK
            k_hi = (c1 - 1) // SLOT_BLOCK + 1

            def add_block(k, acc):
                return acc + jnp.dot(og_scr[k], one_hot(j, k), preferred_element_type=F32)
            contrib = lax.fori_loop(k_lo, k_hi, add_block, jnp.zeros((d, MOE_TILE), F32))
            y_ref[0, j] += contrib * aff_ref[0, j, pl.ds(e, 1), :]
        return carry

    lax.fori_loop(0, nt, scatter_tile, 0)


def moe_ffn(cnt, ht, pos, aff, wg_t, wu_t, wd_t, cap):
    bsz, nt, d, tile = ht.shape
    ne, f, _ = wg_t.shape
    ns = pl.cdiv(cap, SLOT_BLOCK)
    per_b = lambda b, e, c: (b, 0, 0, 0)
    per_e = lambda b, e, c: (e, 0, 0)
    grid_spec = pltpu.PrefetchScalarGridSpec(
        num_scalar_prefetch=1,
        grid=(bsz, ne),
        in_specs=[
            pl.BlockSpec((1, nt, d, tile), per_b, pipeline_mode=pl.Buffered(1)),
            pl.BlockSpec((1, nt, ne, tile), per_b),
            pl.BlockSpec((1, nt, ne, tile), per_b),
            pl.BlockSpec((1, f, d), per_e),
            pl.BlockSpec((1, f, d), per_e),
            pl.BlockSpec((1, d, f), per_e),
        ],
        out_specs=pl.BlockSpec((1, nt, d, tile), per_b, pipeline_mode=pl.Buffered(1)),
        scratch_shapes=[pltpu.VMEM((ns, d, SLOT_BLOCK), BF16), pltpu.VMEM((ns, d, SLOT_BLOCK), BF16)],
    )
    return pl.pallas_call(
        functools.partial(_moe_ffn_kernel, nt=nt, ns=ns),
        grid_spec=grid_spec,
        out_shape=jax.ShapeDtypeStruct((bsz, nt, d, tile), F32),
        compiler_params=_compiler_params(("parallel", "arbitrary")),
        name="moe_ffn",
    )(cnt, ht, pos, aff, wg_t, wu_t, wd_t)


def _moe_out_kernel(y_ref, g_ref, gate_ref, res_ref, o_ref):
    o_ref[0] = res_ref[0] + gate_ref[0] * (_rms(y_ref[0, 0].T) * g_ref[...])


def moe_out(y_t, g, gate, res):
    bsz, t, d = res.shape
    nt = t // MOE_TILE
    return pl.pallas_call(
        _moe_out_kernel,
        grid=(bsz, nt),
        in_specs=[
            pl.BlockSpec((1, 1, d, MOE_TILE), lambda b, j: (b, j, 0, 0)),
            pl.BlockSpec((1, d), lambda b, j: (0, 0)),
            pl.BlockSpec((1, 1, d), lambda b, j: (b, 0, 0)),
            pl.BlockSpec((1, MOE_TILE, d), lambda b, j: (b, j, 0)),
        ],
        out_specs=pl.BlockSpec((1, MOE_TILE, d), lambda b, j: (b, j, 0)),
        out_shape=jax.ShapeDtypeStruct((bsz, t, d), F32),
        compiler_params=_compiler_params(("parallel", "parallel")),
        name="moe_out",
    )(y_t, g.reshape(1, d), gate.reshape(bsz, 1, d), res)


def ec_moe_residual(x, g_in, shift, scale, wr_t, wg_t, wu_t, wd_t, g_out, gate):
    bsz, t, _ = x.shape
    cap = EC_CAPACITY_FACTOR * t // N_EXPERTS
    ht, aff = router(x, g_in, shift, scale, wr_t)
    pos, counts = select_slots(aff, cap)
    cnt = counts[:, :, :CNT_LANES].reshape(bsz * N_EXPERTS, CNT_LANES)
    y_t = moe_ffn(cnt, ht, pos, aff, wg_t, wu_t, wd_t, cap)
    return moe_out(y_t, g_out, gate, x)


def _token_tile(t):
    return min(512, t)


def mixer_stream(x, g0, shift, scale, wc, wz, ws, w9, conv_b, bias_rows, nega_rows, dn_state, ssm_state, on_grid):
    t = x.shape[1]
    tm = _token_tile(t)
    conv_in, z, small = in_proj(x, g0, shift, scale, wc, wz, ws, tm)
    feat = conv_features(conv_in, w9, conv_b, on_grid)
    gcol, grow, arow = gates(small, bias_rows, nega_rows, tm)
    o_f, o_b, dn_state = delta_scan(feat, gcol, grow, dn_state, tm)
    y_f, y_b, ssm_state = ssd_scan(feat, gcol, arow, ssm_state, tm)
    return (o_f, o_b, y_f, y_b, feat, z), dn_state, ssm_state


def _gate_param_rows(dn_bias, dn_a_log, ssm_bias, ssm_a_log):
    zeros = jnp.zeros((NDH,), F32)
    bias = jnp.concatenate([zeros, dn_bias.reshape(-1), ssm_bias.reshape(-1)])
    nega = jnp.concatenate([zeros, -jnp.exp(dn_a_log.reshape(-1)), -jnp.exp(ssm_a_log.reshape(-1))])
    pad = LANES - bias.shape[0]
    expand = lambda v: jnp.broadcast_to(jnp.pad(v, (0, pad))[:, None], (LANES, LANES))
    return expand(bias), expand(nega)


def kernel(x, c, ctx, c_ctx, ada_w, ada_b, norm_g, w_in, conv_w, conv_b, dn_A_log, dn_dt_bias,
           dn_norm_g, ssm_A_log, ssm_dt_bias, ssm_D, ssm_norm_g, w_out, router_w,
           exp_w_gate, exp_w_up, exp_w_down):
    bsz = x.shape[0]
    s_lat = jax.nn.silu(c)
    s_ctx = jax.nn.silu(c_ctx)
    for l in range(DEPTH):
        last = l == DEPTH - 1
        mod_lat = jnp.split(s_lat @ ada_w[l] + ada_b[l], 6, axis=-1)
        mod_ctx_row = s_ctx @ ada_w[l] + ada_b[l]
        mod_ctx = [jnp.broadcast_to(m[None, :], (bsz, D_MODEL)) for m in jnp.split(mod_ctx_row, 6)]

        w_in_l = w_in[l]
        wc = w_in_l[:, :CONV_CH].astype(BF16)
        wz = w_in_l[:, CONV_CH:CONV_CH + D_Z].astype(BF16)
        ws = jnp.pad(w_in_l[:, CONV_CH + D_Z:], ((0, 0), (0, LANES - N_GATE_COLS))).astype(BF16)
        w9 = conv_w[l].reshape(CONV_CH, CONV_K * CONV_K).T
        cb = conv_b[l].reshape(1, CONV_CH)
        bias_rows, nega_rows = _gate_param_rows(dn_dt_bias[l], dn_A_log[l], ssm_dt_bias[l], ssm_A_log[l])
        dskip = jnp.repeat(ssm_D[l], SSM_P).reshape(1, SSM_DI)
        w_out_l = w_out[l].astype(BF16)
        to_t = lambda w: jnp.swapaxes(w, -1, -2).astype(BF16)
        moe_w = (to_t(router_w[l]), to_t(exp_w_gate[l]), to_t(exp_w_up[l]), to_t(exp_w_down[l]), norm_g[l, 3])

        dn0 = jnp.zeros((bsz, 2, DN_HEADS, DN_DK, DN_DV), F32)
        ssm0 = jnp.zeros((bsz, 2, SSM_G, SSM_N, SSM_GW), F32)
        shared = (wc, wz, ws, w9, cb, bias_rows, nega_rows)
        mix_ctx, dn_c, ssm_c = mixer_stream(ctx, norm_g[l, 0], mod_ctx[0], mod_ctx[1], *shared, dn0, ssm0, False)
        mix_lat, _, _ = mixer_stream(x, norm_g[l, 0], mod_lat[0], mod_lat[1], *shared, dn_c, ssm_c, True)
        merge_w = (dn_norm_g[l], dskip, ssm_norm_g[l], w_out_l, norm_g[l, 1])
        x = merge_out(*mix_lat, *merge_w, mod_lat[2], x, _token_tile(x.shape[1]))

        x = ec_moe_residual(x, norm_g[l, 2], mod_lat[3], mod_lat[4], *moe_w, mod_lat[5])

        if not last:
            ctx = merge_out(*mix_ctx, *merge_w, mod_ctx[2], ctx, _token_tile(ctx.shape[1]))
            ctx = ec_moe_residual(ctx, norm_g[l, 2], mod_ctx[3], mod_ctx[4], *moe_w, mod_ctx[5])
    return x
```

```python
import functools

import jax
import jax.numpy as jnp
import numpy as np
from jax import lax
from jax.experimental import pallas as pl
from jax.experimental.pallas import tpu as pltpu

D_MODEL = 1024
DEPTH = 4
GRID_W = 64
DN_HEADS = 4
DN_DK = 128
DN_DV = 128
SSM_HEADS = 8
SSM_P = 64
SSM_N = 128
SSM_G = 2
CHUNK = 64
CONV_K = 3
N_EXPERTS = 16
EC_CAPACITY_FACTOR = 2
D_EXPERT = 512
EPS = 1e-6

DN_QK = DN_HEADS * DN_DK
DN_VW = DN_HEADS * DN_DV
SSM_DI = SSM_HEADS * SSM_P
SSM_BC = SSM_G * SSM_N
SSM_HPG = SSM_HEADS // SSM_G
SSM_GW = SSM_HPG * SSM_P
D_MIX = DN_VW + SSM_DI
CONV_SPLITS = (DN_QK, DN_QK, DN_VW, SSM_DI, SSM_BC, SSM_BC)
CONV_CH = sum(CONV_SPLITS)
D_Z = DN_VW + SSM_DI
N_GATE_COLS = 2 * DN_HEADS + 2 * DN_HEADS + 2 * SSM_HEADS
D_IN_PROJ = CONV_CH + D_Z + N_GATE_COLS

LANES = 128
SUBLANES = 8
VMEM_LIMIT_BYTES = 56 * 1024 * 1024

OFF_Q, OFF_K, OFF_V = 0, DN_QK, 2 * DN_QK
OFF_X = 2 * DN_QK + DN_VW
OFF_B = OFF_X + SSM_DI
OFF_C = OFF_B + SSM_BC

NDH = 2 * DN_HEADS
NSH = 2 * SSM_HEADS
COL_BETA, COL_G, COL_EG, COL_EGL, COL_EGT = (i * NDH for i in range(5))
COL_DT, COL_A, COL_EA, COL_DTEAL, COL_EAT = (5 * NDH + i * NSH for i in range(5))

F32 = jnp.float32
BF16 = jnp.bfloat16


def _compiler_params(semantics):
    return pltpu.CompilerParams(dimension_semantics=semantics, vmem_limit_bytes=VMEM_LIMIT_BYTES)


def _rms(x):
    return x * lax.rsqrt(jnp.mean(x * x, axis=-1, keepdims=True) + EPS)


def _silu(x):
    return x * jax.nn.sigmoid(x)


def _softplus(x):
    return jnp.maximum(x, 0.0) + jnp.log(1.0 + jnp.exp(-jnp.abs(x)))


def _dot(a, b):
    return jnp.dot(a.astype(BF16), b.astype(BF16), preferred_element_type=F32)


def _dot_nt(a, b):
    return lax.dot_general(a.astype(BF16), b.astype(BF16), (((1,), (1,)), ((), ())),
                           preferred_element_type=F32)


def _dot_tn(a, b):
    return lax.dot_general(a.astype(BF16), b.astype(BF16), (((0,), (0,)), ((), ())),
                           preferred_element_type=F32)


def _dot_exact01(a, m01):
    a1 = a.astype(BF16)
    r1 = a - a1.astype(F32)
    a2 = r1.astype(BF16)
    a3 = (r1 - a2.astype(F32)).astype(BF16)
    m = m01.astype(BF16)
    out = jnp.dot(a3, m, preferred_element_type=F32)
    out = out + jnp.dot(a2, m, preferred_element_type=F32)
    return out + jnp.dot(a1, m, preferred_element_type=F32)


def _in_proj_kernel(x_ref, g_ref, shift_ref, scale_ref, wc_ref, wz_ref, ws_ref, oc_ref, oz_ref, os_ref):
    h = _rms(x_ref[0]) * g_ref[...]
    h = (h * (1.0 + scale_ref[0]) + shift_ref[0]).astype(BF16)
    oc_ref[0] = jnp.dot(h, wc_ref[...], preferred_element_type=F32)
    oz_ref[0] = jnp.dot(h, wz_ref[...], preferred_element_type=F32)
    os_ref[0] = jnp.dot(h, ws_ref[...], preferred_element_type=F32)


def in_proj(x, g, shift, scale, wc, wz, ws, tm):
    bsz, t, d = x.shape
    row = lambda b, i: (b, i, 0)
    const = lambda b, i: (0, 0)
    per_b = lambda b, i: (b, 0, 0)
    return pl.pallas_call(
        _in_proj_kernel,
        grid=(bsz, t // tm),
        in_specs=[
            pl.BlockSpec((1, tm, d), row),
            pl.BlockSpec((1, d), const),
            pl.BlockSpec((1, 1, d), per_b),
            pl.BlockSpec((1, 1, d), per_b),
            pl.BlockSpec((d, CONV_CH), const),
            pl.BlockSpec((d, D_Z), const),
            pl.BlockSpec((d, LANES), const),
        ],
        out_specs=[
            pl.BlockSpec((1, tm, CONV_CH), row),
            pl.BlockSpec((1, tm, D_Z), row),
            pl.BlockSpec((1, tm, LANES), row),
        ],
        out_shape=[
            jax.ShapeDtypeStruct((bsz, t, CONV_CH), F32),
            jax.ShapeDtypeStruct((bsz, t, D_Z), F32),
            jax.ShapeDtypeStruct((bsz, t, LANES), F32),
        ],
        compiler_params=_compiler_params(("parallel", "parallel")),
        name="in_proj",
    )(x, g.reshape(1, d), shift.reshape(bsz, 1, d), scale.reshape(bsz, 1, d), wc, wz, ws)


CONV_PAD = GRID_W + SUBLANES
CONV_TILE = 256


def _conv_feat_kernel(x_ref, w_ref, b_ref, o_ref, xp_ref, *, t, on_grid):
    j = pl.program_id(1)
    zeros = jnp.zeros((CONV_PAD, LANES), F32)
    xp_ref[0:CONV_PAD, :] = zeros
    xp_ref[CONV_PAD + t:CONV_PAD + t + CONV_PAD, :] = zeros
    xp_ref[CONV_PAD:CONV_PAD + t, :] = x_ref[0]

    tt = min(CONV_TILE, t)
    dys = (-1, 0, 1) if on_grid else (0,)
    col = lax.broadcasted_iota(jnp.int32, (tt, LANES), 0) % GRID_W
    bias = b_ref[...]
    w = w_ref[...]

    def conv_tile(i):
        base = pl.multiple_of(i * tt, tt)
        acc_c = acc_m = acc_p = None
        for dy in dys:
            win = xp_ref[pl.ds(base + CONV_PAD + dy * GRID_W - SUBLANES, tt + 2 * SUBLANES), :]
            c = win[SUBLANES:SUBLANES + tt]
            m = pltpu.roll(win, 1, 0)[SUBLANES:SUBLANES + tt]
            p = pltpu.roll(win, tt + 2 * SUBLANES - 1, 0)[SUBLANES:SUBLANES + tt]
            wr = 3 * (dy + 1)
            tc, tm_, tp = c * w[wr + 1:wr + 2], m * w[wr:wr + 1], p * w[wr + 2:wr + 3]
            acc_c = tc if acc_c is None else acc_c + tc
            acc_m = tm_ if acc_m is None else acc_m + tm_
            acc_p = tp if acc_p is None else acc_p + tp
        if on_grid:
            acc_m = jnp.where(col != 0, acc_m, 0.0)
            acc_p = jnp.where(col != GRID_W - 1, acc_p, 0.0)
        return base, _silu(acc_c + acc_m + acc_p + bias)

    n_qk_blocks = 2 * DN_QK // LANES

    @pl.when(j < n_qk_blocks)
    def _():
        qscale = jnp.where(j < DN_QK // LANES, DN_DK ** -0.5, 1.0).astype(F32)

        def body(i, carry):
            base, u = conv_tile(i)
            nrm = lax.rsqrt(jnp.sum(u * u, axis=-1, keepdims=True) + EPS)
            o_ref[0, pl.ds(base, tt), :] = u * nrm * qscale
            return carry
        lax.fori_loop(0, t // tt, body, 0)

    @pl.when(j >= n_qk_blocks)
    def _():
        def body(i, carry):
            base, u = conv_tile(i)
            o_ref[0, pl.ds(base, tt), :] = u
            return carry
        lax.fori_loop(0, t // tt, body, 0)


def conv_features(conv_in, w9, bias, on_grid):
    bsz, t, ch = conv_in.shape
    blk = lambda b, j: (b, 0, j)
    return pl.pallas_call(
        functools.partial(_conv_feat_kernel, t=t, on_grid=on_grid),
        grid=(bsz, ch // LANES),
        in_specs=[
            pl.BlockSpec((1, t, LANES), blk),
            pl.BlockSpec((9, LANES), lambda b, j: (0, j)),
            pl.BlockSpec((1, LANES), lambda b, j: (0, j)),
        ],
        out_specs=pl.BlockSpec((1, t, LANES), blk),
        out_shape=jax.ShapeDtypeStruct((bsz, t, ch), F32),
        scratch_shapes=[pltpu.VMEM((t + 2 * CONV_PAD, LANES), F32)],
        compiler_params=_compiler_params(("parallel", "parallel")),
        name="conv_features",
    )(conv_in, w9, bias)


def _gates_kernel(s_ref, bias_ref, nega_ref, col_ref, grow_ref, arow_ref, *, tg):
    tok = lax.broadcasted_iota(jnp.int32, (LANES, LANES), 0)
    out = lax.broadcasted_iota(jnp.int32, (LANES, LANES), 1)
    same = (tok // CHUNK) == (out // CHUNK)
    m_fwd = jnp.where(same & (tok <= out), 1.0, 0.0)
    m_bwd = jnp.where(same & (tok >= out), 1.0, 0.0)
    m_all = jnp.where(same, 1.0, 0.0)
    bias = bias_ref[...]
    nega = nega_ref[...]

    def dir_cumsum(v, heads):
        f = _dot_exact01(v, m_fwd)
        b = _dot_exact01(v, m_bwd)
        is_fwd = lax.broadcasted_iota(jnp.int32, v.shape, 0) < heads
        return jnp.where(is_fwd, f, b), _dot_exact01(v, m_all)

    for s in range(tg // LANES):
        st = s_ref[0, s * LANES:(s + 1) * LANES, :].T
        beta = jax.nn.sigmoid(st[0:NDH])
        sp = _softplus(st[NDH:2 * NDH + NSH] + bias[NDH:2 * NDH + NSH])
        logg = sp[0:NDH] * nega[NDH:2 * NDH]
        dt = sp[NDH:]
        a = dt * nega[2 * NDH:2 * NDH + NSH]
        g_cs, g_tot = dir_cumsum(logg, DN_HEADS)
        a_cs, a_tot = dir_cumsum(a, SSM_HEADS)
        rows = jnp.concatenate([
            beta, g_cs, jnp.exp(g_cs), jnp.exp(g_tot - g_cs), jnp.exp(g_tot),
            dt, a_cs, jnp.exp(a_cs), dt * jnp.exp(a_tot - a_cs), jnp.exp(a_tot),
            jnp.zeros((LANES - 5 * NDH - 5 * NSH, LANES), F32)], axis=0)
        col_ref[0, s * LANES:(s + 1) * LANES, :] = rows.T
        for half in range(LANES // CHUNK):
            c = s * (LANES // CHUNK) + half
            lo = half * CHUNK
            grow_ref[0, c] = g_cs[:, lo:lo + CHUNK]
            for dg in range(2 * SSM_G):
                r0 = dg * SSM_HPG
                arow_ref[0, c, dg:dg + 1, :] = jnp.concatenate(
                    [a_cs[r0 + hh:r0 + hh + 1, lo:lo + CHUNK] for hh in range(SSM_HPG)], axis=1)


def gates(small, bias_rows, nega_rows, tg):
    bsz, t, _ = small.shape
    nc = t // CHUNK
    ncg = tg // CHUNK
    const = lambda b, i: (0, 0)
    return pl.pallas_call(
        functools.partial(_gates_kernel, tg=tg),
        grid=(bsz, t // tg),
        in_specs=[
            pl.BlockSpec((1, tg, LANES), lambda b, i: (b, i, 0)),
            pl.BlockSpec((LANES, LANES), const),
            pl.BlockSpec((LANES, LANES), const),
        ],
        out_specs=[
            pl.BlockSpec((1, tg, LANES), lambda b, i: (b, i, 0)),
            pl.BlockSpec((1, ncg, NDH, CHUNK), lambda b, i: (b, i, 0, 0)),
            pl.BlockSpec((1, ncg, 2 * SSM_G, SSM_GW), lambda b, i: (b, i, 0, 0)),
        ],
        out_shape=[
            jax.ShapeDtypeStruct((bsz, t, LANES), F32),
            jax.ShapeDtypeStruct((bsz, nc, NDH, CHUNK), F32),
            jax.ShapeDtypeStruct((bsz, nc, 2 * SSM_G, SSM_GW), F32),
        ],
        compiler_params=_compiler_params(("parallel", "parallel")),
        name="gates",
    )(small, bias_rows, nega_rows)


def _chunk_masks(rev):
    r = lax.broadcasted_iota(jnp.int32, (CHUNK, CHUNK), 0)
    c = lax.broadcasted_iota(jnp.int32, (CHUNK, CHUNK), 1)
    return ((r <= c), (r < c)) if rev else ((r >= c), (r > c))


def _unit_tri_inverse(ms):
    eye = (lax.broadcasted_iota(jnp.int32, (CHUNK, CHUNK), 0)
           == lax.broadcasted_iota(jnp.int32, (CHUNK, CHUNK), 1)).astype(F32)
    ps = [_dot(m, m) for m in ms]
    ts = [eye + m for m in ms]
    n_sq = int(np.log2(CHUNK)) - 1
    for _ in range(n_sq - 1):
        rs = [_dot(jnp.concatenate([t, p], axis=0), p) for t, p in zip(ts, ps)]
        ts = [t + r[:CHUNK] for t, r in zip(ts, rs)]
        ps = [r[CHUNK:] for r in rs]
    return [t + _dot(t, p) for t, p in zip(ts, ps)]


def _dn_kernel(qf_ref, kf_ref, vf_ref, cf_ref, rf_ref, qb_ref, kb_ref, vb_ref, cb_ref, rb_ref, s0_ref,
               of_ref, ob_ref, st_ref, s_scr, *, nct):
    j = pl.program_id(1)

    @pl.when(j == 0)
    def _():
        s_scr[...] = s0_ref[0]

    masks = (_chunk_masks(False), _chunk_masks(True))
    dir_refs = ((qf_ref, kf_ref, vf_ref, cf_ref, rf_ref, of_ref),
                (qb_ref, kb_ref, vb_ref, cb_ref, rb_ref, ob_ref))

    def body(c, carry):
        chains = []
        for d in range(2):
            cc = c if d == 0 else nct - 1 - c
            r0 = pl.multiple_of(cc * CHUNK, CHUNK)
            ct = dir_refs[d][3][0, pl.ds(r0, CHUNK), :]
            for h in range(DN_HEADS):
                chains.append((d, h, cc, r0, ct))

        def load(chain, which):
            d, h, _, r0, _ = chain
            return dir_refs[d][which][0, pl.ds(r0, CHUNK), h * DN_DK:(h + 1) * DN_DK]

        def col(chain, off, rows=slice(None)):
            d, h, _, _, ct = chain
            lane = off + d * DN_HEADS + h
            return ct[rows, lane:lane + 1]

        def decay_of(chain):
            d, h, cc, _, _ = chain
            incl = masks[d][0]
            g_row = dir_refs[d][4][0, cc, d * DN_HEADS + h:d * DN_HEADS + h + 1, :]
            return jnp.where(incl, jnp.exp(jnp.where(incl, col(chain, COL_G) - g_row, 0.0)), 0.0)

        kbeta = [load(ch, 1) * col(ch, COL_BETA) for ch in chains]
        a = [_dot_nt(jnp.concatenate([kb, load(ch, 0)], axis=0), load(ch, 1)) for ch, kb in zip(chains, kbeta)]
        decay = [decay_of(ch) for ch in chains]
        neg_l = [jnp.where(masks[ch[0]][1], -(ai[:CHUNK] * dc), 0.0) for ch, ai, dc in zip(chains, a, decay)]
        qk = [ai[CHUNK:] * dc for ai, dc in zip(a, decay)]
        t_inv = _unit_tri_inverse(neg_l)
        uw = [_dot(ti, jnp.concatenate([load(ch, 2) * col(ch, COL_BETA), kb * col(ch, COL_EG)], axis=1))
              for ch, ti, kb in zip(chains, t_inv, kbeta)]
        s = [s_scr[ch[0], ch[1]] for ch in chains]
        ws_qs = [_dot(jnp.concatenate([uwi[:, DN_DV:], load(ch, 0) * col(ch, COL_EG)], axis=0), si)
                 for ch, uwi, si in zip(chains, uw, s)]
        v_new = [uwi[:, :DN_DV] - wq[:CHUNK] for uwi, wq in zip(uw, ws_qs)]
        o = [wq[CHUNK:] + _dot(qki, vn) for wq, qki, vn in zip(ws_qs, qk, v_new)]
        s_new = [si * col(ch, COL_EGT, slice(0, 1)) + _dot_tn(load(ch, 1) * col(ch, COL_EGL), vn)
                 for ch, si, vn in zip(chains, s, v_new)]
        for ch, oi, sn in zip(chains, o, s_new):
            d, h, _, r0, _ = ch
            dir_refs[d][5][0, pl.ds(r0, CHUNK), h * DN_DV:(h + 1) * DN_DV] = oi
            s_scr[d, h] = sn
        return carry

    lax.fori_loop(0, nct, body, 0)

    @pl.when(j == pl.num_programs(1) - 1)
    def _():
        st_ref[0] = s_scr[...]


def delta_scan(feat, gcol, grow, s0, tt):
    bsz, t, _ = feat.shape
    nt = t // tt
    nct = tt // CHUNK
    fwd = lambda lane_blk: (lambda b, j: (b, j, lane_blk))
    bwd = lambda lane_blk: (lambda b, j: (b, nt - 1 - j, lane_blk))
    qkv_blk = (1, tt, DN_QK)

    def side(im):
        return [pl.BlockSpec(qkv_blk, im(OFF_Q // DN_QK)), pl.BlockSpec(qkv_blk, im(OFF_K // DN_QK)),
                pl.BlockSpec(qkv_blk, im(OFF_V // DN_QK)), pl.BlockSpec((1, tt, LANES), im(0))]

    row_spec = lambda rev: pl.BlockSpec((1, nct, NDH, CHUNK),
                                        (lambda b, j: (b, nt - 1 - j, 0, 0)) if rev else (lambda b, j: (b, j, 0, 0)))
    state_spec = pl.BlockSpec((1, 2, DN_HEADS, DN_DK, DN_DV), lambda b, j: (b, 0, 0, 0, 0))
    return pl.pallas_call(
        functools.partial(_dn_kernel, nct=nct),
        grid=(bsz, nt),
        in_specs=side(fwd) + [row_spec(False)] + side(bwd) + [row_spec(True)] + [state_spec],
        out_specs=[pl.BlockSpec((1, tt, DN_VW), fwd(0)), pl.BlockSpec((1, tt, DN_VW), bwd(0)), state_spec],
        out_shape=[
            jax.ShapeDtypeStruct((bsz, t, DN_VW), F32),
            jax.ShapeDtypeStruct((bsz, t, DN_VW), F32),
            jax.ShapeDtypeStruct((bsz, 2, DN_HEADS, DN_DK, DN_DV), F32),
        ],
        scratch_shapes=[pltpu.VMEM((2, DN_HEADS, DN_DK, DN_DV), F32)],
        compiler_params=_compiler_params(("parallel", "arbitrary")),
        name="delta_scan",
    )(feat, feat, feat, gcol, grow, feat, feat, feat, gcol, grow, s0)


def _group_lane_select(cols):
    shape = (cols[0].shape[0], SSM_GW)
    head = lax.broadcasted_iota(jnp.int32, shape, 1) // SSM_P
    out = jnp.broadcast_to(cols[SSM_HPG - 1], shape)
    for hh in range(SSM_HPG - 2, -1, -1):
        out = jnp.where(head == hh, jnp.broadcast_to(cols[hh], shape), out)
    return out


def _ssd_kernel(xf_ref, bf_ref, cf_ref, gf_ref, rf_ref, xb_ref, bb_ref, cb_ref, gb_ref, rb_ref, h0_ref,
                yf_ref, yb_ref, ht_ref, h_scr, *, nct):
    j = pl.program_id(1)

    @pl.when(j == 0)
    def _():
        h_scr[...] = h0_ref[0]

    row_i = lax.broadcasted_iota(jnp.int32, (CHUNK, SSM_GW), 0)
    col_j = lax.broadcasted_iota(jnp.int32, (CHUNK, SSM_GW), 1) % SSM_P
    incl_dir = (row_i >= col_j, row_i <= col_j)
    bd_rows = lax.broadcasted_iota(jnp.int32, (SSM_HPG * CHUNK, SSM_GW), 0) // CHUNK
    bd_cols = lax.broadcasted_iota(jnp.int32, (SSM_HPG * CHUNK, SSM_GW), 1) // SSM_P
    block_diag = bd_rows == bd_cols
    dir_refs = ((xf_ref, bf_ref, cf_ref, gf_ref, rf_ref, yf_ref),
                (xb_ref, bb_ref, cb_ref, gb_ref, rb_ref, yb_ref))

    def body(c, carry):
        chains = []
        for d in range(2):
            cc = c if d == 0 else nct - 1 - c
            r0 = pl.multiple_of(cc * CHUNK, CHUNK)
            ct = dir_refs[d][3][0, pl.ds(r0, CHUNK), :]
            for g in range(SSM_G):
                chains.append((d, g, cc, r0, ct))

        def load(chain, which, width):
            d, g, _, r0, _ = chain
            return dir_refs[d][which][0, pl.ds(r0, CHUNK), g * width:(g + 1) * width]

        def pick(chain, off, rows=slice(None)):
            d, g, _, _, ct = chain
            h0 = off + d * SSM_HEADS + g * SSM_HPG
            return _group_lane_select([ct[rows, h0 + hh:h0 + hh + 1] for hh in range(SSM_HPG)])

        def decay_of(chain):
            d, g, cc, _, _ = chain
            incl = incl_dir[d]
            a_row = dir_refs[d][4][0, cc, d * SSM_G + g:d * SSM_G + g + 1, :]
            return jnp.where(incl, jnp.exp(jnp.where(incl, pick(chain, COL_A) - a_row, 0.0)), 0.0)

        cb4 = [_dot_nt(load(ch, 2, SSM_N), jnp.concatenate([load(ch, 1, SSM_N)] * SSM_HPG, axis=0)) for ch in chains]
        hg = [h_scr[ch[0], ch[1]] for ch in chains]
        y_off = [_dot(load(ch, 2, SSM_N), h) for ch, h in zip(chains, hg)]
        h_in = [_dot_tn(load(ch, 1, SSM_N), load(ch, 0, SSM_GW) * pick(ch, COL_DTEAL)) for ch in chains]
        scores = [cb * decay_of(ch) for ch, cb in zip(chains, cb4)]
        xdt_bd = [jnp.where(block_diag,
                            jnp.concatenate([load(ch, 0, SSM_GW) * pick(ch, COL_DT)] * SSM_HPG, axis=0), 0.0)
                  for ch in chains]
        y_diag = [_dot(sc, xb) for sc, xb in zip(scores, xdt_bd)]
        for ch, yd, yo, h, hi in zip(chains, y_diag, y_off, hg, h_in):
            d, g, _, r0, _ = ch
            dir_refs[d][5][0, pl.ds(r0, CHUNK), g * SSM_GW:(g + 1) * SSM_GW] = yd + yo * pick(ch, COL_EA)
            h_scr[d, g] = h * pick(ch, COL_EAT, slice(0, 1)) + hi
        return carry

    lax.fori_loop(0, nct, body, 0)

    @pl.when(j == pl.num_programs(1) - 1)
    def _():
        ht_ref[0] = h_scr[...]


def ssd_scan(feat, gcol, arow, h0, tt):
    bsz, t, _ = feat.shape
    nt = t // tt
    nct = tt // CHUNK
    fwd = lambda lane_blk: (lambda b, j: (b, j, lane_blk))
    bwd = lambda lane_blk: (lambda b, j: (b, nt - 1 - j, lane_blk))

    def side(im):
        return [pl.BlockSpec((1, tt, SSM_DI), im(OFF_X // SSM_DI)),
                pl.BlockSpec((1, tt, SSM_BC), im(OFF_B // SSM_BC)),
                pl.BlockSpec((1, tt, SSM_BC), im(OFF_C // SSM_BC)),
                pl.BlockSpec((1, tt, LANES), im(0))]

    row_spec = lambda rev: pl.BlockSpec((1, nct, 2 * SSM_G, SSM_GW),
                                        (lambda b, j: (b, nt - 1 - j, 0, 0)) if rev else (lambda b, j: (b, j, 0, 0)))
    state_spec = pl.BlockSpec((1, 2, SSM_G, SSM_N, SSM_GW), lambda b, j: (b, 0, 0, 0, 0))
    return pl.pallas_call(
        functools.partial(_ssd_kernel, nct=nct),
        grid=(bsz, nt),
        in_specs=side(fwd) + [row_spec(False)] + side(bwd) + [row_spec(True)] + [state_spec],
        out_specs=[pl.BlockSpec((1, tt, SSM_DI), fwd(0)), pl.BlockSpec((1, tt, SSM_DI), bwd(0)), state_spec],
        out_shape=[
            jax.ShapeDtypeStruct((bsz, t, SSM_DI), F32),
            jax.ShapeDtypeStruct((bsz, t, SSM_DI), F32),
            jax.ShapeDtypeStruct((bsz, 2, SSM_G, SSM_N, SSM_GW), F32),
        ],
        scratch_shapes=[pltpu.VMEM((2, SSM_G, SSM_N, SSM_GW), F32)],
        compiler_params=_compiler_params(("parallel", "arbitrary")),
        name="ssd_scan",
    )(feat, feat, feat, gcol, arow, feat, feat, feat, gcol, arow, h0)


def _merge_out_kernel(of_ref, ob_ref, yf_ref, yb_ref, xs_ref, z_ref, dng_ref, dskip_ref, ssg_ref,
                      w_ref, g_ref, gate_ref, res_ref, o_ref):
    o = of_ref[0] + ob_ref[0]
    z = z_ref[0]
    parts = []
    for h in range(DN_HEADS):
        lanes = slice(h * DN_DV, (h + 1) * DN_DV)
        parts.append(_rms(o[:, lanes]) * dng_ref[...] * _silu(z[:, lanes]))
    y = yf_ref[0] + yb_ref[0] + dskip_ref[...] * xs_ref[0]
    yz = y * _silu(z[:, DN_VW:])
    gw = SSM_DI // SSM_G
    for g in range(SSM_G):
        lanes = slice(g * gw, (g + 1) * gw)
        parts.append(_rms(yz[:, lanes]) * ssg_ref[:, lanes])
    mixed = jnp.concatenate(parts, axis=1).astype(BF16)
    m = jnp.dot(mixed, w_ref[...], preferred_element_type=F32)
    o_ref[0] = res_ref[0] + gate_ref[0] * (_rms(m) * g_ref[...])


def merge_out(o_f, o_b, y_f, y_b, feat, z, dn_norm_g, dskip, ssm_norm_g, w_out, g, gate, res, tm):
    bsz, t, d = res.shape
    row = lambda b, i: (b, i, 0)
    const = lambda b, i: (0, 0)
    half = lambda: pl.BlockSpec((1, tm, DN_VW), row)
    return pl.pallas_call(
        _merge_out_kernel,
        grid=(bsz, t // tm),
        in_specs=[
            half(), half(), half(), half(),
            pl.BlockSpec((1, tm, SSM_DI), lambda b, i: (b, i, OFF_X // SSM_DI)),
            pl.BlockSpec((1, tm, D_Z), row),
            pl.BlockSpec((1, DN_DV), const),
            pl.BlockSpec((1, SSM_DI), const),
            pl.BlockSpec((1, SSM_DI), const),
            pl.BlockSpec((D_MIX, d), const),
            pl.BlockSpec((1, d), const),
            pl.BlockSpec((1, 1, d), lambda b, i: (b, 0, 0)),
            pl.BlockSpec((1, tm, d), row),
        ],
        out_specs=pl.BlockSpec((1, tm, d), row),
        out_shape=jax.ShapeDtypeStruct((bsz, t, d), F32),
        compiler_params=_compiler_params(("parallel", "parallel")),
        name="merge_out",
    )(o_f, o_b, y_f, y_b, feat, z, dn_norm_g.reshape(1, DN_DV), dskip, ssm_norm_g.reshape(1, SSM_DI),
      w_out, g.reshape(1, d), gate.reshape(bsz, 1, d), res)


MOE_TILE = 256
SLOT_BLOCK = LANES
SLOT_WINDOW = 2 * SLOT_BLOCK
SPARE_BLOCKS = 2 * (SLOT_WINDOW // SLOT_BLOCK) - 1
CNT_LANES = 32


def _router_kernel(x_ref, g_ref, shift_ref, scale_ref, wr_ref, ht_ref, aff_ref):
    h = _rms(x_ref[0]) * g_ref[...]
    h = h * (1.0 + scale_ref[0]) + shift_ref[0]
    ht = h.T.astype(BF16)
    logits = jnp.dot(wr_ref[...], ht, preferred_element_type=F32)
    ex = jnp.exp(logits - jnp.max(logits, axis=0, keepdims=True))
    ht_ref[0, 0] = ht
    aff_ref[0, 0] = ex / jnp.sum(ex, axis=0, keepdims=True)


def router(x, g, shift, scale, wr_t):
    bsz, t, d = x.shape
    nt = t // MOE_TILE
    const = lambda b, j: (0, 0)
    per_b = lambda b, j: (b, 0, 0)
    return pl.pallas_call(
        _router_kernel,
        grid=(bsz, nt),
        in_specs=[
            pl.BlockSpec((1, MOE_TILE, d), lambda b, j: (b, j, 0)),
            pl.BlockSpec((1, d), const),
            pl.BlockSpec((1, 1, d), per_b),
            pl.BlockSpec((1, 1, d), per_b),
            pl.BlockSpec((N_EXPERTS, d), const),
        ],
        out_specs=[
            pl.BlockSpec((1, 1, d, MOE_TILE), lambda b, j: (b, j, 0, 0)),
            pl.BlockSpec((1, 1, N_EXPERTS, MOE_TILE), lambda b, j: (b, j, 0, 0)),
        ],
        out_shape=[
            jax.ShapeDtypeStruct((bsz, nt, d, MOE_TILE), BF16),
            jax.ShapeDtypeStruct((bsz, nt, N_EXPERTS, MOE_TILE), F32),
        ],
        compiler_params=_compiler_params(("parallel", "parallel")),
        name="router",
    )(x, g.reshape(1, d), shift.reshape(bsz, 1, d), scale.reshape(bsz, 1, d), wr_t)


def _lane_prefix(mask, upper):
    carry = jnp.zeros((mask.shape[0], 1), F32)
    blocks = []
    for m in range(mask.shape[1] // LANES):
        p = jnp.dot(mask[:, m * LANES:(m + 1) * LANES].astype(BF16), upper, preferred_element_type=F32) + carry
        carry = p[:, LANES - 1:LANES]
        blocks.append(p)
    return jnp.concatenate(blocks, axis=1)


def _select_kernel(aff_ref, pos_ref, cnt_ref, *, nt, cap):
    aff = jnp.concatenate([aff_ref[0, j] for j in range(nt)], axis=1)
    bits = pltpu.bitcast(aff, jnp.int32)

    def bisect(i, thr):
        cand = thr | lax.shift_left(jnp.int32(1), jnp.int32(30) - i)
        n_ge = jnp.sum(jnp.where(bits >= cand, 1.0, 0.0), axis=1, keepdims=True)
        return jnp.where(n_ge >= cap, cand, thr)

    thr = lax.fori_loop(0, 31, bisect, jnp.zeros((N_EXPERTS, 1), jnp.int32))
    above = bits > thr
    tied = bits == thr
    room = cap - jnp.sum(jnp.where(above, 1.0, 0.0), axis=1, keepdims=True)
    src = lax.broadcasted_iota(jnp.int32, (LANES, LANES), 0)
    dst = lax.broadcasted_iota(jnp.int32, (LANES, LANES), 1)
    upper = jnp.where(src <= dst, 1.0, 0.0).astype(BF16)
    tie_rank = _lane_prefix(jnp.where(tied, 1.0, 0.0), upper)
    sel = above | (tied & (tie_rank <= room))
    selc = jnp.where(sel, 1.0, 0.0)
    slot = jnp.where(sel, _lane_prefix(selc, upper) - 1.0, -1.0)
    lane = lax.broadcasted_iota(jnp.int32, (N_EXPERTS, LANES), 1)
    before = jnp.zeros((N_EXPERTS, 1), F32)
    counts = jnp.zeros((N_EXPERTS, LANES), F32)
    for j in range(nt):
        pos_ref[0, j] = slot[:, j * MOE_TILE:(j + 1) * MOE_TILE]
        before = before + jnp.sum(selc[:, j * MOE_TILE:(j + 1) * MOE_TILE], axis=1, keepdims=True)
        counts = jnp.where(lane == j + 1, before, counts)
    cnt_ref[0] = counts.astype(jnp.int32)


def select_slots(aff, cap):
    bsz, nt, ne, tile = aff.shape
    blk = lambda b: (b, 0, 0, 0)
    return pl.pallas_call(
        functools.partial(_select_kernel, nt=nt, cap=cap),
        grid=(bsz,),
        in_specs=[pl.BlockSpec((1, nt, ne, tile), blk)],
        out_specs=[pl.BlockSpec((1, nt, ne, tile), blk), pl.BlockSpec((1, ne, LANES), lambda b: (b, 0, 0))],
        out_shape=[jax.ShapeDtypeStruct((bsz, nt, ne, tile), F32),
                   jax.ShapeDtypeStruct((bsz, ne, LANES), jnp.int32)],
        compiler_params=_compiler_params(("parallel",)),
        name="select_slots",
    )(aff)


def _moe_ffn_kernel(cnt_ref, ht_ref, pos_ref, aff_ref, wg_ref, wu_ref, wd_ref, y_ref, acc_scr, og_scr,
                    *, nt, ns):
    e = pl.program_id(1)
    row = pl.program_id(0) * N_EXPERTS + e
    d = ht_ref.shape[2]

    @pl.when(e == 0)
    def _():
        y_ref[...] = jnp.zeros_like(y_ref)
        for k in range(ns, ns + SPARE_BLOCKS):
            acc_scr[k] = jnp.zeros((d, SLOT_BLOCK), F32)
            og_scr[k] = jnp.zeros((d, SLOT_BLOCK), BF16)

    for k in range(ns):
        acc_scr[k] = jnp.zeros((d, SLOT_BLOCK), F32)

    win_iota = lax.broadcasted_iota(jnp.int32, (SLOT_WINDOW, MOE_TILE), 0).astype(F32)

    def one_hot(j, w):
        pos = pos_ref[0, j, pl.ds(e, 1), :]
        return jnp.where(pos - (w * SLOT_BLOCK).astype(F32) == win_iota, 1.0, 0.0).astype(BF16)

    def first_block(j):
        return cnt_ref[row, j] // SLOT_BLOCK

    def spills_past_window(j):
        return (cnt_ref[row, j + 1] - 1) // SLOT_BLOCK >= first_block(j) + SLOT_WINDOW // SLOT_BLOCK

    def gather_window(j, w):
        r = lax.dot_general(ht_ref[0, j], one_hot(j, w), (((1,), (1,)), ((), ())), preferred_element_type=F32)
        acc_scr[w] += r[:, :SLOT_BLOCK]
        acc_scr[w + 1] += r[:, SLOT_BLOCK:]

    def gather_tile(j, carry):
        gather_window(j, first_block(j))
        return carry

    def gather_tile_rest(j, carry):
        @pl.when(spills_past_window(j))
        def _():
            gather_window(j, first_block(j) + SLOT_WINDOW // SLOT_BLOCK)
        return carry

    lax.fori_loop(0, nt, gather_tile, 0, unroll=2 if nt % 2 == 0 else 1)
    lax.fori_loop(0, nt, gather_tile_rest, 0)

    xs = jnp.concatenate([acc_scr[k].astype(BF16) for k in range(ns)], axis=1)
    a = jnp.dot(wg_ref[0], xs, preferred_element_type=F32)
    u = jnp.dot(wu_ref[0], xs, preferred_element_type=F32)
    hid = (_silu(a) * u).astype(BF16)
    og = jnp.dot(wd_ref[0], hid, preferred_element_type=F32)
    for k in range(ns):
        og_scr[k] = og[:, k * SLOT_BLOCK:(k + 1) * SLOT_BLOCK].astype(BF16)

    def scatter_window(j, w):
        og2 = jnp.concatenate([og_scr[w], og_scr[w + 1]], axis=1)
        gate = aff_ref[0, j, pl.ds(e, 1), :]
        y_ref[0, j] += jnp.dot(og2, one_hot(j, w), preferred_element_type=F32) * gate

    def scatter_tile(j, carry):
        scatter_window(j, first_block(j))
        return carry

    def scatter_tile_rest(j, carry):
        @pl.when(spills_past_window(j))
        def _():
            scatter_window(j, first_block(j) + SLOT_WINDOW // SLOT_BLOCK)
        return carry

    lax.fori_loop(0, nt, scatter_tile, 0, unroll=2 if nt % 2 == 0 else 1)
    lax.fori_loop(0, nt, scatter_tile_rest, 0)


def moe_ffn(cnt, ht, pos, aff, wg_t, wu_t, wd_t, cap):
    bsz, nt, d, tile = ht.shape
    ne, f, _ = wg_t.shape
    ns = pl.cdiv(cap, SLOT_BLOCK)
    per_b = lambda b, e, c: (b, 0, 0, 0)
    per_e = lambda b, e, c: (e, 0, 0)
    grid_spec = pltpu.PrefetchScalarGridSpec(
        num_scalar_prefetch=1,
        grid=(bsz, ne),
        in_specs=[
            pl.BlockSpec((1, nt, d, tile), per_b, pipeline_mode=pl.Buffered(1)),
            pl.BlockSpec((1, nt, ne, tile), per_b),
            pl.BlockSpec((1, nt, ne, tile), per_b),
            pl.BlockSpec((1, f, d), per_e),
            pl.BlockSpec((1, f, d), per_e),
            pl.BlockSpec((1, d, f), per_e),
        ],
        out_specs=pl.BlockSpec((1, nt, d, tile), per_b, pipeline_mode=pl.Buffered(1)),
        scratch_shapes=[pltpu.VMEM((ns + SPARE_BLOCKS, d, SLOT_BLOCK), F32),
                        pltpu.VMEM((ns + SPARE_BLOCKS, d, SLOT_BLOCK), BF16)],
    )
    return pl.pallas_call(
        functools.partial(_moe_ffn_kernel, nt=nt, ns=ns),
        grid_spec=grid_spec,
        out_shape=jax.ShapeDtypeStruct((bsz, nt, d, tile), F32),
        compiler_params=_compiler_params(("parallel", "arbitrary")),
        name="moe_ffn",
    )(cnt, ht, pos, aff, wg_t, wu_t, wd_t)


def _moe_out_kernel(y_ref, g_ref, gate_ref, res_ref, o_ref):
    o_ref[0] = res_ref[0] + gate_ref[0] * (_rms(y_ref[0, 0].T) * g_ref[...])


def moe_out(y_t, g, gate, res):
    bsz, t, d = res.shape
    nt = t // MOE_TILE
    return pl.pallas_call(
        _moe_out_kernel,
        grid=(bsz, nt),
        in_specs=[
            pl.BlockSpec((1, 1, d, MOE_TILE), lambda b, j: (b, j, 0, 0)),
            pl.BlockSpec((1, d), lambda b, j: (0, 0)),
            pl.BlockSpec((1, 1, d), lambda b, j: (b, 0, 0)),
            pl.BlockSpec((1, MOE_TILE, d), lambda b, j: (b, j, 0)),
        ],
        out_specs=pl.BlockSpec((1, MOE_TILE, d), lambda b, j: (b, j, 0)),
        out_shape=jax.ShapeDtypeStruct((bsz, t, d), F32),
        compiler_params=_compiler_params(("parallel", "parallel")),
        name="moe_out",
    )(y_t, g.reshape(1, d), gate.reshape(bsz, 1, d), res)


def ec_moe_residual(x, g_in, shift, scale, wr_t, wg_t, wu_t, wd_t, g_out, gate):
    bsz, t, _ = x.shape
    cap = EC_CAPACITY_FACTOR * t // N_EXPERTS
    ht, aff = router(x, g_in, shift, scale, wr_t)
    pos, counts = select_slots(aff, cap)
    cnt = counts[:, :, :CNT_LANES].reshape(bsz * N_EXPERTS, CNT_LANES)
    y_t = moe_ffn(cnt, ht, pos, aff, wg_t, wu_t, wd_t, cap)
    return moe_out(y_t, g_out, gate, x)


def _token_tile(t):
    return min(512, t)


def mixer_stream(x, g0, shift, scale, wc, wz, ws, w9, conv_b, bias_rows, nega_rows, dn_state, ssm_state, on_grid):
    t = x.shape[1]
    tm = _token_tile(t)
    conv_in, z, small = in_proj(x, g0, shift, scale, wc, wz, ws, tm)
    feat = conv_features(conv_in, w9, conv_b, on_grid)
    gcol, grow, arow = gates(small, bias_rows, nega_rows, tm)
    o_f, o_b, dn_state = delta_scan(feat, gcol, grow, dn_state, tm)
    y_f, y_b, ssm_state = ssd_scan(feat, gcol, arow, ssm_state, tm)
    return (o_f, o_b, y_f, y_b, feat, z), dn_state, ssm_state


def _gate_param_rows(dn_bias, dn_a_log, ssm_bias, ssm_a_log):
    zeros = jnp.zeros((NDH,), F32)
    bias = jnp.concatenate([zeros, dn_bias.reshape(-1), ssm_bias.reshape(-1)])
    nega = jnp.concatenate([zeros, -jnp.exp(dn_a_log.reshape(-1)), -jnp.exp(ssm_a_log.reshape(-1))])
    pad = LANES - bias.shape[0]
    expand = lambda v: jnp.broadcast_to(jnp.pad(v, (0, pad))[:, None], (LANES, LANES))
    return expand(bias), expand(nega)


def kernel(x, c, ctx, c_ctx, ada_w, ada_b, norm_g, w_in, conv_w, conv_b, dn_A_log, dn_dt_bias,
           dn_norm_g, ssm_A_log, ssm_dt_bias, ssm_D, ssm_norm_g, w_out, router_w,
           exp_w_gate, exp_w_up, exp_w_down):
    bsz = x.shape[0]
    s_lat = jax.nn.silu(c)
    s_ctx = jax.nn.silu(c_ctx)
    for l in range(DEPTH):
        last = l == DEPTH - 1
        mod_lat = jnp.split(s_lat @ ada_w[l] + ada_b[l], 6, axis=-1)
        mod_ctx_row = s_ctx @ ada_w[l] + ada_b[l]
        mod_ctx = [jnp.broadcast_to(m[None, :], (bsz, D_MODEL)) for m in jnp.split(mod_ctx_row, 6)]

        w_in_l = w_in[l]
        wc = w_in_l[:, :CONV_CH].astype(BF16)
        wz = w_in_l[:, CONV_CH:CONV_CH + D_Z].astype(BF16)
        ws = jnp.pad(w_in_l[:, CONV_CH + D_Z:], ((0, 0), (0, LANES - N_GATE_COLS))).astype(BF16)
        w9 = conv_w[l].reshape(CONV_CH, CONV_K * CONV_K).T
        cb = conv_b[l].reshape(1, CONV_CH)
        bias_rows, nega_rows = _gate_param_rows(dn_dt_bias[l], dn_A_log[l], ssm_dt_bias[l], ssm_A_log[l])
        dskip = jnp.repeat(ssm_D[l], SSM_P).reshape(1, SSM_DI)
        w_out_l = w_out[l].astype(BF16)
        to_t = lambda w: jnp.swapaxes(w, -1, -2).astype(BF16)
        moe_w = (to_t(router_w[l]), to_t(exp_w_gate[l]), to_t(exp_w_up[l]), to_t(exp_w_down[l]), norm_g[l, 3])

        dn0 = jnp.zeros((bsz, 2, DN_HEADS, DN_DK, DN_DV), F32)
        ssm0 = jnp.zeros((bsz, 2, SSM_G, SSM_N, SSM_GW), F32)
        shared = (wc, wz, ws, w9, cb, bias_rows, nega_rows)
        mix_ctx, dn_c, ssm_c = mixer_stream(ctx, norm_g[l, 0], mod_ctx[0], mod_ctx[1], *shared, dn0, ssm0, False)
        mix_lat, _, _ = mixer_stream(x, norm_g[l, 0], mod_lat[0], mod_lat[1], *shared, dn_c, ssm_c, True)
        merge_w = (dn_norm_g[l], dskip, ssm_norm_g[l], w_out_l, norm_g[l, 1])
        x = merge_out(*mix_lat, *merge_w, mod_lat[2], x, _token_tile(x.shape[1]))

        x = ec_moe_residual(x, norm_g[l, 2], mod_lat[3], mod_lat[4], *moe_w, mod_lat[5])

        if not last:
            ctx = merge_out(*mix_ctx, *merge_w, mod_ctx[2], ctx, _token_tile(ctx.shape[1]))
            ctx = ec_moe_residual(ctx, norm_g[l, 2], mod_ctx[3], mod_ctx[4], *moe_w, mod_ctx[5])
    return x
```

```python
import functools

import jax
import jax.numpy as jnp
import numpy as np
from jax import lax
from jax.experimental import pallas as pl
from jax.experimental.pallas import tpu as pltpu

D_MODEL = 1024
DEPTH = 4
GRID_W = 64
DN_HEADS = 4
DN_DK = 128
DN_DV = 128
SSM_HEADS = 8
SSM_P = 64
SSM_N = 128
SSM_G = 2
CHUNK = 64
CONV_K = 3
N_EXPERTS = 16
EC_CAPACITY_FACTOR = 2
D_EXPERT = 512
EPS = 1e-6

DN_QK = DN_HEADS * DN_DK
DN_VW = DN_HEADS * DN_DV
SSM_DI = SSM_HEADS * SSM_P
SSM_BC = SSM_G * SSM_N
SSM_HPG = SSM_HEADS // SSM_G
SSM_GW = SSM_HPG * SSM_P
D_MIX = DN_VW + SSM_DI
CONV_SPLITS = (DN_QK, DN_QK, DN_VW, SSM_DI, SSM_BC, SSM_BC)
CONV_CH = sum(CONV_SPLITS)
D_Z = DN_VW + SSM_DI
N_GATE_COLS = 2 * DN_HEADS + 2 * DN_HEADS + 2 * SSM_HEADS
D_IN_PROJ = CONV_CH + D_Z + N_GATE_COLS

LANES = 128
SUBLANES = 8
VMEM_LIMIT_BYTES = 56 * 1024 * 1024

OFF_Q, OFF_K, OFF_V = 0, DN_QK, 2 * DN_QK
OFF_X = 2 * DN_QK + DN_VW
OFF_B = OFF_X + SSM_DI
OFF_C = OFF_B + SSM_BC

NDH = 2 * DN_HEADS
NSH = 2 * SSM_HEADS
COL_BETA, COL_G, COL_EG, COL_EGL, COL_EGT = (i * NDH for i in range(5))
COL_DT, COL_A, COL_EA, COL_DTEAL, COL_EAT = (5 * NDH + i * NSH for i in range(5))

F32 = jnp.float32
BF16 = jnp.bfloat16


def _compiler_params(semantics):
    return pltpu.CompilerParams(dimension_semantics=semantics, vmem_limit_bytes=VMEM_LIMIT_BYTES)


def _rms(x):
    return x * lax.rsqrt(jnp.mean(x * x, axis=-1, keepdims=True) + EPS)


def _silu(x):
    return x * jax.nn.sigmoid(x)


def _softplus(x):
    return jnp.maximum(x, 0.0) + jnp.log(1.0 + jnp.exp(-jnp.abs(x)))


def _dot(a, b):
    return jnp.dot(a.astype(BF16), b.astype(BF16), preferred_element_type=F32)


def _dot_nt(a, b):
    return lax.dot_general(a.astype(BF16), b.astype(BF16), (((1,), (1,)), ((), ())),
                           preferred_element_type=F32)


def _dot_tn(a, b):
    return lax.dot_general(a.astype(BF16), b.astype(BF16), (((0,), (0,)), ((), ())),
                           preferred_element_type=F32)


def _dot_exact01(a, m01):
    a1 = a.astype(BF16)
    r1 = a - a1.astype(F32)
    a2 = r1.astype(BF16)
    a3 = (r1 - a2.astype(F32)).astype(BF16)
    m = m01.astype(BF16)
    out = jnp.dot(a3, m, preferred_element_type=F32)
    out = out + jnp.dot(a2, m, preferred_element_type=F32)
    return out + jnp.dot(a1, m, preferred_element_type=F32)


def _in_proj_kernel(x_ref, g_ref, shift_ref, scale_ref, wc_ref, wz_ref, ws_ref, oc_ref, oz_ref, os_ref):
    h = _rms(x_ref[0]) * g_ref[...]
    h = (h * (1.0 + scale_ref[0]) + shift_ref[0]).astype(BF16)
    oc_ref[0] = jnp.dot(h, wc_ref[...], preferred_element_type=F32)
    oz_ref[0] = jnp.dot(h, wz_ref[...], preferred_element_type=F32)
    os_ref[0] = jnp.dot(h, ws_ref[...], preferred_element_type=F32)


def in_proj(x, g, shift, scale, wc, wz, ws, tm):
    bsz, t, d = x.shape
    row = lambda b, i: (b, i, 0)
    const = lambda b, i: (0, 0)
    per_b = lambda b, i: (b, 0, 0)
    return pl.pallas_call(
        _in_proj_kernel,
        grid=(bsz, t // tm),
        in_specs=[
            pl.BlockSpec((1, tm, d), row),
            pl.BlockSpec((1, d), const),
            pl.BlockSpec((1, 1, d), per_b),
            pl.BlockSpec((1, 1, d), per_b),
            pl.BlockSpec((d, CONV_CH), const),
            pl.BlockSpec((d, D_Z), const),
            pl.BlockSpec((d, LANES), const),
        ],
        out_specs=[
            pl.BlockSpec((1, tm, CONV_CH), row),
            pl.BlockSpec((1, tm, D_Z), row),
            pl.BlockSpec((1, tm, LANES), row),
        ],
        out_shape=[
            jax.ShapeDtypeStruct((bsz, t, CONV_CH), F32),
            jax.ShapeDtypeStruct((bsz, t, D_Z), F32),
            jax.ShapeDtypeStruct((bsz, t, LANES), F32),
        ],
        compiler_params=_compiler_params(("parallel", "parallel")),
        name="in_proj",
    )(x, g.reshape(1, d), shift.reshape(bsz, 1, d), scale.reshape(bsz, 1, d), wc, wz, ws)


CONV_PAD = GRID_W + SUBLANES
CONV_TILE = 256


def _conv_feat_kernel(x_ref, w_ref, b_ref, o_ref, xp_ref, *, t, on_grid):
    j = pl.program_id(1)
    zeros = jnp.zeros((CONV_PAD, LANES), F32)
    xp_ref[0:CONV_PAD, :] = zeros
    xp_ref[CONV_PAD + t:CONV_PAD + t + CONV_PAD, :] = zeros
    xp_ref[CONV_PAD:CONV_PAD + t, :] = x_ref[0]

    tt = min(CONV_TILE, t)
    dys = (-1, 0, 1) if on_grid else (0,)
    col = lax.broadcasted_iota(jnp.int32, (tt, LANES), 0) % GRID_W
    bias = b_ref[...]
    w = w_ref[...]

    def conv_tile(i):
        base = pl.multiple_of(i * tt, tt)
        acc_c = acc_m = acc_p = None
        for dy in dys:
            win = xp_ref[pl.ds(base + CONV_PAD + dy * GRID_W - SUBLANES, tt + 2 * SUBLANES), :]
            c = win[SUBLANES:SUBLANES + tt]
            m = pltpu.roll(win, 1, 0)[SUBLANES:SUBLANES + tt]
            p = pltpu.roll(win, tt + 2 * SUBLANES - 1, 0)[SUBLANES:SUBLANES + tt]
            wr = 3 * (dy + 1)
            tc, tm_, tp = c * w[wr + 1:wr + 2], m * w[wr:wr + 1], p * w[wr + 2:wr + 3]
            acc_c = tc if acc_c is None else acc_c + tc
            acc_m = tm_ if acc_m is None else acc_m + tm_
            acc_p = tp if acc_p is None else acc_p + tp
        if on_grid:
            acc_m = jnp.where(col != 0, acc_m, 0.0)
            acc_p = jnp.where(col != GRID_W - 1, acc_p, 0.0)
        return base, _silu(acc_c + acc_m + acc_p + bias)

    n_qk_blocks = 2 * DN_QK // LANES

    @pl.when(j < n_qk_blocks)
    def _():
        qscale = jnp.where(j < DN_QK // LANES, DN_DK ** -0.5, 1.0).astype(F32)

        def body(i, carry):
            base, u = conv_tile(i)
            nrm = lax.rsqrt(jnp.sum(u * u, axis=-1, keepdims=True) + EPS)
            o_ref[0, pl.ds(base, tt), :] = u * nrm * qscale
            return carry
        lax.fori_loop(0, t // tt, body, 0)

    @pl.when(j >= n_qk_blocks)
    def _():
        def body(i, carry):
            base, u = conv_tile(i)
            o_ref[0, pl.ds(base, tt), :] = u
            return carry
        lax.fori_loop(0, t // tt, body, 0)


def conv_features(conv_in, w9, bias, on_grid):
    bsz, t, ch = conv_in.shape
    blk = lambda b, j: (b, 0, j)
    return pl.pallas_call(
        functools.partial(_conv_feat_kernel, t=t, on_grid=on_grid),
        grid=(bsz, ch // LANES),
        in_specs=[
            pl.BlockSpec((1, t, LANES), blk),
            pl.BlockSpec((9, LANES), lambda b, j: (0, j)),
            pl.BlockSpec((1, LANES), lambda b, j: (0, j)),
        ],
        out_specs=pl.BlockSpec((1, t, LANES), blk),
        out_shape=jax.ShapeDtypeStruct((bsz, t, ch), F32),
        scratch_shapes=[pltpu.VMEM((t + 2 * CONV_PAD, LANES), F32)],
        compiler_params=_compiler_params(("parallel", "parallel")),
        name="conv_features",
    )(conv_in, w9, bias)


def _gates_kernel(s_ref, bias_ref, nega_ref, col_ref, grow_ref, arow_ref, *, tg):
    tok = lax.broadcasted_iota(jnp.int32, (LANES, LANES), 0)
    out = lax.broadcasted_iota(jnp.int32, (LANES, LANES), 1)
    same = (tok // CHUNK) == (out // CHUNK)
    m_fwd = jnp.where(same & (tok <= out), 1.0, 0.0)
    m_bwd = jnp.where(same & (tok >= out), 1.0, 0.0)
    m_all = jnp.where(same, 1.0, 0.0)
    bias = bias_ref[...]
    nega = nega_ref[...]

    def dir_cumsum(v, heads):
        f = _dot_exact01(v, m_fwd)
        b = _dot_exact01(v, m_bwd)
        is_fwd = lax.broadcasted_iota(jnp.int32, v.shape, 0) < heads
        return jnp.where(is_fwd, f, b), _dot_exact01(v, m_all)

    for s in range(tg // LANES):
        st = s_ref[0, s * LANES:(s + 1) * LANES, :].T
        beta = jax.nn.sigmoid(st[0:NDH])
        sp = _softplus(st[NDH:2 * NDH + NSH] + bias[NDH:2 * NDH + NSH])
        logg = sp[0:NDH] * nega[NDH:2 * NDH]
        dt = sp[NDH:]
        a = dt * nega[2 * NDH:2 * NDH + NSH]
        g_cs, g_tot = dir_cumsum(logg, DN_HEADS)
        a_cs, a_tot = dir_cumsum(a, SSM_HEADS)
        rows = jnp.concatenate([
            beta, g_cs, jnp.exp(g_cs), jnp.exp(g_tot - g_cs), jnp.exp(g_tot),
            dt, a_cs, jnp.exp(a_cs), dt * jnp.exp(a_tot - a_cs), jnp.exp(a_tot),
            jnp.zeros((LANES - 5 * NDH - 5 * NSH, LANES), F32)], axis=0)
        col_ref[0, s * LANES:(s + 1) * LANES, :] = rows.T
        for half in range(LANES // CHUNK):
            c = s * (LANES // CHUNK) + half
            lo = half * CHUNK
            grow_ref[0, c] = g_cs[:, lo:lo + CHUNK]
            for dg in range(2 * SSM_G):
                r0 = dg * SSM_HPG
                arow_ref[0, c, dg:dg + 1, :] = jnp.concatenate(
                    [a_cs[r0 + hh:r0 + hh + 1, lo:lo + CHUNK] for hh in range(SSM_HPG)], axis=1)


def gates(small, bias_rows, nega_rows, tg):
    bsz, t, _ = small.shape
    nc = t // CHUNK
    ncg = tg // CHUNK
    const = lambda b, i: (0, 0)
    return pl.pallas_call(
        functools.partial(_gates_kernel, tg=tg),
        grid=(bsz, t // tg),
        in_specs=[
            pl.BlockSpec((1, tg, LANES), lambda b, i: (b, i, 0)),
            pl.BlockSpec((LANES, LANES), const),
            pl.BlockSpec((LANES, LANES), const),
        ],
        out_specs=[
            pl.BlockSpec((1, tg, LANES), lambda b, i: (b, i, 0)),
            pl.BlockSpec((1, ncg, NDH, CHUNK), lambda b, i: (b, i, 0, 0)),
            pl.BlockSpec((1, ncg, 2 * SSM_G, SSM_GW), lambda b, i: (b, i, 0, 0)),
        ],
        out_shape=[
            jax.ShapeDtypeStruct((bsz, t, LANES), F32),
            jax.ShapeDtypeStruct((bsz, nc, NDH, CHUNK), F32),
            jax.ShapeDtypeStruct((bsz, nc, 2 * SSM_G, SSM_GW), F32),
        ],
        compiler_params=_compiler_params(("parallel", "parallel")),
        name="gates",
    )(small, bias_rows, nega_rows)


SCAN_SAMPLES_PER_STEP = 4
SCAN_TILE = 256


def _chunk_masks(rev):
    r = lax.broadcasted_iota(jnp.int32, (CHUNK, CHUNK), 0)
    c = lax.broadcasted_iota(jnp.int32, (CHUNK, CHUNK), 1)
    return ((r <= c), (r < c)) if rev else ((r >= c), (r > c))


def _unit_tri_inverse(ms):
    eye = (lax.broadcasted_iota(jnp.int32, (CHUNK, CHUNK), 0)
           == lax.broadcasted_iota(jnp.int32, (CHUNK, CHUNK), 1)).astype(F32)
    ps = [_dot(m, m) for m in ms]
    ts = [eye + m for m in ms]
    n_sq = int(np.log2(CHUNK)) - 1
    for _ in range(n_sq - 1):
        rs = [_dot(jnp.concatenate([t, p], axis=0), p) for t, p in zip(ts, ps)]
        ts = [t + r[:CHUNK] for t, r in zip(ts, rs)]
        ps = [r[CHUNK:] for r in rs]
    return [t + _dot(t, p) for t, p in zip(ts, ps)]


def _dn_kernel(qf_ref, kf_ref, vf_ref, cf_ref, rf_ref, qb_ref, kb_ref, vb_ref, cb_ref, rb_ref, s0_ref,
               of_ref, ob_ref, st_ref, s_scr, *, nct, nb):
    j = pl.program_id(1)

    @pl.when(j == 0)
    def _():
        s_scr[...] = s0_ref[...]

    masks = (_chunk_masks(False), _chunk_masks(True))
    dir_refs = ((qf_ref, kf_ref, vf_ref, cf_ref, rf_ref, of_ref),
                (qb_ref, kb_ref, vb_ref, cb_ref, rb_ref, ob_ref))

    def body(c, carry):
        chains = []
        for n in range(nb):
            for d in range(2):
                cc = c if d == 0 else nct - 1 - c
                r0 = pl.multiple_of(cc * CHUNK, CHUNK)
                ct = dir_refs[d][3][n, pl.ds(r0, CHUNK), :]
                for h in range(DN_HEADS):
                    chains.append((d, h, cc, r0, ct, n))

        def load(chain, which):
            d, h, _, r0, _, n = chain
            return dir_refs[d][which][n, pl.ds(r0, CHUNK), h * DN_DK:(h + 1) * DN_DK]

        def col(chain, off, rows=slice(None)):
            d, h, _, _, ct, _ = chain
            lane = off + d * DN_HEADS + h
            return ct[rows, lane:lane + 1]

        def decay_of(chain):
            d, h, cc, _, _, n = chain
            incl = masks[d][0]
            g_row = dir_refs[d][4][n, cc, d * DN_HEADS + h:d * DN_HEADS + h + 1, :]
            return jnp.where(incl, jnp.exp(jnp.where(incl, col(chain, COL_G) - g_row, 0.0)), 0.0)

        kbeta = [load(ch, 1) * col(ch, COL_BETA) for ch in chains]
        a = [_dot_nt(jnp.concatenate([kb, load(ch, 0)], axis=0), load(ch, 1)) for ch, kb in zip(chains, kbeta)]
        decay = [decay_of(ch) for ch in chains]
        neg_l = [jnp.where(masks[ch[0]][1], -(ai[:CHUNK] * dc), 0.0) for ch, ai, dc in zip(chains, a, decay)]
        qk = [ai[CHUNK:] * dc for ai, dc in zip(a, decay)]
        t_inv = _unit_tri_inverse(neg_l)
        uw = [_dot(ti, jnp.concatenate([load(ch, 2) * col(ch, COL_BETA), kb * col(ch, COL_EG)], axis=1))
              for ch, ti, kb in zip(chains, t_inv, kbeta)]
        s = [s_scr[ch[5], ch[0], ch[1]] for ch in chains]
        ws_qs = [_dot(jnp.concatenate([uwi[:, DN_DV:], load(ch, 0) * col(ch, COL_EG)], axis=0), si)
                 for ch, uwi, si in zip(chains, uw, s)]
        v_new = [uwi[:, :DN_DV] - wq[:CHUNK] for uwi, wq in zip(uw, ws_qs)]
        o = [wq[CHUNK:] + _dot(qki, vn) for wq, qki, vn in zip(ws_qs, qk, v_new)]
        s_new = [si * col(ch, COL_EGT, slice(0, 1)) + _dot_tn(load(ch, 1) * col(ch, COL_EGL), vn)
                 for ch, si, vn in zip(chains, s, v_new)]
        for ch, oi, sn in zip(chains, o, s_new):
            d, h, _, r0, _, n = ch
            dir_refs[d][5][n, pl.ds(r0, CHUNK), h * DN_DV:(h + 1) * DN_DV] = oi
            s_scr[n, d, h] = sn
        return carry

    lax.fori_loop(0, nct, body, 0)

    @pl.when(j == pl.num_programs(1) - 1)
    def _():
        st_ref[...] = s_scr[...]


def delta_scan(feat, gcol, grow, s0, tt, nb):
    bsz, t, _ = feat.shape
    nt = t // tt
    nct = tt // CHUNK
    fwd = lambda lane_blk: (lambda b, j: (b, j, lane_blk))
    bwd = lambda lane_blk: (lambda b, j: (b, nt - 1 - j, lane_blk))
    qkv_blk = (nb, tt, DN_QK)

    def side(im):
        return [pl.BlockSpec(qkv_blk, im(OFF_Q // DN_QK)), pl.BlockSpec(qkv_blk, im(OFF_K // DN_QK)),
                pl.BlockSpec(qkv_blk, im(OFF_V // DN_QK)), pl.BlockSpec((nb, tt, LANES), im(0))]

    row_spec = lambda rev: pl.BlockSpec((nb, nct, NDH, CHUNK),
                                        (lambda b, j: (b, nt - 1 - j, 0, 0)) if rev else (lambda b, j: (b, j, 0, 0)))
    state_spec = pl.BlockSpec((nb, 2, DN_HEADS, DN_DK, DN_DV), lambda b, j: (b, 0, 0, 0, 0))
    return pl.pallas_call(
        functools.partial(_dn_kernel, nct=nct, nb=nb),
        grid=(bsz // nb, nt),
        in_specs=side(fwd) + [row_spec(False)] + side(bwd) + [row_spec(True)] + [state_spec],
        out_specs=[pl.BlockSpec((nb, tt, DN_VW), fwd(0)), pl.BlockSpec((nb, tt, DN_VW), bwd(0)), state_spec],
        out_shape=[
            jax.ShapeDtypeStruct((bsz, t, DN_VW), F32),
            jax.ShapeDtypeStruct((bsz, t, DN_VW), F32),
            jax.ShapeDtypeStruct((bsz, 2, DN_HEADS, DN_DK, DN_DV), F32),
        ],
        scratch_shapes=[pltpu.VMEM((nb, 2, DN_HEADS, DN_DK, DN_DV), F32)],
        compiler_params=_compiler_params(("parallel", "arbitrary")),
        name="delta_scan",
    )(feat, feat, feat, gcol, grow, feat, feat, feat, gcol, grow, s0)


def _group_lane_select(cols):
    shape = (cols[0].shape[0], SSM_GW)
    head = lax.broadcasted_iota(jnp.int32, shape, 1) // SSM_P
    out = jnp.broadcast_to(cols[SSM_HPG - 1], shape)
    for hh in range(SSM_HPG - 2, -1, -1):
        out = jnp.where(head == hh, jnp.broadcast_to(cols[hh], shape), out)
    return out


def _ssd_kernel(xf_ref, bf_ref, cf_ref, gf_ref, rf_ref, xb_ref, bb_ref, cb_ref, gb_ref, rb_ref, h0_ref,
                yf_ref, yb_ref, ht_ref, h_scr, *, nct, nb):
    j = pl.program_id(1)

    @pl.when(j == 0)
    def _():
        h_scr[...] = h0_ref[...]

    row_i = lax.broadcasted_iota(jnp.int32, (CHUNK, SSM_GW), 0)
    col_j = lax.broadcasted_iota(jnp.int32, (CHUNK, SSM_GW), 1) % SSM_P
    incl_dir = (row_i >= col_j, row_i <= col_j)
    bd_rows = lax.broadcasted_iota(jnp.int32, (SSM_HPG * CHUNK, SSM_GW), 0) // CHUNK
    bd_cols = lax.broadcasted_iota(jnp.int32, (SSM_HPG * CHUNK, SSM_GW), 1) // SSM_P
    block_diag = bd_rows == bd_cols
    dir_refs = ((xf_ref, bf_ref, cf_ref, gf_ref, rf_ref, yf_ref),
                (xb_ref, bb_ref, cb_ref, gb_ref, rb_ref, yb_ref))

    def body(c, carry):
        chains = []
        for n in range(nb):
            for d in range(2):
                cc = c if d == 0 else nct - 1 - c
                r0 = pl.multiple_of(cc * CHUNK, CHUNK)
                ct = dir_refs[d][3][n, pl.ds(r0, CHUNK), :]
                for g in range(SSM_G):
                    chains.append((d, g, cc, r0, ct, n))

        def load(chain, which, width):
            d, g, _, r0, _, n = chain
            return dir_refs[d][which][n, pl.ds(r0, CHUNK), g * width:(g + 1) * width]

        def pick(chain, off, rows=slice(None)):
            d, g, _, _, ct, _ = chain
            h0 = off + d * SSM_HEADS + g * SSM_HPG
            return _group_lane_select([ct[rows, h0 + hh:h0 + hh + 1] for hh in range(SSM_HPG)])

        def decay_of(chain):
            d, g, cc, _, _, n = chain
            incl = incl_dir[d]
            a_row = dir_refs[d][4][n, cc, d * SSM_G + g:d * SSM_G + g + 1, :]
            return jnp.where(incl, jnp.exp(jnp.where(incl, pick(chain, COL_A) - a_row, 0.0)), 0.0)

        cb4 = [_dot_nt(load(ch, 2, SSM_N), jnp.concatenate([load(ch, 1, SSM_N)] * SSM_HPG, axis=0)) for ch in chains]
        hg = [h_scr[ch[5], ch[0], ch[1]] for ch in chains]
        y_off = [_dot(load(ch, 2, SSM_N), h) for ch, h in zip(chains, hg)]
        h_in = [_dot_tn(load(ch, 1, SSM_N), load(ch, 0, SSM_GW) * pick(ch, COL_DTEAL)) for ch in chains]
        scores = [cb * decay_of(ch) for ch, cb in zip(chains, cb4)]
        xdt_bd = [jnp.where(block_diag,
                            jnp.concatenate([load(ch, 0, SSM_GW) * pick(ch, COL_DT)] * SSM_HPG, axis=0), 0.0)
                  for ch in chains]
        y_diag = [_dot(sc, xb) for sc, xb in zip(scores, xdt_bd)]
        for ch, yd, yo, h, hi in zip(chains, y_diag, y_off, hg, h_in):
            d, g, _, r0, _, n = ch
            dir_refs[d][5][n, pl.ds(r0, CHUNK), g * SSM_GW:(g + 1) * SSM_GW] = yd + yo * pick(ch, COL_EA)
            h_scr[n, d, g] = h * pick(ch, COL_EAT, slice(0, 1)) + hi
        return carry

    lax.fori_loop(0, nct, body, 0)

    @pl.when(j == pl.num_programs(1) - 1)
    def _():
        ht_ref[...] = h_scr[...]


def ssd_scan(feat, gcol, arow, h0, tt, nb):
    bsz, t, _ = feat.shape
    nt = t // tt
    nct = tt // CHUNK
    fwd = lambda lane_blk: (lambda b, j: (b, j, lane_blk))
    bwd = lambda lane_blk: (lambda b, j: (b, nt - 1 - j, lane_blk))

    def side(im):
        return [pl.BlockSpec((nb, tt, SSM_DI), im(OFF_X // SSM_DI)),
                pl.BlockSpec((nb, tt, SSM_BC), im(OFF_B // SSM_BC)),
                pl.BlockSpec((nb, tt, SSM_BC), im(OFF_C // SSM_BC)),
                pl.BlockSpec((nb, tt, LANES), im(0))]

    row_spec = lambda rev: pl.BlockSpec((nb, nct, 2 * SSM_G, SSM_GW),
                                        (lambda b, j: (b, nt - 1 - j, 0, 0)) if rev else (lambda b, j: (b, j, 0, 0)))
    state_spec = pl.BlockSpec((nb, 2, SSM_G, SSM_N, SSM_GW), lambda b, j: (b, 0, 0, 0, 0))
    return pl.pallas_call(
        functools.partial(_ssd_kernel, nct=nct, nb=nb),
        grid=(bsz // nb, nt),
        in_specs=side(fwd) + [row_spec(False)] + side(bwd) + [row_spec(True)] + [state_spec],
        out_specs=[pl.BlockSpec((nb, tt, SSM_DI), fwd(0)), pl.BlockSpec((nb, tt, SSM_DI), bwd(0)), state_spec],
        out_shape=[
            jax.ShapeDtypeStruct((bsz, t, SSM_DI), F32),
            jax.ShapeDtypeStruct((bsz, t, SSM_DI), F32),
            jax.ShapeDtypeStruct((bsz, 2, SSM_G, SSM_N, SSM_GW), F32),
        ],
        scratch_shapes=[pltpu.VMEM((nb, 2, SSM_G, SSM_N, SSM_GW), F32)],
        compiler_params=_compiler_params(("parallel", "arbitrary")),
        name="ssd_scan",
    )(feat, feat, feat, gcol, arow, feat, feat, feat, gcol, arow, h0)


def _merge_out_kernel(of_ref, ob_ref, yf_ref, yb_ref, xs_ref, z_ref, dng_ref, dskip_ref, ssg_ref,
                      w_ref, g_ref, gate_ref, res_ref, o_ref):
    o = of_ref[0] + ob_ref[0]
    z = z_ref[0]
    parts = []
    for h in range(DN_HEADS):
        lanes = slice(h * DN_DV, (h + 1) * DN_DV)
        parts.append(_rms(o[:, lanes]) * dng_ref[...] * _silu(z[:, lanes]))
    y = yf_ref[0] + yb_ref[0] + dskip_ref[...] * xs_ref[0]
    yz = y * _silu(z[:, DN_VW:])
    gw = SSM_DI // SSM_G
    for g in range(SSM_G):
        lanes = slice(g * gw, (g + 1) * gw)
        parts.append(_rms(yz[:, lanes]) * ssg_ref[:, lanes])
    mixed = jnp.concatenate(parts, axis=1).astype(BF16)
    m = jnp.dot(mixed, w_ref[...], preferred_element_type=F32)
    o_ref[0] = res_ref[0] + gate_ref[0] * (_rms(m) * g_ref[...])


def merge_out(o_f, o_b, y_f, y_b, feat, z, dn_norm_g, dskip, ssm_norm_g, w_out, g, gate, res, tm):
    bsz, t, d = res.shape
    row = lambda b, i: (b, i, 0)
    const = lambda b, i: (0, 0)
    half = lambda: pl.BlockSpec((1, tm, DN_VW), row)
    return pl.pallas_call(
        _merge_out_kernel,
        grid=(bsz, t // tm),
        in_specs=[
            half(), half(), half(), half(),
            pl.BlockSpec((1, tm, SSM_DI), lambda b, i: (b, i, OFF_X // SSM_DI)),
            pl.BlockSpec((1, tm, D_Z), row),
            pl.BlockSpec((1, DN_DV), const),
            pl.BlockSpec((1, SSM_DI), const),
            pl.BlockSpec((1, SSM_DI), const),
            pl.BlockSpec((D_MIX, d), const),
            pl.BlockSpec((1, d), const),
            pl.BlockSpec((1, 1, d), lambda b, i: (b, 0, 0)),
            pl.BlockSpec((1, tm, d), row),
        ],
        out_specs=pl.BlockSpec((1, tm, d), row),
        out_shape=jax.ShapeDtypeStruct((bsz, t, d), F32),
        compiler_params=_compiler_params(("parallel", "parallel")),
        name="merge_out",
    )(o_f, o_b, y_f, y_b, feat, z, dn_norm_g.reshape(1, DN_DV), dskip, ssm_norm_g.reshape(1, SSM_DI),
      w_out, g.reshape(1, d), gate.reshape(bsz, 1, d), res)


MOE_TILE = 256
SLOT_BLOCK = LANES
SLOT_WINDOW = 2 * SLOT_BLOCK
SPARE_BLOCKS = 2 * (SLOT_WINDOW // SLOT_BLOCK) - 1
CNT_LANES = 32


def _router_kernel(x_ref, g_ref, shift_ref, scale_ref, wr_ref, ht_ref, aff_ref):
    h = _rms(x_ref[0]) * g_ref[...]
    h = h * (1.0 + scale_ref[0]) + shift_ref[0]
    ht = h.T.astype(BF16)
    logits = jnp.dot(wr_ref[...], ht, preferred_element_type=F32)
    ex = jnp.exp(logits - jnp.max(logits, axis=0, keepdims=True))
    ht_ref[0, 0] = ht
    aff_ref[0, 0] = ex / jnp.sum(ex, axis=0, keepdims=True)


def router(x, g, shift, scale, wr_t):
    bsz, t, d = x.shape
    nt = t // MOE_TILE
    const = lambda b, j: (0, 0)
    per_b = lambda b, j: (b, 0, 0)
    return pl.pallas_call(
        _router_kernel,
        grid=(bsz, nt),
        in_specs=[
            pl.BlockSpec((1, MOE_TILE, d), lambda b, j: (b, j, 0)),
            pl.BlockSpec((1, d), const),
            pl.BlockSpec((1, 1, d), per_b),
            pl.BlockSpec((1, 1, d), per_b),
            pl.BlockSpec((N_EXPERTS, d), const),
        ],
        out_specs=[
            pl.BlockSpec((1, 1, d, MOE_TILE), lambda b, j: (b, j, 0, 0)),
            pl.BlockSpec((1, 1, N_EXPERTS, MOE_TILE), lambda b, j: (b, j, 0, 0)),
        ],
        out_shape=[
            jax.ShapeDtypeStruct((bsz, nt, d, MOE_TILE), BF16),
            jax.ShapeDtypeStruct((bsz, nt, N_EXPERTS, MOE_TILE), F32),
        ],
        compiler_params=_compiler_params(("parallel", "parallel")),
        name="router",
    )(x, g.reshape(1, d), shift.reshape(bsz, 1, d), scale.reshape(bsz, 1, d), wr_t)


def _lane_prefix(mask, upper):
    carry = jnp.zeros((mask.shape[0], 1), F32)
    blocks = []
    for m in range(mask.shape[1] // LANES):
        p = jnp.dot(mask[:, m * LANES:(m + 1) * LANES].astype(BF16), upper, preferred_element_type=F32) + carry
        carry = p[:, LANES - 1:LANES]
        blocks.append(p)
    return jnp.concatenate(blocks, axis=1)


def _select_kernel(aff_ref, pos_ref, cnt_ref, *, nt, cap):
    aff = jnp.concatenate([aff_ref[0, j] for j in range(nt)], axis=1)
    bits = pltpu.bitcast(aff, jnp.int32)

    def bisect(i, thr):
        cand = thr | lax.shift_left(jnp.int32(1), jnp.int32(30) - i)
        n_ge = jnp.sum(jnp.where(bits >= cand, 1.0, 0.0), axis=1, keepdims=True)
        return jnp.where(n_ge >= cap, cand, thr)

    thr = lax.fori_loop(0, 31, bisect, jnp.zeros((N_EXPERTS, 1), jnp.int32))
    above = bits > thr
    tied = bits == thr
    room = cap - jnp.sum(jnp.where(above, 1.0, 0.0), axis=1, keepdims=True)
    src = lax.broadcasted_iota(jnp.int32, (LANES, LANES), 0)
    dst = lax.broadcasted_iota(jnp.int32, (LANES, LANES), 1)
    upper = jnp.where(src <= dst, 1.0, 0.0).astype(BF16)
    tie_rank = _lane_prefix(jnp.where(tied, 1.0, 0.0), upper)
    sel = above | (tied & (tie_rank <= room))
    selc = jnp.where(sel, 1.0, 0.0)
    slot = jnp.where(sel, _lane_prefix(selc, upper) - 1.0, -1.0)
    lane = lax.broadcasted_iota(jnp.int32, (N_EXPERTS, LANES), 1)
    before = jnp.zeros((N_EXPERTS, 1), F32)
    counts = jnp.zeros((N_EXPERTS, LANES), F32)
    for j in range(nt):
        pos_ref[0, j] = slot[:, j * MOE_TILE:(j + 1) * MOE_TILE]
        before = before + jnp.sum(selc[:, j * MOE_TILE:(j + 1) * MOE_TILE], axis=1, keepdims=True)
        counts = jnp.where(lane == j + 1, before, counts)
    cnt_ref[0] = counts.astype(jnp.int32)


def select_slots(aff, cap):
    bsz, nt, ne, tile = aff.shape
    blk = lambda b: (b, 0, 0, 0)
    return pl.pallas_call(
        functools.partial(_select_kernel, nt=nt, cap=cap),
        grid=(bsz,),
        in_specs=[pl.BlockSpec((1, nt, ne, tile), blk)],
        out_specs=[pl.BlockSpec((1, nt, ne, tile), blk), pl.BlockSpec((1, ne, LANES), lambda b: (b, 0, 0))],
        out_shape=[jax.ShapeDtypeStruct((bsz, nt, ne, tile), F32),
                   jax.ShapeDtypeStruct((bsz, ne, LANES), jnp.int32)],
        compiler_params=_compiler_params(("parallel",)),
        name="select_slots",
    )(aff)


def _moe_ffn_kernel(cnt_ref, ht_ref, pos_ref, aff_ref, wg_ref, wu_ref, wd_ref, y_ref, acc_scr, og_scr,
                    *, nt, ns):
    e = pl.program_id(1)
    row = pl.program_id(0) * N_EXPERTS + e
    d = ht_ref.shape[2]

    @pl.when(e == 0)
    def _():
        y_ref[...] = jnp.zeros_like(y_ref)
        for k in range(ns, ns + SPARE_BLOCKS):
            acc_scr[k] = jnp.zeros((d, SLOT_BLOCK), F32)
            og_scr[k] = jnp.zeros((d, SLOT_BLOCK), BF16)

    for k in range(ns):
        acc_scr[k] = jnp.zeros((d, SLOT_BLOCK), F32)

    win_iota = lax.broadcasted_iota(jnp.int32, (SLOT_WINDOW, MOE_TILE), 0).astype(F32)

    def one_hot(j, w):
        pos = pos_ref[0, j, pl.ds(e, 1), :]
        return jnp.where(pos - (w * SLOT_BLOCK).astype(F32) == win_iota, 1.0, 0.0).astype(BF16)

    def first_block(j):
        return cnt_ref[row, j] // SLOT_BLOCK

    def spills_past_window(j):
        return (cnt_ref[row, j + 1] - 1) // SLOT_BLOCK >= first_block(j) + SLOT_WINDOW // SLOT_BLOCK

    def gather_window(j, w):
        r = lax.dot_general(ht_ref[0, j], one_hot(j, w), (((1,), (1,)), ((), ())), preferred_element_type=F32)
        acc_scr[w] += r[:, :SLOT_BLOCK]
        acc_scr[w + 1] += r[:, SLOT_BLOCK:]

    def gather_tile(j, carry):
        gather_window(j, first_block(j))
        return carry

    def gather_tile_rest(j, carry):
        @pl.when(spills_past_window(j))
        def _():
            gather_window(j, first_block(j) + SLOT_WINDOW // SLOT_BLOCK)
        return carry

    lax.fori_loop(0, nt, gather_tile, 0, unroll=2 if nt % 2 == 0 else 1)
    lax.fori_loop(0, nt, gather_tile_rest, 0)

    xs = jnp.concatenate([acc_scr[k].astype(BF16) for k in range(ns)], axis=1)
    a = jnp.dot(wg_ref[0], xs, preferred_element_type=F32)
    u = jnp.dot(wu_ref[0], xs, preferred_element_type=F32)
    hid = (_silu(a) * u).astype(BF16)
    og = jnp.dot(wd_ref[0], hid, preferred_element_type=F32)
    for k in range(ns):
        og_scr[k] = og[:, k * SLOT_BLOCK:(k + 1) * SLOT_BLOCK].astype(BF16)

    def scatter_window(j, w):
        og2 = jnp.concatenate([og_scr[w], og_scr[w + 1]], axis=1)
        gate = aff_ref[0, j, pl.ds(e, 1), :]
        y_ref[0, j] += jnp.dot(og2, one_hot(j, w), preferred_element_type=F32) * gate

    def scatter_tile(j, carry):
        scatter_window(j, first_block(j))
        return carry

    def scatter_tile_rest(j, carry):
        @pl.when(spills_past_window(j))
        def _():
            scatter_window(j, first_block(j) + SLOT_WINDOW // SLOT_BLOCK)
        return carry

    lax.fori_loop(0, nt, scatter_tile, 0, unroll=2 if nt % 2 == 0 else 1)
    lax.fori_loop(0, nt, scatter_tile_rest, 0)


def moe_ffn(cnt, ht, pos, aff, wg_t, wu_t, wd_t, cap):
    bsz, nt, d, tile = ht.shape
    ne, f, _ = wg_t.shape
    ns = pl.cdiv(cap, SLOT_BLOCK)
    per_b = lambda b, e, c: (b, 0, 0, 0)
    per_e = lambda b, e, c: (e, 0, 0)
    grid_spec = pltpu.PrefetchScalarGridSpec(
        num_scalar_prefetch=1,
        grid=(bsz, ne),
        in_specs=[
            pl.BlockSpec((1, nt, d, tile), per_b, pipeline_mode=pl.Buffered(1)),
            pl.BlockSpec((1, nt, ne, tile), per_b),
            pl.BlockSpec((1, nt, ne, tile), per_b),
            pl.BlockSpec((1, f, d), per_e),
            pl.BlockSpec((1, f, d), per_e),
            pl.BlockSpec((1, d, f), per_e),
        ],
        out_specs=pl.BlockSpec((1, nt, d, tile), per_b, pipeline_mode=pl.Buffered(1)),
        scratch_shapes=[pltpu.VMEM((ns + SPARE_BLOCKS, d, SLOT_BLOCK), F32),
                        pltpu.VMEM((ns + SPARE_BLOCKS, d, SLOT_BLOCK), BF16)],
    )
    return pl.pallas_call(
        functools.partial(_moe_ffn_kernel, nt=nt, ns=ns),
        grid_spec=grid_spec,
        out_shape=jax.ShapeDtypeStruct((bsz, nt, d, tile), F32),
        compiler_params=_compiler_params(("parallel", "arbitrary")),
        name="moe_ffn",
    )(cnt, ht, pos, aff, wg_t, wu_t, wd_t)


def _moe_out_kernel(y_ref, g_ref, gate_ref, res_ref, o_ref):
    o_ref[0] = res_ref[0] + gate_ref[0] * (_rms(y_ref[0, 0].T) * g_ref[...])


def moe_out(y_t, g, gate, res):
    bsz, t, d = res.shape
    nt = t // MOE_TILE
    return pl.pallas_call(
        _moe_out_kernel,
        grid=(bsz, nt),
        in_specs=[
            pl.BlockSpec((1, 1, d, MOE_TILE), lambda b, j: (b, j, 0, 0)),
            pl.BlockSpec((1, d), lambda b, j: (0, 0)),
            pl.BlockSpec((1, 1, d), lambda b, j: (b, 0, 0)),
            pl.BlockSpec((1, MOE_TILE, d), lambda b, j: (b, j, 0)),
        ],
        out_specs=pl.BlockSpec((1, MOE_TILE, d), lambda b, j: (b, j, 0)),
        out_shape=jax.ShapeDtypeStruct((bsz, t, d), F32),
        compiler_params=_compiler_params(("parallel", "parallel")),
        name="moe_out",
    )(y_t, g.reshape(1, d), gate.reshape(bsz, 1, d), res)


def ec_moe_residual(x, g_in, shift, scale, wr_t, wg_t, wu_t, wd_t, g_out, gate):
    bsz, t, _ = x.shape
    cap = EC_CAPACITY_FACTOR * t // N_EXPERTS
    ht, aff = router(x, g_in, shift, scale, wr_t)
    pos, counts = select_slots(aff, cap)
    cnt = counts[:, :, :CNT_LANES].reshape(bsz * N_EXPERTS, CNT_LANES)
    y_t = moe_ffn(cnt, ht, pos, aff, wg_t, wu_t, wd_t, cap)
    return moe_out(y_t, g_out, gate, x)


def _token_tile(t):
    return min(512, t)


def mixer_stream(x, g0, shift, scale, wc, wz, ws, w9, conv_b, bias_rows, nega_rows, dn_state, ssm_state, on_grid):
    t = x.shape[1]
    tm = _token_tile(t)
    conv_in, z, small = in_proj(x, g0, shift, scale, wc, wz, ws, tm)
    feat = conv_features(conv_in, w9, conv_b, on_grid)
    gcol, grow, arow = gates(small, bias_rows, nega_rows, tm)
    ts = min(SCAN_TILE, t)
    o_f, o_b, dn_state = delta_scan(feat, gcol, grow, dn_state, ts, SCAN_SAMPLES_PER_STEP)
    y_f, y_b, ssm_state = ssd_scan(feat, gcol, arow, ssm_state, ts, SCAN_SAMPLES_PER_STEP)
    return (o_f, o_b, y_f, y_b, feat, z), dn_state, ssm_state


def _gate_param_rows(dn_bias, dn_a_log, ssm_bias, ssm_a_log):
    zeros = jnp.zeros((NDH,), F32)
    bias = jnp.concatenate([zeros, dn_bias.reshape(-1), ssm_bias.reshape(-1)])
    nega = jnp.concatenate([zeros, -jnp.exp(dn_a_log.reshape(-1)), -jnp.exp(ssm_a_log.reshape(-1))])
    pad = LANES - bias.shape[0]
    expand = lambda v: jnp.broadcast_to(jnp.pad(v, (0, pad))[:, None], (LANES, LANES))
    return expand(bias), expand(nega)


def kernel(x, c, ctx, c_ctx, ada_w, ada_b, norm_g, w_in, conv_w, conv_b, dn_A_log, dn_dt_bias,
           dn_norm_g, ssm_A_log, ssm_dt_bias, ssm_D, ssm_norm_g, w_out, router_w,
           exp_w_gate, exp_w_up, exp_w_down):
    bsz = x.shape[0]
    s_lat = jax.nn.silu(c)
    s_ctx = jax.nn.silu(c_ctx)
    for l in range(DEPTH):
        last = l == DEPTH - 1
        mod_lat = jnp.split(s_lat @ ada_w[l] + ada_b[l], 6, axis=-1)
        mod_ctx_row = s_ctx @ ada_w[l] + ada_b[l]
        mod_ctx = [jnp.broadcast_to(m[None, :], (bsz, D_MODEL)) for m in jnp.split(mod_ctx_row, 6)]

        w_in_l = w_in[l]
        wc = w_in_l[:, :CONV_CH].astype(BF16)
        wz = w_in_l[:, CONV_CH:CONV_CH + D_Z].astype(BF16)
        ws = jnp.pad(w_in_l[:, CONV_CH + D_Z:], ((0, 0), (0, LANES - N_GATE_COLS))).astype(BF16)
        w9 = conv_w[l].reshape(CONV_CH, CONV_K * CONV_K).T
        cb = conv_b[l].reshape(1, CONV_CH)
        bias_rows, nega_rows = _gate_param_rows(dn_dt_bias[l], dn_A_log[l], ssm_dt_bias[l], ssm_A_log[l])
        dskip = jnp.repeat(ssm_D[l], SSM_P).reshape(1, SSM_DI)
        w_out_l = w_out[l].astype(BF16)
        to_t = lambda w: jnp.swapaxes(w, -1, -2).astype(BF16)
        moe_w = (to_t(router_w[l]), to_t(exp_w_gate[l]), to_t(exp_w_up[l]), to_t(exp_w_down[l]), norm_g[l, 3])

        dn0 = jnp.zeros((bsz, 2, DN_HEADS, DN_DK, DN_DV), F32)
        ssm0 = jnp.zeros((bsz, 2, SSM_G, SSM_N, SSM_GW), F32)
        shared = (wc, wz, ws, w9, cb, bias_rows, nega_rows)
        mix_ctx, dn_c, ssm_c = mixer_stream(ctx, norm_g[l, 0], mod_ctx[0], mod_ctx[1], *shared, dn0, ssm0, False)
        mix_lat, _, _ = mixer_stream(x, norm_g[l, 0], mod_lat[0], mod_lat[1], *shared, dn_c, ssm_c, True)
        merge_w = (dn_norm_g[l], dskip, ssm_norm_g[l], w_out_l, norm_g[l, 1])
        x = merge_out(*mix_lat, *merge_w, mod_lat[2], x, _token_tile(x.shape[1]))

        x = ec_moe_residual(x, norm_g[l, 2], mod_lat[3], mod_lat[4], *moe_w, mod_lat[5])

        if not last:
            ctx = merge_out(*mix_ctx, *merge_w, mod_ctx[2], ctx, _token_tile(ctx.shape[1]))
            ctx = ec_moe_residual(ctx, norm_g[l, 2], mod_ctx[3], mod_ctx[4], *moe_w, mod_ctx[5])
    return x
```

```python
import functools

import jax
import jax.numpy as jnp
import numpy as np
from jax import lax
from jax.experimental import pallas as pl
from jax.experimental.pallas import tpu as pltpu

D_MODEL = 1024
DEPTH = 4
GRID_W = 64
DN_HEADS = 4
DN_DK = 128
DN_DV = 128
SSM_HEADS = 8
SSM_P = 64
SSM_N = 128
SSM_G = 2
CHUNK = 64
CONV_K = 3
N_EXPERTS = 16
EC_CAPACITY_FACTOR = 2
D_EXPERT = 512
EPS = 1e-6

DN_QK = DN_HEADS * DN_DK
DN_VW = DN_HEADS * DN_DV
SSM_DI = SSM_HEADS * SSM_P
SSM_BC = SSM_G * SSM_N
SSM_HPG = SSM_HEADS // SSM_G
SSM_GW = SSM_HPG * SSM_P
D_MIX = DN_VW + SSM_DI
CONV_SPLITS = (DN_QK, DN_QK, DN_VW, SSM_DI, SSM_BC, SSM_BC)
CONV_CH = sum(CONV_SPLITS)
D_Z = DN_VW + SSM_DI
N_GATE_COLS = 2 * DN_HEADS + 2 * DN_HEADS + 2 * SSM_HEADS
D_IN_PROJ = CONV_CH + D_Z + N_GATE_COLS

LANES = 128
SUBLANES = 8
VMEM_LIMIT_BYTES = 56 * 1024 * 1024

OFF_Q, OFF_K, OFF_V = 0, DN_QK, 2 * DN_QK
OFF_X = 2 * DN_QK + DN_VW
OFF_B = OFF_X + SSM_DI
OFF_C = OFF_B + SSM_BC

NDH = 2 * DN_HEADS
NSH = 2 * SSM_HEADS
COL_BETA, COL_G, COL_EG, COL_EGL, COL_EGT = (i * NDH for i in range(5))
COL_DT, COL_A, COL_EA, COL_DTEAL, COL_EAT = (5 * NDH + i * NSH for i in range(5))

F32 = jnp.float32
BF16 = jnp.bfloat16


def _compiler_params(semantics):
    return pltpu.CompilerParams(dimension_semantics=semantics, vmem_limit_bytes=VMEM_LIMIT_BYTES)


def _rms(x):
    return x * lax.rsqrt(jnp.mean(x * x, axis=-1, keepdims=True) + EPS)


def _silu(x):
    return x * jax.nn.sigmoid(x)


def _softplus(x):
    return jnp.maximum(x, 0.0) + jnp.log(1.0 + jnp.exp(-jnp.abs(x)))


def _dot(a, b):
    return jnp.dot(a.astype(BF16), b.astype(BF16), preferred_element_type=F32)


def _dot_nt(a, b):
    return lax.dot_general(a.astype(BF16), b.astype(BF16), (((1,), (1,)), ((), ())),
                           preferred_element_type=F32)


def _dot_tn(a, b):
    return lax.dot_general(a.astype(BF16), b.astype(BF16), (((0,), (0,)), ((), ())),
                           preferred_element_type=F32)


def _dot_exact01(a, m01):
    a1 = a.astype(BF16)
    r1 = a - a1.astype(F32)
    a2 = r1.astype(BF16)
    a3 = (r1 - a2.astype(F32)).astype(BF16)
    m = m01.astype(BF16)
    out = jnp.dot(a3, m, preferred_element_type=F32)
    out = out + jnp.dot(a2, m, preferred_element_type=F32)
    return out + jnp.dot(a1, m, preferred_element_type=F32)


def _in_proj_kernel(x_ref, g_ref, shift_ref, scale_ref, wc_ref, wz_ref, ws_ref, oc_ref, oz_ref, os_ref):
    h = _rms(x_ref[0]) * g_ref[...]
    h = (h * (1.0 + scale_ref[0]) + shift_ref[0]).astype(BF16)
    oc_ref[0] = jnp.dot(h, wc_ref[...], preferred_element_type=F32)
    oz_ref[0] = jnp.dot(h, wz_ref[...], preferred_element_type=F32)
    os_ref[0] = jnp.dot(h, ws_ref[...], preferred_element_type=F32)


def in_proj(x, g, shift, scale, wc, wz, ws, tm):
    bsz, t, d = x.shape
    row = lambda b, i: (b, i, 0)
    const = lambda b, i: (0, 0)
    per_b = lambda b, i: (b, 0, 0)
    return pl.pallas_call(
        _in_proj_kernel,
        grid=(bsz, t // tm),
        in_specs=[
            pl.BlockSpec((1, tm, d), row),
            pl.BlockSpec((1, d), const),
            pl.BlockSpec((1, 1, d), per_b),
            pl.BlockSpec((1, 1, d), per_b),
            pl.BlockSpec((d, CONV_CH), const),
            pl.BlockSpec((d, D_Z), const),
            pl.BlockSpec((d, LANES), const),
        ],
        out_specs=[
            pl.BlockSpec((1, tm, CONV_CH), row),
            pl.BlockSpec((1, tm, D_Z), row),
            pl.BlockSpec((1, tm, LANES), row),
        ],
        out_shape=[
            jax.ShapeDtypeStruct((bsz, t, CONV_CH), F32),
            jax.ShapeDtypeStruct((bsz, t, D_Z), F32),
            jax.ShapeDtypeStruct((bsz, t, LANES), F32),
        ],
        compiler_params=_compiler_params(("parallel", "parallel")),
        name="in_proj",
    )(x, g.reshape(1, d), shift.reshape(bsz, 1, d), scale.reshape(bsz, 1, d), wc, wz, ws)


CONV_PAD = GRID_W + SUBLANES
CONV_TILE = 256


def _conv_feat_kernel(x_ref, w_ref, b_ref, o_ref, xp_ref, *, t, on_grid):
    j = pl.program_id(1)
    zeros = jnp.zeros((CONV_PAD, LANES), F32)
    xp_ref[0:CONV_PAD, :] = zeros
    xp_ref[CONV_PAD + t:CONV_PAD + t + CONV_PAD, :] = zeros
    xp_ref[CONV_PAD:CONV_PAD + t, :] = x_ref[0]

    tt = min(CONV_TILE, t)
    dys = (-1, 0, 1) if on_grid else (0,)
    col = lax.broadcasted_iota(jnp.int32, (tt, LANES), 0) % GRID_W
    bias = b_ref[...]
    w = w_ref[...]

    def conv_tile(i):
        base = pl.multiple_of(i * tt, tt)
        acc_c = acc_m = acc_p = None
        for dy in dys:
            win = xp_ref[pl.ds(base + CONV_PAD + dy * GRID_W - SUBLANES, tt + 2 * SUBLANES), :]
            c = win[SUBLANES:SUBLANES + tt]
            m = pltpu.roll(win, 1, 0)[SUBLANES:SUBLANES + tt]
            p = pltpu.roll(win, tt + 2 * SUBLANES - 1, 0)[SUBLANES:SUBLANES + tt]
            wr = 3 * (dy + 1)
            tc, tm_, tp = c * w[wr + 1:wr + 2], m * w[wr:wr + 1], p * w[wr + 2:wr + 3]
            acc_c = tc if acc_c is None else acc_c + tc
            acc_m = tm_ if acc_m is None else acc_m + tm_
            acc_p = tp if acc_p is None else acc_p + tp
        if on_grid:
            acc_m = jnp.where(col != 0, acc_m, 0.0)
            acc_p = jnp.where(col != GRID_W - 1, acc_p, 0.0)
        return base, _silu(acc_c + acc_m + acc_p + bias)

    n_qk_blocks = 2 * DN_QK // LANES

    @pl.when(j < n_qk_blocks)
    def _():
        qscale = jnp.where(j < DN_QK // LANES, DN_DK ** -0.5, 1.0).astype(F32)

        def body(i, carry):
            base, u = conv_tile(i)
            nrm = lax.rsqrt(jnp.sum(u * u, axis=-1, keepdims=True) + EPS)
            o_ref[0, pl.ds(base, tt), :] = u * nrm * qscale
            return carry
        lax.fori_loop(0, t // tt, body, 0)

    @pl.when(j >= n_qk_blocks)
    def _():
        def body(i, carry):
            base, u = conv_tile(i)
            o_ref[0, pl.ds(base, tt), :] = u
            return carry
        lax.fori_loop(0, t // tt, body, 0)


def conv_features(conv_in, w9, bias, on_grid):
    bsz, t, ch = conv_in.shape
    blk = lambda b, j: (b, 0, j)
    return pl.pallas_call(
        functools.partial(_conv_feat_kernel, t=t, on_grid=on_grid),
        grid=(bsz, ch // LANES),
        in_specs=[
            pl.BlockSpec((1, t, LANES), blk),
            pl.BlockSpec((9, LANES), lambda b, j: (0, j)),
            pl.BlockSpec((1, LANES), lambda b, j: (0, j)),
        ],
        out_specs=pl.BlockSpec((1, t, LANES), blk),
        out_shape=jax.ShapeDtypeStruct((bsz, t, ch), F32),
        scratch_shapes=[pltpu.VMEM((t + 2 * CONV_PAD, LANES), F32)],
        compiler_params=_compiler_params(("parallel", "parallel")),
        name="conv_features",
    )(conv_in, w9, bias)


def _gates_kernel(s_ref, bias_ref, nega_ref, col_ref, grow_ref, arow_ref, *, tg):
    tok = lax.broadcasted_iota(jnp.int32, (LANES, LANES), 0)
    out = lax.broadcasted_iota(jnp.int32, (LANES, LANES), 1)
    same = (tok // CHUNK) == (out // CHUNK)
    m_fwd = jnp.where(same & (tok <= out), 1.0, 0.0)
    m_bwd = jnp.where(same & (tok >= out), 1.0, 0.0)
    m_all = jnp.where(same, 1.0, 0.0)
    bias = bias_ref[...]
    nega = nega_ref[...]

    def dir_cumsum(v, heads):
        f = _dot_exact01(v, m_fwd)
        b = _dot_exact01(v, m_bwd)
        is_fwd = lax.broadcasted_iota(jnp.int32, v.shape, 0) < heads
        return jnp.where(is_fwd, f, b), _dot_exact01(v, m_all)

    for s in range(tg // LANES):
        st = s_ref[0, s * LANES:(s + 1) * LANES, :].T
        beta = jax.nn.sigmoid(st[0:NDH])
        sp = _softplus(st[NDH:2 * NDH + NSH] + bias[NDH:2 * NDH + NSH])
        logg = sp[0:NDH] * nega[NDH:2 * NDH]
        dt = sp[NDH:]
        a = dt * nega[2 * NDH:2 * NDH + NSH]
        g_cs, g_tot = dir_cumsum(logg, DN_HEADS)
        a_cs, a_tot = dir_cumsum(a, SSM_HEADS)
        rows = jnp.concatenate([
            beta, g_cs, jnp.exp(g_cs), jnp.exp(g_tot - g_cs), jnp.exp(g_tot),
            dt, a_cs, jnp.exp(a_cs), dt * jnp.exp(a_tot - a_cs), jnp.exp(a_tot),
            jnp.zeros((LANES - 5 * NDH - 5 * NSH, LANES), F32)], axis=0)
        col_ref[0, s * LANES:(s + 1) * LANES, :] = rows.T
        for half in range(LANES // CHUNK):
            c = s * (LANES // CHUNK) + half
            lo = half * CHUNK
            grow_ref[0, c] = g_cs[:, lo:lo + CHUNK]
            for dg in range(2 * SSM_G):
                r0 = dg * SSM_HPG
                arow_ref[0, c, dg:dg + 1, :] = jnp.concatenate(
                    [a_cs[r0 + hh:r0 + hh + 1, lo:lo + CHUNK] for hh in range(SSM_HPG)], axis=1)


def gates(small, bias_rows, nega_rows, tg):
    bsz, t, _ = small.shape
    nc = t // CHUNK
    ncg = tg // CHUNK
    const = lambda b, i: (0, 0)
    return pl.pallas_call(
        functools.partial(_gates_kernel, tg=tg),
        grid=(bsz, t // tg),
        in_specs=[
            pl.BlockSpec((1, tg, LANES), lambda b, i: (b, i, 0)),
            pl.BlockSpec((LANES, LANES), const),
            pl.BlockSpec((LANES, LANES), const),
        ],
        out_specs=[
            pl.BlockSpec((1, tg, LANES), lambda b, i: (b, i, 0)),
            pl.BlockSpec((1, ncg, NDH, CHUNK), lambda b, i: (b, i, 0, 0)),
            pl.BlockSpec((1, ncg, 2 * SSM_G, SSM_GW), lambda b, i: (b, i, 0, 0)),
        ],
        out_shape=[
            jax.ShapeDtypeStruct((bsz, t, LANES), F32),
            jax.ShapeDtypeStruct((bsz, nc, NDH, CHUNK), F32),
            jax.ShapeDtypeStruct((bsz, nc, 2 * SSM_G, SSM_GW), F32),
        ],
        compiler_params=_compiler_params(("parallel", "parallel")),
        name="gates",
    )(small, bias_rows, nega_rows)


SCAN_SAMPLES_PER_STEP = 4
SCAN_TILE = 256


def _chunk_masks(rev):
    r = lax.broadcasted_iota(jnp.int32, (CHUNK, CHUNK), 0)
    c = lax.broadcasted_iota(jnp.int32, (CHUNK, CHUNK), 1)
    return ((r <= c), (r < c)) if rev else ((r >= c), (r > c))


def _unit_tri_inverse(ms):
    eye = (lax.broadcasted_iota(jnp.int32, (CHUNK, CHUNK), 0)
           == lax.broadcasted_iota(jnp.int32, (CHUNK, CHUNK), 1)).astype(F32)
    ps = [_dot(m, m) for m in ms]
    ts = [eye + m for m in ms]
    n_sq = int(np.log2(CHUNK)) - 1
    for _ in range(n_sq - 1):
        rs = [_dot(jnp.concatenate([t, p], axis=0), p) for t, p in zip(ts, ps)]
        ts = [t + r[:CHUNK] for t, r in zip(ts, rs)]
        ps = [r[CHUNK:] for r in rs]
    return [t + _dot(t, p) for t, p in zip(ts, ps)]


def _dn_kernel(qf_ref, kf_ref, vf_ref, cf_ref, rf_ref, qb_ref, kb_ref, vb_ref, cb_ref, rb_ref, s0_ref,
               of_ref, ob_ref, st_ref, s_scr, *, nct, nb):
    j = pl.program_id(1)

    @pl.when(j == 0)
    def _():
        s_scr[...] = s0_ref[...]

    masks = (_chunk_masks(False), _chunk_masks(True))
    dir_refs = ((qf_ref, kf_ref, vf_ref, cf_ref, rf_ref, of_ref),
                (qb_ref, kb_ref, vb_ref, cb_ref, rb_ref, ob_ref))

    def body(c, carry):
        chains = []
        for n in range(nb):
            for d in range(2):
                cc = c if d == 0 else nct - 1 - c
                r0 = pl.multiple_of(cc * CHUNK, CHUNK)
                ct = dir_refs[d][3][n, pl.ds(r0, CHUNK), :]
                for h in range(DN_HEADS):
                    chains.append((d, h, cc, r0, ct, n))

        def load(chain, which):
            d, h, _, r0, _, n = chain
            return dir_refs[d][which][n, pl.ds(r0, CHUNK), h * DN_DK:(h + 1) * DN_DK]

        def col(chain, off, rows=slice(None)):
            d, h, _, _, ct, _ = chain
            lane = off + d * DN_HEADS + h
            return ct[rows, lane:lane + 1]

        def decay_of(chain):
            d, h, cc, _, _, n = chain
            incl = masks[d][0]
            g_row = dir_refs[d][4][n, cc, d * DN_HEADS + h:d * DN_HEADS + h + 1, :]
            return jnp.where(incl, jnp.exp(jnp.where(incl, col(chain, COL_G) - g_row, 0.0)), 0.0)

        kbeta = [load(ch, 1) * col(ch, COL_BETA) for ch in chains]
        a = [_dot_nt(jnp.concatenate([kb, load(ch, 0)], axis=0), load(ch, 1)) for ch, kb in zip(chains, kbeta)]
        decay = [decay_of(ch) for ch in chains]
        neg_l = [jnp.where(masks[ch[0]][1], -(ai[:CHUNK] * dc), 0.0) for ch, ai, dc in zip(chains, a, decay)]
        qk = [ai[CHUNK:] * dc for ai, dc in zip(a, decay)]
        t_inv = _unit_tri_inverse(neg_l)
        uw = [_dot(ti, jnp.concatenate([load(ch, 2) * col(ch, COL_BETA), kb * col(ch, COL_EG)], axis=1))
              for ch, ti, kb in zip(chains, t_inv, kbeta)]
        s = [s_scr[ch[5], ch[0], ch[1]] for ch in chains]
        ws_qs = [_dot(jnp.concatenate([uwi[:, DN_DV:], load(ch, 0) * col(ch, COL_EG)], axis=0), si)
                 for ch, uwi, si in zip(chains, uw, s)]
        v_new = [uwi[:, :DN_DV] - wq[:CHUNK] for uwi, wq in zip(uw, ws_qs)]
        o = [wq[CHUNK:] + _dot(qki, vn) for wq, qki, vn in zip(ws_qs, qk, v_new)]
        s_new = [si * col(ch, COL_EGT, slice(0, 1)) + _dot_tn(load(ch, 1) * col(ch, COL_EGL), vn)
                 for ch, si, vn in zip(chains, s, v_new)]
        for ch, oi, sn in zip(chains, o, s_new):
            d, h, _, r0, _, n = ch
            dir_refs[d][5][n, pl.ds(r0, CHUNK), h * DN_DV:(h + 1) * DN_DV] = oi
            s_scr[n, d, h] = sn
        return carry

    lax.fori_loop(0, nct, body, 0)

    @pl.when(j == pl.num_programs(1) - 1)
    def _():
        st_ref[...] = s_scr[...]


def delta_scan(feat, gcol, grow, s0, tt, nb):
    bsz, t, _ = feat.shape
    nt = t // tt
    nct = tt // CHUNK
    fwd = lambda lane_blk: (lambda b, j: (b, j, lane_blk))
    bwd = lambda lane_blk: (lambda b, j: (b, nt - 1 - j, lane_blk))
    qkv_blk = (nb, tt, DN_QK)

    def side(im):
        return [pl.BlockSpec(qkv_blk, im(OFF_Q // DN_QK)), pl.BlockSpec(qkv_blk, im(OFF_K // DN_QK)),
                pl.BlockSpec(qkv_blk, im(OFF_V // DN_QK)), pl.BlockSpec((nb, tt, LANES), im(0))]

    row_spec = lambda rev: pl.BlockSpec((nb, nct, NDH, CHUNK),
                                        (lambda b, j: (b, nt - 1 - j, 0, 0)) if rev else (lambda b, j: (b, j, 0, 0)))
    state_spec = pl.BlockSpec((nb, 2, DN_HEADS, DN_DK, DN_DV), lambda b, j: (b, 0, 0, 0, 0))
    return pl.pallas_call(
        functools.partial(_dn_kernel, nct=nct, nb=nb),
        grid=(bsz // nb, nt),
        in_specs=side(fwd) + [row_spec(False)] + side(bwd) + [row_spec(True)] + [state_spec],
        out_specs=[pl.BlockSpec((nb, tt, DN_VW), fwd(0)), pl.BlockSpec((nb, tt, DN_VW), bwd(0)), state_spec],
        out_shape=[
            jax.ShapeDtypeStruct((bsz, t, DN_VW), F32),
            jax.ShapeDtypeStruct((bsz, t, DN_VW), F32),
            jax.ShapeDtypeStruct((bsz, 2, DN_HEADS, DN_DK, DN_DV), F32),
        ],
        scratch_shapes=[pltpu.VMEM((nb, 2, DN_HEADS, DN_DK, DN_DV), F32)],
        compiler_params=_compiler_params(("parallel", "arbitrary")),
        name="delta_scan",
    )(feat, feat, feat, gcol, grow, feat, feat, feat, gcol, grow, s0)


def _group_lane_select(cols):
    shape = (cols[0].shape[0], SSM_GW)
    head = lax.broadcasted_iota(jnp.int32, shape, 1) // SSM_P
    out = jnp.broadcast_to(cols[SSM_HPG - 1], shape)
    for hh in range(SSM_HPG - 2, -1, -1):
        out = jnp.where(head == hh, jnp.broadcast_to(cols[hh], shape), out)
    return out


def _ssd_kernel(xf_ref, bf_ref, cf_ref, gf_ref, rf_ref, xb_ref, bb_ref, cb_ref, gb_ref, rb_ref, h0_ref,
                yf_ref, yb_ref, ht_ref, h_scr, *, nct, nb):
    j = pl.program_id(1)

    @pl.when(j == 0)
    def _():
        h_scr[...] = h0_ref[...]

    row_i = lax.broadcasted_iota(jnp.int32, (CHUNK, SSM_GW), 0)
    col_j = lax.broadcasted_iota(jnp.int32, (CHUNK, SSM_GW), 1) % SSM_P
    incl_dir = (row_i >= col_j, row_i <= col_j)
    bd_rows = lax.broadcasted_iota(jnp.int32, (SSM_HPG * CHUNK, SSM_GW), 0) // CHUNK
    bd_cols = lax.broadcasted_iota(jnp.int32, (SSM_HPG * CHUNK, SSM_GW), 1) // SSM_P
    block_diag = bd_rows == bd_cols
    dir_refs = ((xf_ref, bf_ref, cf_ref, gf_ref, rf_ref, yf_ref),
                (xb_ref, bb_ref, cb_ref, gb_ref, rb_ref, yb_ref))

    def body(c, carry):
        chains = []
        for n in range(nb):
            for d in range(2):
                cc = c if d == 0 else nct - 1 - c
                r0 = pl.multiple_of(cc * CHUNK, CHUNK)
                ct = dir_refs[d][3][n, pl.ds(r0, CHUNK), :]
                for g in range(SSM_G):
                    chains.append((d, g, cc, r0, ct, n))

        def load(chain, which, width):
            d, g, _, r0, _, n = chain
            return dir_refs[d][which][n, pl.ds(r0, CHUNK), g * width:(g + 1) * width]

        def pick(chain, off, rows=slice(None)):
            d, g, _, _, ct, _ = chain
            h0 = off + d * SSM_HEADS + g * SSM_HPG
            return _group_lane_select([ct[rows, h0 + hh:h0 + hh + 1] for hh in range(SSM_HPG)])

        def decay_of(chain):
            d, g, cc, _, _, n = chain
            incl = incl_dir[d]
            a_row = dir_refs[d][4][n, cc, d * SSM_G + g:d * SSM_G + g + 1, :]
            return jnp.where(incl, jnp.exp(jnp.where(incl, pick(chain, COL_A) - a_row, 0.0)), 0.0)

        cb4 = [_dot_nt(load(ch, 2, SSM_N), jnp.concatenate([load(ch, 1, SSM_N)] * SSM_HPG, axis=0)) for ch in chains]
        hg = [h_scr[ch[5], ch[0], ch[1]] for ch in chains]
        y_off = [_dot(load(ch, 2, SSM_N), h) for ch, h in zip(chains, hg)]
        h_in = [_dot_tn(load(ch, 1, SSM_N), load(ch, 0, SSM_GW) * pick(ch, COL_DTEAL)) for ch in chains]
        scores = [cb * decay_of(ch) for ch, cb in zip(chains, cb4)]
        xdt_bd = [jnp.where(block_diag,
                            jnp.concatenate([load(ch, 0, SSM_GW) * pick(ch, COL_DT)] * SSM_HPG, axis=0), 0.0)
                  for ch in chains]
        y_diag = [_dot(sc, xb) for sc, xb in zip(scores, xdt_bd)]
        for ch, yd, yo, h, hi in zip(chains, y_diag, y_off, hg, h_in):
            d, g, _, r0, _, n = ch
            dir_refs[d][5][n, pl.ds(r0, CHUNK), g * SSM_GW:(g + 1) * SSM_GW] = yd + yo * pick(ch, COL_EA)
            h_scr[n, d, g] = h * pick(ch, COL_EAT, slice(0, 1)) + hi
        return carry

    lax.fori_loop(0, nct, body, 0)

    @pl.when(j == pl.num_programs(1) - 1)
    def _():
        ht_ref[...] = h_scr[...]


def ssd_scan(feat, gcol, arow, h0, tt, nb):
    bsz, t, _ = feat.shape
    nt = t // tt
    nct = tt // CHUNK
    fwd = lambda lane_blk: (lambda b, j: (b, j, lane_blk))
    bwd = lambda lane_blk: (lambda b, j: (b, nt - 1 - j, lane_blk))

    def side(im):
        return [pl.BlockSpec((nb, tt, SSM_DI), im(OFF_X // SSM_DI)),
                pl.BlockSpec((nb, tt, SSM_BC), im(OFF_B // SSM_BC)),
                pl.BlockSpec((nb, tt, SSM_BC), im(OFF_C // SSM_BC)),
                pl.BlockSpec((nb, tt, LANES), im(0))]

    row_spec = lambda rev: pl.BlockSpec((nb, nct, 2 * SSM_G, SSM_GW),
                                        (lambda b, j: (b, nt - 1 - j, 0, 0)) if rev else (lambda b, j: (b, j, 0, 0)))
    state_spec = pl.BlockSpec((nb, 2, SSM_G, SSM_N, SSM_GW), lambda b, j: (b, 0, 0, 0, 0))
    return pl.pallas_call(
        functools.partial(_ssd_kernel, nct=nct, nb=nb),
        grid=(bsz // nb, nt),
        in_specs=side(fwd) + [row_spec(False)] + side(bwd) + [row_spec(True)] + [state_spec],
        out_specs=[pl.BlockSpec((nb, tt, SSM_DI), fwd(0)), pl.BlockSpec((nb, tt, SSM_DI), bwd(0)), state_spec],
        out_shape=[
            jax.ShapeDtypeStruct((bsz, t, SSM_DI), F32),
            jax.ShapeDtypeStruct((bsz, t, SSM_DI), F32),
            jax.ShapeDtypeStruct((bsz, 2, SSM_G, SSM_N, SSM_GW), F32),
        ],
        scratch_shapes=[pltpu.VMEM((nb, 2, SSM_G, SSM_N, SSM_GW), F32)],
        compiler_params=_compiler_params(("parallel", "arbitrary")),
        name="ssd_scan",
    )(feat, feat, feat, gcol, arow, feat, feat, feat, gcol, arow, h0)


def _merge_out_kernel(of_ref, ob_ref, yf_ref, yb_ref, xs_ref, z_ref, dng_ref, dskip_ref, ssg_ref,
                      w_ref, g_ref, gate_ref, res_ref, o_ref):
    o = of_ref[0] + ob_ref[0]
    z = z_ref[0]
    parts = []
    for h in range(DN_HEADS):
        lanes = slice(h * DN_DV, (h + 1) * DN_DV)
        parts.append(_rms(o[:, lanes]) * dng_ref[...] * _silu(z[:, lanes]))
    y = yf_ref[0] + yb_ref[0] + dskip_ref[...] * xs_ref[0]
    yz = y * _silu(z[:, DN_VW:])
    gw = SSM_DI // SSM_G
    for g in range(SSM_G):
        lanes = slice(g * gw, (g + 1) * gw)
        parts.append(_rms(yz[:, lanes]) * ssg_ref[:, lanes])
    mixed = jnp.concatenate(parts, axis=1).astype(BF16)
    m = jnp.dot(mixed, w_ref[...], preferred_element_type=F32)
    o_ref[0] = res_ref[0] + gate_ref[0] * (_rms(m) * g_ref[...])


def merge_out(o_f, o_b, y_f, y_b, feat, z, dn_norm_g, dskip, ssm_norm_g, w_out, g, gate, res, tm):
    bsz, t, d = res.shape
    row = lambda b, i: (b, i, 0)
    const = lambda b, i: (0, 0)
    half = lambda: pl.BlockSpec((1, tm, DN_VW), row)
    return pl.pallas_call(
        _merge_out_kernel,
        grid=(bsz, t // tm),
        in_specs=[
            half(), half(), half(), half(),
            pl.BlockSpec((1, tm, SSM_DI), lambda b, i: (b, i, OFF_X // SSM_DI)),
            pl.BlockSpec((1, tm, D_Z), row),
            pl.BlockSpec((1, DN_DV), const),
            pl.BlockSpec((1, SSM_DI), const),
            pl.BlockSpec((1, SSM_DI), const),
            pl.BlockSpec((D_MIX, d), const),
            pl.BlockSpec((1, d), const),
            pl.BlockSpec((1, 1, d), lambda b, i: (b, 0, 0)),
            pl.BlockSpec((1, tm, d), row),
        ],
        out_specs=pl.BlockSpec((1, tm, d), row),
        out_shape=jax.ShapeDtypeStruct((bsz, t, d), F32),
        compiler_params=_compiler_params(("parallel", "parallel")),
        name="merge_out",
    )(o_f, o_b, y_f, y_b, feat, z, dn_norm_g.reshape(1, DN_DV), dskip, ssm_norm_g.reshape(1, SSM_DI),
      w_out, g.reshape(1, d), gate.reshape(bsz, 1, d), res)


MOE_TILE = 256
SLOT_WINDOW = 64
SLOT_ALIGN = 16
CNT_LANES = 32
GATE_PIECES = 3
D_EXT = D_MODEL + LANES


def _router_kernel(x_ref, g_ref, shift_ref, scale_ref, wr_ref, hx_ref, aff_ref):
    d = x_ref.shape[2]
    h = _rms(x_ref[0]) * g_ref[...]
    hb = (h * (1.0 + scale_ref[0]) + shift_ref[0]).astype(BF16)
    logits = lax.dot_general(wr_ref[...], hb, (((1,), (1,)), ((), ())), preferred_element_type=F32)
    ex = jnp.exp(logits - jnp.max(logits, axis=0, keepdims=True))
    aff = ex / jnp.sum(ex, axis=0, keepdims=True)
    aff_ref[0, 0] = aff
    pieces, rest = [], aff
    for _ in range(GATE_PIECES):
        p = rest.astype(BF16).astype(F32)
        pieces.append(p)
        rest = rest - p
    rows = jnp.concatenate(pieces + [jnp.zeros((LANES - GATE_PIECES * N_EXPERTS, MOE_TILE), F32)], axis=0)
    ext = jnp.concatenate([rows[:, m * LANES:(m + 1) * LANES].T for m in range(MOE_TILE // LANES)], axis=0)
    hx_ref[0, :, :d] = hb
    hx_ref[0, :, d:] = ext.astype(BF16)


def router(x, g, shift, scale, wr_t):
    bsz, t, d = x.shape
    nt = t // MOE_TILE
    const = lambda b, j: (0, 0)
    per_b = lambda b, j: (b, 0, 0)
    return pl.pallas_call(
        _router_kernel,
        grid=(bsz, nt),
        in_specs=[
            pl.BlockSpec((1, MOE_TILE, d), lambda b, j: (b, j, 0)),
            pl.BlockSpec((1, d), const),
            pl.BlockSpec((1, 1, d), per_b),
            pl.BlockSpec((1, 1, d), per_b),
            pl.BlockSpec((N_EXPERTS, d), const),
        ],
        out_specs=[
            pl.BlockSpec((1, MOE_TILE, D_EXT), lambda b, j: (b, j, 0)),
            pl.BlockSpec((1, 1, N_EXPERTS, MOE_TILE), lambda b, j: (b, j, 0, 0)),
        ],
        out_shape=[
            jax.ShapeDtypeStruct((bsz, t, D_EXT), BF16),
            jax.ShapeDtypeStruct((bsz, nt, N_EXPERTS, MOE_TILE), F32),
        ],
        compiler_params=_compiler_params(("parallel", "parallel")),
        name="router",
    )(x, g.reshape(1, d), shift.reshape(bsz, 1, d), scale.reshape(bsz, 1, d), wr_t)


def _lane_prefix(mask, upper):
    carry = jnp.zeros((mask.shape[0], 1), F32)
    blocks = []
    for m in range(mask.shape[1] // LANES):
        p = jnp.dot(mask[:, m * LANES:(m + 1) * LANES].astype(BF16), upper, preferred_element_type=F32) + carry
        carry = p[:, LANES - 1:LANES]
        blocks.append(p)
    return jnp.concatenate(blocks, axis=1)


def _select_kernel(aff_ref, pos_ref, posc_ref, cnt_ref, *, nt, cap):
    aff = jnp.concatenate([aff_ref[0, j] for j in range(nt)], axis=1)
    bits = pltpu.bitcast(aff, jnp.int32)

    def bisect(i, thr):
        cand = thr | lax.shift_left(jnp.int32(1), jnp.int32(30) - i)
        n_ge = jnp.sum(jnp.where(bits >= cand, 1.0, 0.0), axis=1, keepdims=True)
        return jnp.where(n_ge >= cap, cand, thr)

    thr = lax.fori_loop(0, 31, bisect, jnp.zeros((N_EXPERTS, 1), jnp.int32))
    above = bits > thr
    tied = bits == thr
    room = cap - jnp.sum(jnp.where(above, 1.0, 0.0), axis=1, keepdims=True)
    src = lax.broadcasted_iota(jnp.int32, (LANES, LANES), 0)
    dst = lax.broadcasted_iota(jnp.int32, (LANES, LANES), 1)
    upper = jnp.where(src <= dst, 1.0, 0.0).astype(BF16)
    tie_rank = _lane_prefix(jnp.where(tied, 1.0, 0.0), upper)
    sel = above | (tied & (tie_rank <= room))
    selc = jnp.where(sel, 1.0, 0.0)
    slot = jnp.where(sel, _lane_prefix(selc, upper) - 1.0, -1.0)
    lane = lax.broadcasted_iota(jnp.int32, (N_EXPERTS, LANES), 1)
    before = jnp.zeros((N_EXPERTS, 1), F32)
    counts = jnp.zeros((N_EXPERTS, LANES), F32)
    for j in range(nt):
        pos_ref[0, j] = slot[:, j * MOE_TILE:(j + 1) * MOE_TILE]
        before = before + jnp.sum(selc[:, j * MOE_TILE:(j + 1) * MOE_TILE], axis=1, keepdims=True)
        counts = jnp.where(lane == j + 1, before, counts)
    cnt_ref[0] = counts.astype(jnp.int32)
    pad = jnp.zeros((LANES - N_EXPERTS, LANES), F32)
    for m in range(slot.shape[1] // LANES):
        posc_ref[0, m * LANES:(m + 1) * LANES, :] = jnp.concatenate([slot[:, m * LANES:(m + 1) * LANES], pad], axis=0).T


def select_slots(aff, cap):
    bsz, nt, ne, tile = aff.shape
    blk = lambda b: (b, 0, 0, 0)
    return pl.pallas_call(
        functools.partial(_select_kernel, nt=nt, cap=cap),
        grid=(bsz,),
        in_specs=[pl.BlockSpec((1, nt, ne, tile), blk)],
        out_specs=[pl.BlockSpec((1, nt, ne, tile), blk),
                   pl.BlockSpec((1, nt * tile, LANES), lambda b: (b, 0, 0)),
                   pl.BlockSpec((1, ne, LANES), lambda b: (b, 0, 0))],
        out_shape=[jax.ShapeDtypeStruct((bsz, nt, ne, tile), F32),
                   jax.ShapeDtypeStruct((bsz, nt * tile, LANES), F32),
                   jax.ShapeDtypeStruct((bsz, ne, LANES), jnp.int32)],
        compiler_params=_compiler_params(("parallel",)),
        name="select_slots",
    )(aff)


def _window_start(cnt_ref, row, j):
    return pl.multiple_of((cnt_ref[row, j] // SLOT_ALIGN) * SLOT_ALIGN, SLOT_ALIGN)


def _extra_windows(cnt_ref, row, j, start):
    return (cnt_ref[row, j + 1] - start - 1) // SLOT_WINDOW


def _gather_kernel(cnt_ref, hx_ref, pos_ref, xs_ref):
    b = pl.program_id(0)
    j = pl.program_id(1)

    @pl.when(j == 0)
    def _():
        xs_ref[...] = jnp.zeros_like(xs_ref)

    hx = hx_ref[0]
    slot_iota = lax.broadcasted_iota(jnp.int32, (SLOT_WINDOW, MOE_TILE), 0).astype(F32)

    def one_hot(e, start):
        return jnp.where(pos_ref[0, 0, e:e + 1, :] - start.astype(F32) == slot_iota, 1.0, 0.0).astype(BF16)

    starts = [_window_start(cnt_ref, b * N_EXPERTS + e, j) for e in range(N_EXPERTS)]
    stacked = jnp.concatenate([one_hot(e, starts[e]) for e in range(N_EXPERTS)], axis=0)
    rows = jnp.dot(stacked, hx, preferred_element_type=F32)
    for e in range(N_EXPERTS):
        xs_ref[0, e, pl.ds(starts[e], SLOT_WINDOW), :] += rows[e * SLOT_WINDOW:(e + 1) * SLOT_WINDOW].astype(BF16)

    for e in range(N_EXPERTS):
        n_more = _extra_windows(cnt_ref, b * N_EXPERTS + e, j, starts[e])

        @pl.when(n_more > 0)
        def _(e=e, n_more=n_more):
            def more(i, carry):
                start = pl.multiple_of(starts[e] + i * SLOT_WINDOW, SLOT_ALIGN)
                r = jnp.dot(one_hot(e, start), hx, preferred_element_type=F32)
                xs_ref[0, e, pl.ds(start, SLOT_WINDOW), :] += r.astype(BF16)
                return carry
            lax.fori_loop(1, n_more + 1, more, 0)


def gather_tokens(cnt, hx, pos, cap):
    bsz, t, d_ext = hx.shape
    nt = t // MOE_TILE
    rows = cap + SLOT_WINDOW
    grid_spec = pltpu.PrefetchScalarGridSpec(
        num_scalar_prefetch=1,
        grid=(bsz, nt),
        in_specs=[
            pl.BlockSpec((1, MOE_TILE, d_ext), lambda b, j, c: (b, j, 0)),
            pl.BlockSpec((1, 1, N_EXPERTS, MOE_TILE), lambda b, j, c: (b, j, 0, 0)),
        ],
        out_specs=pl.BlockSpec((1, N_EXPERTS, rows, d_ext), lambda b, j, c: (b, 0, 0, 0),
                               pipeline_mode=pl.Buffered(1)),
    )
    return pl.pallas_call(
        _gather_kernel,
        grid_spec=grid_spec,
        out_shape=jax.ShapeDtypeStruct((bsz, N_EXPERTS, rows, d_ext), BF16),
        compiler_params=_compiler_params(("parallel", "arbitrary")),
        name="gather_tokens",
    )(cnt, hx, pos)


def _expert_ffn_kernel(xs_ref, wg_ref, wu_ref, wd_ref, o_ref):
    e = pl.program_id(0)
    d = wg_ref.shape[1]
    xs = xs_ref[0, 0, :, :d]
    ext = xs_ref[0, 0, :, d:].astype(F32)
    lane = lax.broadcasted_iota(jnp.int32, ext.shape, 1)
    mine = (lane % N_EXPERTS == e) & (lane < GATE_PIECES * N_EXPERTS)
    gate = jnp.sum(jnp.where(mine, ext, 0.0), axis=1, keepdims=True)
    a = jnp.dot(xs, wg_ref[0], preferred_element_type=F32)
    u = jnp.dot(xs, wu_ref[0], preferred_element_type=F32)
    hid = (_silu(a) * u).astype(BF16)
    o_ref[0, 0] = (jnp.dot(hid, wd_ref[0], preferred_element_type=F32) * gate).astype(BF16)


def expert_ffn(xs, wg, wu, wd):
    bsz, ne, rows, d_ext = xs.shape
    _, d, f = wg.shape
    return pl.pallas_call(
        _expert_ffn_kernel,
        grid=(ne, bsz),
        in_specs=[
            pl.BlockSpec((1, 1, rows, d_ext), lambda e, b: (b, e, 0, 0)),
            pl.BlockSpec((1, d, f), lambda e, b: (e, 0, 0)),
            pl.BlockSpec((1, d, f), lambda e, b: (e, 0, 0)),
            pl.BlockSpec((1, f, d), lambda e, b: (e, 0, 0)),
        ],
        out_specs=pl.BlockSpec((1, 1, rows, d), lambda e, b: (b, e, 0, 0)),
        out_shape=jax.ShapeDtypeStruct((bsz, ne, rows, d), BF16),
        compiler_params=_compiler_params(("parallel", "parallel")),
        name="expert_ffn",
    )(xs, wg, wu, wd)


def _scatter_out_kernel(cnt_ref, og_ref, posc_ref, g_ref, gate_ref, res_ref, o_ref):
    b = pl.program_id(0)
    j = pl.program_id(1)
    posc = posc_ref[0]
    starts = [_window_start(cnt_ref, b * N_EXPERTS + e, j) for e in range(N_EXPERTS)]
    lane = lax.broadcasted_iota(jnp.int32, (MOE_TILE, LANES), 1)
    lane_f = lane.astype(F32)
    per_tile = LANES // SLOT_WINDOW
    tiles = []
    for m in range(N_EXPERTS // per_tile):
        rel = None
        for q in range(per_tile - 1, -1, -1):
            e = m * per_tile + q
            v = posc[:, e:e + 1] - starts[e].astype(F32) + float(q * SLOT_WINDOW)
            rel = v if rel is None else jnp.where(lane < (q + 1) * SLOT_WINDOW, v, rel)
        tiles.append(jnp.where(rel == lane_f, 1.0, 0.0).astype(BF16))
    sel_t = jnp.concatenate(tiles, axis=1)
    og = jnp.concatenate([og_ref[0, e, pl.ds(starts[e], SLOT_WINDOW), :] for e in range(N_EXPERTS)], axis=0)
    o_ref[0] = jnp.dot(sel_t, og, preferred_element_type=F32)

    win_lane = lax.broadcasted_iota(jnp.int32, (MOE_TILE, SLOT_WINDOW), 1).astype(F32)
    for e in range(N_EXPERTS):
        n_more = _extra_windows(cnt_ref, b * N_EXPERTS + e, j, starts[e])

        @pl.when(n_more > 0)
        def _(e=e, n_more=n_more):
            def more(i, carry):
                start = pl.multiple_of(starts[e] + i * SLOT_WINDOW, SLOT_ALIGN)
                p = jnp.where(posc[:, e:e + 1] - start.astype(F32) == win_lane, 1.0, 0.0).astype(BF16)
                o_ref[0] += jnp.dot(p, og_ref[0, e, pl.ds(start, SLOT_WINDOW), :], preferred_element_type=F32)
                return carry
            lax.fori_loop(1, n_more + 1, more, 0)

    o_ref[0] = res_ref[0] + gate_ref[0] * (_rms(o_ref[0]) * g_ref[...])


def scatter_out(cnt, og, posc, g, gate, res):
    bsz, t, d = res.shape
    nt = t // MOE_TILE
    _, ne, rows, _ = og.shape
    grid_spec = pltpu.PrefetchScalarGridSpec(
        num_scalar_prefetch=1,
        grid=(bsz, nt),
        in_specs=[
            pl.BlockSpec((1, ne, rows, d), lambda b, j, c: (b, 0, 0, 0), pipeline_mode=pl.Buffered(1)),
            pl.BlockSpec((1, MOE_TILE, LANES), lambda b, j, c: (b, j, 0)),
            pl.BlockSpec((1, d), lambda b, j, c: (0, 0)),
            pl.BlockSpec((1, 1, d), lambda b, j, c: (b, 0, 0)),
            pl.BlockSpec((1, MOE_TILE, d), lambda b, j, c: (b, j, 0)),
        ],
        out_specs=pl.BlockSpec((1, MOE_TILE, d), lambda b, j, c: (b, j, 0)),
    )
    return pl.pallas_call(
        _scatter_out_kernel,
        grid_spec=grid_spec,
        out_shape=jax.ShapeDtypeStruct((bsz, t, d), F32),
        compiler_params=_compiler_params(("parallel", "parallel")),
        name="scatter_out",
    )(cnt, og, posc, g.reshape(1, d), gate.reshape(bsz, 1, d), res)


def ec_moe_residual(x, g_in, shift, scale, wr_t, wg, wu, wd, g_out, gate):
    bsz, t, _ = x.shape
    cap = EC_CAPACITY_FACTOR * t // N_EXPERTS
    hx, aff = router(x, g_in, shift, scale, wr_t)
    pos, posc, counts = select_slots(aff, cap)
    cnt = counts[:, :, :CNT_LANES].reshape(bsz * N_EXPERTS, CNT_LANES)
    xs = gather_tokens(cnt, hx, pos, cap)
    og = expert_ffn(xs, wg, wu, wd)
    return scatter_out(cnt, og, posc, g_out, gate, x)


def _token_tile(t):
    return min(512, t)


def mixer_stream(x, g0, shift, scale, wc, wz, ws, w9, conv_b, bias_rows, nega_rows, dn_state, ssm_state, on_grid):
    t = x.shape[1]
    tm = _token_tile(t)
    conv_in, z, small = in_proj(x, g0, shift, scale, wc, wz, ws, tm)
    feat = conv_features(conv_in, w9, conv_b, on_grid)
    gcol, grow, arow = gates(small, bias_rows, nega_rows, tm)
    ts = min(SCAN_TILE, t)
    o_f, o_b, dn_state = delta_scan(feat, gcol, grow, dn_state, ts, SCAN_SAMPLES_PER_STEP)
    y_f, y_b, ssm_state = ssd_scan(feat, gcol, arow, ssm_state, ts, SCAN_SAMPLES_PER_STEP)
    return (o_f, o_b, y_f, y_b, feat, z), dn_state, ssm_state


def _gate_param_rows(dn_bias, dn_a_log, ssm_bias, ssm_a_log):
    zeros = jnp.zeros((NDH,), F32)
    bias = jnp.concatenate([zeros, dn_bias.reshape(-1), ssm_bias.reshape(-1)])
    nega = jnp.concatenate([zeros, -jnp.exp(dn_a_log.reshape(-1)), -jnp.exp(ssm_a_log.reshape(-1))])
    pad = LANES - bias.shape[0]
    expand = lambda v: jnp.broadcast_to(jnp.pad(v, (0, pad))[:, None], (LANES, LANES))
    return expand(bias), expand(nega)


def kernel(x, c, ctx, c_ctx, ada_w, ada_b, norm_g, w_in, conv_w, conv_b, dn_A_log, dn_dt_bias,
           dn_norm_g, ssm_A_log, ssm_dt_bias, ssm_D, ssm_norm_g, w_out, router_w,
           exp_w_gate, exp_w_up, exp_w_down):
    bsz = x.shape[0]
    s_lat = jax.nn.silu(c)
    s_ctx = jax.nn.silu(c_ctx)
    for l in range(DEPTH):
        last = l == DEPTH - 1
        mod_lat = jnp.split(s_lat @ ada_w[l] + ada_b[l], 6, axis=-1)
        mod_ctx_row = s_ctx @ ada_w[l] + ada_b[l]
        mod_ctx = [jnp.broadcast_to(m[None, :], (bsz, D_MODEL)) for m in jnp.split(mod_ctx_row, 6)]

        w_in_l = w_in[l]
        wc = w_in_l[:, :CONV_CH].astype(BF16)
        wz = w_in_l[:, CONV_CH:CONV_CH + D_Z].astype(BF16)
        ws = jnp.pad(w_in_l[:, CONV_CH + D_Z:], ((0, 0), (0, LANES - N_GATE_COLS))).astype(BF16)
        w9 = conv_w[l].reshape(CONV_CH, CONV_K * CONV_K).T
        cb = conv_b[l].reshape(1, CONV_CH)
        bias_rows, nega_rows = _gate_param_rows(dn_dt_bias[l], dn_A_log[l], ssm_dt_bias[l], ssm_A_log[l])
        dskip = jnp.repeat(ssm_D[l], SSM_P).reshape(1, SSM_DI)
        w_out_l = w_out[l].astype(BF16)
        moe_w = (router_w[l].T.astype(BF16), exp_w_gate[l].astype(BF16), exp_w_up[l].astype(BF16),
                 exp_w_down[l].astype(BF16), norm_g[l, 3])

        dn0 = jnp.zeros((bsz, 2, DN_HEADS, DN_DK, DN_DV), F32)
        ssm0 = jnp.zeros((bsz, 2, SSM_G, SSM_N, SSM_GW), F32)
        shared = (wc, wz, ws, w9, cb, bias_rows, nega_rows)
        mix_ctx, dn_c, ssm_c = mixer_stream(ctx, norm_g[l, 0], mod_ctx[0], mod_ctx[1], *shared, dn0, ssm0, False)
        mix_lat, _, _ = mixer_stream(x, norm_g[l, 0], mod_lat[0], mod_lat[1], *shared, dn_c, ssm_c, True)
        merge_w = (dn_norm_g[l], dskip, ssm_norm_g[l], w_out_l, norm_g[l, 1])
        x = merge_out(*mix_lat, *merge_w, mod_lat[2], x, _token_tile(x.shape[1]))

        x = ec_moe_residual(x, norm_g[l, 2], mod_lat[3], mod_lat[4], *moe_w, mod_lat[5])

        if not last:
            ctx = merge_out(*mix_ctx, *merge_w, mod_ctx[2], ctx, _token_tile(ctx.shape[1]))
            ctx = ec_moe_residual(ctx, norm_g[l, 2], mod_ctx[3], mod_ctx[4], *moe_w, mod_ctx[5])
    return x
```

```python
import functools

import jax
import jax.numpy as jnp
import numpy as np
from jax import lax
from jax.experimental import pallas as pl
from jax.experimental.pallas import tpu as pltpu

D_MODEL = 1024
DEPTH = 4
GRID_W = 64
DN_HEADS = 4
DN_DK = 128
DN_DV = 128
SSM_HEADS = 8
SSM_P = 64
SSM_N = 128
SSM_G = 2
CHUNK = 64
CONV_K = 3
N_EXPERTS = 16
EC_CAPACITY_FACTOR = 2
D_EXPERT = 512
EPS = 1e-6

DN_QK = DN_HEADS * DN_DK
DN_VW = DN_HEADS * DN_DV
SSM_DI = SSM_HEADS * SSM_P
SSM_BC = SSM_G * SSM_N
SSM_HPG = SSM_HEADS // SSM_G
SSM_GW = SSM_HPG * SSM_P
D_MIX = DN_VW + SSM_DI
CONV_SPLITS = (DN_QK, DN_QK, DN_VW, SSM_DI, SSM_BC, SSM_BC)
CONV_CH = sum(CONV_SPLITS)
D_Z = DN_VW + SSM_DI
N_GATE_COLS = 2 * DN_HEADS + 2 * DN_HEADS + 2 * SSM_HEADS
D_IN_PROJ = CONV_CH + D_Z + N_GATE_COLS

LANES = 128
SUBLANES = 8
VMEM_LIMIT_BYTES = 56 * 1024 * 1024

OFF_Q, OFF_K, OFF_V = 0, DN_QK, 2 * DN_QK
OFF_X = 2 * DN_QK + DN_VW
OFF_B = OFF_X + SSM_DI
OFF_C = OFF_B + SSM_BC

NDH = 2 * DN_HEADS
NSH = 2 * SSM_HEADS
COL_BETA, COL_G, COL_EG, COL_EGL, COL_EGT = (i * NDH for i in range(5))
COL_DT, COL_A, COL_EA, COL_DTEAL, COL_EAT = (5 * NDH + i * NSH for i in range(5))

F32 = jnp.float32
BF16 = jnp.bfloat16


def _compiler_params(semantics):
    return pltpu.CompilerParams(dimension_semantics=semantics, vmem_limit_bytes=VMEM_LIMIT_BYTES)


def _rms(x):
    return x * lax.rsqrt(jnp.mean(x * x, axis=-1, keepdims=True) + EPS)


def _silu(x):
    return x * jax.nn.sigmoid(x)


def _softplus(x):
    return jnp.maximum(x, 0.0) + jnp.log(1.0 + jnp.exp(-jnp.abs(x)))


def _dot(a, b):
    return jnp.dot(a.astype(BF16), b.astype(BF16), preferred_element_type=F32)


def _dot_nt(a, b):
    return lax.dot_general(a.astype(BF16), b.astype(BF16), (((1,), (1,)), ((), ())),
                           preferred_element_type=F32)


def _dot_tn(a, b):
    return lax.dot_general(a.astype(BF16), b.astype(BF16), (((0,), (0,)), ((), ())),
                           preferred_element_type=F32)


def _dot_exact01(a, m01):
    a1 = a.astype(BF16)
    r1 = a - a1.astype(F32)
    a2 = r1.astype(BF16)
    a3 = (r1 - a2.astype(F32)).astype(BF16)
    m = m01.astype(BF16)
    out = jnp.dot(a3, m, preferred_element_type=F32)
    out = out + jnp.dot(a2, m, preferred_element_type=F32)
    return out + jnp.dot(a1, m, preferred_element_type=F32)


def _in_proj_kernel(x_ref, g_ref, shift_ref, scale_ref, wc_ref, wz_ref, ws_ref, oc_ref, oz_ref, os_ref):
    h = _rms(x_ref[0]) * g_ref[...]
    h = (h * (1.0 + scale_ref[0]) + shift_ref[0]).astype(BF16)
    oc_ref[0] = jnp.dot(h, wc_ref[...], preferred_element_type=F32)
    oz_ref[0] = jnp.dot(h, wz_ref[...], preferred_element_type=F32)
    os_ref[0] = jnp.dot(h, ws_ref[...], preferred_element_type=F32)


def in_proj(x, g, shift, scale, wc, wz, ws, tm):
    bsz, t, d = x.shape
    row = lambda b, i: (b, i, 0)
    const = lambda b, i: (0, 0)
    per_b = lambda b, i: (b, 0, 0)
    return pl.pallas_call(
        _in_proj_kernel,
        grid=(bsz, t // tm),
        in_specs=[
            pl.BlockSpec((1, tm, d), row),
            pl.BlockSpec((1, d), const),
            pl.BlockSpec((1, 1, d), per_b),
            pl.BlockSpec((1, 1, d), per_b),
            pl.BlockSpec((d, CONV_CH), const),
            pl.BlockSpec((d, D_Z), const),
            pl.BlockSpec((d, LANES), const),
        ],
        out_specs=[
            pl.BlockSpec((1, tm, CONV_CH), row),
            pl.BlockSpec((1, tm, D_Z), row),
            pl.BlockSpec((1, tm, LANES), row),
        ],
        out_shape=[
            jax.ShapeDtypeStruct((bsz, t, CONV_CH), F32),
            jax.ShapeDtypeStruct((bsz, t, D_Z), F32),
            jax.ShapeDtypeStruct((bsz, t, LANES), F32),
        ],
        compiler_params=_compiler_params(("parallel", "parallel")),
        name="in_proj",
    )(x, g.reshape(1, d), shift.reshape(bsz, 1, d), scale.reshape(bsz, 1, d), wc, wz, ws)


CONV_PAD = GRID_W + SUBLANES
CONV_TILE = 256


def _conv_feat_kernel(x_ref, w_ref, b_ref, o_ref, xp_ref, *, t, on_grid):
    j = pl.program_id(1)
    zeros = jnp.zeros((CONV_PAD, LANES), F32)
    xp_ref[0:CONV_PAD, :] = zeros
    xp_ref[CONV_PAD + t:CONV_PAD + t + CONV_PAD, :] = zeros
    xp_ref[CONV_PAD:CONV_PAD + t, :] = x_ref[0]

    tt = min(CONV_TILE, t)
    dys = (-1, 0, 1) if on_grid else (0,)
    col = lax.broadcasted_iota(jnp.int32, (tt, LANES), 0) % GRID_W
    bias = b_ref[...]
    w = w_ref[...]

    def conv_tile(i):
        base = pl.multiple_of(i * tt, tt)
        acc_c = acc_m = acc_p = None
        for dy in dys:
            win = xp_ref[pl.ds(base + CONV_PAD + dy * GRID_W - SUBLANES, tt + 2 * SUBLANES), :]
            wr = 3 * (dy + 1)
            tc = win[SUBLANES:SUBLANES + tt] * w[wr + 1:wr + 2]
            tm_, tp = win * w[wr:wr + 1], win * w[wr + 2:wr + 3]
            acc_c = tc if acc_c is None else acc_c + tc
            acc_m = tm_ if acc_m is None else acc_m + tm_
            acc_p = tp if acc_p is None else acc_p + tp
        acc_m = pltpu.roll(acc_m, 1, 0)[SUBLANES:SUBLANES + tt]
        acc_p = pltpu.roll(acc_p, tt + 2 * SUBLANES - 1, 0)[SUBLANES:SUBLANES + tt]
        if on_grid:
            acc_m = jnp.where(col != 0, acc_m, 0.0)
            acc_p = jnp.where(col != GRID_W - 1, acc_p, 0.0)
        return base, _silu(acc_c + acc_m + acc_p + bias)

    n_qk_blocks = 2 * DN_QK // LANES

    @pl.when(j < n_qk_blocks)
    def _():
        qscale = jnp.where(j < DN_QK // LANES, DN_DK ** -0.5, 1.0).astype(F32)

        def body(i, carry):
            base, u = conv_tile(i)
            nrm = lax.rsqrt(jnp.sum(u * u, axis=-1, keepdims=True) + EPS)
            o_ref[0, pl.ds(base, tt), :] = u * nrm * qscale
            return carry
        lax.fori_loop(0, t // tt, body, 0)

    @pl.when(j >= n_qk_blocks)
    def _():
        def body(i, carry):
            base, u = conv_tile(i)
            o_ref[0, pl.ds(base, tt), :] = u
            return carry
        lax.fori_loop(0, t // tt, body, 0)


def conv_features(conv_in, w9, bias, on_grid):
    bsz, t, ch = conv_in.shape
    blk = lambda b, j: (b, 0, j)
    return pl.pallas_call(
        functools.partial(_conv_feat_kernel, t=t, on_grid=on_grid),
        grid=(bsz, ch // LANES),
        in_specs=[
            pl.BlockSpec((1, t, LANES), blk),
            pl.BlockSpec((9, LANES), lambda b, j: (0, j)),
            pl.BlockSpec((1, LANES), lambda b, j: (0, j)),
        ],
        out_specs=pl.BlockSpec((1, t, LANES), blk),
        out_shape=jax.ShapeDtypeStruct((bsz, t, ch), F32),
        scratch_shapes=[pltpu.VMEM((t + 2 * CONV_PAD, LANES), F32)],
        compiler_params=_compiler_params(("parallel", "parallel")),
        name="conv_features",
    )(conv_in, w9, bias)


def _gates_kernel(s_ref, bias_ref, nega_ref, col_ref, grow_ref, arow_ref, *, tg):
    tok = lax.broadcasted_iota(jnp.int32, (LANES, LANES), 0)
    out = lax.broadcasted_iota(jnp.int32, (LANES, LANES), 1)
    same = (tok // CHUNK) == (out // CHUNK)
    m_fwd = jnp.where(same & (tok <= out), 1.0, 0.0)
    m_bwd = jnp.where(same & (tok >= out), 1.0, 0.0)
    m_all = jnp.where(same, 1.0, 0.0)
    bias = bias_ref[...]
    nega = nega_ref[...]

    def dir_cumsum(v, heads):
        f = _dot_exact01(v, m_fwd)
        b = _dot_exact01(v, m_bwd)
        is_fwd = lax.broadcasted_iota(jnp.int32, v.shape, 0) < heads
        return jnp.where(is_fwd, f, b), _dot_exact01(v, m_all)

    for s in range(tg // LANES):
        st = s_ref[0, s * LANES:(s + 1) * LANES, :].T
        beta = jax.nn.sigmoid(st[0:NDH])
        sp = _softplus(st[NDH:2 * NDH + NSH] + bias[NDH:2 * NDH + NSH])
        logg = sp[0:NDH] * nega[NDH:2 * NDH]
        dt = sp[NDH:]
        a = dt * nega[2 * NDH:2 * NDH + NSH]
        g_cs, g_tot = dir_cumsum(logg, DN_HEADS)
        a_cs, a_tot = dir_cumsum(a, SSM_HEADS)
        rows = jnp.concatenate([
            beta, g_cs, jnp.exp(g_cs), jnp.exp(g_tot - g_cs), jnp.exp(g_tot),
            dt, a_cs, jnp.exp(a_cs), dt * jnp.exp(a_tot - a_cs), jnp.exp(a_tot),
            jnp.zeros((LANES - 5 * NDH - 5 * NSH, LANES), F32)], axis=0)
        col_ref[0, s * LANES:(s + 1) * LANES, :] = rows.T
        for half in range(LANES // CHUNK):
            c = s * (LANES // CHUNK) + half
            lo = half * CHUNK
            grow_ref[0, c] = g_cs[:, lo:lo + CHUNK]
            for dg in range(2 * SSM_G):
                r0 = dg * SSM_HPG
                arow_ref[0, c, dg:dg + 1, :] = jnp.concatenate(
                    [a_cs[r0 + hh:r0 + hh + 1, lo:lo + CHUNK] for hh in range(SSM_HPG)], axis=1)


def gates(small, bias_rows, nega_rows, tg):
    bsz, t, _ = small.shape
    nc = t // CHUNK
    ncg = tg // CHUNK
    const = lambda b, i: (0, 0)
    return pl.pallas_call(
        functools.partial(_gates_kernel, tg=tg),
        grid=(bsz, t // tg),
        in_specs=[
            pl.BlockSpec((1, tg, LANES), lambda b, i: (b, i, 0)),
            pl.BlockSpec((LANES, LANES), const),
            pl.BlockSpec((LANES, LANES), const),
        ],
        out_specs=[
            pl.BlockSpec((1, tg, LANES), lambda b, i: (b, i, 0)),
            pl.BlockSpec((1, ncg, NDH, CHUNK), lambda b, i: (b, i, 0, 0)),
            pl.BlockSpec((1, ncg, 2 * SSM_G, SSM_GW), lambda b, i: (b, i, 0, 0)),
        ],
        out_shape=[
            jax.ShapeDtypeStruct((bsz, t, LANES), F32),
            jax.ShapeDtypeStruct((bsz, nc, NDH, CHUNK), F32),
            jax.ShapeDtypeStruct((bsz, nc, 2 * SSM_G, SSM_GW), F32),
        ],
        compiler_params=_compiler_params(("parallel", "parallel")),
        name="gates",
    )(small, bias_rows, nega_rows)


SCAN_SAMPLES_PER_STEP = 4
SCAN_TILE = 256


def _chunk_masks(rev):
    r = lax.broadcasted_iota(jnp.int32, (CHUNK, CHUNK), 0)
    c = lax.broadcasted_iota(jnp.int32, (CHUNK, CHUNK), 1)
    return ((r <= c), (r < c)) if rev else ((r >= c), (r > c))


def _unit_tri_inverse(ms):
    eye = (lax.broadcasted_iota(jnp.int32, (CHUNK, CHUNK), 0)
           == lax.broadcasted_iota(jnp.int32, (CHUNK, CHUNK), 1)).astype(F32)
    ps = [_dot(m, m) for m in ms]
    ts = [eye + m for m in ms]
    n_sq = int(np.log2(CHUNK)) - 1
    for _ in range(n_sq - 1):
        rs = [_dot(jnp.concatenate([t, p], axis=0), p) for t, p in zip(ts, ps)]
        ts = [t + r[:CHUNK] for t, r in zip(ts, rs)]
        ps = [r[CHUNK:] for r in rs]
    return [t + _dot(t, p) for t, p in zip(ts, ps)]


def _dn_kernel(qf_ref, kf_ref, vf_ref, cf_ref, rf_ref, qb_ref, kb_ref, vb_ref, cb_ref, rb_ref, s0_ref,
               of_ref, ob_ref, st_ref, s_scr, *, nct, nb):
    j = pl.program_id(1)

    @pl.when(j == 0)
    def _():
        s_scr[...] = s0_ref[...]

    masks = (_chunk_masks(False), _chunk_masks(True))
    dir_refs = ((qf_ref, kf_ref, vf_ref, cf_ref, rf_ref, of_ref),
                (qb_ref, kb_ref, vb_ref, cb_ref, rb_ref, ob_ref))

    def body(c, carry):
        chains = []
        for n in range(nb):
            for d in range(2):
                cc = c if d == 0 else nct - 1 - c
                r0 = pl.multiple_of(cc * CHUNK, CHUNK)
                ct = dir_refs[d][3][n, pl.ds(r0, CHUNK), :]
                for h in range(DN_HEADS):
                    chains.append((d, h, cc, r0, ct, n))

        def load(chain, which):
            d, h, _, r0, _, n = chain
            return dir_refs[d][which][n, pl.ds(r0, CHUNK), h * DN_DK:(h + 1) * DN_DK]

        def col(chain, off, rows=slice(None)):
            d, h, _, _, ct, _ = chain
            lane = off + d * DN_HEADS + h
            return ct[rows, lane:lane + 1]

        def decay_of(chain):
            d, h, cc, _, _, n = chain
            incl = masks[d][0]
            g_row = dir_refs[d][4][n, cc, d * DN_HEADS + h:d * DN_HEADS + h + 1, :]
            return jnp.where(incl, jnp.exp(jnp.where(incl, col(chain, COL_G) - g_row, 0.0)), 0.0)

        kbeta = [load(ch, 1) * col(ch, COL_BETA) for ch in chains]
        a = [_dot_nt(jnp.concatenate([kb, load(ch, 0)], axis=0), load(ch, 1)) for ch, kb in zip(chains, kbeta)]
        decay = [decay_of(ch) for ch in chains]
        neg_l = [jnp.where(masks[ch[0]][1], -(ai[:CHUNK] * dc), 0.0) for ch, ai, dc in zip(chains, a, decay)]
        qk = [ai[CHUNK:] * dc for ai, dc in zip(a, decay)]
        t_inv = _unit_tri_inverse(neg_l)
        uw = [_dot(ti, jnp.concatenate([load(ch, 2) * col(ch, COL_BETA), kb * col(ch, COL_EG)], axis=1))
              for ch, ti, kb in zip(chains, t_inv, kbeta)]
        s = [s_scr[ch[5], ch[0], ch[1]] for ch in chains]
        ws_qs = [_dot(jnp.concatenate([uwi[:, DN_DV:], load(ch, 0) * col(ch, COL_EG)], axis=0), si)
                 for ch, uwi, si in zip(chains, uw, s)]
        v_new = [uwi[:, :DN_DV] - wq[:CHUNK] for uwi, wq in zip(uw, ws_qs)]
        o = [wq[CHUNK:] + _dot(qki, vn) for wq, qki, vn in zip(ws_qs, qk, v_new)]
        s_new = [si * col(ch, COL_EGT, slice(0, 1)) + _dot_tn(load(ch, 1) * col(ch, COL_EGL), vn)
                 for ch, si, vn in zip(chains, s, v_new)]
        for ch, oi, sn in zip(chains, o, s_new):
            d, h, _, r0, _, n = ch
            dir_refs[d][5][n, pl.ds(r0, CHUNK), h * DN_DV:(h + 1) * DN_DV] = oi
            s_scr[n, d, h] = sn
        return carry

    lax.fori_loop(0, nct, body, 0)

    @pl.when(j == pl.num_programs(1) - 1)
    def _():
        st_ref[...] = s_scr[...]


def delta_scan(feat, gcol, grow, s0, tt, nb):
    bsz, t, _ = feat.shape
    nt = t // tt
    nct = tt // CHUNK
    fwd = lambda lane_blk: (lambda b, j: (b, j, lane_blk))
    bwd = lambda lane_blk: (lambda b, j: (b, nt - 1 - j, lane_blk))
    qkv_blk = (nb, tt, DN_QK)

    def side(im):
        return [pl.BlockSpec(qkv_blk, im(OFF_Q // DN_QK)), pl.BlockSpec(qkv_blk, im(OFF_K // DN_QK)),
                pl.BlockSpec(qkv_blk, im(OFF_V // DN_QK)), pl.BlockSpec((nb, tt, LANES), im(0))]

    row_spec = lambda rev: pl.BlockSpec((nb, nct, NDH, CHUNK),
                                        (lambda b, j: (b, nt - 1 - j, 0, 0)) if rev else (lambda b, j: (b, j, 0, 0)))
    state_spec = pl.BlockSpec((nb, 2, DN_HEADS, DN_DK, DN_DV), lambda b, j: (b, 0, 0, 0, 0))
    return pl.pallas_call(
        functools.partial(_dn_kernel, nct=nct, nb=nb),
        grid=(bsz // nb, nt),
        in_specs=side(fwd) + [row_spec(False)] + side(bwd) + [row_spec(True)] + [state_spec],
        out_specs=[pl.BlockSpec((nb, tt, DN_VW), fwd(0)), pl.BlockSpec((nb, tt, DN_VW), bwd(0)), state_spec],
        out_shape=[
            jax.ShapeDtypeStruct((bsz, t, DN_VW), F32),
            jax.ShapeDtypeStruct((bsz, t, DN_VW), F32),
            jax.ShapeDtypeStruct((bsz, 2, DN_HEADS, DN_DK, DN_DV), F32),
        ],
        scratch_shapes=[pltpu.VMEM((nb, 2, DN_HEADS, DN_DK, DN_DV), F32)],
        compiler_params=_compiler_params(("parallel", "arbitrary")),
        name="delta_scan",
    )(feat, feat, feat, gcol, grow, feat, feat, feat, gcol, grow, s0)


def _group_lane_select(cols):
    shape = (cols[0].shape[0], SSM_GW)
    head = lax.broadcasted_iota(jnp.int32, shape, 1) // SSM_P
    out = jnp.broadcast_to(cols[SSM_HPG - 1], shape)
    for hh in range(SSM_HPG - 2, -1, -1):
        out = jnp.where(head == hh, jnp.broadcast_to(cols[hh], shape), out)
    return out


def _ssd_kernel(xf_ref, bf_ref, cf_ref, gf_ref, rf_ref, xb_ref, bb_ref, cb_ref, gb_ref, rb_ref, h0_ref,
                yf_ref, yb_ref, ht_ref, h_scr, *, nct, nb):
    j = pl.program_id(1)

    @pl.when(j == 0)
    def _():
        h_scr[...] = h0_ref[...]

    row_i = lax.broadcasted_iota(jnp.int32, (CHUNK, SSM_GW), 0)
    col_j = lax.broadcasted_iota(jnp.int32, (CHUNK, SSM_GW), 1) % SSM_P
    incl_dir = (row_i >= col_j, row_i <= col_j)
    bd_rows = lax.broadcasted_iota(jnp.int32, (SSM_HPG * CHUNK, SSM_GW), 0) // CHUNK
    bd_cols = lax.broadcasted_iota(jnp.int32, (SSM_HPG * CHUNK, SSM_GW), 1) // SSM_P
    block_diag = bd_rows == bd_cols
    dir_refs = ((xf_ref, bf_ref, cf_ref, gf_ref, rf_ref, yf_ref),
                (xb_ref, bb_ref, cb_ref, gb_ref, rb_ref, yb_ref))

    def body(c, carry):
        chains = []
        for n in range(nb):
            for d in range(2):
                cc = c if d == 0 else nct - 1 - c
                r0 = pl.multiple_of(cc * CHUNK, CHUNK)
                ct = dir_refs[d][3][n, pl.ds(r0, CHUNK), :]
                for g in range(SSM_G):
                    chains.append((d, g, cc, r0, ct, n))

        def load(chain, which, width):
            d, g, _, r0, _, n = chain
            return dir_refs[d][which][n, pl.ds(r0, CHUNK), g * width:(g + 1) * width]

        def pick(chain, off, rows=slice(None)):
            d, g, _, _, ct, _ = chain
            h0 = off + d * SSM_HEADS + g * SSM_HPG
            return _group_lane_select([ct[rows, h0 + hh:h0 + hh + 1] for hh in range(SSM_HPG)])

        def decay_of(chain):
            d, g, cc, _, _, n = chain
            incl = incl_dir[d]
            a_row = dir_refs[d][4][n, cc, d * SSM_G + g:d * SSM_G + g + 1, :]
            return jnp.where(incl, jnp.exp(jnp.where(incl, pick(chain, COL_A) - a_row, 0.0)), 0.0)

        cb4 = [_dot_nt(load(ch, 2, SSM_N), jnp.concatenate([load(ch, 1, SSM_N)] * SSM_HPG, axis=0)) for ch in chains]
        hg = [h_scr[ch[5], ch[0], ch[1]] for ch in chains]
        y_off = [_dot(load(ch, 2, SSM_N), h) for ch, h in zip(chains, hg)]
        h_in = [_dot_tn(load(ch, 1, SSM_N), load(ch, 0, SSM_GW) * pick(ch, COL_DTEAL)) for ch in chains]
        scores = [cb * decay_of(ch) for ch, cb in zip(chains, cb4)]
        xdt_bd = [jnp.where(block_diag,
                            jnp.concatenate([load(ch, 0, SSM_GW) * pick(ch, COL_DT)] * SSM_HPG, axis=0), 0.0)
                  for ch in chains]
        y_diag = [_dot(sc, xb) for sc, xb in zip(scores, xdt_bd)]
        for ch, yd, yo, h, hi in zip(chains, y_diag, y_off, hg, h_in):
            d, g, _, r0, _, n = ch
            dir_refs[d][5][n, pl.ds(r0, CHUNK), g * SSM_GW:(g + 1) * SSM_GW] = yd + yo * pick(ch, COL_EA)
            h_scr[n, d, g] = h * pick(ch, COL_EAT, slice(0, 1)) + hi
        return carry

    lax.fori_loop(0, nct, body, 0)

    @pl.when(j == pl.num_programs(1) - 1)
    def _():
        ht_ref[...] = h_scr[...]


def ssd_scan(feat, gcol, arow, h0, tt, nb):
    bsz, t, _ = feat.shape
    nt = t // tt
    nct = tt // CHUNK
    fwd = lambda lane_blk: (lambda b, j: (b, j, lane_blk))
    bwd = lambda lane_blk: (lambda b, j: (b, nt - 1 - j, lane_blk))

    def side(im):
        return [pl.BlockSpec((nb, tt, SSM_DI), im(OFF_X // SSM_DI)),
                pl.BlockSpec((nb, tt, SSM_BC), im(OFF_B // SSM_BC)),
                pl.BlockSpec((nb, tt, SSM_BC), im(OFF_C // SSM_BC)),
                pl.BlockSpec((nb, tt, LANES), im(0))]

    row_spec = lambda rev: pl.BlockSpec((nb, nct, 2 * SSM_G, SSM_GW),
                                        (lambda b, j: (b, nt - 1 - j, 0, 0)) if rev else (lambda b, j: (b, j, 0, 0)))
    state_spec = pl.BlockSpec((nb, 2, SSM_G, SSM_N, SSM_GW), lambda b, j: (b, 0, 0, 0, 0))
    return pl.pallas_call(
        functools.partial(_ssd_kernel, nct=nct, nb=nb),
        grid=(bsz // nb, nt),
        in_specs=side(fwd) + [row_spec(False)] + side(bwd) + [row_spec(True)] + [state_spec],
        out_specs=[pl.BlockSpec((nb, tt, SSM_DI), fwd(0)), pl.BlockSpec((nb, tt, SSM_DI), bwd(0)), state_spec],
        out_shape=[
            jax.ShapeDtypeStruct((bsz, t, SSM_DI), F32),
            jax.ShapeDtypeStruct((bsz, t, SSM_DI), F32),
            jax.ShapeDtypeStruct((bsz, 2, SSM_G, SSM_N, SSM_GW), F32),
        ],
        scratch_shapes=[pltpu.VMEM((nb, 2, SSM_G, SSM_N, SSM_GW), F32)],
        compiler_params=_compiler_params(("parallel", "arbitrary")),
        name="ssd_scan",
    )(feat, feat, feat, gcol, arow, feat, feat, feat, gcol, arow, h0)


def _merge_out_kernel(of_ref, ob_ref, yf_ref, yb_ref, xs_ref, z_ref, dng_ref, dskip_ref, ssg_ref,
                      w_ref, g_ref, gate_ref, res_ref, o_ref):
    o = of_ref[0] + ob_ref[0]
    z = z_ref[0]
    parts = []
    for h in range(DN_HEADS):
        lanes = slice(h * DN_DV, (h + 1) * DN_DV)
        parts.append(_rms(o[:, lanes]) * dng_ref[...] * _silu(z[:, lanes]))
    y = yf_ref[0] + yb_ref[0] + dskip_ref[...] * xs_ref[0]
    yz = y * _silu(z[:, DN_VW:])
    gw = SSM_DI // SSM_G
    for g in range(SSM_G):
        lanes = slice(g * gw, (g + 1) * gw)
        parts.append(_rms(yz[:, lanes]) * ssg_ref[:, lanes])
    mixed = jnp.concatenate(parts, axis=1).astype(BF16)
    m = jnp.dot(mixed, w_ref[...], preferred_element_type=F32)
    o_ref[0] = res_ref[0] + gate_ref[0] * (_rms(m) * g_ref[...])


def merge_out(o_f, o_b, y_f, y_b, feat, z, dn_norm_g, dskip, ssm_norm_g, w_out, g, gate, res, tm):
    bsz, t, d = res.shape
    row = lambda b, i: (b, i, 0)
    const = lambda b, i: (0, 0)
    half = lambda: pl.BlockSpec((1, tm, DN_VW), row)
    return pl.pallas_call(
        _merge_out_kernel,
        grid=(bsz, t // tm),
        in_specs=[
            half(), half(), half(), half(),
            pl.BlockSpec((1, tm, SSM_DI), lambda b, i: (b, i, OFF_X // SSM_DI)),
            pl.BlockSpec((1, tm, D_Z), row),
            pl.BlockSpec((1, DN_DV), const),
            pl.BlockSpec((1, SSM_DI), const),
            pl.BlockSpec((1, SSM_DI), const),
            pl.BlockSpec((D_MIX, d), const),
            pl.BlockSpec((1, d), const),
            pl.BlockSpec((1, 1, d), lambda b, i: (b, 0, 0)),
            pl.BlockSpec((1, tm, d), row),
        ],
        out_specs=pl.BlockSpec((1, tm, d), row),
        out_shape=jax.ShapeDtypeStruct((bsz, t, d), F32),
        compiler_params=_compiler_params(("parallel", "parallel")),
        name="merge_out",
    )(o_f, o_b, y_f, y_b, feat, z, dn_norm_g.reshape(1, DN_DV), dskip, ssm_norm_g.reshape(1, SSM_DI),
      w_out, g.reshape(1, d), gate.reshape(bsz, 1, d), res)


MOE_TILE = 256
SLOT_WINDOW = 64
SLOT_ALIGN = 16
CNT_LANES = 32
GATE_PIECES = 3
D_EXT = D_MODEL + LANES


def _router_kernel(x_ref, g_ref, shift_ref, scale_ref, wr_ref, hx_ref, aff_ref):
    d = x_ref.shape[2]
    h = _rms(x_ref[0]) * g_ref[...]
    hb = (h * (1.0 + scale_ref[0]) + shift_ref[0]).astype(BF16)
    logits = lax.dot_general(wr_ref[...], hb, (((1,), (1,)), ((), ())), preferred_element_type=F32)
    ex = jnp.exp(logits - jnp.max(logits, axis=0, keepdims=True))
    aff = ex / jnp.sum(ex, axis=0, keepdims=True)
    aff_ref[0, 0] = aff
    pieces, rest = [], aff
    for _ in range(GATE_PIECES):
        p = rest.astype(BF16).astype(F32)
        pieces.append(p)
        rest = rest - p
    rows = jnp.concatenate(pieces + [jnp.zeros((LANES - GATE_PIECES * N_EXPERTS, MOE_TILE), F32)], axis=0)
    ext = jnp.concatenate([rows[:, m * LANES:(m + 1) * LANES].T for m in range(MOE_TILE // LANES)], axis=0)
    hx_ref[0, :, :d] = hb
    hx_ref[0, :, d:] = ext.astype(BF16)


def router(x, g, shift, scale, wr_t):
    bsz, t, d = x.shape
    nt = t // MOE_TILE
    const = lambda b, j: (0, 0)
    per_b = lambda b, j: (b, 0, 0)
    return pl.pallas_call(
        _router_kernel,
        grid=(bsz, nt),
        in_specs=[
            pl.BlockSpec((1, MOE_TILE, d), lambda b, j: (b, j, 0)),
            pl.BlockSpec((1, d), const),
            pl.BlockSpec((1, 1, d), per_b),
            pl.BlockSpec((1, 1, d), per_b),
            pl.BlockSpec((N_EXPERTS, d), const),
        ],
        out_specs=[
            pl.BlockSpec((1, MOE_TILE, D_EXT), lambda b, j: (b, j, 0)),
            pl.BlockSpec((1, 1, N_EXPERTS, MOE_TILE), lambda b, j: (b, j, 0, 0)),
        ],
        out_shape=[
            jax.ShapeDtypeStruct((bsz, t, D_EXT), BF16),
            jax.ShapeDtypeStruct((bsz, nt, N_EXPERTS, MOE_TILE), F32),
        ],
        compiler_params=_compiler_params(("parallel", "parallel")),
        name="router",
    )(x, g.reshape(1, d), shift.reshape(bsz, 1, d), scale.reshape(bsz, 1, d), wr_t)


def _lane_prefix(mask, upper):
    carry = jnp.zeros((mask.shape[0], 1), F32)
    blocks = []
    for m in range(mask.shape[1] // LANES):
        p = jnp.dot(mask[:, m * LANES:(m + 1) * LANES].astype(BF16), upper, preferred_element_type=F32) + carry
        carry = p[:, LANES - 1:LANES]
        blocks.append(p)
    return jnp.concatenate(blocks, axis=1)


def _select_kernel(aff_ref, pos_ref, posc_ref, cnt_ref, *, nt, cap):
    aff = jnp.concatenate([aff_ref[0, j] for j in range(nt)], axis=1)
    bits = pltpu.bitcast(aff, jnp.int32)

    def bisect(i, thr):
        cand = thr | lax.shift_left(jnp.int32(1), jnp.int32(30) - i)
        n_ge = jnp.sum(jnp.where(bits >= cand, 1.0, 0.0), axis=1, keepdims=True)
        return jnp.where(n_ge >= cap, cand, thr)

    thr = lax.fori_loop(0, 31, bisect, jnp.zeros((N_EXPERTS, 1), jnp.int32))
    above = bits > thr
    tied = bits == thr
    room = cap - jnp.sum(jnp.where(above, 1.0, 0.0), axis=1, keepdims=True)
    src = lax.broadcasted_iota(jnp.int32, (LANES, LANES), 0)
    dst = lax.broadcasted_iota(jnp.int32, (LANES, LANES), 1)
    upper = jnp.where(src <= dst, 1.0, 0.0).astype(BF16)
    tie_rank = _lane_prefix(jnp.where(tied, 1.0, 0.0), upper)
    sel = above | (tied & (tie_rank <= room))
    selc = jnp.where(sel, 1.0, 0.0)
    slot = jnp.where(sel, _lane_prefix(selc, upper) - 1.0, -1.0)
    lane = lax.broadcasted_iota(jnp.int32, (N_EXPERTS, LANES), 1)
    before = jnp.zeros((N_EXPERTS, 1), F32)
    counts = jnp.zeros((N_EXPERTS, LANES), F32)
    for j in range(nt):
        pos_ref[0, j] = slot[:, j * MOE_TILE:(j + 1) * MOE_TILE]
        before = before + jnp.sum(selc[:, j * MOE_TILE:(j + 1) * MOE_TILE], axis=1, keepdims=True)
        counts = jnp.where(lane == j + 1, before, counts)
    cnt_ref[0] = counts.astype(jnp.int32)
    pad = jnp.zeros((LANES - N_EXPERTS, LANES), F32)
    for m in range(slot.shape[1] // LANES):
        posc_ref[0, m * LANES:(m + 1) * LANES, :] = jnp.concatenate([slot[:, m * LANES:(m + 1) * LANES], pad], axis=0).T


def select_slots(aff, cap):
    bsz, nt, ne, tile = aff.shape
    blk = lambda b: (b, 0, 0, 0)
    return pl.pallas_call(
        functools.partial(_select_kernel, nt=nt, cap=cap),
        grid=(bsz,),
        in_specs=[pl.BlockSpec((1, nt, ne, tile), blk)],
        out_specs=[pl.BlockSpec((1, nt, ne, tile), blk),
                   pl.BlockSpec((1, nt * tile, LANES), lambda b: (b, 0, 0)),
                   pl.BlockSpec((1, ne, LANES), lambda b: (b, 0, 0))],
        out_shape=[jax.ShapeDtypeStruct((bsz, nt, ne, tile), F32),
                   jax.ShapeDtypeStruct((bsz, nt * tile, LANES), F32),
                   jax.ShapeDtypeStruct((bsz, ne, LANES), jnp.int32)],
        compiler_params=_compiler_params(("parallel",)),
        name="select_slots",
    )(aff)


def _window_start(cnt_ref, row, j):
    return pl.multiple_of((cnt_ref[row, j] // SLOT_ALIGN) * SLOT_ALIGN, SLOT_ALIGN)


def _extra_windows(cnt_ref, row, j, start):
    return (cnt_ref[row, j + 1] - start - 1) // SLOT_WINDOW


def _gather_kernel(cnt_ref, hx_ref, pos_ref, xs_ref):
    b = pl.program_id(0)
    j = pl.program_id(1)

    @pl.when(j == 0)
    def _():
        xs_ref[...] = jnp.zeros_like(xs_ref)

    hx = hx_ref[0]
    slot_iota = lax.broadcasted_iota(jnp.int32, (SLOT_WINDOW, MOE_TILE), 0).astype(F32)

    def one_hot(e, start):
        return jnp.where(pos_ref[0, 0, e:e + 1, :] - start.astype(F32) == slot_iota, 1.0, 0.0).astype(BF16)

    starts = [_window_start(cnt_ref, b * N_EXPERTS + e, j) for e in range(N_EXPERTS)]
    stacked = jnp.concatenate([one_hot(e, starts[e]) for e in range(N_EXPERTS)], axis=0)
    rows = jnp.dot(stacked, hx, preferred_element_type=F32)
    for e in range(N_EXPERTS):
        xs_ref[0, e, pl.ds(starts[e], SLOT_WINDOW), :] += rows[e * SLOT_WINDOW:(e + 1) * SLOT_WINDOW].astype(BF16)

    for e in range(N_EXPERTS):
        n_more = _extra_windows(cnt_ref, b * N_EXPERTS + e, j, starts[e])

        @pl.when(n_more > 0)
        def _(e=e, n_more=n_more):
            def more(i, carry):
                start = pl.multiple_of(starts[e] + i * SLOT_WINDOW, SLOT_ALIGN)
                r = jnp.dot(one_hot(e, start), hx, preferred_element_type=F32)
                xs_ref[0, e, pl.ds(start, SLOT_WINDOW), :] += r.astype(BF16)
                return carry
            lax.fori_loop(1, n_more + 1, more, 0)


def gather_tokens(cnt, hx, pos, cap):
    bsz, t, d_ext = hx.shape
    nt = t // MOE_TILE
    rows = cap + SLOT_WINDOW
    grid_spec = pltpu.PrefetchScalarGridSpec(
        num_scalar_prefetch=1,
        grid=(bsz, nt),
        in_specs=[
            pl.BlockSpec((1, MOE_TILE, d_ext), lambda b, j, c: (b, j, 0)),
            pl.BlockSpec((1, 1, N_EXPERTS, MOE_TILE), lambda b, j, c: (b, j, 0, 0)),
        ],
        out_specs=pl.BlockSpec((1, N_EXPERTS, rows, d_ext), lambda b, j, c: (b, 0, 0, 0),
                               pipeline_mode=pl.Buffered(1)),
    )
    return pl.pallas_call(
        _gather_kernel,
        grid_spec=grid_spec,
        out_shape=jax.ShapeDtypeStruct((bsz, N_EXPERTS, rows, d_ext), BF16),
        compiler_params=_compiler_params(("parallel", "arbitrary")),
        name="gather_tokens",
    )(cnt, hx, pos)


def _expert_ffn_kernel(xs_ref, wg_ref, wu_ref, wd_ref, o_ref):
    e = pl.program_id(0)
    d = wg_ref.shape[1]
    cap = xs_ref.shape[2] - SLOT_WINDOW
    xs = xs_ref[0, 0, :cap, :d]
    ext = xs_ref[0, 0, :cap, d:].astype(F32)
    lane = lax.broadcasted_iota(jnp.int32, ext.shape, 1)
    mine = (lane % N_EXPERTS == e) & (lane < GATE_PIECES * N_EXPERTS)
    gate = jnp.sum(jnp.where(mine, ext, 0.0), axis=1, keepdims=True)
    a = jnp.dot(xs, wg_ref[0], preferred_element_type=F32)
    u = jnp.dot(xs, wu_ref[0], preferred_element_type=F32)
    hid = (_silu(a) * u).astype(BF16)
    o_ref[0, 0, :cap] = (jnp.dot(hid, wd_ref[0], preferred_element_type=F32) * gate).astype(BF16)
    o_ref[0, 0, cap:] = jnp.zeros((SLOT_WINDOW, d), BF16)


def expert_ffn(xs, wg, wu, wd):
    bsz, ne, rows, d_ext = xs.shape
    _, d, f = wg.shape
    return pl.pallas_call(
        _expert_ffn_kernel,
        grid=(ne, bsz),
        in_specs=[
            pl.BlockSpec((1, 1, rows, d_ext), lambda e, b: (b, e, 0, 0)),
            pl.BlockSpec((1, d, f), lambda e, b: (e, 0, 0)),
            pl.BlockSpec((1, d, f), lambda e, b: (e, 0, 0)),
            pl.BlockSpec((1, f, d), lambda e, b: (e, 0, 0)),
        ],
        out_specs=pl.BlockSpec((1, 1, rows, d), lambda e, b: (b, e, 0, 0)),
        out_shape=jax.ShapeDtypeStruct((bsz, ne, rows, d), BF16),
        compiler_params=_compiler_params(("parallel", "parallel")),
        name="expert_ffn",
    )(xs, wg, wu, wd)


def _scatter_out_kernel(cnt_ref, og_ref, posc_ref, g_ref, gate_ref, res_ref, o_ref):
    b = pl.program_id(0)
    j = pl.program_id(1)
    posc = posc_ref[0]
    starts = [_window_start(cnt_ref, b * N_EXPERTS + e, j) for e in range(N_EXPERTS)]
    lane = lax.broadcasted_iota(jnp.int32, (MOE_TILE, LANES), 1)
    lane_f = lane.astype(F32)
    per_tile = LANES // SLOT_WINDOW
    tiles = []
    for m in range(N_EXPERTS // per_tile):
        rel = None
        for q in range(per_tile - 1, -1, -1):
            e = m * per_tile + q
            v = posc[:, e:e + 1] - starts[e].astype(F32) + float(q * SLOT_WINDOW)
            rel = v if rel is None else jnp.where(lane < (q + 1) * SLOT_WINDOW, v, rel)
        tiles.append(jnp.where(rel == lane_f, 1.0, 0.0).astype(BF16))
    sel_t = jnp.concatenate(tiles, axis=1)
    og = jnp.concatenate([og_ref[0, e, pl.ds(starts[e], SLOT_WINDOW), :] for e in range(N_EXPERTS)], axis=0)
    o_ref[0] = jnp.dot(sel_t, og, preferred_element_type=F32)

    win_lane = lax.broadcasted_iota(jnp.int32, (MOE_TILE, SLOT_WINDOW), 1).astype(F32)
    for e in range(N_EXPERTS):
        n_more = _extra_windows(cnt_ref, b * N_EXPERTS + e, j, starts[e])

        @pl.when(n_more > 0)
        def _(e=e, n_more=n_more):
            def more(i, carry):
                start = pl.multiple_of(starts[e] + i * SLOT_WINDOW, SLOT_ALIGN)
                p = jnp.where(posc[:, e:e + 1] - start.astype(F32) == win_lane, 1.0, 0.0).astype(BF16)
                o_ref[0] += jnp.dot(p, og_ref[0, e, pl.ds(start, SLOT_WINDOW), :], preferred_element_type=F32)
                return carry
            lax.fori_loop(1, n_more + 1, more, 0)

    o_ref[0] = res_ref[0] + gate_ref[0] * (_rms(o_ref[0]) * g_ref[...])


def scatter_out(cnt, og, posc, g, gate, res):
    bsz, t, d = res.shape
    nt = t // MOE_TILE
    _, ne, rows, _ = og.shape
    grid_spec = pltpu.PrefetchScalarGridSpec(
        num_scalar_prefetch=1,
        grid=(bsz, nt),
        in_specs=[
            pl.BlockSpec((1, ne, rows, d), lambda b, j, c: (b, 0, 0, 0), pipeline_mode=pl.Buffered(1)),
            pl.BlockSpec((1, MOE_TILE, LANES), lambda b, j, c: (b, j, 0)),
            pl.BlockSpec((1, d), lambda b, j, c: (0, 0)),
            pl.BlockSpec((1, 1, d), lambda b, j, c: (b, 0, 0)),
            pl.BlockSpec((1, MOE_TILE, d), lambda b, j, c: (b, j, 0)),
        ],
        out_specs=pl.BlockSpec((1, MOE_TILE, d), lambda b, j, c: (b, j, 0)),
    )
    return pl.pallas_call(
        _scatter_out_kernel,
        grid_spec=grid_spec,
        out_shape=jax.ShapeDtypeStruct((bsz, t, d), F32),
        compiler_params=_compiler_params(("parallel", "parallel")),
        name="scatter_out",
    )(cnt, og, posc, g.reshape(1, d), gate.reshape(bsz, 1, d), res)


def ec_moe_residual(x, g_in, shift, scale, wr_t, wg, wu, wd, g_out, gate):
    bsz, t, _ = x.shape
    cap = EC_CAPACITY_FACTOR * t // N_EXPERTS
    hx, aff = router(x, g_in, shift, scale, wr_t)
    pos, posc, counts = select_slots(aff, cap)
    cnt = counts[:, :, :CNT_LANES].reshape(bsz * N_EXPERTS, CNT_LANES)
    xs = gather_tokens(cnt, hx, pos, cap)
    og = expert_ffn(xs, wg, wu, wd)
    return scatter_out(cnt, og, posc, g_out, gate, x)


def _token_tile(t):
    return min(512, t)


def mixer_stream(x, g0, shift, scale, wc, wz, ws, w9, conv_b, bias_rows, nega_rows, dn_state, ssm_state, on_grid):
    t = x.shape[1]
    tm = _token_tile(t)
    conv_in, z, small = in_proj(x, g0, shift, scale, wc, wz, ws, tm)
    feat = conv_features(conv_in, w9, conv_b, on_grid)
    gcol, grow, arow = gates(small, bias_rows, nega_rows, tm)
    ts = min(SCAN_TILE, t)
    o_f, o_b, dn_state = delta_scan(feat, gcol, grow, dn_state, ts, SCAN_SAMPLES_PER_STEP)
    y_f, y_b, ssm_state = ssd_scan(feat, gcol, arow, ssm_state, ts, SCAN_SAMPLES_PER_STEP)
    return (o_f, o_b, y_f, y_b, feat, z), dn_state, ssm_state


def _gate_param_rows(dn_bias, dn_a_log, ssm_bias, ssm_a_log):
    zeros = jnp.zeros((NDH,), F32)
    bias = jnp.concatenate([zeros, dn_bias.reshape(-1), ssm_bias.reshape(-1)])
    nega = jnp.concatenate([zeros, -jnp.exp(dn_a_log.reshape(-1)), -jnp.exp(ssm_a_log.reshape(-1))])
    pad = LANES - bias.shape[0]
    expand = lambda v: jnp.broadcast_to(jnp.pad(v, (0, pad))[:, None], (LANES, LANES))
    return expand(bias), expand(nega)


def kernel(x, c, ctx, c_ctx, ada_w, ada_b, norm_g, w_in, conv_w, conv_b, dn_A_log, dn_dt_bias,
           dn_norm_g, ssm_A_log, ssm_dt_bias, ssm_D, ssm_norm_g, w_out, router_w,
           exp_w_gate, exp_w_up, exp_w_down):
    bsz = x.shape[0]
    s_lat = jax.nn.silu(c)
    s_ctx = jax.nn.silu(c_ctx)
    for l in range(DEPTH):
        last = l == DEPTH - 1
        mod_lat = jnp.split(s_lat @ ada_w[l] + ada_b[l], 6, axis=-1)
        mod_ctx_row = s_ctx @ ada_w[l] + ada_b[l]
        mod_ctx = [jnp.broadcast_to(m[None, :], (bsz, D_MODEL)) for m in jnp.split(mod_ctx_row, 6)]

        w_in_l = w_in[l]
        wc = w_in_l[:, :CONV_CH].astype(BF16)
        wz = w_in_l[:, CONV_CH:CONV_CH + D_Z].astype(BF16)
        ws = jnp.pad(w_in_l[:, CONV_CH + D_Z:], ((0, 0), (0, LANES - N_GATE_COLS))).astype(BF16)
        w9 = conv_w[l].reshape(CONV_CH, CONV_K * CONV_K).T
        cb = conv_b[l].reshape(1, CONV_CH)
        bias_rows, nega_rows = _gate_param_rows(dn_dt_bias[l], dn_A_log[l], ssm_dt_bias[l], ssm_A_log[l])
        dskip = jnp.repeat(ssm_D[l], SSM_P).reshape(1, SSM_DI)
        w_out_l = w_out[l].astype(BF16)
        moe_w = (router_w[l].T.astype(BF16), exp_w_gate[l].astype(BF16), exp_w_up[l].astype(BF16),
                 exp_w_down[l].astype(BF16), norm_g[l, 3])

        dn0 = jnp.zeros((bsz, 2, DN_HEADS, DN_DK, DN_DV), F32)
        ssm0 = jnp.zeros((bsz, 2, SSM_G, SSM_N, SSM_GW), F32)
        shared = (wc, wz, ws, w9, cb, bias_rows, nega_rows)
        mix_ctx, dn_c, ssm_c = mixer_stream(ctx, norm_g[l, 0], mod_ctx[0], mod_ctx[1], *shared, dn0, ssm0, False)
        mix_lat, _, _ = mixer_stream(x, norm_g[l, 0], mod_lat[0], mod_lat[1], *shared, dn_c, ssm_c, True)
        merge_w = (dn_norm_g[l], dskip, ssm_norm_g[l], w_out_l, norm_g[l, 1])
        x = merge_out(*mix_lat, *merge_w, mod_lat[2], x, _token_tile(x.shape[1]))

        x = ec_moe_residual(x, norm_g[l, 2], mod_lat[3], mod_lat[4], *moe_w, mod_lat[5])

        if not last:
            ctx = merge_out(*mix_ctx, *merge_w, mod_ctx[2], ctx, _token_tile(ctx.shape[1]))
            ctx = ec_moe_residual(ctx, norm_g[l, 2], mod_ctx[3], mod_ctx[4], *moe_w, mod_ctx[5])
    return x
```

```python
import functools

import jax
import jax.numpy as jnp
import numpy as np
from jax import lax
from jax.experimental import pallas as pl
from jax.experimental.pallas import tpu as pltpu

D_MODEL = 1024
DEPTH = 4
GRID_W = 64
DN_HEADS = 4
DN_DK = 128
DN_DV = 128
SSM_HEADS = 8
SSM_P = 64
SSM_N = 128
SSM_G = 2
CHUNK = 64
CONV_K = 3
N_EXPERTS = 16
EC_CAPACITY_FACTOR = 2
D_EXPERT = 512
EPS = 1e-6

DN_QK = DN_HEADS * DN_DK
DN_VW = DN_HEADS * DN_DV
SSM_DI = SSM_HEADS * SSM_P
SSM_BC = SSM_G * SSM_N
SSM_HPG = SSM_HEADS // SSM_G
SSM_GW = SSM_HPG * SSM_P
D_MIX = DN_VW + SSM_DI
CONV_SPLITS = (DN_QK, DN_QK, DN_VW, SSM_DI, SSM_BC, SSM_BC)
CONV_CH = sum(CONV_SPLITS)
D_Z = DN_VW + SSM_DI
N_GATE_COLS = 2 * DN_HEADS + 2 * DN_HEADS + 2 * SSM_HEADS
D_IN_PROJ = CONV_CH + D_Z + N_GATE_COLS

LANES = 128
SUBLANES = 8
VMEM_LIMIT_BYTES = 56 * 1024 * 1024

OFF_Q, OFF_K, OFF_V = 0, DN_QK, 2 * DN_QK
OFF_X = 2 * DN_QK + DN_VW
OFF_B = OFF_X + SSM_DI
OFF_C = OFF_B + SSM_BC

NDH = 2 * DN_HEADS
NSH = 2 * SSM_HEADS
COL_BETA, COL_G, COL_EG, COL_EGL, COL_EGT = (i * NDH for i in range(5))
COL_DT, COL_A, COL_EA, COL_DTEAL, COL_EAT = (5 * NDH + i * NSH for i in range(5))

F32 = jnp.float32
BF16 = jnp.bfloat16


def _compiler_params(semantics):
    return pltpu.CompilerParams(dimension_semantics=semantics, vmem_limit_bytes=VMEM_LIMIT_BYTES)


def _rms(x):
    return x * lax.rsqrt(jnp.mean(x * x, axis=-1, keepdims=True) + EPS)


def _silu(x):
    return x * jax.nn.sigmoid(x)


def _softplus(x):
    return jnp.maximum(x, 0.0) + jnp.log(1.0 + jnp.exp(-jnp.abs(x)))


def _dot(a, b):
    return jnp.dot(a.astype(BF16), b.astype(BF16), preferred_element_type=F32)


def _dot_nt(a, b):
    return lax.dot_general(a.astype(BF16), b.astype(BF16), (((1,), (1,)), ((), ())),
                           preferred_element_type=F32)


def _dot_tn(a, b):
    return lax.dot_general(a.astype(BF16), b.astype(BF16), (((0,), (0,)), ((), ())),
                           preferred_element_type=F32)


def _dot_exact01(a, m01):
    a1 = a.astype(BF16)
    r1 = a - a1.astype(F32)
    a2 = r1.astype(BF16)
    a3 = (r1 - a2.astype(F32)).astype(BF16)
    m = m01.astype(BF16)
    out = jnp.dot(a3, m, preferred_element_type=F32)
    out = out + jnp.dot(a2, m, preferred_element_type=F32)
    return out + jnp.dot(a1, m, preferred_element_type=F32)


HALO_BLOCK = 128
HALO = GRID_W + 16
CONV_LANES = 256


def _in_proj_conv_kernel(xp_ref, x_ref, xn_ref, g_ref, shift_ref, scale_ref, wc_ref, wz_ref, ws_ref, w9_ref, cb_ref,
                         of_ref, oz_ref, os_ref, *, on_grid):
    i = pl.program_id(1)
    n_i = pl.num_programs(1)
    tm = x_ref.shape[1]

    def modulated(x):
        h = _rms(x) * g_ref[...]
        return h * (1.0 + scale_ref[0]) + shift_ref[0]

    has_prev = jnp.where(i > 0, 1.0, 0.0)
    has_next = jnp.where(i < n_i - 1, 1.0, 0.0)
    h_main = modulated(x_ref[0]).astype(BF16)
    h_prev = (modulated(xp_ref[0, HALO_BLOCK - HALO:, :]) * has_prev).astype(BF16)
    h_next = (modulated(xn_ref[0, :HALO, :]) * has_next).astype(BF16)
    h_all = jnp.concatenate([h_prev, h_main, h_next], axis=0)

    oz_ref[0] = jnp.dot(h_main, wz_ref[...], preferred_element_type=F32)
    os_ref[0] = jnp.dot(h_main, ws_ref[...], preferred_element_type=F32)

    dys = (-1, 0, 1) if on_grid else (0,)
    col = lax.broadcasted_iota(jnp.int32, (tm, CONV_LANES), 0) % GRID_W
    n_blocks = CONV_CH // CONV_LANES

    def project(c):
        return jnp.dot(h_all, wc_ref[:, c * CONV_LANES:(c + 1) * CONV_LANES], preferred_element_type=F32)

    def conv(c, p):
        lanes = slice(c * CONV_LANES, (c + 1) * CONV_LANES)
        acc_c = acc_m = acc_p = None
        for dy in dys:
            lo = HALO + dy * GRID_W
            win = p[lo - SUBLANES:lo + tm + SUBLANES]
            wr = 3 * (dy + 1)
            tc = win[SUBLANES:SUBLANES + tm] * w9_ref[wr + 1:wr + 2, lanes]
            tl, tr = win * w9_ref[wr:wr + 1, lanes], win * w9_ref[wr + 2:wr + 3, lanes]
            acc_c = tc if acc_c is None else acc_c + tc
            acc_m = tl if acc_m is None else acc_m + tl
            acc_p = tr if acc_p is None else acc_p + tr
        acc_m = pltpu.roll(acc_m, 1, 0)[SUBLANES:SUBLANES + tm]
        acc_p = pltpu.roll(acc_p, tm + 2 * SUBLANES - 1, 0)[SUBLANES:SUBLANES + tm]
        if on_grid:
            acc_m = jnp.where(col != 0, acc_m, 0.0)
            acc_p = jnp.where(col != GRID_W - 1, acc_p, 0.0)
        u = _silu(acc_c + acc_m + acc_p + cb_ref[:, lanes])
        if c * CONV_LANES < 2 * DN_QK:
            scale = DN_DK ** -0.5 if c * CONV_LANES < DN_QK else 1.0
            heads = []
            for hh in range(CONV_LANES // DN_DK):
                uh = u[:, hh * DN_DK:(hh + 1) * DN_DK]
                heads.append(uh * (lax.rsqrt(jnp.sum(uh * uh, axis=-1, keepdims=True) + EPS) * scale))
            u = jnp.concatenate(heads, axis=1)
        of_ref[0, :, lanes] = u

    p = project(0)
    for c in range(n_blocks):
        p_next = project(c + 1) if c + 1 < n_blocks else None
        conv(c, p)
        p = p_next


def in_proj_conv(x, g, shift, scale, wc, wz, ws, w9, conv_b, tm, on_grid):
    bsz, t, d = x.shape
    per_tile = tm // HALO_BLOCK
    last_halo = t // HALO_BLOCK - 1
    row = lambda b, i: (b, i, 0)
    const = lambda b, i: (0, 0)
    per_b = lambda b, i: (b, 0, 0)
    return pl.pallas_call(
        functools.partial(_in_proj_conv_kernel, on_grid=on_grid),
        grid=(bsz, t // tm),
        in_specs=[
            pl.BlockSpec((1, HALO_BLOCK, d), lambda b, i: (b, jnp.maximum(i * per_tile - 1, 0), 0)),
            pl.BlockSpec((1, tm, d), row),
            pl.BlockSpec((1, HALO_BLOCK, d), lambda b, i: (b, jnp.minimum((i + 1) * per_tile, last_halo), 0)),
            pl.BlockSpec((1, d), const),
            pl.BlockSpec((1, 1, d), per_b),
            pl.BlockSpec((1, 1, d), per_b),
            pl.BlockSpec((d, CONV_CH), const),
            pl.BlockSpec((d, D_Z), const),
            pl.BlockSpec((d, LANES), const),
            pl.BlockSpec((9, CONV_CH), const),
            pl.BlockSpec((1, CONV_CH), const),
        ],
        out_specs=[
            pl.BlockSpec((1, tm, CONV_CH), row),
            pl.BlockSpec((1, tm, D_Z), row),
            pl.BlockSpec((1, tm, LANES), row),
        ],
        out_shape=[
            jax.ShapeDtypeStruct((bsz, t, CONV_CH), F32),
            jax.ShapeDtypeStruct((bsz, t, D_Z), F32),
            jax.ShapeDtypeStruct((bsz, t, LANES), F32),
        ],
        compiler_params=_compiler_params(("parallel", "parallel")),
        name="in_proj_conv",
    )(x, x, x, g.reshape(1, d), shift.reshape(bsz, 1, d), scale.reshape(bsz, 1, d), wc, wz, ws, w9, conv_b)


def _gates_kernel(s_ref, bias_ref, nega_ref, col_ref, grow_ref, arow_ref, *, tg):
    tok = lax.broadcasted_iota(jnp.int32, (LANES, LANES), 0)
    out = lax.broadcasted_iota(jnp.int32, (LANES, LANES), 1)
    same = (tok // CHUNK) == (out // CHUNK)
    m_fwd = jnp.where(same & (tok <= out), 1.0, 0.0)
    m_bwd = jnp.where(same & (tok >= out), 1.0, 0.0)
    m_all = jnp.where(same, 1.0, 0.0)
    bias = bias_ref[...]
    nega = nega_ref[...]

    def dir_cumsum(v, heads):
        f = _dot_exact01(v, m_fwd)
        b = _dot_exact01(v, m_bwd)
        is_fwd = lax.broadcasted_iota(jnp.int32, v.shape, 0) < heads
        return jnp.where(is_fwd, f, b), _dot_exact01(v, m_all)

    for s in range(tg // LANES):
        st = s_ref[0, s * LANES:(s + 1) * LANES, :].T
        beta = jax.nn.sigmoid(st[0:NDH])
        sp = _softplus(st[NDH:2 * NDH + NSH] + bias[NDH:2 * NDH + NSH])
        logg = sp[0:NDH] * nega[NDH:2 * NDH]
        dt = sp[NDH:]
        a = dt * nega[2 * NDH:2 * NDH + NSH]
        g_cs, g_tot = dir_cumsum(logg, DN_HEADS)
        a_cs, a_tot = dir_cumsum(a, SSM_HEADS)
        rows = jnp.concatenate([
            beta, g_cs, jnp.exp(g_cs), jnp.exp(g_tot - g_cs), jnp.exp(g_tot),
            dt, a_cs, jnp.exp(a_cs), dt * jnp.exp(a_tot - a_cs), jnp.exp(a_tot),
            jnp.zeros((LANES - 5 * NDH - 5 * NSH, LANES), F32)], axis=0)
        col_ref[0, s * LANES:(s + 1) * LANES, :] = rows.T
        for half in range(LANES // CHUNK):
            c = s * (LANES // CHUNK) + half
            lo = half * CHUNK
            grow_ref[0, c] = g_cs[:, lo:lo + CHUNK]
            for dg in range(2 * SSM_G):
                r0 = dg * SSM_HPG
                arow_ref[0, c, dg:dg + 1, :] = jnp.concatenate(
                    [a_cs[r0 + hh:r0 + hh + 1, lo:lo + CHUNK] for hh in range(SSM_HPG)], axis=1)


def gates(small, bias_rows, nega_rows, tg):
    bsz, t, _ = small.shape
    nc = t // CHUNK
    ncg = tg // CHUNK
    const = lambda b, i: (0, 0)
    return pl.pallas_call(
        functools.partial(_gates_kernel, tg=tg),
        grid=(bsz, t // tg),
        in_specs=[
            pl.BlockSpec((1, tg, LANES), lambda b, i: (b, i, 0)),
            pl.BlockSpec((LANES, LANES), const),
            pl.BlockSpec((LANES, LANES), const),
        ],
        out_specs=[
            pl.BlockSpec((1, tg, LANES), lambda b, i: (b, i, 0)),
            pl.BlockSpec((1, ncg, NDH, CHUNK), lambda b, i: (b, i, 0, 0)),
            pl.BlockSpec((1, ncg, 2 * SSM_G, SSM_GW), lambda b, i: (b, i, 0, 0)),
        ],
        out_shape=[
            jax.ShapeDtypeStruct((bsz, t, LANES), F32),
            jax.ShapeDtypeStruct((bsz, nc, NDH, CHUNK), F32),
            jax.ShapeDtypeStruct((bsz, nc, 2 * SSM_G, SSM_GW), F32),
        ],
        compiler_params=_compiler_params(("parallel", "parallel")),
        name="gates",
    )(small, bias_rows, nega_rows)


SCAN_SAMPLES_PER_STEP = 4
SCAN_TILE = 256


def _chunk_masks(rev):
    r = lax.broadcasted_iota(jnp.int32, (CHUNK, CHUNK), 0)
    c = lax.broadcasted_iota(jnp.int32, (CHUNK, CHUNK), 1)
    return ((r <= c), (r < c)) if rev else ((r >= c), (r > c))


def _unit_tri_inverse(ms):
    eye = (lax.broadcasted_iota(jnp.int32, (CHUNK, CHUNK), 0)
           == lax.broadcasted_iota(jnp.int32, (CHUNK, CHUNK), 1)).astype(F32)
    ps = [_dot(m, m) for m in ms]
    ts = [eye + m for m in ms]
    n_sq = int(np.log2(CHUNK)) - 1
    for _ in range(n_sq - 1):
        rs = [_dot(jnp.concatenate([t, p], axis=0), p) for t, p in zip(ts, ps)]
        ts = [t + r[:CHUNK] for t, r in zip(ts, rs)]
        ps = [r[CHUNK:] for r in rs]
    return [t + _dot(t, p) for t, p in zip(ts, ps)]


def _dn_kernel(qf_ref, kf_ref, vf_ref, cf_ref, rf_ref, qb_ref, kb_ref, vb_ref, cb_ref, rb_ref, s0_ref,
               of_ref, ob_ref, st_ref, s_scr, *, nct, nb):
    j = pl.program_id(1)

    @pl.when(j == 0)
    def _():
        s_scr[...] = s0_ref[...]

    masks = (_chunk_masks(False), _chunk_masks(True))
    dir_refs = ((qf_ref, kf_ref, vf_ref, cf_ref, rf_ref, of_ref),
                (qb_ref, kb_ref, vb_ref, cb_ref, rb_ref, ob_ref))

    def body(c, carry):
        chains = []
        for n in range(nb):
            for d in range(2):
                cc = c if d == 0 else nct - 1 - c
                r0 = pl.multiple_of(cc * CHUNK, CHUNK)
                ct = dir_refs[d][3][n, pl.ds(r0, CHUNK), :]
                for h in range(DN_HEADS):
                    chains.append((d, h, cc, r0, ct, n))

        def load(chain, which):
            d, h, _, r0, _, n = chain
            return dir_refs[d][which][n, pl.ds(r0, CHUNK), h * DN_DK:(h + 1) * DN_DK]

        def col(chain, off, rows=slice(None)):
            d, h, _, _, ct, _ = chain
            lane = off + d * DN_HEADS + h
            return ct[rows, lane:lane + 1]

        def decay_of(chain):
            d, h, cc, _, _, n = chain
            incl = masks[d][0]
            g_row = dir_refs[d][4][n, cc, d * DN_HEADS + h:d * DN_HEADS + h + 1, :]
            return jnp.where(incl, jnp.exp(jnp.where(incl, col(chain, COL_G) - g_row, 0.0)), 0.0)

        kbeta = [load(ch, 1) * col(ch, COL_BETA) for ch in chains]
        a = [_dot_nt(jnp.concatenate([kb, load(ch, 0)], axis=0), load(ch, 1)) for ch, kb in zip(chains, kbeta)]
        decay = [decay_of(ch) for ch in chains]
        neg_l = [jnp.where(masks[ch[0]][1], -(ai[:CHUNK] * dc), 0.0) for ch, ai, dc in zip(chains, a, decay)]
        qk = [ai[CHUNK:] * dc for ai, dc in zip(a, decay)]
        t_inv = _unit_tri_inverse(neg_l)
        uw = [_dot(ti, jnp.concatenate([load(ch, 2) * col(ch, COL_BETA), kb * col(ch, COL_EG)], axis=1))
              for ch, ti, kb in zip(chains, t_inv, kbeta)]
        s = [s_scr[ch[5], ch[0], ch[1]] for ch in chains]
        ws_qs = [_dot(jnp.concatenate([uwi[:, DN_DV:], load(ch, 0) * col(ch, COL_EG)], axis=0), si)
                 for ch, uwi, si in zip(chains, uw, s)]
        v_new = [uwi[:, :DN_DV] - wq[:CHUNK] for uwi, wq in zip(uw, ws_qs)]
        o = [wq[CHUNK:] + _dot(qki, vn) for wq, qki, vn in zip(ws_qs, qk, v_new)]
        s_new = [si * col(ch, COL_EGT, slice(0, 1)) + _dot_tn(load(ch, 1) * col(ch, COL_EGL), vn)
                 for ch, si, vn in zip(chains, s, v_new)]
        for ch, oi, sn in zip(chains, o, s_new):
            d, h, _, r0, _, n = ch
            dir_refs[d][5][n, pl.ds(r0, CHUNK), h * DN_DV:(h + 1) * DN_DV] = oi
            s_scr[n, d, h] = sn
        return carry

    lax.fori_loop(0, nct, body, 0)

    @pl.when(j == pl.num_programs(1) - 1)
    def _():
        st_ref[...] = s_scr[...]


def delta_scan(feat, gcol, grow, s0, tt, nb):
    bsz, t, _ = feat.shape
    nt = t // tt
    nct = tt // CHUNK
    fwd = lambda lane_blk: (lambda b, j: (b, j, lane_blk))
    bwd = lambda lane_blk: (lambda b, j: (b, nt - 1 - j, lane_blk))
    qkv_blk = (nb, tt, DN_QK)

    def side(im):
        return [pl.BlockSpec(qkv_blk, im(OFF_Q // DN_QK)), pl.BlockSpec(qkv_blk, im(OFF_K // DN_QK)),
                pl.BlockSpec(qkv_blk, im(OFF_V // DN_QK)), pl.BlockSpec((nb, tt, LANES), im(0))]

    row_spec = lambda rev: pl.BlockSpec((nb, nct, NDH, CHUNK),
                                        (lambda b, j: (b, nt - 1 - j, 0, 0)) if rev else (lambda b, j: (b, j, 0, 0)))
    state_spec = pl.BlockSpec((nb, 2, DN_HEADS, DN_DK, DN_DV), lambda b, j: (b, 0, 0, 0, 0))
    return pl.pallas_call(
        functools.partial(_dn_kernel, nct=nct, nb=nb),
        grid=(bsz // nb, nt),
        in_specs=side(fwd) + [row_spec(False)] + side(bwd) + [row_spec(True)] + [state_spec],
        out_specs=[pl.BlockSpec((nb, tt, DN_VW), fwd(0)), pl.BlockSpec((nb, tt, DN_VW), bwd(0)), state_spec],
        out_shape=[
            jax.ShapeDtypeStruct((bsz, t, DN_VW), F32),
            jax.ShapeDtypeStruct((bsz, t, DN_VW), F32),
            jax.ShapeDtypeStruct((bsz, 2, DN_HEADS, DN_DK, DN_DV), F32),
        ],
        scratch_shapes=[pltpu.VMEM((nb, 2, DN_HEADS, DN_DK, DN_DV), F32)],
        compiler_params=_compiler_params(("parallel", "arbitrary")),
        name="delta_scan",
    )(feat, feat, feat, gcol, grow, feat, feat, feat, gcol, grow, s0)


def _group_lane_select(cols):
    shape = (cols[0].shape[0], SSM_GW)
    head = lax.broadcasted_iota(jnp.int32, shape, 1) // SSM_P
    out = jnp.broadcast_to(cols[SSM_HPG - 1], shape)
    for hh in range(SSM_HPG - 2, -1, -1):
        out = jnp.where(head == hh, jnp.broadcast_to(cols[hh], shape), out)
    return out


def _ssd_kernel(xf_ref, bf_ref, cf_ref, gf_ref, rf_ref, xb_ref, bb_ref, cb_ref, gb_ref, rb_ref, h0_ref,
                yf_ref, yb_ref, ht_ref, h_scr, *, nct, nb):
    j = pl.program_id(1)

    @pl.when(j == 0)
    def _():
        h_scr[...] = h0_ref[...]

    row_i = lax.broadcasted_iota(jnp.int32, (CHUNK, SSM_GW), 0)
    col_j = lax.broadcasted_iota(jnp.int32, (CHUNK, SSM_GW), 1) % SSM_P
    incl_dir = (row_i >= col_j, row_i <= col_j)
    bd_rows = lax.broadcasted_iota(jnp.int32, (SSM_HPG * CHUNK, SSM_GW), 0) // CHUNK
    bd_cols = lax.broadcasted_iota(jnp.int32, (SSM_HPG * CHUNK, SSM_GW), 1) // SSM_P
    block_diag = bd_rows == bd_cols
    dir_refs = ((xf_ref, bf_ref, cf_ref, gf_ref, rf_ref, yf_ref),
                (xb_ref, bb_ref, cb_ref, gb_ref, rb_ref, yb_ref))

    def body(c, carry):
        chains = []
        for n in range(nb):
            for d in range(2):
                cc = c if d == 0 else nct - 1 - c
                r0 = pl.multiple_of(cc * CHUNK, CHUNK)
                ct = dir_refs[d][3][n, pl.ds(r0, CHUNK), :]
                for g in range(SSM_G):
                    chains.append((d, g, cc, r0, ct, n))

        def load(chain, which, width):
            d, g, _, r0, _, n = chain
            return dir_refs[d][which][n, pl.ds(r0, CHUNK), g * width:(g + 1) * width]

        def pick(chain, off, rows=slice(None)):
            d, g, _, _, ct, _ = chain
            h0 = off + d * SSM_HEADS + g * SSM_HPG
            return _group_lane_select([ct[rows, h0 + hh:h0 + hh + 1] for hh in range(SSM_HPG)])

        def decay_of(chain):
            d, g, cc, _, _, n = chain
            incl = incl_dir[d]
            a_row = dir_refs[d][4][n, cc, d * SSM_G + g:d * SSM_G + g + 1, :]
            return jnp.where(incl, jnp.exp(jnp.where(incl, pick(chain, COL_A) - a_row, 0.0)), 0.0)

        cb4 = [_dot_nt(load(ch, 2, SSM_N), jnp.concatenate([load(ch, 1, SSM_N)] * SSM_HPG, axis=0)) for ch in chains]
        hg = [h_scr[ch[5], ch[0], ch[1]] for ch in chains]
        y_off = [_dot(load(ch, 2, SSM_N), h) for ch, h in zip(chains, hg)]
        h_in = [_dot_tn(load(ch, 1, SSM_N), load(ch, 0, SSM_GW) * pick(ch, COL_DTEAL)) for ch in chains]
        scores = [cb * decay_of(ch) for ch, cb in zip(chains, cb4)]
        xdt_bd = [jnp.where(block_diag,
                            jnp.concatenate([load(ch, 0, SSM_GW) * pick(ch, COL_DT)] * SSM_HPG, axis=0), 0.0)
                  for ch in chains]
        y_diag = [_dot(sc, xb) for sc, xb in zip(scores, xdt_bd)]
        for ch, yd, yo, h, hi in zip(chains, y_diag, y_off, hg, h_in):
            d, g, _, r0, _, n = ch
            dir_refs[d][5][n, pl.ds(r0, CHUNK), g * SSM_GW:(g + 1) * SSM_GW] = yd + yo * pick(ch, COL_EA)
            h_scr[n, d, g] = h * pick(ch, COL_EAT, slice(0, 1)) + hi
        return carry

    lax.fori_loop(0, nct, body, 0)

    @pl.when(j == pl.num_programs(1) - 1)
    def _():
        ht_ref[...] = h_scr[...]


def ssd_scan(feat, gcol, arow, h0, tt, nb):
    bsz, t, _ = feat.shape
    nt = t // tt
    nct = tt // CHUNK
    fwd = lambda lane_blk: (lambda b, j: (b, j, lane_blk))
    bwd = lambda lane_blk: (lambda b, j: (b, nt - 1 - j, lane_blk))

    def side(im):
        return [pl.BlockSpec((nb, tt, SSM_DI), im(OFF_X // SSM_DI)),
                pl.BlockSpec((nb, tt, SSM_BC), im(OFF_B // SSM_BC)),
                pl.BlockSpec((nb, tt, SSM_BC), im(OFF_C // SSM_BC)),
                pl.BlockSpec((nb, tt, LANES), im(0))]

    row_spec = lambda rev: pl.BlockSpec((nb, nct, 2 * SSM_G, SSM_GW),
                                        (lambda b, j: (b, nt - 1 - j, 0, 0)) if rev else (lambda b, j: (b, j, 0, 0)))
    state_spec = pl.BlockSpec((nb, 2, SSM_G, SSM_N, SSM_GW), lambda b, j: (b, 0, 0, 0, 0))
    return pl.pallas_call(
        functools.partial(_ssd_kernel, nct=nct, nb=nb),
        grid=(bsz // nb, nt),
        in_specs=side(fwd) + [row_spec(False)] + side(bwd) + [row_spec(True)] + [state_spec],
        out_specs=[pl.BlockSpec((nb, tt, SSM_DI), fwd(0)), pl.BlockSpec((nb, tt, SSM_DI), bwd(0)), state_spec],
        out_shape=[
            jax.ShapeDtypeStruct((bsz, t, SSM_DI), F32),
            jax.ShapeDtypeStruct((bsz, t, SSM_DI), F32),
            jax.ShapeDtypeStruct((bsz, 2, SSM_G, SSM_N, SSM_GW), F32),
        ],
        scratch_shapes=[pltpu.VMEM((nb, 2, SSM_G, SSM_N, SSM_GW), F32)],
        compiler_params=_compiler_params(("parallel", "arbitrary")),
        name="ssd_scan",
    )(feat, feat, feat, gcol, arow, feat, feat, feat, gcol, arow, h0)


def _merge_out_kernel(of_ref, ob_ref, yf_ref, yb_ref, xs_ref, z_ref, dng_ref, dskip_ref, ssg_ref,
                      w_ref, g_ref, gate_ref, res_ref, o_ref):
    o = of_ref[0] + ob_ref[0]
    z = z_ref[0]
    parts = []
    for h in range(DN_HEADS):
        lanes = slice(h * DN_DV, (h + 1) * DN_DV)
        parts.append(_rms(o[:, lanes]) * dng_ref[...] * _silu(z[:, lanes]))
    y = yf_ref[0] + yb_ref[0] + dskip_ref[...] * xs_ref[0]
    yz = y * _silu(z[:, DN_VW:])
    gw = SSM_DI // SSM_G
    for g in range(SSM_G):
        lanes = slice(g * gw, (g + 1) * gw)
        parts.append(_rms(yz[:, lanes]) * ssg_ref[:, lanes])
    mixed = jnp.concatenate(parts, axis=1).astype(BF16)
    m = jnp.dot(mixed, w_ref[...], preferred_element_type=F32)
    o_ref[0] = res_ref[0] + gate_ref[0] * (_rms(m) * g_ref[...])


def merge_out(o_f, o_b, y_f, y_b, feat, z, dn_norm_g, dskip, ssm_norm_g, w_out, g, gate, res, tm):
    bsz, t, d = res.shape
    row = lambda b, i: (b, i, 0)
    const = lambda b, i: (0, 0)
    half = lambda: pl.BlockSpec((1, tm, DN_VW), row)
    return pl.pallas_call(
        _merge_out_kernel,
        grid=(bsz, t // tm),
        in_specs=[
            half(), half(), half(), half(),
            pl.BlockSpec((1, tm, SSM_DI), lambda b, i: (b, i, OFF_X // SSM_DI)),
            pl.BlockSpec((1, tm, D_Z), row),
            pl.BlockSpec((1, DN_DV), const),
            pl.BlockSpec((1, SSM_DI), const),
            pl.BlockSpec((1, SSM_DI), const),
            pl.BlockSpec((D_MIX, d), const),
            pl.BlockSpec((1, d), const),
            pl.BlockSpec((1, 1, d), lambda b, i: (b, 0, 0)),
            pl.BlockSpec((1, tm, d), row),
        ],
        out_specs=pl.BlockSpec((1, tm, d), row),
        out_shape=jax.ShapeDtypeStruct((bsz, t, d), F32),
        compiler_params=_compiler_params(("parallel", "parallel")),
        name="merge_out",
    )(o_f, o_b, y_f, y_b, feat, z, dn_norm_g.reshape(1, DN_DV), dskip, ssm_norm_g.reshape(1, SSM_DI),
      w_out, g.reshape(1, d), gate.reshape(bsz, 1, d), res)


MOE_TILE = 256
SLOT_WINDOW = 64
SLOT_ALIGN = 16
CNT_LANES = 32
GATE_PIECES = 3
D_EXT = D_MODEL + LANES


def _router_kernel(x_ref, g_ref, shift_ref, scale_ref, wr_ref, hx_ref, aff_ref):
    d = x_ref.shape[2]
    h = _rms(x_ref[0]) * g_ref[...]
    hb = (h * (1.0 + scale_ref[0]) + shift_ref[0]).astype(BF16)
    logits = lax.dot_general(wr_ref[...], hb, (((1,), (1,)), ((), ())), preferred_element_type=F32)
    ex = jnp.exp(logits - jnp.max(logits, axis=0, keepdims=True))
    aff = ex / jnp.sum(ex, axis=0, keepdims=True)
    aff_ref[0, 0] = aff
    pieces, rest = [], aff
    for _ in range(GATE_PIECES):
        p = rest.astype(BF16).astype(F32)
        pieces.append(p)
        rest = rest - p
    rows = jnp.concatenate(pieces + [jnp.zeros((LANES - GATE_PIECES * N_EXPERTS, MOE_TILE), F32)], axis=0)
    ext = jnp.concatenate([rows[:, m * LANES:(m + 1) * LANES].T for m in range(MOE_TILE // LANES)], axis=0)
    hx_ref[0, :, :d] = hb
    hx_ref[0, :, d:] = ext.astype(BF16)


def router(x, g, shift, scale, wr_t):
    bsz, t, d = x.shape
    nt = t // MOE_TILE
    const = lambda b, j: (0, 0)
    per_b = lambda b, j: (b, 0, 0)
    return pl.pallas_call(
        _router_kernel,
        grid=(bsz, nt),
        in_specs=[
            pl.BlockSpec((1, MOE_TILE, d), lambda b, j: (b, j, 0)),
            pl.BlockSpec((1, d), const),
            pl.BlockSpec((1, 1, d), per_b),
            pl.BlockSpec((1, 1, d), per_b),
            pl.BlockSpec((N_EXPERTS, d), const),
        ],
        out_specs=[
            pl.BlockSpec((1, MOE_TILE, D_EXT), lambda b, j: (b, j, 0)),
            pl.BlockSpec((1, 1, N_EXPERTS, MOE_TILE), lambda b, j: (b, j, 0, 0)),
        ],
        out_shape=[
            jax.ShapeDtypeStruct((bsz, t, D_EXT), BF16),
            jax.ShapeDtypeStruct((bsz, nt, N_EXPERTS, MOE_TILE), F32),
        ],
        compiler_params=_compiler_params(("parallel", "parallel")),
        name="router",
    )(x, g.reshape(1, d), shift.reshape(bsz, 1, d), scale.reshape(bsz, 1, d), wr_t)


def _lane_prefix(mask, upper):
    carry = jnp.zeros((mask.shape[0], 1), F32)
    blocks = []
    for m in range(mask.shape[1] // LANES):
        p = jnp.dot(mask[:, m * LANES:(m + 1) * LANES].astype(BF16), upper, preferred_element_type=F32) + carry
        carry = p[:, LANES - 1:LANES]
        blocks.append(p)
    return jnp.concatenate(blocks, axis=1)


def _select_kernel(aff_ref, pos_ref, posc_ref, cnt_ref, *, nt, cap):
    aff = jnp.concatenate([aff_ref[0, j] for j in range(nt)], axis=1)
    bits = pltpu.bitcast(aff, jnp.int32)

    def bisect(i, thr):
        cand = thr | lax.shift_left(jnp.int32(1), jnp.int32(30) - i)
        n_ge = jnp.sum(jnp.where(bits >= cand, 1.0, 0.0), axis=1, keepdims=True)
        return jnp.where(n_ge >= cap, cand, thr)

    thr = lax.fori_loop(0, 31, bisect, jnp.zeros((N_EXPERTS, 1), jnp.int32))
    above = bits > thr
    tied = bits == thr
    room = cap - jnp.sum(jnp.where(above, 1.0, 0.0), axis=1, keepdims=True)
    src = lax.broadcasted_iota(jnp.int32, (LANES, LANES), 0)
    dst = lax.broadcasted_iota(jnp.int32, (LANES, LANES), 1)
    upper = jnp.where(src <= dst, 1.0, 0.0).astype(BF16)
    tie_rank = _lane_prefix(jnp.where(tied, 1.0, 0.0), upper)
    sel = above | (tied & (tie_rank <= room))
    selc = jnp.where(sel, 1.0, 0.0)
    slot = jnp.where(sel, _lane_prefix(selc, upper) - 1.0, -1.0)
    lane = lax.broadcasted_iota(jnp.int32, (N_EXPERTS, LANES), 1)
    before = jnp.zeros((N_EXPERTS, 1), F32)
    counts = jnp.zeros((N_EXPERTS, LANES), F32)
    for j in range(nt):
        pos_ref[0, j] = slot[:, j * MOE_TILE:(j + 1) * MOE_TILE]
        before = before + jnp.sum(selc[:, j * MOE_TILE:(j + 1) * MOE_TILE], axis=1, keepdims=True)
        counts = jnp.where(lane == j + 1, before, counts)
    cnt_ref[0] = counts.astype(jnp.int32)
    pad = jnp.zeros((LANES - N_EXPERTS, LANES), F32)
    for m in range(slot.shape[1] // LANES):
        posc_ref[0, m * LANES:(m + 1) * LANES, :] = jnp.concatenate([slot[:, m * LANES:(m + 1) * LANES], pad], axis=0).T


def select_slots(aff, cap):
    bsz, nt, ne, tile = aff.shape
    blk = lambda b: (b, 0, 0, 0)
    return pl.pallas_call(
        functools.partial(_select_kernel, nt=nt, cap=cap),
        grid=(bsz,),
        in_specs=[pl.BlockSpec((1, nt, ne, tile), blk)],
        out_specs=[pl.BlockSpec((1, nt, ne, tile), blk),
                   pl.BlockSpec((1, nt * tile, LANES), lambda b: (b, 0, 0)),
                   pl.BlockSpec((1, ne, LANES), lambda b: (b, 0, 0))],
        out_shape=[jax.ShapeDtypeStruct((bsz, nt, ne, tile), F32),
                   jax.ShapeDtypeStruct((bsz, nt * tile, LANES), F32),
                   jax.ShapeDtypeStruct((bsz, ne, LANES), jnp.int32)],
        compiler_params=_compiler_params(("parallel",)),
        name="select_slots",
    )(aff)


def _window_start(cnt_ref, row, j):
    return pl.multiple_of((cnt_ref[row, j] // SLOT_ALIGN) * SLOT_ALIGN, SLOT_ALIGN)


def _extra_windows(cnt_ref, row, j, start):
    return (cnt_ref[row, j + 1] - start - 1) // SLOT_WINDOW


def _gather_kernel(cnt_ref, hx_ref, pos_ref, xs_ref):
    b = pl.program_id(0)
    j = pl.program_id(1)

    @pl.when(j == 0)
    def _():
        xs_ref[...] = jnp.zeros_like(xs_ref)

    hx = hx_ref[0]
    slot_iota = lax.broadcasted_iota(jnp.int32, (SLOT_WINDOW, MOE_TILE), 0).astype(F32)

    def one_hot(e, start):
        return jnp.where(pos_ref[0, 0, e:e + 1, :] - start.astype(F32) == slot_iota, 1.0, 0.0).astype(BF16)

    starts = [_window_start(cnt_ref, b * N_EXPERTS + e, j) for e in range(N_EXPERTS)]
    stacked = jnp.concatenate([one_hot(e, starts[e]) for e in range(N_EXPERTS)], axis=0)
    rows = jnp.dot(stacked, hx, preferred_element_type=F32)
    for e in range(N_EXPERTS):
        xs_ref[0, e, pl.ds(starts[e], SLOT_WINDOW), :] += rows[e * SLOT_WINDOW:(e + 1) * SLOT_WINDOW].astype(BF16)

    for e in range(N_EXPERTS):
        n_more = _extra_windows(cnt_ref, b * N_EXPERTS + e, j, starts[e])

        @pl.when(n_more > 0)
        def _(e=e, n_more=n_more):
            def more(i, carry):
                start = pl.multiple_of(starts[e] + i * SLOT_WINDOW, SLOT_ALIGN)
                r = jnp.dot(one_hot(e, start), hx, preferred_element_type=F32)
                xs_ref[0, e, pl.ds(start, SLOT_WINDOW), :] += r.astype(BF16)
                return carry
            lax.fori_loop(1, n_more + 1, more, 0)


def gather_tokens(cnt, hx, pos, cap):
    bsz, t, d_ext = hx.shape
    nt = t // MOE_TILE
    rows = cap + SLOT_WINDOW
    grid_spec = pltpu.PrefetchScalarGridSpec(
        num_scalar_prefetch=1,
        grid=(bsz, nt),
        in_specs=[
            pl.BlockSpec((1, MOE_TILE, d_ext), lambda b, j, c: (b, j, 0)),
            pl.BlockSpec((1, 1, N_EXPERTS, MOE_TILE), lambda b, j, c: (b, j, 0, 0)),
        ],
        out_specs=pl.BlockSpec((1, N_EXPERTS, rows, d_ext), lambda b, j, c: (b, 0, 0, 0),
                               pipeline_mode=pl.Buffered(1)),
    )
    return pl.pallas_call(
        _gather_kernel,
        grid_spec=grid_spec,
        out_shape=jax.ShapeDtypeStruct((bsz, N_EXPERTS, rows, d_ext), BF16),
        compiler_params=_compiler_params(("parallel", "arbitrary")),
        name="gather_tokens",
    )(cnt, hx, pos)


def _expert_ffn_kernel(xs_ref, wg_ref, wu_ref, wd_ref, o_ref):
    e = pl.program_id(0)
    d = wg_ref.shape[1]
    cap = xs_ref.shape[2] - SLOT_WINDOW
    xs = xs_ref[0, 0, :cap, :d]
    ext = xs_ref[0, 0, :cap, d:].astype(F32)
    lane = lax.broadcasted_iota(jnp.int32, ext.shape, 1)
    mine = (lane % N_EXPERTS == e) & (lane < GATE_PIECES * N_EXPERTS)
    gate = jnp.sum(jnp.where(mine, ext, 0.0), axis=1, keepdims=True)
    a = jnp.dot(xs, wg_ref[0], preferred_element_type=F32)
    u = jnp.dot(xs, wu_ref[0], preferred_element_type=F32)
    hid = (_silu(a) * u).astype(BF16)
    o_ref[0, 0, :cap] = (jnp.dot(hid, wd_ref[0], preferred_element_type=F32) * gate).astype(BF16)
    o_ref[0, 0, cap:] = jnp.zeros((SLOT_WINDOW, d), BF16)


def expert_ffn(xs, wg, wu, wd):
    bsz, ne, rows, d_ext = xs.shape
    _, d, f = wg.shape
    return pl.pallas_call(
        _expert_ffn_kernel,
        grid=(ne, bsz),
        in_specs=[
            pl.BlockSpec((1, 1, rows, d_ext), lambda e, b: (b, e, 0, 0)),
            pl.BlockSpec((1, d, f), lambda e, b: (e, 0, 0)),
            pl.BlockSpec((1, d, f), lambda e, b: (e, 0, 0)),
            pl.BlockSpec((1, f, d), lambda e, b: (e, 0, 0)),
        ],
        out_specs=pl.BlockSpec((1, 1, rows, d), lambda e, b: (b, e, 0, 0)),
        out_shape=jax.ShapeDtypeStruct((bsz, ne, rows, d), BF16),
        compiler_params=_compiler_params(("parallel", "parallel")),
        name="expert_ffn",
    )(xs, wg, wu, wd)


def _scatter_out_kernel(cnt_ref, og_ref, posc_ref, g_ref, gate_ref, res_ref, o_ref):
    b = pl.program_id(0)
    j = pl.program_id(1)
    posc = posc_ref[0]
    starts = [_window_start(cnt_ref, b * N_EXPERTS + e, j) for e in range(N_EXPERTS)]
    lane = lax.broadcasted_iota(jnp.int32, (MOE_TILE, LANES), 1)
    lane_f = lane.astype(F32)
    per_tile = LANES // SLOT_WINDOW
    tiles = []
    for m in range(N_EXPERTS // per_tile):
        rel = None
        for q in range(per_tile - 1, -1, -1):
            e = m * per_tile + q
            v = posc[:, e:e + 1] - starts[e].astype(F32) + float(q * SLOT_WINDOW)
            rel = v if rel is None else jnp.where(lane < (q + 1) * SLOT_WINDOW, v, rel)
        tiles.append(jnp.where(rel == lane_f, 1.0, 0.0).astype(BF16))
    sel_t = jnp.concatenate(tiles, axis=1)
    og = jnp.concatenate([og_ref[0, e, pl.ds(starts[e], SLOT_WINDOW), :] for e in range(N_EXPERTS)], axis=0)
    o_ref[0] = jnp.dot(sel_t, og, preferred_element_type=F32)

    win_lane = lax.broadcasted_iota(jnp.int32, (MOE_TILE, SLOT_WINDOW), 1).astype(F32)
    for e in range(N_EXPERTS):
        n_more = _extra_windows(cnt_ref, b * N_EXPERTS + e, j, starts[e])

        @pl.when(n_more > 0)
        def _(e=e, n_more=n_more):
            def more(i, carry):
                start = pl.multiple_of(starts[e] + i * SLOT_WINDOW, SLOT_ALIGN)
                p = jnp.where(posc[:, e:e + 1] - start.astype(F32) == win_lane, 1.0, 0.0).astype(BF16)
                o_ref[0] += jnp.dot(p, og_ref[0, e, pl.ds(start, SLOT_WINDOW), :], preferred_element_type=F32)
                return carry
            lax.fori_loop(1, n_more + 1, more, 0)

    o_ref[0] = res_ref[0] + gate_ref[0] * (_rms(o_ref[0]) * g_ref[...])


def scatter_out(cnt, og, posc, g, gate, res):
    bsz, t, d = res.shape
    nt = t // MOE_TILE
    _, ne, rows, _ = og.shape
    grid_spec = pltpu.PrefetchScalarGridSpec(
        num_scalar_prefetch=1,
        grid=(bsz, nt),
        in_specs=[
            pl.BlockSpec((1, ne, rows, d), lambda b, j, c: (b, 0, 0, 0), pipeline_mode=pl.Buffered(1)),
            pl.BlockSpec((1, MOE_TILE, LANES), lambda b, j, c: (b, j, 0)),
            pl.BlockSpec((1, d), lambda b, j, c: (0, 0)),
            pl.BlockSpec((1, 1, d), lambda b, j, c: (b, 0, 0)),
            pl.BlockSpec((1, MOE_TILE, d), lambda b, j, c: (b, j, 0)),
        ],
        out_specs=pl.BlockSpec((1, MOE_TILE, d), lambda b, j, c: (b, j, 0)),
    )
    return pl.pallas_call(
        _scatter_out_kernel,
        grid_spec=grid_spec,
        out_shape=jax.ShapeDtypeStruct((bsz, t, d), F32),
        compiler_params=_compiler_params(("parallel", "parallel")),
        name="scatter_out",
    )(cnt, og, posc, g.reshape(1, d), gate.reshape(bsz, 1, d), res)


def ec_moe_residual(x, g_in, shift, scale, wr_t, wg, wu, wd, g_out, gate):
    bsz, t, _ = x.shape
    cap = EC_CAPACITY_FACTOR * t // N_EXPERTS
    hx, aff = router(x, g_in, shift, scale, wr_t)
    pos, posc, counts = select_slots(aff, cap)
    cnt = counts[:, :, :CNT_LANES].reshape(bsz * N_EXPERTS, CNT_LANES)
    xs = gather_tokens(cnt, hx, pos, cap)
    og = expert_ffn(xs, wg, wu, wd)
    return scatter_out(cnt, og, posc, g_out, gate, x)


def _token_tile(t):
    return min(512, t)


def mixer_stream(x, g0, shift, scale, wc, wz, ws, w9, conv_b, bias_rows, nega_rows, dn_state, ssm_state, on_grid):
    t = x.shape[1]
    tm = _token_tile(t)
    feat, z, small = in_proj_conv(x, g0, shift, scale, wc, wz, ws, w9, conv_b, tm, on_grid)
    gcol, grow, arow = gates(small, bias_rows, nega_rows, tm)
    ts = min(SCAN_TILE, t)
    o_f, o_b, dn_state = delta_scan(feat, gcol, grow, dn_state, ts, SCAN_SAMPLES_PER_STEP)
    y_f, y_b, ssm_state = ssd_scan(feat, gcol, arow, ssm_state, ts, SCAN_SAMPLES_PER_STEP)
    return (o_f, o_b, y_f, y_b, feat, z), dn_state, ssm_state


def _gate_param_rows(dn_bias, dn_a_log, ssm_bias, ssm_a_log):
    zeros = jnp.zeros((NDH,), F32)
    bias = jnp.concatenate([zeros, dn_bias.reshape(-1), ssm_bias.reshape(-1)])
    nega = jnp.concatenate([zeros, -jnp.exp(dn_a_log.reshape(-1)), -jnp.exp(ssm_a_log.reshape(-1))])
    pad = LANES - bias.shape[0]
    expand = lambda v: jnp.broadcast_to(jnp.pad(v, (0, pad))[:, None], (LANES, LANES))
    return expand(bias), expand(nega)


def kernel(x, c, ctx, c_ctx, ada_w, ada_b, norm_g, w_in, conv_w, conv_b, dn_A_log, dn_dt_bias,
           dn_norm_g, ssm_A_log, ssm_dt_bias, ssm_D, ssm_norm_g, w_out, router_w,
           exp_w_gate, exp_w_up, exp_w_down):
    bsz = x.shape[0]
    s_lat = jax.nn.silu(c)
    s_ctx = jax.nn.silu(c_ctx)
    for l in range(DEPTH):
        last = l == DEPTH - 1
        mod_lat = jnp.split(s_lat @ ada_w[l] + ada_b[l], 6, axis=-1)
        mod_ctx_row = s_ctx @ ada_w[l] + ada_b[l]
        mod_ctx = [jnp.broadcast_to(m[None, :], (bsz, D_MODEL)) for m in jnp.split(mod_ctx_row, 6)]

        w_in_l = w_in[l]
        wc = w_in_l[:, :CONV_CH].astype(BF16)
        wz = w_in_l[:, CONV_CH:CONV_CH + D_Z].astype(BF16)
        ws = jnp.pad(w_in_l[:, CONV_CH + D_Z:], ((0, 0), (0, LANES - N_GATE_COLS))).astype(BF16)
        w9 = conv_w[l].reshape(CONV_CH, CONV_K * CONV_K).T
        cb = conv_b[l].reshape(1, CONV_CH)
        bias_rows, nega_rows = _gate_param_rows(dn_dt_bias[l], dn_A_log[l], ssm_dt_bias[l], ssm_A_log[l])
        dskip = jnp.repeat(ssm_D[l], SSM_P).reshape(1, SSM_DI)
        w_out_l = w_out[l].astype(BF16)
        moe_w = (router_w[l].T.astype(BF16), exp_w_gate[l].astype(BF16), exp_w_up[l].astype(BF16),
                 exp_w_down[l].astype(BF16), norm_g[l, 3])

        dn0 = jnp.zeros((bsz, 2, DN_HEADS, DN_DK, DN_DV), F32)
        ssm0 = jnp.zeros((bsz, 2, SSM_G, SSM_N, SSM_GW), F32)
        shared = (wc, wz, ws, w9, cb, bias_rows, nega_rows)
        mix_ctx, dn_c, ssm_c = mixer_stream(ctx, norm_g[l, 0], mod_ctx[0], mod_ctx[1], *shared, dn0, ssm0, False)
        mix_lat, _, _ = mixer_stream(x, norm_g[l, 0], mod_lat[0], mod_lat[1], *shared, dn_c, ssm_c, True)
        merge_w = (dn_norm_g[l], dskip, ssm_norm_g[l], w_out_l, norm_g[l, 1])
        x = merge_out(*mix_lat, *merge_w, mod_lat[2], x, _token_tile(x.shape[1]))

        x = ec_moe_residual(x, norm_g[l, 2], mod_lat[3], mod_lat[4], *moe_w, mod_lat[5])

        if not last:
            ctx = merge_out(*mix_ctx, *merge_w, mod_ctx[2], ctx, _token_tile(ctx.shape[1]))
            ctx = ec_moe_residual(ctx, norm_g[l, 2], mod_ctx[3], mod_ctx[4], *moe_w, mod_ctx[5])
    return x
```

```python
import functools

import jax
import jax.numpy as jnp
import numpy as np
from jax import lax
from jax.experimental import pallas as pl
from jax.experimental.pallas import tpu as pltpu

D_MODEL = 1024
DEPTH = 4
GRID_W = 64
DN_HEADS = 4
DN_DK = 128
DN_DV = 128
SSM_HEADS = 8
SSM_P = 64
SSM_N = 128
SSM_G = 2
CHUNK = 64
CONV_K = 3
N_EXPERTS = 16
EC_CAPACITY_FACTOR = 2
D_EXPERT = 512
EPS = 1e-6

DN_QK = DN_HEADS * DN_DK
DN_VW = DN_HEADS * DN_DV
SSM_DI = SSM_HEADS * SSM_P
SSM_BC = SSM_G * SSM_N
SSM_HPG = SSM_HEADS // SSM_G
SSM_GW = SSM_HPG * SSM_P
D_MIX = DN_VW + SSM_DI
CONV_SPLITS = (DN_QK, DN_QK, DN_VW, SSM_DI, SSM_BC, SSM_BC)
CONV_CH = sum(CONV_SPLITS)
D_Z = DN_VW + SSM_DI
N_GATE_COLS = 2 * DN_HEADS + 2 * DN_HEADS + 2 * SSM_HEADS
D_IN_PROJ = CONV_CH + D_Z + N_GATE_COLS

LANES = 128
SUBLANES = 8
VMEM_LIMIT_BYTES = 56 * 1024 * 1024

OFF_Q, OFF_K, OFF_V = 0, DN_QK, 2 * DN_QK
OFF_X = 2 * DN_QK + DN_VW
OFF_B = OFF_X + SSM_DI
OFF_C = OFF_B + SSM_BC

NDH = 2 * DN_HEADS
NSH = 2 * SSM_HEADS
COL_BETA, COL_G, COL_EG, COL_EGL, COL_EGT = (i * NDH for i in range(5))
COL_DT, COL_A, COL_EA, COL_DTEAL, COL_EAT = (5 * NDH + i * NSH for i in range(5))

F32 = jnp.float32
BF16 = jnp.bfloat16


def _compiler_params(semantics):
    return pltpu.CompilerParams(dimension_semantics=semantics, vmem_limit_bytes=VMEM_LIMIT_BYTES)


def _rms(x):
    return x * lax.rsqrt(jnp.mean(x * x, axis=-1, keepdims=True) + EPS)


def _silu(x):
    return x * jax.nn.sigmoid(x)


def _softplus(x):
    return jnp.maximum(x, 0.0) + jnp.log(1.0 + jnp.exp(-jnp.abs(x)))


def _dot(a, b):
    return jnp.dot(a.astype(BF16), b.astype(BF16), preferred_element_type=F32)


def _dot_nt(a, b):
    return lax.dot_general(a.astype(BF16), b.astype(BF16), (((1,), (1,)), ((), ())),
                           preferred_element_type=F32)


def _dot_tn(a, b):
    return lax.dot_general(a.astype(BF16), b.astype(BF16), (((0,), (0,)), ((), ())),
                           preferred_element_type=F32)


def _dot_exact01(a, m01):
    a1 = a.astype(BF16)
    r1 = a - a1.astype(F32)
    a2 = r1.astype(BF16)
    a3 = (r1 - a2.astype(F32)).astype(BF16)
    m = m01.astype(BF16)
    out = jnp.dot(a3, m, preferred_element_type=F32)
    out = out + jnp.dot(a2, m, preferred_element_type=F32)
    return out + jnp.dot(a1, m, preferred_element_type=F32)


HALO_BLOCK = 128
HALO = GRID_W + 16
CONV_LANES = 256


def _in_proj_conv_kernel(xp_ref, x_ref, xn_ref, g_ref, shift_ref, scale_ref, wc_ref, wz_ref, ws_ref, w9_ref, cb_ref,
                         of_ref, oz_ref, os_ref, *, on_grid):
    i = pl.program_id(1)
    n_i = pl.num_programs(1)
    tm = x_ref.shape[1]

    def modulated(x):
        h = _rms(x) * g_ref[...]
        return h * (1.0 + scale_ref[0]) + shift_ref[0]

    has_prev = jnp.where(i > 0, 1.0, 0.0)
    has_next = jnp.where(i < n_i - 1, 1.0, 0.0)
    h_main = modulated(x_ref[0]).astype(BF16)
    h_prev = (modulated(xp_ref[0, HALO_BLOCK - HALO:, :]) * has_prev).astype(BF16)
    h_next = (modulated(xn_ref[0, :HALO, :]) * has_next).astype(BF16)
    h_all = jnp.concatenate([h_prev, h_main, h_next], axis=0)

    oz_ref[0] = jnp.dot(h_main, wz_ref[...], preferred_element_type=F32)
    os_ref[0] = jnp.dot(h_main, ws_ref[...], preferred_element_type=F32)

    dys = (-1, 0, 1) if on_grid else (0,)
    col = lax.broadcasted_iota(jnp.int32, (tm, CONV_LANES), 0) % GRID_W
    n_blocks = CONV_CH // CONV_LANES

    def project(c):
        return jnp.dot(h_all, wc_ref[:, c * CONV_LANES:(c + 1) * CONV_LANES], preferred_element_type=F32)

    def conv(c, p):
        lanes = slice(c * CONV_LANES, (c + 1) * CONV_LANES)
        acc_c = acc_m = acc_p = None
        for dy in dys:
            lo = HALO + dy * GRID_W
            win = p[lo - SUBLANES:lo + tm + SUBLANES]
            wr = 3 * (dy + 1)
            tc = win[SUBLANES:SUBLANES + tm] * w9_ref[wr + 1:wr + 2, lanes]
            tl, tr = win * w9_ref[wr:wr + 1, lanes], win * w9_ref[wr + 2:wr + 3, lanes]
            acc_c = tc if acc_c is None else acc_c + tc
            acc_m = tl if acc_m is None else acc_m + tl
            acc_p = tr if acc_p is None else acc_p + tr
        acc_m = pltpu.roll(acc_m, 1, 0)[SUBLANES:SUBLANES + tm]
        acc_p = pltpu.roll(acc_p, tm + 2 * SUBLANES - 1, 0)[SUBLANES:SUBLANES + tm]
        if on_grid:
            acc_m = jnp.where(col != 0, acc_m, 0.0)
            acc_p = jnp.where(col != GRID_W - 1, acc_p, 0.0)
        u = _silu(acc_c + acc_m + acc_p + cb_ref[:, lanes])
        if c * CONV_LANES < 2 * DN_QK:
            scale = DN_DK ** -0.5 if c * CONV_LANES < DN_QK else 1.0
            heads = []
            for hh in range(CONV_LANES // DN_DK):
                uh = u[:, hh * DN_DK:(hh + 1) * DN_DK]
                heads.append(uh * (lax.rsqrt(jnp.sum(uh * uh, axis=-1, keepdims=True) + EPS) * scale))
            u = jnp.concatenate(heads, axis=1)
        of_ref[0, :, lanes] = u

    p = project(0)
    for c in range(n_blocks):
        p_next = project(c + 1) if c + 1 < n_blocks else None
        conv(c, p)
        p = p_next


def in_proj_conv(x, g, shift, scale, wc, wz, ws, w9, conv_b, tm, on_grid):
    bsz, t, d = x.shape
    per_tile = tm // HALO_BLOCK
    last_halo = t // HALO_BLOCK - 1
    row = lambda b, i: (b, i, 0)
    const = lambda b, i: (0, 0)
    per_b = lambda b, i: (b, 0, 0)
    return pl.pallas_call(
        functools.partial(_in_proj_conv_kernel, on_grid=on_grid),
        grid=(bsz, t // tm),
        in_specs=[
            pl.BlockSpec((1, HALO_BLOCK, d), lambda b, i: (b, jnp.maximum(i * per_tile - 1, 0), 0)),
            pl.BlockSpec((1, tm, d), row),
            pl.BlockSpec((1, HALO_BLOCK, d), lambda b, i: (b, jnp.minimum((i + 1) * per_tile, last_halo), 0)),
            pl.BlockSpec((1, d), const),
            pl.BlockSpec((1, 1, d), per_b),
            pl.BlockSpec((1, 1, d), per_b),
            pl.BlockSpec((d, CONV_CH), const),
            pl.BlockSpec((d, D_Z), const),
            pl.BlockSpec((d, LANES), const),
            pl.BlockSpec((9, CONV_CH), const),
            pl.BlockSpec((1, CONV_CH), const),
        ],
        out_specs=[
            pl.BlockSpec((1, tm, CONV_CH), row),
            pl.BlockSpec((1, tm, D_Z), row),
            pl.BlockSpec((1, tm, LANES), row),
        ],
        out_shape=[
            jax.ShapeDtypeStruct((bsz, t, CONV_CH), F32),
            jax.ShapeDtypeStruct((bsz, t, D_Z), F32),
            jax.ShapeDtypeStruct((bsz, t, LANES), F32),
        ],
        compiler_params=_compiler_params(("parallel", "parallel")),
        name="in_proj_conv",
    )(x, x, x, g.reshape(1, d), shift.reshape(bsz, 1, d), scale.reshape(bsz, 1, d), wc, wz, ws, w9, conv_b)


def _gates_kernel(s_ref, bias_ref, nega_ref, col_ref, grow_ref, arow_ref, *, tg):
    tok = lax.broadcasted_iota(jnp.int32, (LANES, LANES), 0)
    out = lax.broadcasted_iota(jnp.int32, (LANES, LANES), 1)
    same = (tok // CHUNK) == (out // CHUNK)
    m_fwd = jnp.where(same & (tok <= out), 1.0, 0.0)
    m_bwd = jnp.where(same & (tok >= out), 1.0, 0.0)
    m_all = jnp.where(same, 1.0, 0.0)
    bias = bias_ref[...]
    nega = nega_ref[...]

    def dir_cumsum(v, heads):
        f = _dot_exact01(v, m_fwd)
        b = _dot_exact01(v, m_bwd)
        is_fwd = lax.broadcasted_iota(jnp.int32, v.shape, 0) < heads
        return jnp.where(is_fwd, f, b), _dot_exact01(v, m_all)

    for s in range(tg // LANES):
        st = s_ref[0, s * LANES:(s + 1) * LANES, :].T
        beta = jax.nn.sigmoid(st[0:NDH])
        sp = _softplus(st[NDH:2 * NDH + NSH] + bias[NDH:2 * NDH + NSH])
        logg = sp[0:NDH] * nega[NDH:2 * NDH]
        dt = sp[NDH:]
        a = dt * nega[2 * NDH:2 * NDH + NSH]
        g_cs, g_tot = dir_cumsum(logg, DN_HEADS)
        a_cs, a_tot = dir_cumsum(a, SSM_HEADS)
        rows = jnp.concatenate([
            beta, g_cs, jnp.exp(g_cs), jnp.exp(g_tot - g_cs), jnp.exp(g_tot),
            dt, a_cs, jnp.exp(a_cs), dt * jnp.exp(a_tot - a_cs), jnp.exp(a_tot),
            jnp.zeros((LANES - 5 * NDH - 5 * NSH, LANES), F32)], axis=0)
        col_ref[0, s * LANES:(s + 1) * LANES, :] = rows.T
        for half in range(LANES // CHUNK):
            c = s * (LANES // CHUNK) + half
            lo = half * CHUNK
            grow_ref[0, c] = g_cs[:, lo:lo + CHUNK]
            for q, v in enumerate((a_cs, dt)):
                for dg in range(2 * SSM_G):
                    r0 = dg * SSM_HPG
                    r = q * 2 * SSM_G + dg
                    arow_ref[0, c, r:r + 1, :] = jnp.concatenate(
                        [v[r0 + hh:r0 + hh + 1, lo:lo + CHUNK] for hh in range(SSM_HPG)], axis=1)


SSD_ROWS = 2 * 2 * SSM_G


def gates(small, bias_rows, nega_rows, tg):
    bsz, t, _ = small.shape
    nc = t // CHUNK
    ncg = tg // CHUNK
    const = lambda b, i: (0, 0)
    return pl.pallas_call(
        functools.partial(_gates_kernel, tg=tg),
        grid=(bsz, t // tg),
        in_specs=[
            pl.BlockSpec((1, tg, LANES), lambda b, i: (b, i, 0)),
            pl.BlockSpec((LANES, LANES), const),
            pl.BlockSpec((LANES, LANES), const),
        ],
        out_specs=[
            pl.BlockSpec((1, tg, LANES), lambda b, i: (b, i, 0)),
            pl.BlockSpec((1, ncg, NDH, CHUNK), lambda b, i: (b, i, 0, 0)),
            pl.BlockSpec((1, ncg, SSD_ROWS, SSM_GW), lambda b, i: (b, i, 0, 0)),
        ],
        out_shape=[
            jax.ShapeDtypeStruct((bsz, t, LANES), F32),
            jax.ShapeDtypeStruct((bsz, nc, NDH, CHUNK), F32),
            jax.ShapeDtypeStruct((bsz, nc, SSD_ROWS, SSM_GW), F32),
        ],
        compiler_params=_compiler_params(("parallel", "parallel")),
        name="gates",
    )(small, bias_rows, nega_rows)


SCAN_SAMPLES_PER_STEP = 4
SCAN_TILE = 256


def _chunk_masks(rev):
    r = lax.broadcasted_iota(jnp.int32, (CHUNK, CHUNK), 0)
    c = lax.broadcasted_iota(jnp.int32, (CHUNK, CHUNK), 1)
    return ((r <= c), (r < c)) if rev else ((r >= c), (r > c))


def _unit_tri_inverse(ms):
    eye = (lax.broadcasted_iota(jnp.int32, (CHUNK, CHUNK), 0)
           == lax.broadcasted_iota(jnp.int32, (CHUNK, CHUNK), 1)).astype(F32)
    ps = [_dot(m, m) for m in ms]
    ts = [eye + m for m in ms]
    n_sq = int(np.log2(CHUNK)) - 1
    for _ in range(n_sq - 1):
        rs = [_dot(jnp.concatenate([t, p], axis=0), p) for t, p in zip(ts, ps)]
        ts = [t + r[:CHUNK] for t, r in zip(ts, rs)]
        ps = [r[CHUNK:] for r in rs]
    return [t + _dot(t, p) for t, p in zip(ts, ps)]


def _dn_kernel(qf_ref, kf_ref, vf_ref, cf_ref, rf_ref, qb_ref, kb_ref, vb_ref, cb_ref, rb_ref, s0_ref,
               of_ref, ob_ref, st_ref, s_scr, *, nct, nb):
    j = pl.program_id(1)

    @pl.when(j == 0)
    def _():
        s_scr[...] = s0_ref[...]

    masks = (_chunk_masks(False), _chunk_masks(True))
    dir_refs = ((qf_ref, kf_ref, vf_ref, cf_ref, rf_ref, of_ref),
                (qb_ref, kb_ref, vb_ref, cb_ref, rb_ref, ob_ref))

    def body(c, carry):
        chains = []
        for n in range(nb):
            for d in range(2):
                cc = c if d == 0 else nct - 1 - c
                r0 = pl.multiple_of(cc * CHUNK, CHUNK)
                ct = dir_refs[d][3][n, pl.ds(r0, CHUNK), :]
                for h in range(DN_HEADS):
                    chains.append((d, h, cc, r0, ct, n))

        def load(chain, which):
            d, h, _, r0, _, n = chain
            return dir_refs[d][which][n, pl.ds(r0, CHUNK), h * DN_DK:(h + 1) * DN_DK]

        def col(chain, off, rows=slice(None)):
            d, h, _, _, ct, _ = chain
            lane = off + d * DN_HEADS + h
            return ct[rows, lane:lane + 1]

        def decay_of(chain):
            d, h, cc, _, _, n = chain
            incl = masks[d][0]
            g_row = dir_refs[d][4][n, cc, d * DN_HEADS + h:d * DN_HEADS + h + 1, :]
            return jnp.where(incl, jnp.exp(jnp.where(incl, col(chain, COL_G) - g_row, 0.0)), 0.0)

        kbeta = [load(ch, 1) * col(ch, COL_BETA) for ch in chains]
        a = [_dot_nt(jnp.concatenate([kb, load(ch, 0)], axis=0), load(ch, 1)) for ch, kb in zip(chains, kbeta)]
        decay = [decay_of(ch) for ch in chains]
        neg_l = [jnp.where(masks[ch[0]][1], -(ai[:CHUNK] * dc), 0.0) for ch, ai, dc in zip(chains, a, decay)]
        qk = [ai[CHUNK:] * dc for ai, dc in zip(a, decay)]
        t_inv = _unit_tri_inverse(neg_l)
        uw = [_dot(ti, jnp.concatenate([load(ch, 2) * col(ch, COL_BETA), kb * col(ch, COL_EG)], axis=1))
              for ch, ti, kb in zip(chains, t_inv, kbeta)]
        s = [s_scr[ch[5], ch[0], ch[1]] for ch in chains]
        ws_qs = [_dot(jnp.concatenate([uwi[:, DN_DV:], load(ch, 0) * col(ch, COL_EG)], axis=0), si)
                 for ch, uwi, si in zip(chains, uw, s)]
        v_new = [uwi[:, :DN_DV] - wq[:CHUNK] for uwi, wq in zip(uw, ws_qs)]
        o = [wq[CHUNK:] + _dot(qki, vn) for wq, qki, vn in zip(ws_qs, qk, v_new)]
        s_new = [si * col(ch, COL_EGT, slice(0, 1)) + _dot_tn(load(ch, 1) * col(ch, COL_EGL), vn)
                 for ch, si, vn in zip(chains, s, v_new)]
        for ch, oi, sn in zip(chains, o, s_new):
            d, h, _, r0, _, n = ch
            dir_refs[d][5][n, pl.ds(r0, CHUNK), h * DN_DV:(h + 1) * DN_DV] = oi
            s_scr[n, d, h] = sn
        return carry

    lax.fori_loop(0, nct, body, 0)

    @pl.when(j == pl.num_programs(1) - 1)
    def _():
        st_ref[...] = s_scr[...]


def delta_scan(feat, gcol, grow, s0, tt, nb):
    bsz, t, _ = feat.shape
    nt = t // tt
    nct = tt // CHUNK
    fwd = lambda lane_blk: (lambda b, j: (b, j, lane_blk))
    bwd = lambda lane_blk: (lambda b, j: (b, nt - 1 - j, lane_blk))
    qkv_blk = (nb, tt, DN_QK)

    def side(im):
        return [pl.BlockSpec(qkv_blk, im(OFF_Q // DN_QK)), pl.BlockSpec(qkv_blk, im(OFF_K // DN_QK)),
                pl.BlockSpec(qkv_blk, im(OFF_V // DN_QK)), pl.BlockSpec((nb, tt, LANES), im(0))]

    row_spec = lambda rev: pl.BlockSpec((nb, nct, NDH, CHUNK),
                                        (lambda b, j: (b, nt - 1 - j, 0, 0)) if rev else (lambda b, j: (b, j, 0, 0)))
    state_spec = pl.BlockSpec((nb, 2, DN_HEADS, DN_DK, DN_DV), lambda b, j: (b, 0, 0, 0, 0))
    return pl.pallas_call(
        functools.partial(_dn_kernel, nct=nct, nb=nb),
        grid=(bsz // nb, nt),
        in_specs=side(fwd) + [row_spec(False)] + side(bwd) + [row_spec(True)] + [state_spec],
        out_specs=[pl.BlockSpec((nb, tt, DN_VW), fwd(0)), pl.BlockSpec((nb, tt, DN_VW), bwd(0)), state_spec],
        out_shape=[
            jax.ShapeDtypeStruct((bsz, t, DN_VW), F32),
            jax.ShapeDtypeStruct((bsz, t, DN_VW), F32),
            jax.ShapeDtypeStruct((bsz, 2, DN_HEADS, DN_DK, DN_DV), F32),
        ],
        scratch_shapes=[pltpu.VMEM((nb, 2, DN_HEADS, DN_DK, DN_DV), F32)],
        compiler_params=_compiler_params(("parallel", "arbitrary")),
        name="delta_scan",
    )(feat, feat, feat, gcol, grow, feat, feat, feat, gcol, grow, s0)


def _group_lane_select(cols):
    shape = (cols[0].shape[0], SSM_GW)
    head = lax.broadcasted_iota(jnp.int32, shape, 1) // SSM_P
    out = jnp.broadcast_to(cols[SSM_HPG - 1], shape)
    for hh in range(SSM_HPG - 2, -1, -1):
        out = jnp.where(head == hh, jnp.broadcast_to(cols[hh], shape), out)
    return out


def _ssd_kernel(xf_ref, bf_ref, cf_ref, gf_ref, rf_ref, xb_ref, bb_ref, cb_ref, gb_ref, rb_ref, h0_ref,
                yf_ref, yb_ref, ht_ref, h_scr, *, nct, nb):
    j = pl.program_id(1)

    @pl.when(j == 0)
    def _():
        h_scr[...] = h0_ref[...]

    row_i = lax.broadcasted_iota(jnp.int32, (CHUNK, SSM_GW), 0)
    col_j = lax.broadcasted_iota(jnp.int32, (CHUNK, SSM_GW), 1) % SSM_P
    incl_dir = (row_i >= col_j, row_i <= col_j)
    bd_rows = lax.broadcasted_iota(jnp.int32, (SSM_HPG * CHUNK, SSM_GW), 0) // CHUNK
    bd_cols = lax.broadcasted_iota(jnp.int32, (SSM_HPG * CHUNK, SSM_GW), 1) // SSM_P
    block_diag = bd_rows == bd_cols
    dir_refs = ((xf_ref, bf_ref, cf_ref, gf_ref, rf_ref, yf_ref),
                (xb_ref, bb_ref, cb_ref, gb_ref, rb_ref, yb_ref))

    def body(c, carry):
        chains = []
        for n in range(nb):
            for d in range(2):
                cc = c if d == 0 else nct - 1 - c
                r0 = pl.multiple_of(cc * CHUNK, CHUNK)
                ct = dir_refs[d][3][n, pl.ds(r0, CHUNK), :]
                for g in range(SSM_G):
                    chains.append((d, g, cc, r0, ct, n))

        def load(chain, which, width):
            d, g, _, r0, _, n = chain
            return dir_refs[d][which][n, pl.ds(r0, CHUNK), g * width:(g + 1) * width]

        def pick(chain, off, rows=slice(None)):
            d, g, _, _, ct, _ = chain
            h0 = off + d * SSM_HEADS + g * SSM_HPG
            return _group_lane_select([ct[rows, h0 + hh:h0 + hh + 1] for hh in range(SSM_HPG)])

        def decay_dt_of(chain):
            d, g, cc, _, _, n = chain
            incl = incl_dir[d]
            dg = d * SSM_G + g
            a_row = dir_refs[d][4][n, cc, dg:dg + 1, :]
            dt_row = dir_refs[d][4][n, cc, 2 * SSM_G + dg:2 * SSM_G + dg + 1, :]
            return jnp.where(incl, jnp.exp(jnp.where(incl, pick(chain, COL_A) - a_row, 0.0)) * dt_row, 0.0)

        cb4 = [_dot_nt(load(ch, 2, SSM_N), jnp.concatenate([load(ch, 1, SSM_N)] * SSM_HPG, axis=0)) for ch in chains]
        hg = [h_scr[ch[5], ch[0], ch[1]] for ch in chains]
        y_off = [_dot(load(ch, 2, SSM_N), h) for ch, h in zip(chains, hg)]
        h_in = [_dot_tn(load(ch, 1, SSM_N), load(ch, 0, SSM_GW) * pick(ch, COL_DTEAL)) for ch in chains]
        scores = [cb * decay_dt_of(ch) for ch, cb in zip(chains, cb4)]
        x_bd = [jnp.where(block_diag, jnp.concatenate([load(ch, 0, SSM_GW)] * SSM_HPG, axis=0), 0.0)
                for ch in chains]
        y_diag = [_dot(sc, xb) for sc, xb in zip(scores, x_bd)]
        for ch, yd, yo, h, hi in zip(chains, y_diag, y_off, hg, h_in):
            d, g, _, r0, _, n = ch
            dir_refs[d][5][n, pl.ds(r0, CHUNK), g * SSM_GW:(g + 1) * SSM_GW] = yd + yo * pick(ch, COL_EA)
            h_scr[n, d, g] = h * pick(ch, COL_EAT, slice(0, 1)) + hi
        return carry

    lax.fori_loop(0, nct, body, 0)

    @pl.when(j == pl.num_programs(1) - 1)
    def _():
        ht_ref[...] = h_scr[...]


def ssd_scan(feat, gcol, arow, h0, tt, nb):
    bsz, t, _ = feat.shape
    nt = t // tt
    nct = tt // CHUNK
    fwd = lambda lane_blk: (lambda b, j: (b, j, lane_blk))
    bwd = lambda lane_blk: (lambda b, j: (b, nt - 1 - j, lane_blk))

    def side(im):
        return [pl.BlockSpec((nb, tt, SSM_DI), im(OFF_X // SSM_DI)),
                pl.BlockSpec((nb, tt, SSM_BC), im(OFF_B // SSM_BC)),
                pl.BlockSpec((nb, tt, SSM_BC), im(OFF_C // SSM_BC)),
                pl.BlockSpec((nb, tt, LANES), im(0))]

    row_spec = lambda rev: pl.BlockSpec((nb, nct, SSD_ROWS, SSM_GW),
                                        (lambda b, j: (b, nt - 1 - j, 0, 0)) if rev else (lambda b, j: (b, j, 0, 0)))
    state_spec = pl.BlockSpec((nb, 2, SSM_G, SSM_N, SSM_GW), lambda b, j: (b, 0, 0, 0, 0))
    return pl.pallas_call(
        functools.partial(_ssd_kernel, nct=nct, nb=nb),
        grid=(bsz // nb, nt),
        in_specs=side(fwd) + [row_spec(False)] + side(bwd) + [row_spec(True)] + [state_spec],
        out_specs=[pl.BlockSpec((nb, tt, SSM_DI), fwd(0)), pl.BlockSpec((nb, tt, SSM_DI), bwd(0)), state_spec],
        out_shape=[
            jax.ShapeDtypeStruct((bsz, t, SSM_DI), F32),
            jax.ShapeDtypeStruct((bsz, t, SSM_DI), F32),
            jax.ShapeDtypeStruct((bsz, 2, SSM_G, SSM_N, SSM_GW), F32),
        ],
        scratch_shapes=[pltpu.VMEM((nb, 2, SSM_G, SSM_N, SSM_GW), F32)],
        compiler_params=_compiler_params(("parallel", "arbitrary")),
        name="ssd_scan",
    )(feat, feat, feat, gcol, arow, feat, feat, feat, gcol, arow, h0)


def _merge_out_kernel(of_ref, ob_ref, yf_ref, yb_ref, xs_ref, z_ref, dng_ref, dskip_ref, ssg_ref,
                      w_ref, g_ref, gate_ref, res_ref, o_ref):
    o = of_ref[0] + ob_ref[0]
    z = z_ref[0]
    parts = []
    for h in range(DN_HEADS):
        lanes = slice(h * DN_DV, (h + 1) * DN_DV)
        parts.append(_rms(o[:, lanes]) * dng_ref[...] * _silu(z[:, lanes]))
    y = yf_ref[0] + yb_ref[0] + dskip_ref[...] * xs_ref[0]
    yz = y * _silu(z[:, DN_VW:])
    gw = SSM_DI // SSM_G
    for g in range(SSM_G):
        lanes = slice(g * gw, (g + 1) * gw)
        parts.append(_rms(yz[:, lanes]) * ssg_ref[:, lanes])
    mixed = jnp.concatenate(parts, axis=1).astype(BF16)
    m = jnp.dot(mixed, w_ref[...], preferred_element_type=F32)
    o_ref[0] = res_ref[0] + gate_ref[0] * (_rms(m) * g_ref[...])


def merge_out(o_f, o_b, y_f, y_b, feat, z, dn_norm_g, dskip, ssm_norm_g, w_out, g, gate, res, tm):
    bsz, t, d = res.shape
    row = lambda b, i: (b, i, 0)
    const = lambda b, i: (0, 0)
    half = lambda: pl.BlockSpec((1, tm, DN_VW), row)
    return pl.pallas_call(
        _merge_out_kernel,
        grid=(bsz, t // tm),
        in_specs=[
            half(), half(), half(), half(),
            pl.BlockSpec((1, tm, SSM_DI), lambda b, i: (b, i, OFF_X // SSM_DI)),
            pl.BlockSpec((1, tm, D_Z), row),
            pl.BlockSpec((1, DN_DV), const),
            pl.BlockSpec((1, SSM_DI), const),
            pl.BlockSpec((1, SSM_DI), const),
            pl.BlockSpec((D_MIX, d), const),
            pl.BlockSpec((1, d), const),
            pl.BlockSpec((1, 1, d), lambda b, i: (b, 0, 0)),
            pl.BlockSpec((1, tm, d), row),
        ],
        out_specs=pl.BlockSpec((1, tm, d), row),
        out_shape=jax.ShapeDtypeStruct((bsz, t, d), F32),
        compiler_params=_compiler_params(("parallel", "parallel")),
        name="merge_out",
    )(o_f, o_b, y_f, y_b, feat, z, dn_norm_g.reshape(1, DN_DV), dskip, ssm_norm_g.reshape(1, SSM_DI),
      w_out, g.reshape(1, d), gate.reshape(bsz, 1, d), res)


MOE_TILE = 256
SLOT_WINDOW = 64
SLOT_ALIGN = 16
CNT_LANES = 32
GATE_PIECES = 3
D_EXT = D_MODEL + LANES


def _router_kernel(x_ref, g_ref, shift_ref, scale_ref, wr_ref, hx_ref, aff_ref):
    d = x_ref.shape[2]
    h = _rms(x_ref[0]) * g_ref[...]
    hb = (h * (1.0 + scale_ref[0]) + shift_ref[0]).astype(BF16)
    logits = lax.dot_general(wr_ref[...], hb, (((1,), (1,)), ((), ())), preferred_element_type=F32)
    ex = jnp.exp(logits - jnp.max(logits, axis=0, keepdims=True))
    aff = ex / jnp.sum(ex, axis=0, keepdims=True)
    aff_ref[0, 0] = aff
    pieces, rest = [], aff
    for _ in range(GATE_PIECES):
        p = rest.astype(BF16).astype(F32)
        pieces.append(p)
        rest = rest - p
    rows = jnp.concatenate(pieces + [jnp.zeros((LANES - GATE_PIECES * N_EXPERTS, MOE_TILE), F32)], axis=0)
    ext = jnp.concatenate([rows[:, m * LANES:(m + 1) * LANES].T for m in range(MOE_TILE // LANES)], axis=0)
    hx_ref[0, :, :d] = hb
    hx_ref[0, :, d:] = ext.astype(BF16)


def router(x, g, shift, scale, wr_t):
    bsz, t, d = x.shape
    nt = t // MOE_TILE
    const = lambda b, j: (0, 0)
    per_b = lambda b, j: (b, 0, 0)
    return pl.pallas_call(
        _router_kernel,
        grid=(bsz, nt),
        in_specs=[
            pl.BlockSpec((1, MOE_TILE, d), lambda b, j: (b, j, 0)),
            pl.BlockSpec((1, d), const),
            pl.BlockSpec((1, 1, d), per_b),
            pl.BlockSpec((1, 1, d), per_b),
            pl.BlockSpec((N_EXPERTS, d), const),
        ],
        out_specs=[
            pl.BlockSpec((1, MOE_TILE, D_EXT), lambda b, j: (b, j, 0)),
            pl.BlockSpec((1, 1, N_EXPERTS, MOE_TILE), lambda b, j: (b, j, 0, 0)),
        ],
        out_shape=[
            jax.ShapeDtypeStruct((bsz, t, D_EXT), BF16),
            jax.ShapeDtypeStruct((bsz, nt, N_EXPERTS, MOE_TILE), F32),
        ],
        compiler_params=_compiler_params(("parallel", "parallel")),
        name="router",
    )(x, g.reshape(1, d), shift.reshape(bsz, 1, d), scale.reshape(bsz, 1, d), wr_t)


def _lane_prefix(mask, upper):
    carry = jnp.zeros((mask.shape[0], 1), F32)
    blocks = []
    for m in range(mask.shape[1] // LANES):
        p = jnp.dot(mask[:, m * LANES:(m + 1) * LANES].astype(BF16), upper, preferred_element_type=F32) + carry
        carry = p[:, LANES - 1:LANES]
        blocks.append(p)
    return jnp.concatenate(blocks, axis=1)


def _select_kernel(aff_ref, pos_ref, posc_ref, cnt_ref, *, nt, cap):
    aff = jnp.concatenate([aff_ref[0, j] for j in range(nt)], axis=1)
    bits = pltpu.bitcast(aff, jnp.int32)

    def bisect(i, thr):
        cand = thr | lax.shift_left(jnp.int32(1), jnp.int32(30) - i)
        n_ge = jnp.sum(jnp.where(bits >= cand, 1.0, 0.0), axis=1, keepdims=True)
        return jnp.where(n_ge >= cap, cand, thr)

    thr = lax.fori_loop(0, 31, bisect, jnp.zeros((N_EXPERTS, 1), jnp.int32))
    above = bits > thr
    tied = bits == thr
    room = cap - jnp.sum(jnp.where(above, 1.0, 0.0), axis=1, keepdims=True)
    src = lax.broadcasted_iota(jnp.int32, (LANES, LANES), 0)
    dst = lax.broadcasted_iota(jnp.int32, (LANES, LANES), 1)
    upper = jnp.where(src <= dst, 1.0, 0.0).astype(BF16)
    tie_rank = _lane_prefix(jnp.where(tied, 1.0, 0.0), upper)
    sel = above | (tied & (tie_rank <= room))
    selc = jnp.where(sel, 1.0, 0.0)
    slot = jnp.where(sel, _lane_prefix(selc, upper) - 1.0, -1.0)
    lane = lax.broadcasted_iota(jnp.int32, (N_EXPERTS, LANES), 1)
    before = jnp.zeros((N_EXPERTS, 1), F32)
    counts = jnp.zeros((N_EXPERTS, LANES), F32)
    for j in range(nt):
        pos_ref[0, j] = slot[:, j * MOE_TILE:(j + 1) * MOE_TILE]
        before = before + jnp.sum(selc[:, j * MOE_TILE:(j + 1) * MOE_TILE], axis=1, keepdims=True)
        counts = jnp.where(lane == j + 1, before, counts)
    cnt_ref[0] = counts.astype(jnp.int32)
    pad = jnp.zeros((LANES - N_EXPERTS, LANES), F32)
    for m in range(slot.shape[1] // LANES):
        posc_ref[0, m * LANES:(m + 1) * LANES, :] = jnp.concatenate([slot[:, m * LANES:(m + 1) * LANES], pad], axis=0).T


def select_slots(aff, cap):
    bsz, nt, ne, tile = aff.shape
    blk = lambda b: (b, 0, 0, 0)
    return pl.pallas_call(
        functools.partial(_select_kernel, nt=nt, cap=cap),
        grid=(bsz,),
        in_specs=[pl.BlockSpec((1, nt, ne, tile), blk)],
        out_specs=[pl.BlockSpec((1, nt, ne, tile), blk),
                   pl.BlockSpec((1, nt * tile, LANES), lambda b: (b, 0, 0)),
                   pl.BlockSpec((1, ne, LANES), lambda b: (b, 0, 0))],
        out_shape=[jax.ShapeDtypeStruct((bsz, nt, ne, tile), F32),
                   jax.ShapeDtypeStruct((bsz, nt * tile, LANES), F32),
                   jax.ShapeDtypeStruct((bsz, ne, LANES), jnp.int32)],
        compiler_params=_compiler_params(("parallel",)),
        name="select_slots",
    )(aff)


def _window_start(cnt_ref, row, j):
    return pl.multiple_of((cnt_ref[row, j] // SLOT_ALIGN) * SLOT_ALIGN, SLOT_ALIGN)


def _extra_windows(cnt_ref, row, j, start):
    return (cnt_ref[row, j + 1] - start - 1) // SLOT_WINDOW


def _gather_kernel(cnt_ref, hx_ref, pos_ref, xs_ref):
    b = pl.program_id(0)
    j = pl.program_id(1)

    @pl.when(j == 0)
    def _():
        xs_ref[...] = jnp.zeros_like(xs_ref)

    hx = hx_ref[0]
    slot_iota = lax.broadcasted_iota(jnp.int32, (SLOT_WINDOW, MOE_TILE), 0).astype(F32)

    def one_hot(e, start):
        return jnp.where(pos_ref[0, 0, e:e + 1, :] - start.astype(F32) == slot_iota, 1.0, 0.0).astype(BF16)

    starts = [_window_start(cnt_ref, b * N_EXPERTS + e, j) for e in range(N_EXPERTS)]
    stacked = jnp.concatenate([one_hot(e, starts[e]) for e in range(N_EXPERTS)], axis=0)
    rows = jnp.dot(stacked, hx, preferred_element_type=F32)
    for e in range(N_EXPERTS):
        xs_ref[0, e, pl.ds(starts[e], SLOT_WINDOW), :] += rows[e * SLOT_WINDOW:(e + 1) * SLOT_WINDOW].astype(BF16)

    for e in range(N_EXPERTS):
        n_more = _extra_windows(cnt_ref, b * N_EXPERTS + e, j, starts[e])

        @pl.when(n_more > 0)
        def _(e=e, n_more=n_more):
            def more(i, carry):
                start = pl.multiple_of(starts[e] + i * SLOT_WINDOW, SLOT_ALIGN)
                r = jnp.dot(one_hot(e, start), hx, preferred_element_type=F32)
                xs_ref[0, e, pl.ds(start, SLOT_WINDOW), :] += r.astype(BF16)
                return carry
            lax.fori_loop(1, n_more + 1, more, 0)


def gather_tokens(cnt, hx, pos, cap):
    bsz, t, d_ext = hx.shape
    nt = t // MOE_TILE
    rows = cap + SLOT_WINDOW
    grid_spec = pltpu.PrefetchScalarGridSpec(
        num_scalar_prefetch=1,
        grid=(bsz, nt),
        in_specs=[
            pl.BlockSpec((1, MOE_TILE, d_ext), lambda b, j, c: (b, j, 0)),
            pl.BlockSpec((1, 1, N_EXPERTS, MOE_TILE), lambda b, j, c: (b, j, 0, 0)),
        ],
        out_specs=pl.BlockSpec((1, N_EXPERTS, rows, d_ext), lambda b, j, c: (b, 0, 0, 0),
                               pipeline_mode=pl.Buffered(1)),
    )
    return pl.pallas_call(
        _gather_kernel,
        grid_spec=grid_spec,
        out_shape=jax.ShapeDtypeStruct((bsz, N_EXPERTS, rows, d_ext), BF16),
        compiler_params=_compiler_params(("parallel", "arbitrary")),
        name="gather_tokens",
    )(cnt, hx, pos)


def _expert_ffn_kernel(xs_ref, wg_ref, wu_ref, wd_ref, o_ref):
    e = pl.program_id(0)
    d = wg_ref.shape[1]
    nbs = xs_ref.shape[0]
    cap = xs_ref.shape[2] - SLOT_WINDOW
    xs = jnp.concatenate([xs_ref[n, 0, :cap, :d] for n in range(nbs)], axis=0)
    ext = jnp.concatenate([xs_ref[n, 0, :cap, d:] for n in range(nbs)], axis=0).astype(F32)
    lane = lax.broadcasted_iota(jnp.int32, ext.shape, 1)
    mine = (lane % N_EXPERTS == e) & (lane < GATE_PIECES * N_EXPERTS)
    gate = jnp.sum(jnp.where(mine, ext, 0.0), axis=1, keepdims=True)
    a = jnp.dot(xs, wg_ref[0], preferred_element_type=F32)
    u = jnp.dot(xs, wu_ref[0], preferred_element_type=F32)
    hid = (_silu(a) * u).astype(BF16)
    out = (jnp.dot(hid, wd_ref[0], preferred_element_type=F32) * gate).astype(BF16)
    for n in range(nbs):
        o_ref[n, 0, :cap] = out[n * cap:(n + 1) * cap]
        o_ref[n, 0, cap:] = jnp.zeros((SLOT_WINDOW, d), BF16)


FFN_ROWS = 512


def expert_ffn(xs, wg, wu, wd):
    bsz, ne, rows, d_ext = xs.shape
    _, d, f = wg.shape
    nbs = min(bsz, max(1, FFN_ROWS // (rows - SLOT_WINDOW)))
    return pl.pallas_call(
        _expert_ffn_kernel,
        grid=(ne, bsz // nbs),
        in_specs=[
            pl.BlockSpec((nbs, 1, rows, d_ext), lambda e, b: (b, e, 0, 0)),
            pl.BlockSpec((1, d, f), lambda e, b: (e, 0, 0)),
            pl.BlockSpec((1, d, f), lambda e, b: (e, 0, 0)),
            pl.BlockSpec((1, f, d), lambda e, b: (e, 0, 0)),
        ],
        out_specs=pl.BlockSpec((nbs, 1, rows, d), lambda e, b: (b, e, 0, 0)),
        out_shape=jax.ShapeDtypeStruct((bsz, ne, rows, d), BF16),
        compiler_params=_compiler_params(("parallel", "parallel")),
        name="expert_ffn",
    )(xs, wg, wu, wd)


def _scatter_out_kernel(cnt_ref, og_ref, posc_ref, g_ref, gate_ref, res_ref, o_ref):
    b = pl.program_id(0)
    j = pl.program_id(1)
    posc = posc_ref[0]
    starts = [_window_start(cnt_ref, b * N_EXPERTS + e, j) for e in range(N_EXPERTS)]
    lane = lax.broadcasted_iota(jnp.int32, (MOE_TILE, LANES), 1)
    lane_f = lane.astype(F32)
    per_tile = LANES // SLOT_WINDOW
    tiles = []
    for m in range(N_EXPERTS // per_tile):
        rel = None
        for q in range(per_tile - 1, -1, -1):
            e = m * per_tile + q
            v = posc[:, e:e + 1] - starts[e].astype(F32) + float(q * SLOT_WINDOW)
            rel = v if rel is None else jnp.where(lane < (q + 1) * SLOT_WINDOW, v, rel)
        tiles.append(jnp.where(rel == lane_f, 1.0, 0.0).astype(BF16))
    sel_t = jnp.concatenate(tiles, axis=1)
    og = jnp.concatenate([og_ref[0, e, pl.ds(starts[e], SLOT_WINDOW), :] for e in range(N_EXPERTS)], axis=0)
    o_ref[0] = jnp.dot(sel_t, og, preferred_element_type=F32)

    win_lane = lax.broadcasted_iota(jnp.int32, (MOE_TILE, SLOT_WINDOW), 1).astype(F32)
    for e in range(N_EXPERTS):
        n_more = _extra_windows(cnt_ref, b * N_EXPERTS + e, j, starts[e])

        @pl.when(n_more > 0)
        def _(e=e, n_more=n_more):
            def more(i, carry):
                start = pl.multiple_of(starts[e] + i * SLOT_WINDOW, SLOT_ALIGN)
                p = jnp.where(posc[:, e:e + 1] - start.astype(F32) == win_lane, 1.0, 0.0).astype(BF16)
                o_ref[0] += jnp.dot(p, og_ref[0, e, pl.ds(start, SLOT_WINDOW), :], preferred_element_type=F32)
                return carry
            lax.fori_loop(1, n_more + 1, more, 0)

    o_ref[0] = res_ref[0] + gate_ref[0] * (_rms(o_ref[0]) * g_ref[...])


def scatter_out(cnt, og, posc, g, gate, res):
    bsz, t, d = res.shape
    nt = t // MOE_TILE
    _, ne, rows, _ = og.shape
    grid_spec = pltpu.PrefetchScalarGridSpec(
        num_scalar_prefetch=1,
        grid=(bsz, nt),
        in_specs=[
            pl.BlockSpec((1, ne, rows, d), lambda b, j, c: (b, 0, 0, 0), pipeline_mode=pl.Buffered(1)),
            pl.BlockSpec((1, MOE_TILE, LANES), lambda b, j, c: (b, j, 0)),
            pl.BlockSpec((1, d), lambda b, j, c: (0, 0)),
            pl.BlockSpec((1, 1, d), lambda b, j, c: (b, 0, 0)),
            pl.BlockSpec((1, MOE_TILE, d), lambda b, j, c: (b, j, 0)),
        ],
        out_specs=pl.BlockSpec((1, MOE_TILE, d), lambda b, j, c: (b, j, 0)),
    )
    return pl.pallas_call(
        _scatter_out_kernel,
        grid_spec=grid_spec,
        out_shape=jax.ShapeDtypeStruct((bsz, t, d), F32),
        compiler_params=_compiler_params(("parallel", "parallel")),
        name="scatter_out",
    )(cnt, og, posc, g.reshape(1, d), gate.reshape(bsz, 1, d), res)


def ec_moe_residual(x, g_in, shift, scale, wr_t, wg, wu, wd, g_out, gate):
    bsz, t, _ = x.shape
    cap = EC_CAPACITY_FACTOR * t // N_EXPERTS
    hx, aff = router(x, g_in, shift, scale, wr_t)
    pos, posc, counts = select_slots(aff, cap)
    cnt = counts[:, :, :CNT_LANES].reshape(bsz * N_EXPERTS, CNT_LANES)
    xs = gather_tokens(cnt, hx, pos, cap)
    og = expert_ffn(xs, wg, wu, wd)
    return scatter_out(cnt, og, posc, g_out, gate, x)


def _token_tile(t):
    return min(512, t)


def mixer_stream(x, g0, shift, scale, wc, wz, ws, w9, conv_b, bias_rows, nega_rows, dn_state, ssm_state, on_grid):
    t = x.shape[1]
    tm = _token_tile(t)
    feat, z, small = in_proj_conv(x, g0, shift, scale, wc, wz, ws, w9, conv_b, tm, on_grid)
    gcol, grow, arow = gates(small, bias_rows, nega_rows, tm)
    ts = min(SCAN_TILE, t)
    o_f, o_b, dn_state = delta_scan(feat, gcol, grow, dn_state, ts, SCAN_SAMPLES_PER_STEP)
    y_f, y_b, ssm_state = ssd_scan(feat, gcol, arow, ssm_state, ts, SCAN_SAMPLES_PER_STEP)
    return (o_f, o_b, y_f, y_b, feat, z), dn_state, ssm_state


def _gate_param_rows(dn_bias, dn_a_log, ssm_bias, ssm_a_log):
    zeros = jnp.zeros((NDH,), F32)
    bias = jnp.concatenate([zeros, dn_bias.reshape(-1), ssm_bias.reshape(-1)])
    nega = jnp.concatenate([zeros, -jnp.exp(dn_a_log.reshape(-1)), -jnp.exp(ssm_a_log.reshape(-1))])
    pad = LANES - bias.shape[0]
    expand = lambda v: jnp.broadcast_to(jnp.pad(v, (0, pad))[:, None], (LANES, LANES))
    return expand(bias), expand(nega)


def kernel(x, c, ctx, c_ctx, ada_w, ada_b, norm_g, w_in, conv_w, conv_b, dn_A_log, dn_dt_bias,
           dn_norm_g, ssm_A_log, ssm_dt_bias, ssm_D, ssm_norm_g, w_out, router_w,
           exp_w_gate, exp_w_up, exp_w_down):
    bsz = x.shape[0]
    s_lat = jax.nn.silu(c)
    s_ctx = jax.nn.silu(c_ctx)
    for l in range(DEPTH):
        last = l == DEPTH - 1
        mod_lat = jnp.split(s_lat @ ada_w[l] + ada_b[l], 6, axis=-1)
        mod_ctx_row = s_ctx @ ada_w[l] + ada_b[l]
        mod_ctx = [jnp.broadcast_to(m[None, :], (bsz, D_MODEL)) for m in jnp.split(mod_ctx_row, 6)]

        w_in_l = w_in[l]
        wc = w_in_l[:, :CONV_CH].astype(BF16)
        wz = w_in_l[:, CONV_CH:CONV_CH + D_Z].astype(BF16)
        ws = jnp.pad(w_in_l[:, CONV_CH + D_Z:], ((0, 0), (0, LANES - N_GATE_COLS))).astype(BF16)
        w9 = conv_w[l].reshape(CONV_CH, CONV_K * CONV_K).T
        cb = conv_b[l].reshape(1, CONV_CH)
        bias_rows, nega_rows = _gate_param_rows(dn_dt_bias[l], dn_A_log[l], ssm_dt_bias[l], ssm_A_log[l])
        dskip = jnp.repeat(ssm_D[l], SSM_P).reshape(1, SSM_DI)
        w_out_l = w_out[l].astype(BF16)
        moe_w = (router_w[l].T.astype(BF16), exp_w_gate[l].astype(BF16), exp_w_up[l].astype(BF16),
                 exp_w_down[l].astype(BF16), norm_g[l, 3])

        dn0 = jnp.zeros((bsz, 2, DN_HEADS, DN_DK, DN_DV), F32)
        ssm0 = jnp.zeros((bsz, 2, SSM_G, SSM_N, SSM_GW), F32)
        shared = (wc, wz, ws, w9, cb, bias_rows, nega_rows)
        mix_ctx, dn_c, ssm_c = mixer_stream(ctx, norm_g[l, 0], mod_ctx[0], mod_ctx[1], *shared, dn0, ssm0, False)
        mix_lat, _, _ = mixer_stream(x, norm_g[l, 0], mod_lat[0], mod_lat[1], *shared, dn_c, ssm_c, True)
        merge_w = (dn_norm_g[l], dskip, ssm_norm_g[l], w_out_l, norm_g[l, 1])
        x = merge_out(*mix_lat, *merge_w, mod_lat[2], x, _token_tile(x.shape[1]))

        x = ec_moe_residual(x, norm_g[l, 2], mod_lat[3], mod_lat[4], *moe_w, mod_lat[5])

        if not last:
            ctx = merge_out(*mix_ctx, *merge_w, mod_ctx[2], ctx, _token_tile(ctx.shape[1]))
            ctx = ec_moe_residual(ctx, norm_g[l, 2], mod_ctx[3], mod_ctx[4], *moe_w, mod_ctx[5])
    return x
```

```python
import functools

import jax
import jax.numpy as jnp
import numpy as np
from jax import lax
from jax.experimental import pallas as pl
from jax.experimental.pallas import tpu as pltpu

D_MODEL = 1024
DEPTH = 4
GRID_W = 64
DN_HEADS = 4
DN_DK = 128
DN_DV = 128
SSM_HEADS = 8
SSM_P = 64
SSM_N = 128
SSM_G = 2
CHUNK = 64
CONV_K = 3
N_EXPERTS = 16
EC_CAPACITY_FACTOR = 2
D_EXPERT = 512
EPS = 1e-6

DN_QK = DN_HEADS * DN_DK
DN_VW = DN_HEADS * DN_DV
SSM_DI = SSM_HEADS * SSM_P
SSM_BC = SSM_G * SSM_N
SSM_HPG = SSM_HEADS // SSM_G
SSM_GW = SSM_HPG * SSM_P
D_MIX = DN_VW + SSM_DI
CONV_SPLITS = (DN_QK, DN_QK, DN_VW, SSM_DI, SSM_BC, SSM_BC)
CONV_CH = sum(CONV_SPLITS)
D_Z = DN_VW + SSM_DI
N_GATE_COLS = 2 * DN_HEADS + 2 * DN_HEADS + 2 * SSM_HEADS
D_IN_PROJ = CONV_CH + D_Z + N_GATE_COLS

LANES = 128
SUBLANES = 8
VMEM_LIMIT_BYTES = 56 * 1024 * 1024

OFF_Q, OFF_K, OFF_V = 0, DN_QK, 2 * DN_QK
OFF_X = 2 * DN_QK + DN_VW
OFF_B = OFF_X + SSM_DI
OFF_C = OFF_B + SSM_BC

NDH = 2 * DN_HEADS
NSH = 2 * SSM_HEADS
COL_BETA, COL_G, COL_EG, COL_EGL, COL_EGT = (i * NDH for i in range(5))
COL_DT, COL_A, COL_EA, COL_DTEAL, COL_EAT = (5 * NDH + i * NSH for i in range(5))

F32 = jnp.float32
BF16 = jnp.bfloat16


def _compiler_params(semantics):
    return pltpu.CompilerParams(dimension_semantics=semantics, vmem_limit_bytes=VMEM_LIMIT_BYTES)


def _rms(x):
    return x * lax.rsqrt(jnp.mean(x * x, axis=-1, keepdims=True) + EPS)


def _silu(x):
    return x * jax.nn.sigmoid(x)


def _softplus(x):
    return jnp.maximum(x, 0.0) + jnp.log(1.0 + jnp.exp(-jnp.abs(x)))


def _dot(a, b):
    return jnp.dot(a.astype(BF16), b.astype(BF16), preferred_element_type=F32)


def _dot_nt(a, b):
    return lax.dot_general(a.astype(BF16), b.astype(BF16), (((1,), (1,)), ((), ())),
                           preferred_element_type=F32)


def _dot_tn(a, b):
    return lax.dot_general(a.astype(BF16), b.astype(BF16), (((0,), (0,)), ((), ())),
                           preferred_element_type=F32)


def _dot_exact01(a, m01):
    a1 = a.astype(BF16)
    r1 = a - a1.astype(F32)
    a2 = r1.astype(BF16)
    a3 = (r1 - a2.astype(F32)).astype(BF16)
    m = m01.astype(BF16)
    out = jnp.dot(a3, m, preferred_element_type=F32)
    out = out + jnp.dot(a2, m, preferred_element_type=F32)
    return out + jnp.dot(a1, m, preferred_element_type=F32)


SSD_ROWS = 2 * 2 * SSM_G
HALO_BLOCK = 128
HALO = GRID_W + 16
CONV_LANES = 256


def _in_proj_conv_kernel(xp_ref, x_ref, xn_ref, g_ref, shift_ref, scale_ref, wc_ref, wz_ref, ws_ref, w9_ref, cb_ref,
                         gbias_ref, gnega_ref, of_ref, oz_ref, col_ref, grow_ref, arow_ref, *, on_grid):
    i = pl.program_id(1)
    n_i = pl.num_programs(1)
    tm = x_ref.shape[1]

    def modulated(x):
        h = _rms(x) * g_ref[...]
        return h * (1.0 + scale_ref[0]) + shift_ref[0]

    has_prev = jnp.where(i > 0, 1.0, 0.0)
    has_next = jnp.where(i < n_i - 1, 1.0, 0.0)
    h_main = modulated(x_ref[0]).astype(BF16)
    h_prev = (modulated(xp_ref[0, HALO_BLOCK - HALO:, :]) * has_prev).astype(BF16)
    h_next = (modulated(xn_ref[0, :HALO, :]) * has_next).astype(BF16)
    h_all = jnp.concatenate([h_prev, h_main, h_next], axis=0)

    oz_ref[0] = jnp.dot(h_main, wz_ref[...], preferred_element_type=F32)
    _write_gates(jnp.dot(h_main, ws_ref[...], preferred_element_type=F32), gbias_ref[...], gnega_ref[...],
                 col_ref.at[0], grow_ref.at[0], arow_ref.at[0])

    dys = (-1, 0, 1) if on_grid else (0,)
    col = lax.broadcasted_iota(jnp.int32, (tm, CONV_LANES), 0) % GRID_W
    n_blocks = CONV_CH // CONV_LANES

    def project(c):
        return jnp.dot(h_all, wc_ref[:, c * CONV_LANES:(c + 1) * CONV_LANES], preferred_element_type=F32)

    def conv(c, p):
        lanes = slice(c * CONV_LANES, (c + 1) * CONV_LANES)
        acc_c = acc_m = acc_p = None
        for dy in dys:
            lo = HALO + dy * GRID_W
            win = p[lo - SUBLANES:lo + tm + SUBLANES]
            wr = 3 * (dy + 1)
            tc = win[SUBLANES:SUBLANES + tm] * w9_ref[wr + 1:wr + 2, lanes]
            tl, tr = win * w9_ref[wr:wr + 1, lanes], win * w9_ref[wr + 2:wr + 3, lanes]
            acc_c = tc if acc_c is None else acc_c + tc
            acc_m = tl if acc_m is None else acc_m + tl
            acc_p = tr if acc_p is None else acc_p + tr
        acc_m = pltpu.roll(acc_m, 1, 0)[SUBLANES:SUBLANES + tm]
        acc_p = pltpu.roll(acc_p, tm + 2 * SUBLANES - 1, 0)[SUBLANES:SUBLANES + tm]
        if on_grid:
            acc_m = jnp.where(col != 0, acc_m, 0.0)
            acc_p = jnp.where(col != GRID_W - 1, acc_p, 0.0)
        u = _silu(acc_c + acc_m + acc_p + cb_ref[:, lanes])
        if c * CONV_LANES < 2 * DN_QK:
            scale = DN_DK ** -0.5 if c * CONV_LANES < DN_QK else 1.0
            heads = []
            for hh in range(CONV_LANES // DN_DK):
                uh = u[:, hh * DN_DK:(hh + 1) * DN_DK]
                heads.append(uh * (lax.rsqrt(jnp.sum(uh * uh, axis=-1, keepdims=True) + EPS) * scale))
            u = jnp.concatenate(heads, axis=1)
        of_ref[0, :, lanes] = u

    p = project(0)
    for c in range(n_blocks):
        p_next = project(c + 1) if c + 1 < n_blocks else None
        conv(c, p)
        p = p_next


def in_proj_conv(x, g, shift, scale, wc, wz, ws, w9, conv_b, bias_rows, nega_rows, tm, on_grid):
    bsz, t, d = x.shape
    nc = t // CHUNK
    ncg = tm // CHUNK
    per_tile = tm // HALO_BLOCK
    last_halo = t // HALO_BLOCK - 1
    row = lambda b, i: (b, i, 0)
    const = lambda b, i: (0, 0)
    per_b = lambda b, i: (b, 0, 0)
    return pl.pallas_call(
        functools.partial(_in_proj_conv_kernel, on_grid=on_grid),
        grid=(bsz, t // tm),
        in_specs=[
            pl.BlockSpec((1, HALO_BLOCK, d), lambda b, i: (b, jnp.maximum(i * per_tile - 1, 0), 0)),
            pl.BlockSpec((1, tm, d), row),
            pl.BlockSpec((1, HALO_BLOCK, d), lambda b, i: (b, jnp.minimum((i + 1) * per_tile, last_halo), 0)),
            pl.BlockSpec((1, d), const),
            pl.BlockSpec((1, 1, d), per_b),
            pl.BlockSpec((1, 1, d), per_b),
            pl.BlockSpec((d, CONV_CH), const),
            pl.BlockSpec((d, D_Z), const),
            pl.BlockSpec((d, LANES), const),
            pl.BlockSpec((9, CONV_CH), const),
            pl.BlockSpec((1, CONV_CH), const),
            pl.BlockSpec((LANES, LANES), const),
            pl.BlockSpec((LANES, LANES), const),
        ],
        out_specs=[
            pl.BlockSpec((1, tm, CONV_CH), row),
            pl.BlockSpec((1, tm, D_Z), row),
            pl.BlockSpec((1, tm, LANES), row),
            pl.BlockSpec((1, ncg, NDH, CHUNK), lambda b, i: (b, i, 0, 0)),
            pl.BlockSpec((1, ncg, SSD_ROWS, SSM_GW), lambda b, i: (b, i, 0, 0)),
        ],
        out_shape=[
            jax.ShapeDtypeStruct((bsz, t, CONV_CH), F32),
            jax.ShapeDtypeStruct((bsz, t, D_Z), F32),
            jax.ShapeDtypeStruct((bsz, t, LANES), F32),
            jax.ShapeDtypeStruct((bsz, nc, NDH, CHUNK), F32),
            jax.ShapeDtypeStruct((bsz, nc, SSD_ROWS, SSM_GW), F32),
        ],
        compiler_params=_compiler_params(("parallel", "parallel")),
        name="in_proj_conv",
    )(x, x, x, g.reshape(1, d), shift.reshape(bsz, 1, d), scale.reshape(bsz, 1, d), wc, wz, ws, w9, conv_b,
      bias_rows, nega_rows)


def _write_gates(small, bias, nega, col_ref, grow_ref, arow_ref):
    tok = lax.broadcasted_iota(jnp.int32, (LANES, LANES), 0)
    out = lax.broadcasted_iota(jnp.int32, (LANES, LANES), 1)
    same = (tok // CHUNK) == (out // CHUNK)
    m_fwd = jnp.where(same & (tok <= out), 1.0, 0.0)
    m_bwd = jnp.where(same & (tok >= out), 1.0, 0.0)
    m_all = jnp.where(same, 1.0, 0.0)

    def dir_cumsum(v, heads):
        f = _dot_exact01(v, m_fwd)
        b = _dot_exact01(v, m_bwd)
        is_fwd = lax.broadcasted_iota(jnp.int32, v.shape, 0) < heads
        return jnp.where(is_fwd, f, b), _dot_exact01(v, m_all)

    for s in range(small.shape[0] // LANES):
        st = small[s * LANES:(s + 1) * LANES, :].T
        beta = jax.nn.sigmoid(st[0:NDH])
        sp = _softplus(st[NDH:2 * NDH + NSH] + bias[NDH:2 * NDH + NSH])
        logg = sp[0:NDH] * nega[NDH:2 * NDH]
        dt = sp[NDH:]
        a = dt * nega[2 * NDH:2 * NDH + NSH]
        g_cs, g_tot = dir_cumsum(logg, DN_HEADS)
        a_cs, a_tot = dir_cumsum(a, SSM_HEADS)
        rows = jnp.concatenate([
            beta, g_cs, jnp.exp(g_cs), jnp.exp(g_tot - g_cs), jnp.exp(g_tot),
            dt, a_cs, jnp.exp(a_cs), dt * jnp.exp(a_tot - a_cs), jnp.exp(a_tot),
            jnp.zeros((LANES - 5 * NDH - 5 * NSH, LANES), F32)], axis=0)
        col_ref[s * LANES:(s + 1) * LANES, :] = rows.T
        for half in range(LANES // CHUNK):
            c = s * (LANES // CHUNK) + half
            lo = half * CHUNK
            grow_ref[c] = g_cs[:, lo:lo + CHUNK]
            for q, v in enumerate((a_cs, dt)):
                for dg in range(2 * SSM_G):
                    r0 = dg * SSM_HPG
                    r = q * 2 * SSM_G + dg
                    arow_ref[c, r:r + 1, :] = jnp.concatenate(
                        [v[r0 + hh:r0 + hh + 1, lo:lo + CHUNK] for hh in range(SSM_HPG)], axis=1)


SCAN_SAMPLES_PER_STEP = 4
SCAN_TILE = 256


def _chunk_masks(rev):
    r = lax.broadcasted_iota(jnp.int32, (CHUNK, CHUNK), 0)
    c = lax.broadcasted_iota(jnp.int32, (CHUNK, CHUNK), 1)
    return ((r <= c), (r < c)) if rev else ((r >= c), (r > c))


def _unit_tri_inverse(ms):
    eye = (lax.broadcasted_iota(jnp.int32, (CHUNK, CHUNK), 0)
           == lax.broadcasted_iota(jnp.int32, (CHUNK, CHUNK), 1)).astype(F32)
    ps = [_dot(m, m) for m in ms]
    ts = [eye + m for m in ms]
    n_sq = int(np.log2(CHUNK)) - 1
    for _ in range(n_sq - 1):
        rs = [_dot(jnp.concatenate([t, p], axis=0), p) for t, p in zip(ts, ps)]
        ts = [t + r[:CHUNK] for t, r in zip(ts, rs)]
        ps = [r[CHUNK:] for r in rs]
    return [t + _dot(t, p) for t, p in zip(ts, ps)]


def _dn_kernel(qf_ref, kf_ref, vf_ref, cf_ref, rf_ref, qb_ref, kb_ref, vb_ref, cb_ref, rb_ref, s0_ref,
               of_ref, ob_ref, st_ref, s_scr, *, nct, nb):
    j = pl.program_id(1)

    @pl.when(j == 0)
    def _():
        s_scr[...] = s0_ref[...]

    masks = (_chunk_masks(False), _chunk_masks(True))
    dir_refs = ((qf_ref, kf_ref, vf_ref, cf_ref, rf_ref, of_ref),
                (qb_ref, kb_ref, vb_ref, cb_ref, rb_ref, ob_ref))

    def body(c, carry):
        chains = []
        for n in range(nb):
            for d in range(2):
                cc = c if d == 0 else nct - 1 - c
                r0 = pl.multiple_of(cc * CHUNK, CHUNK)
                ct = dir_refs[d][3][n, pl.ds(r0, CHUNK), :]
                for h in range(DN_HEADS):
                    chains.append((d, h, cc, r0, ct, n))

        def load(chain, which):
            d, h, _, r0, _, n = chain
            return dir_refs[d][which][n, pl.ds(r0, CHUNK), h * DN_DK:(h + 1) * DN_DK]

        def col(chain, off, rows=slice(None)):
            d, h, _, _, ct, _ = chain
            lane = off + d * DN_HEADS + h
            return ct[rows, lane:lane + 1]

        def decay_of(chain):
            d, h, cc, _, _, n = chain
            incl = masks[d][0]
            g_row = dir_refs[d][4][n, cc, d * DN_HEADS + h:d * DN_HEADS + h + 1, :]
            return jnp.where(incl, jnp.exp(jnp.where(incl, col(chain, COL_G) - g_row, 0.0)), 0.0)

        kbeta = [load(ch, 1) * col(ch, COL_BETA) for ch in chains]
        a = [_dot_nt(jnp.concatenate([kb, load(ch, 0)], axis=0), load(ch, 1)) for ch, kb in zip(chains, kbeta)]
        decay = [decay_of(ch) for ch in chains]
        neg_l = [jnp.where(masks[ch[0]][1], -(ai[:CHUNK] * dc), 0.0) for ch, ai, dc in zip(chains, a, decay)]
        qk = [ai[CHUNK:] * dc for ai, dc in zip(a, decay)]
        t_inv = _unit_tri_inverse(neg_l)
        uw = [_dot(ti, jnp.concatenate([load(ch, 2) * col(ch, COL_BETA), kb * col(ch, COL_EG)], axis=1))
              for ch, ti, kb in zip(chains, t_inv, kbeta)]
        s = [s_scr[ch[5], ch[0], ch[1]] for ch in chains]
        ws_qs = [_dot(jnp.concatenate([uwi[:, DN_DV:], load(ch, 0) * col(ch, COL_EG)], axis=0), si)
                 for ch, uwi, si in zip(chains, uw, s)]
        v_new = [uwi[:, :DN_DV] - wq[:CHUNK] for uwi, wq in zip(uw, ws_qs)]
        o = [wq[CHUNK:] + _dot(qki, vn) for wq, qki, vn in zip(ws_qs, qk, v_new)]
        s_new = [si * col(ch, COL_EGT, slice(0, 1)) + _dot_tn(load(ch, 1) * col(ch, COL_EGL), vn)
                 for ch, si, vn in zip(chains, s, v_new)]
        for ch, oi, sn in zip(chains, o, s_new):
            d, h, _, r0, _, n = ch
            dir_refs[d][5][n, pl.ds(r0, CHUNK), h * DN_DV:(h + 1) * DN_DV] = oi
            s_scr[n, d, h] = sn
        return carry

    lax.fori_loop(0, nct, body, 0)

    @pl.when(j == pl.num_programs(1) - 1)
    def _():
        st_ref[...] = s_scr[...]


def delta_scan(feat, gcol, grow, s0, tt, nb):
    bsz, t, _ = feat.shape
    nt = t // tt
    nct = tt // CHUNK
    fwd = lambda lane_blk: (lambda b, j: (b, j, lane_blk))
    bwd = lambda lane_blk: (lambda b, j: (b, nt - 1 - j, lane_blk))
    qkv_blk = (nb, tt, DN_QK)

    def side(im):
        return [pl.BlockSpec(qkv_blk, im(OFF_Q // DN_QK)), pl.BlockSpec(qkv_blk, im(OFF_K // DN_QK)),
                pl.BlockSpec(qkv_blk, im(OFF_V // DN_QK)), pl.BlockSpec((nb, tt, LANES), im(0))]

    row_spec = lambda rev: pl.BlockSpec((nb, nct, NDH, CHUNK),
                                        (lambda b, j: (b, nt - 1 - j, 0, 0)) if rev else (lambda b, j: (b, j, 0, 0)))
    state_spec = pl.BlockSpec((nb, 2, DN_HEADS, DN_DK, DN_DV), lambda b, j: (b, 0, 0, 0, 0))
    return pl.pallas_call(
        functools.partial(_dn_kernel, nct=nct, nb=nb),
        grid=(bsz // nb, nt),
        in_specs=side(fwd) + [row_spec(False)] + side(bwd) + [row_spec(True)] + [state_spec],
        out_specs=[pl.BlockSpec((nb, tt, DN_VW), fwd(0)), pl.BlockSpec((nb, tt, DN_VW), bwd(0)), state_spec],
        out_shape=[
            jax.ShapeDtypeStruct((bsz, t, DN_VW), F32),
            jax.ShapeDtypeStruct((bsz, t, DN_VW), F32),
            jax.ShapeDtypeStruct((bsz, 2, DN_HEADS, DN_DK, DN_DV), F32),
        ],
        scratch_shapes=[pltpu.VMEM((nb, 2, DN_HEADS, DN_DK, DN_DV), F32)],
        compiler_params=_compiler_params(("parallel", "arbitrary")),
        name="delta_scan",
    )(feat, feat, feat, gcol, grow, feat, feat, feat, gcol, grow, s0)


def _group_lane_select(cols):
    shape = (cols[0].shape[0], SSM_GW)
    head = lax.broadcasted_iota(jnp.int32, shape, 1) // SSM_P
    out = jnp.broadcast_to(cols[SSM_HPG - 1], shape)
    for hh in range(SSM_HPG - 2, -1, -1):
        out = jnp.where(head == hh, jnp.broadcast_to(cols[hh], shape), out)
    return out


def _ssd_kernel(xf_ref, bf_ref, cf_ref, gf_ref, rf_ref, xb_ref, bb_ref, cb_ref, gb_ref, rb_ref, h0_ref,
                yf_ref, yb_ref, ht_ref, h_scr, *, nct, nb):
    j = pl.program_id(1)

    @pl.when(j == 0)
    def _():
        h_scr[...] = h0_ref[...]

    row_i = lax.broadcasted_iota(jnp.int32, (CHUNK, SSM_GW), 0)
    col_j = lax.broadcasted_iota(jnp.int32, (CHUNK, SSM_GW), 1) % SSM_P
    incl_dir = (row_i >= col_j, row_i <= col_j)
    bd_rows = lax.broadcasted_iota(jnp.int32, (SSM_HPG * CHUNK, SSM_GW), 0) // CHUNK
    bd_cols = lax.broadcasted_iota(jnp.int32, (SSM_HPG * CHUNK, SSM_GW), 1) // SSM_P
    block_diag = bd_rows == bd_cols
    dir_refs = ((xf_ref, bf_ref, cf_ref, gf_ref, rf_ref, yf_ref),
                (xb_ref, bb_ref, cb_ref, gb_ref, rb_ref, yb_ref))

    def body(c, carry):
        chains = []
        for n in range(nb):
            for d in range(2):
                cc = c if d == 0 else nct - 1 - c
                r0 = pl.multiple_of(cc * CHUNK, CHUNK)
                ct = dir_refs[d][3][n, pl.ds(r0, CHUNK), :]
                for g in range(SSM_G):
                    chains.append((d, g, cc, r0, ct, n))

        def load(chain, which, width):
            d, g, _, r0, _, n = chain
            return dir_refs[d][which][n, pl.ds(r0, CHUNK), g * width:(g + 1) * width]

        def pick(chain, off, rows=slice(None)):
            d, g, _, _, ct, _ = chain
            h0 = off + d * SSM_HEADS + g * SSM_HPG
            return _group_lane_select([ct[rows, h0 + hh:h0 + hh + 1] for hh in range(SSM_HPG)])

        def decay_dt_of(chain):
            d, g, cc, _, _, n = chain
            incl = incl_dir[d]
            dg = d * SSM_G + g
            a_row = dir_refs[d][4][n, cc, dg:dg + 1, :]
            dt_row = dir_refs[d][4][n, cc, 2 * SSM_G + dg:2 * SSM_G + dg + 1, :]
            return jnp.where(incl, jnp.exp(jnp.where(incl, pick(chain, COL_A) - a_row, 0.0)) * dt_row, 0.0)

        cb4 = [_dot_nt(load(ch, 2, SSM_N), jnp.concatenate([load(ch, 1, SSM_N)] * SSM_HPG, axis=0)) for ch in chains]
        hg = [h_scr[ch[5], ch[0], ch[1]] for ch in chains]
        y_off = [_dot(load(ch, 2, SSM_N), h) for ch, h in zip(chains, hg)]
        h_in = [_dot_tn(load(ch, 1, SSM_N), load(ch, 0, SSM_GW) * pick(ch, COL_DTEAL)) for ch in chains]
        scores = [cb * decay_dt_of(ch) for ch, cb in zip(chains, cb4)]
        x_bd = [jnp.where(block_diag, jnp.concatenate([load(ch, 0, SSM_GW)] * SSM_HPG, axis=0), 0.0)
                for ch in chains]
        y_diag = [_dot(sc, xb) for sc, xb in zip(scores, x_bd)]
        for ch, yd, yo, h, hi in zip(chains, y_diag, y_off, hg, h_in):
            d, g, _, r0, _, n = ch
            dir_refs[d][5][n, pl.ds(r0, CHUNK), g * SSM_GW:(g + 1) * SSM_GW] = yd + yo * pick(ch, COL_EA)
            h_scr[n, d, g] = h * pick(ch, COL_EAT, slice(0, 1)) + hi
        return carry

    lax.fori_loop(0, nct, body, 0)

    @pl.when(j == pl.num_programs(1) - 1)
    def _():
        ht_ref[...] = h_scr[...]


def ssd_scan(feat, gcol, arow, h0, tt, nb):
    bsz, t, _ = feat.shape
    nt = t // tt
    nct = tt // CHUNK
    fwd = lambda lane_blk: (lambda b, j: (b, j, lane_blk))
    bwd = lambda lane_blk: (lambda b, j: (b, nt - 1 - j, lane_blk))

    def side(im):
        return [pl.BlockSpec((nb, tt, SSM_DI), im(OFF_X // SSM_DI)),
                pl.BlockSpec((nb, tt, SSM_BC), im(OFF_B // SSM_BC)),
                pl.BlockSpec((nb, tt, SSM_BC), im(OFF_C // SSM_BC)),
                pl.BlockSpec((nb, tt, LANES), im(0))]

    row_spec = lambda rev: pl.BlockSpec((nb, nct, SSD_ROWS, SSM_GW),
                                        (lambda b, j: (b, nt - 1 - j, 0, 0)) if rev else (lambda b, j: (b, j, 0, 0)))
    state_spec = pl.BlockSpec((nb, 2, SSM_G, SSM_N, SSM_GW), lambda b, j: (b, 0, 0, 0, 0))
    return pl.pallas_call(
        functools.partial(_ssd_kernel, nct=nct, nb=nb),
        grid=(bsz // nb, nt),
        in_specs=side(fwd) + [row_spec(False)] + side(bwd) + [row_spec(True)] + [state_spec],
        out_specs=[pl.BlockSpec((nb, tt, SSM_DI), fwd(0)), pl.BlockSpec((nb, tt, SSM_DI), bwd(0)), state_spec],
        out_shape=[
            jax.ShapeDtypeStruct((bsz, t, SSM_DI), F32),
            jax.ShapeDtypeStruct((bsz, t, SSM_DI), F32),
            jax.ShapeDtypeStruct((bsz, 2, SSM_G, SSM_N, SSM_GW), F32),
        ],
        scratch_shapes=[pltpu.VMEM((nb, 2, SSM_G, SSM_N, SSM_GW), F32)],
        compiler_params=_compiler_params(("parallel", "arbitrary")),
        name="ssd_scan",
    )(feat, feat, feat, gcol, arow, feat, feat, feat, gcol, arow, h0)


def _merge_out_kernel(of_ref, ob_ref, yf_ref, yb_ref, xs_ref, z_ref, dng_ref, dskip_ref, ssg_ref,
                      w_ref, g_ref, gate_ref, res_ref, o_ref):
    o = of_ref[0] + ob_ref[0]
    z = z_ref[0]
    parts = []
    for h in range(DN_HEADS):
        lanes = slice(h * DN_DV, (h + 1) * DN_DV)
        parts.append(_rms(o[:, lanes]) * dng_ref[...] * _silu(z[:, lanes]))
    y = yf_ref[0] + yb_ref[0] + dskip_ref[...] * xs_ref[0]
    yz = y * _silu(z[:, DN_VW:])
    gw = SSM_DI // SSM_G
    for g in range(SSM_G):
        lanes = slice(g * gw, (g + 1) * gw)
        parts.append(_rms(yz[:, lanes]) * ssg_ref[:, lanes])
    mixed = jnp.concatenate(parts, axis=1).astype(BF16)
    m = jnp.dot(mixed, w_ref[...], preferred_element_type=F32)
    o_ref[0] = res_ref[0] + gate_ref[0] * (_rms(m) * g_ref[...])


def merge_out(o_f, o_b, y_f, y_b, feat, z, dn_norm_g, dskip, ssm_norm_g, w_out, g, gate, res, tm):
    bsz, t, d = res.shape
    row = lambda b, i: (b, i, 0)
    const = lambda b, i: (0, 0)
    half = lambda: pl.BlockSpec((1, tm, DN_VW), row)
    return pl.pallas_call(
        _merge_out_kernel,
        grid=(bsz, t // tm),
        in_specs=[
            half(), half(), half(), half(),
            pl.BlockSpec((1, tm, SSM_DI), lambda b, i: (b, i, OFF_X // SSM_DI)),
            pl.BlockSpec((1, tm, D_Z), row),
            pl.BlockSpec((1, DN_DV), const),
            pl.BlockSpec((1, SSM_DI), const),
            pl.BlockSpec((1, SSM_DI), const),
            pl.BlockSpec((D_MIX, d), const),
            pl.BlockSpec((1, d), const),
            pl.BlockSpec((1, 1, d), lambda b, i: (b, 0, 0)),
            pl.BlockSpec((1, tm, d), row),
        ],
        out_specs=pl.BlockSpec((1, tm, d), row),
        out_shape=jax.ShapeDtypeStruct((bsz, t, d), F32),
        compiler_params=_compiler_params(("parallel", "parallel")),
        name="merge_out",
    )(o_f, o_b, y_f, y_b, feat, z, dn_norm_g.reshape(1, DN_DV), dskip, ssm_norm_g.reshape(1, SSM_DI),
      w_out, g.reshape(1, d), gate.reshape(bsz, 1, d), res)


MOE_TILE = 256
SLOT_WINDOW = 64
SLOT_ALIGN = 16
CNT_LANES = 32
GATE_PIECES = 3
D_EXT = D_MODEL + LANES


def _router_kernel(x_ref, g_ref, shift_ref, scale_ref, wr_ref, hx_ref, aff_ref):
    d = x_ref.shape[2]
    h = _rms(x_ref[0]) * g_ref[...]
    hb = (h * (1.0 + scale_ref[0]) + shift_ref[0]).astype(BF16)
    logits = lax.dot_general(wr_ref[...], hb, (((1,), (1,)), ((), ())), preferred_element_type=F32)
    ex = jnp.exp(logits - jnp.max(logits, axis=0, keepdims=True))
    aff = ex / jnp.sum(ex, axis=0, keepdims=True)
    aff_ref[0, 0] = aff
    pieces, rest = [], aff
    for _ in range(GATE_PIECES):
        p = rest.astype(BF16).astype(F32)
        pieces.append(p)
        rest = rest - p
    rows = jnp.concatenate(pieces + [jnp.zeros((LANES - GATE_PIECES * N_EXPERTS, MOE_TILE), F32)], axis=0)
    ext = jnp.concatenate([rows[:, m * LANES:(m + 1) * LANES].T for m in range(MOE_TILE // LANES)], axis=0)
    hx_ref[0, :, :d] = hb
    hx_ref[0, :, d:] = ext.astype(BF16)


def router(x, g, shift, scale, wr_t):
    bsz, t, d = x.shape
    nt = t // MOE_TILE
    const = lambda b, j: (0, 0)
    per_b = lambda b, j: (b, 0, 0)
    return pl.pallas_call(
        _router_kernel,
        grid=(bsz, nt),
        in_specs=[
            pl.BlockSpec((1, MOE_TILE, d), lambda b, j: (b, j, 0)),
            pl.BlockSpec((1, d), const),
            pl.BlockSpec((1, 1, d), per_b),
            pl.BlockSpec((1, 1, d), per_b),
            pl.BlockSpec((N_EXPERTS, d), const),
        ],
        out_specs=[
            pl.BlockSpec((1, MOE_TILE, D_EXT), lambda b, j: (b, j, 0)),
            pl.BlockSpec((1, 1, N_EXPERTS, MOE_TILE), lambda b, j: (b, j, 0, 0)),
        ],
        out_shape=[
            jax.ShapeDtypeStruct((bsz, t, D_EXT), BF16),
            jax.ShapeDtypeStruct((bsz, nt, N_EXPERTS, MOE_TILE), F32),
        ],
        compiler_params=_compiler_params(("parallel", "parallel")),
        name="router",
    )(x, g.reshape(1, d), shift.reshape(bsz, 1, d), scale.reshape(bsz, 1, d), wr_t)


def _lane_prefix(mask, upper):
    carry = jnp.zeros((mask.shape[0], 1), F32)
    blocks = []
    for m in range(mask.shape[1] // LANES):
        p = jnp.dot(mask[:, m * LANES:(m + 1) * LANES].astype(BF16), upper, preferred_element_type=F32) + carry
        carry = p[:, LANES - 1:LANES]
        blocks.append(p)
    return jnp.concatenate(blocks, axis=1)


def _select_kernel(aff_ref, pos_ref, posc_ref, cnt_ref, *, nt, cap):
    aff = jnp.concatenate([aff_ref[0, j] for j in range(nt)], axis=1)
    bits = pltpu.bitcast(aff, jnp.int32)

    def bisect(i, thr):
        cand = thr | lax.shift_left(jnp.int32(1), jnp.int32(30) - i)
        n_ge = jnp.sum(jnp.where(bits >= cand, 1.0, 0.0), axis=1, keepdims=True)
        return jnp.where(n_ge >= cap, cand, thr)

    thr = lax.fori_loop(0, 31, bisect, jnp.zeros((N_EXPERTS, 1), jnp.int32))
    above = bits > thr
    tied = bits == thr
    room = cap - jnp.sum(jnp.where(above, 1.0, 0.0), axis=1, keepdims=True)
    src = lax.broadcasted_iota(jnp.int32, (LANES, LANES), 0)
    dst = lax.broadcasted_iota(jnp.int32, (LANES, LANES), 1)
    upper = jnp.where(src <= dst, 1.0, 0.0).astype(BF16)
    tie_rank = _lane_prefix(jnp.where(tied, 1.0, 0.0), upper)
    sel = above | (tied & (tie_rank <= room))
    selc = jnp.where(sel, 1.0, 0.0)
    slot = jnp.where(sel, _lane_prefix(selc, upper) - 1.0, -1.0)
    lane = lax.broadcasted_iota(jnp.int32, (N_EXPERTS, LANES), 1)
    before = jnp.zeros((N_EXPERTS, 1), F32)
    counts = jnp.zeros((N_EXPERTS, LANES), F32)
    for j in range(nt):
        pos_ref[0, j] = slot[:, j * MOE_TILE:(j + 1) * MOE_TILE]
        before = before + jnp.sum(selc[:, j * MOE_TILE:(j + 1) * MOE_TILE], axis=1, keepdims=True)
        counts = jnp.where(lane == j + 1, before, counts)
    cnt_ref[0] = counts.astype(jnp.int32)
    pad = jnp.zeros((LANES - N_EXPERTS, LANES), F32)
    for m in range(slot.shape[1] // LANES):
        posc_ref[0, m * LANES:(m + 1) * LANES, :] = jnp.concatenate([slot[:, m * LANES:(m + 1) * LANES], pad], axis=0).T


def select_slots(aff, cap):
    bsz, nt, ne, tile = aff.shape
    blk = lambda b: (b, 0, 0, 0)
    return pl.pallas_call(
        functools.partial(_select_kernel, nt=nt, cap=cap),
        grid=(bsz,),
        in_specs=[pl.BlockSpec((1, nt, ne, tile), blk)],
        out_specs=[pl.BlockSpec((1, nt, ne, tile), blk),
                   pl.BlockSpec((1, nt * tile, LANES), lambda b: (b, 0, 0)),
                   pl.BlockSpec((1, ne, LANES), lambda b: (b, 0, 0))],
        out_shape=[jax.ShapeDtypeStruct((bsz, nt, ne, tile), F32),
                   jax.ShapeDtypeStruct((bsz, nt * tile, LANES), F32),
                   jax.ShapeDtypeStruct((bsz, ne, LANES), jnp.int32)],
        compiler_params=_compiler_params(("parallel",)),
        name="select_slots",
    )(aff)


def _window_start(cnt_ref, row, j):
    return pl.multiple_of((cnt_ref[row, j] // SLOT_ALIGN) * SLOT_ALIGN, SLOT_ALIGN)


def _extra_windows(cnt_ref, row, j, start):
    return (cnt_ref[row, j + 1] - start - 1) // SLOT_WINDOW


def _gather_kernel(cnt_ref, hx_ref, pos_ref, xs_ref):
    b = pl.program_id(0)
    j = pl.program_id(1)

    @pl.when(j == 0)
    def _():
        xs_ref[...] = jnp.zeros_like(xs_ref)

    hx = hx_ref[0]
    slot_iota = lax.broadcasted_iota(jnp.int32, (SLOT_WINDOW, MOE_TILE), 0).astype(F32)

    def one_hot(e, start):
        return jnp.where(pos_ref[0, 0, e:e + 1, :] - start.astype(F32) == slot_iota, 1.0, 0.0).astype(BF16)

    starts = [_window_start(cnt_ref, b * N_EXPERTS + e, j) for e in range(N_EXPERTS)]
    stacked = jnp.concatenate([one_hot(e, starts[e]) for e in range(N_EXPERTS)], axis=0)
    rows = jnp.dot(stacked, hx, preferred_element_type=F32)
    for e in range(N_EXPERTS):
        xs_ref[0, e, pl.ds(starts[e], SLOT_WINDOW), :] += rows[e * SLOT_WINDOW:(e + 1) * SLOT_WINDOW].astype(BF16)

    for e in range(N_EXPERTS):
        n_more = _extra_windows(cnt_ref, b * N_EXPERTS + e, j, starts[e])

        @pl.when(n_more > 0)
        def _(e=e, n_more=n_more):
            def more(i, carry):
                start = pl.multiple_of(starts[e] + i * SLOT_WINDOW, SLOT_ALIGN)
                r = jnp.dot(one_hot(e, start), hx, preferred_element_type=F32)
                xs_ref[0, e, pl.ds(start, SLOT_WINDOW), :] += r.astype(BF16)
                return carry
            lax.fori_loop(1, n_more + 1, more, 0)


def gather_tokens(cnt, hx, pos, cap):
    bsz, t, d_ext = hx.shape
    nt = t // MOE_TILE
    rows = cap + SLOT_WINDOW
    grid_spec = pltpu.PrefetchScalarGridSpec(
        num_scalar_prefetch=1,
        grid=(bsz, nt),
        in_specs=[
            pl.BlockSpec((1, MOE_TILE, d_ext), lambda b, j, c: (b, j, 0)),
            pl.BlockSpec((1, 1, N_EXPERTS, MOE_TILE), lambda b, j, c: (b, j, 0, 0)),
        ],
        out_specs=pl.BlockSpec((1, N_EXPERTS, rows, d_ext), lambda b, j, c: (b, 0, 0, 0),
                               pipeline_mode=pl.Buffered(1)),
    )
    return pl.pallas_call(
        _gather_kernel,
        grid_spec=grid_spec,
        out_shape=jax.ShapeDtypeStruct((bsz, N_EXPERTS, rows, d_ext), BF16),
        compiler_params=_compiler_params(("parallel", "arbitrary")),
        name="gather_tokens",
    )(cnt, hx, pos)


def _expert_ffn_kernel(xs_ref, wg_ref, wu_ref, wd_ref, o_ref):
    e = pl.program_id(0)
    d = wg_ref.shape[1]
    nbs = xs_ref.shape[0]
    cap = xs_ref.shape[2] - SLOT_WINDOW
    xs = jnp.concatenate([xs_ref[n, 0, :cap, :d] for n in range(nbs)], axis=0)
    ext = jnp.concatenate([xs_ref[n, 0, :cap, d:] for n in range(nbs)], axis=0).astype(F32)
    lane = lax.broadcasted_iota(jnp.int32, ext.shape, 1)
    mine = (lane % N_EXPERTS == e) & (lane < GATE_PIECES * N_EXPERTS)
    gate = jnp.sum(jnp.where(mine, ext, 0.0), axis=1, keepdims=True)
    a = jnp.dot(xs, wg_ref[0], preferred_element_type=F32)
    u = jnp.dot(xs, wu_ref[0], preferred_element_type=F32)
    hid = (_silu(a) * u).astype(BF16)
    out = (jnp.dot(hid, wd_ref[0], preferred_element_type=F32) * gate).astype(BF16)
    for n in range(nbs):
        o_ref[n, 0, :cap] = out[n * cap:(n + 1) * cap]
        o_ref[n, 0, cap:] = jnp.zeros((SLOT_WINDOW, d), BF16)


FFN_ROWS = 512


def expert_ffn(xs, wg, wu, wd):
    bsz, ne, rows, d_ext = xs.shape
    _, d, f = wg.shape
    nbs = min(bsz, max(1, FFN_ROWS // (rows - SLOT_WINDOW)))
    return pl.pallas_call(
        _expert_ffn_kernel,
        grid=(ne, bsz // nbs),
        in_specs=[
            pl.BlockSpec((nbs, 1, rows, d_ext), lambda e, b: (b, e, 0, 0)),
            pl.BlockSpec((1, d, f), lambda e, b: (e, 0, 0)),
            pl.BlockSpec((1, d, f), lambda e, b: (e, 0, 0)),
            pl.BlockSpec((1, f, d), lambda e, b: (e, 0, 0)),
        ],
        out_specs=pl.BlockSpec((nbs, 1, rows, d), lambda e, b: (b, e, 0, 0)),
        out_shape=jax.ShapeDtypeStruct((bsz, ne, rows, d), BF16),
        compiler_params=_compiler_params(("parallel", "parallel")),
        name="expert_ffn",
    )(xs, wg, wu, wd)


def _scatter_out_kernel(cnt_ref, og_ref, posc_ref, g_ref, gate_ref, res_ref, o_ref):
    b = pl.program_id(0)
    j = pl.program_id(1)
    posc = posc_ref[0]
    starts = [_window_start(cnt_ref, b * N_EXPERTS + e, j) for e in range(N_EXPERTS)]
    lane = lax.broadcasted_iota(jnp.int32, (MOE_TILE, LANES), 1)
    lane_f = lane.astype(F32)
    per_tile = LANES // SLOT_WINDOW
    tiles = []
    for m in range(N_EXPERTS // per_tile):
        rel = None
        for q in range(per_tile - 1, -1, -1):
            e = m * per_tile + q
            v = posc[:, e:e + 1] - starts[e].astype(F32) + float(q * SLOT_WINDOW)
            rel = v if rel is None else jnp.where(lane < (q + 1) * SLOT_WINDOW, v, rel)
        tiles.append(jnp.where(rel == lane_f, 1.0, 0.0).astype(BF16))
    sel_t = jnp.concatenate(tiles, axis=1)
    og = jnp.concatenate([og_ref[0, e, pl.ds(starts[e], SLOT_WINDOW), :] for e in range(N_EXPERTS)], axis=0)
    o_ref[0] = jnp.dot(sel_t, og, preferred_element_type=F32)

    win_lane = lax.broadcasted_iota(jnp.int32, (MOE_TILE, SLOT_WINDOW), 1).astype(F32)
    for e in range(N_EXPERTS):
        n_more = _extra_windows(cnt_ref, b * N_EXPERTS + e, j, starts[e])

        @pl.when(n_more > 0)
        def _(e=e, n_more=n_more):
            def more(i, carry):
                start = pl.multiple_of(starts[e] + i * SLOT_WINDOW, SLOT_ALIGN)
                p = jnp.where(posc[:, e:e + 1] - start.astype(F32) == win_lane, 1.0, 0.0).astype(BF16)
                o_ref[0] += jnp.dot(p, og_ref[0, e, pl.ds(start, SLOT_WINDOW), :], preferred_element_type=F32)
                return carry
            lax.fori_loop(1, n_more + 1, more, 0)

    o_ref[0] = res_ref[0] + gate_ref[0] * (_rms(o_ref[0]) * g_ref[...])


def scatter_out(cnt, og, posc, g, gate, res):
    bsz, t, d = res.shape
    nt = t // MOE_TILE
    _, ne, rows, _ = og.shape
    grid_spec = pltpu.PrefetchScalarGridSpec(
        num_scalar_prefetch=1,
        grid=(bsz, nt),
        in_specs=[
            pl.BlockSpec((1, ne, rows, d), lambda b, j, c: (b, 0, 0, 0), pipeline_mode=pl.Buffered(1)),
            pl.BlockSpec((1, MOE_TILE, LANES), lambda b, j, c: (b, j, 0)),
            pl.BlockSpec((1, d), lambda b, j, c: (0, 0)),
            pl.BlockSpec((1, 1, d), lambda b, j, c: (b, 0, 0)),
            pl.BlockSpec((1, MOE_TILE, d), lambda b, j, c: (b, j, 0)),
        ],
        out_specs=pl.BlockSpec((1, MOE_TILE, d), lambda b, j, c: (b, j, 0)),
    )
    return pl.pallas_call(
        _scatter_out_kernel,
        grid_spec=grid_spec,
        out_shape=jax.ShapeDtypeStruct((bsz, t, d), F32),
        compiler_params=_compiler_params(("parallel", "parallel")),
        name="scatter_out",
    )(cnt, og, posc, g.reshape(1, d), gate.reshape(bsz, 1, d), res)


def ec_moe_residual(x, g_in, shift, scale, wr_t, wg, wu, wd, g_out, gate):
    bsz, t, _ = x.shape
    cap = EC_CAPACITY_FACTOR * t // N_EXPERTS
    hx, aff = router(x, g_in, shift, scale, wr_t)
    pos, posc, counts = select_slots(aff, cap)
    cnt = counts[:, :, :CNT_LANES].reshape(bsz * N_EXPERTS, CNT_LANES)
    xs = gather_tokens(cnt, hx, pos, cap)
    og = expert_ffn(xs, wg, wu, wd)
    return scatter_out(cnt, og, posc, g_out, gate, x)


def _token_tile(t):
    return min(512, t)


def mixer_stream(x, g0, shift, scale, wc, wz, ws, w9, conv_b, bias_rows, nega_rows, dn_state, ssm_state, on_grid):
    t = x.shape[1]
    tm = _token_tile(t)
    feat, z, gcol, grow, arow = in_proj_conv(x, g0, shift, scale, wc, wz, ws, w9, conv_b, bias_rows, nega_rows,
                                             tm, on_grid)
    ts = min(SCAN_TILE, t)
    o_f, o_b, dn_state = delta_scan(feat, gcol, grow, dn_state, ts, SCAN_SAMPLES_PER_STEP)
    y_f, y_b, ssm_state = ssd_scan(feat, gcol, arow, ssm_state, ts, SCAN_SAMPLES_PER_STEP)
    return (o_f, o_b, y_f, y_b, feat, z), dn_state, ssm_state


def _gate_param_rows(dn_bias, dn_a_log, ssm_bias, ssm_a_log):
    zeros = jnp.zeros((NDH,), F32)
    bias = jnp.concatenate([zeros, dn_bias.reshape(-1), ssm_bias.reshape(-1)])
    nega = jnp.concatenate([zeros, -jnp.exp(dn_a_log.reshape(-1)), -jnp.exp(ssm_a_log.reshape(-1))])
    pad = LANES - bias.shape[0]
    expand = lambda v: jnp.broadcast_to(jnp.pad(v, (0, pad))[:, None], (LANES, LANES))
    return expand(bias), expand(nega)


def kernel(x, c, ctx, c_ctx, ada_w, ada_b, norm_g, w_in, conv_w, conv_b, dn_A_log, dn_dt_bias,
           dn_norm_g, ssm_A_log, ssm_dt_bias, ssm_D, ssm_norm_g, w_out, router_w,
           exp_w_gate, exp_w_up, exp_w_down):
    bsz = x.shape[0]
    s_lat = jax.nn.silu(c)
    s_ctx = jax.nn.silu(c_ctx)
    for l in range(DEPTH):
        last = l == DEPTH - 1
        mod_lat = jnp.split(s_lat @ ada_w[l] + ada_b[l], 6, axis=-1)
        mod_ctx_row = s_ctx @ ada_w[l] + ada_b[l]
        mod_ctx = [jnp.broadcast_to(m[None, :], (bsz, D_MODEL)) for m in jnp.split(mod_ctx_row, 6)]

        w_in_l = w_in[l]
        wc = w_in_l[:, :CONV_CH].astype(BF16)
        wz = w_in_l[:, CONV_CH:CONV_CH + D_Z].astype(BF16)
        ws = jnp.pad(w_in_l[:, CONV_CH + D_Z:], ((0, 0), (0, LANES - N_GATE_COLS))).astype(BF16)
        w9 = conv_w[l].reshape(CONV_CH, CONV_K * CONV_K).T
        cb = conv_b[l].reshape(1, CONV_CH)
        bias_rows, nega_rows = _gate_param_rows(dn_dt_bias[l], dn_A_log[l], ssm_dt_bias[l], ssm_A_log[l])
        dskip = jnp.repeat(ssm_D[l], SSM_P).reshape(1, SSM_DI)
        w_out_l = w_out[l].astype(BF16)
        moe_w = (router_w[l].T.astype(BF16), exp_w_gate[l].astype(BF16), exp_w_up[l].astype(BF16),
                 exp_w_down[l].astype(BF16), norm_g[l, 3])

        dn0 = jnp.zeros((bsz, 2, DN_HEADS, DN_DK, DN_DV), F32)
        ssm0 = jnp.zeros((bsz, 2, SSM_G, SSM_N, SSM_GW), F32)
        shared = (wc, wz, ws, w9, cb, bias_rows, nega_rows)
        mix_ctx, dn_c, ssm_c = mixer_stream(ctx, norm_g[l, 0], mod_ctx[0], mod_ctx[1], *shared, dn0, ssm0, False)
        mix_lat, _, _ = mixer_stream(x, norm_g[l, 0], mod_lat[0], mod_lat[1], *shared, dn_c, ssm_c, True)
        merge_w = (dn_norm_g[l], dskip, ssm_norm_g[l], w_out_l, norm_g[l, 1])
        x = merge_out(*mix_lat, *merge_w, mod_lat[2], x, _token_tile(x.shape[1]))

        x = ec_moe_residual(x, norm_g[l, 2], mod_lat[3], mod_lat[4], *moe_w, mod_lat[5])

        if not last:
            ctx = merge_out(*mix_ctx, *merge_w, mod_ctx[2], ctx, _token_tile(ctx.shape[1]))
            ctx = ec_moe_residual(ctx, norm_g[l, 2], mod_ctx[3], mod_ctx[4], *moe_w, mod_ctx[5])
    return x
```

```python
import functools

import jax
import jax.numpy as jnp
import numpy as np
from jax import lax
from jax.experimental import pallas as pl
from jax.experimental.pallas import tpu as pltpu

D_MODEL = 1024
DEPTH = 4
GRID_W = 64
DN_HEADS = 4
DN_DK = 128
DN_DV = 128
SSM_HEADS = 8
SSM_P = 64
SSM_N = 128
SSM_G = 2
CHUNK = 64
CONV_K = 3
N_EXPERTS = 16
EC_CAPACITY_FACTOR = 2
D_EXPERT = 512
EPS = 1e-6

DN_QK = DN_HEADS * DN_DK
DN_VW = DN_HEADS * DN_DV
SSM_DI = SSM_HEADS * SSM_P
SSM_BC = SSM_G * SSM_N
SSM_HPG = SSM_HEADS // SSM_G
SSM_GW = SSM_HPG * SSM_P
D_MIX = DN_VW + SSM_DI
CONV_SPLITS = (DN_QK, DN_QK, DN_VW, SSM_DI, SSM_BC, SSM_BC)
CONV_CH = sum(CONV_SPLITS)
D_Z = DN_VW + SSM_DI
N_GATE_COLS = 2 * DN_HEADS + 2 * DN_HEADS + 2 * SSM_HEADS
D_IN_PROJ = CONV_CH + D_Z + N_GATE_COLS

LANES = 128
SUBLANES = 8
VMEM_LIMIT_BYTES = 56 * 1024 * 1024

OFF_Q, OFF_K, OFF_V = 0, DN_QK, 2 * DN_QK
OFF_X = 2 * DN_QK + DN_VW
OFF_B = OFF_X + SSM_DI
OFF_C = OFF_B + SSM_BC

NDH = 2 * DN_HEADS
NSH = 2 * SSM_HEADS
COL_BETA, COL_G, COL_EG, COL_EGL, COL_EGT = (i * NDH for i in range(5))
COL_DT, COL_A, COL_EA, COL_DTEAL, COL_EAT = (5 * NDH + i * NSH for i in range(5))

F32 = jnp.float32
BF16 = jnp.bfloat16


def _compiler_params(semantics):
    return pltpu.CompilerParams(dimension_semantics=semantics, vmem_limit_bytes=VMEM_LIMIT_BYTES)


def _rms(x):
    return x * lax.rsqrt(jnp.mean(x * x, axis=-1, keepdims=True) + EPS)


def _silu(x):
    return x * jax.nn.sigmoid(x)


def _softplus(x):
    return jnp.maximum(x, 0.0) + jnp.log(1.0 + jnp.exp(-jnp.abs(x)))


def _dot(a, b):
    return jnp.dot(a.astype(BF16), b.astype(BF16), preferred_element_type=F32)


def _dot_nt(a, b):
    return lax.dot_general(a.astype(BF16), b.astype(BF16), (((1,), (1,)), ((), ())),
                           preferred_element_type=F32)


def _dot_tn(a, b):
    return lax.dot_general(a.astype(BF16), b.astype(BF16), (((0,), (0,)), ((), ())),
                           preferred_element_type=F32)


def _dot_exact01(a, m01):
    a1 = a.astype(BF16)
    r1 = a - a1.astype(F32)
    a2 = r1.astype(BF16)
    a3 = (r1 - a2.astype(F32)).astype(BF16)
    m = m01.astype(BF16)
    out = jnp.dot(a3, m, preferred_element_type=F32)
    out = out + jnp.dot(a2, m, preferred_element_type=F32)
    return out + jnp.dot(a1, m, preferred_element_type=F32)


SSD_ROWS = 2 * 2 * SSM_G
HALO_BLOCK = 128
HALO = GRID_W + 16
CONV_LANES = 256


def _in_proj_conv_kernel(xp_ref, x_ref, xn_ref, g_ref, shift_ref, scale_ref, wc_ref, wz_ref, ws_ref, w9_ref, cb_ref,
                         gbias_ref, gnega_ref, of_ref, oz_ref, col_ref, grow_ref, arow_ref, *, on_grid):
    i = pl.program_id(1)
    n_i = pl.num_programs(1)
    tm = x_ref.shape[1]

    def modulated(x):
        h = _rms(x) * g_ref[...]
        return h * (1.0 + scale_ref[0]) + shift_ref[0]

    has_prev = jnp.where(i > 0, 1.0, 0.0)
    has_next = jnp.where(i < n_i - 1, 1.0, 0.0)
    h_main = modulated(x_ref[0]).astype(BF16)
    h_prev = (modulated(xp_ref[0, HALO_BLOCK - HALO:, :]) * has_prev).astype(BF16)
    h_next = (modulated(xn_ref[0, :HALO, :]) * has_next).astype(BF16)
    h_all = jnp.concatenate([h_prev, h_main, h_next], axis=0)

    oz_ref[0] = jnp.dot(h_main, wz_ref[...], preferred_element_type=F32)
    _write_gates(jnp.dot(h_main, ws_ref[...], preferred_element_type=F32), gbias_ref[...], gnega_ref[...],
                 col_ref.at[0], grow_ref.at[0], arow_ref.at[0])

    dys = (-1, 0, 1) if on_grid else (0,)
    col = lax.broadcasted_iota(jnp.int32, (tm, CONV_LANES), 0) % GRID_W
    n_blocks = CONV_CH // CONV_LANES

    def project(c):
        return jnp.dot(h_all, wc_ref[:, c * CONV_LANES:(c + 1) * CONV_LANES], preferred_element_type=F32)

    def conv(c, p):
        lanes = slice(c * CONV_LANES, (c + 1) * CONV_LANES)
        acc_c = acc_m = acc_p = None
        for dy in dys:
            lo = HALO + dy * GRID_W
            win = p[lo - SUBLANES:lo + tm + SUBLANES]
            wr = 3 * (dy + 1)
            tc = win[SUBLANES:SUBLANES + tm] * w9_ref[wr + 1:wr + 2, lanes]
            tl, tr = win * w9_ref[wr:wr + 1, lanes], win * w9_ref[wr + 2:wr + 3, lanes]
            acc_c = tc if acc_c is None else acc_c + tc
            acc_m = tl if acc_m is None else acc_m + tl
            acc_p = tr if acc_p is None else acc_p + tr
        acc_m = pltpu.roll(acc_m, 1, 0)[SUBLANES:SUBLANES + tm]
        acc_p = pltpu.roll(acc_p, tm + 2 * SUBLANES - 1, 0)[SUBLANES:SUBLANES + tm]
        if on_grid:
            acc_m = jnp.where(col != 0, acc_m, 0.0)
            acc_p = jnp.where(col != GRID_W - 1, acc_p, 0.0)
        u = _silu(acc_c + acc_m + acc_p + cb_ref[:, lanes])
        if c * CONV_LANES < 2 * DN_QK:
            scale = DN_DK ** -0.5 if c * CONV_LANES < DN_QK else 1.0
            heads = []
            for hh in range(CONV_LANES // DN_DK):
                uh = u[:, hh * DN_DK:(hh + 1) * DN_DK]
                heads.append(uh * (lax.rsqrt(jnp.sum(uh * uh, axis=-1, keepdims=True) + EPS) * scale))
            u = jnp.concatenate(heads, axis=1)
        of_ref[0, :, lanes] = u

    p = project(0)
    for c in range(n_blocks):
        p_next = project(c + 1) if c + 1 < n_blocks else None
        conv(c, p)
        p = p_next


def in_proj_conv(x, g, shift, scale, wc, wz, ws, w9, conv_b, bias_rows, nega_rows, tm, on_grid):
    bsz, t, d = x.shape
    nc = t // CHUNK
    ncg = tm // CHUNK
    per_tile = tm // HALO_BLOCK
    last_halo = t // HALO_BLOCK - 1
    row = lambda b, i: (b, i, 0)
    const = lambda b, i: (0, 0)
    per_b = lambda b, i: (b, 0, 0)
    return pl.pallas_call(
        functools.partial(_in_proj_conv_kernel, on_grid=on_grid),
        grid=(bsz, t // tm),
        in_specs=[
            pl.BlockSpec((1, HALO_BLOCK, d), lambda b, i: (b, jnp.maximum(i * per_tile - 1, 0), 0)),
            pl.BlockSpec((1, tm, d), row),
            pl.BlockSpec((1, HALO_BLOCK, d), lambda b, i: (b, jnp.minimum((i + 1) * per_tile, last_halo), 0)),
            pl.BlockSpec((1, d), const),
            pl.BlockSpec((1, 1, d), per_b),
            pl.BlockSpec((1, 1, d), per_b),
            pl.BlockSpec((d, CONV_CH), const),
            pl.BlockSpec((d, D_Z), const),
            pl.BlockSpec((d, LANES), const),
            pl.BlockSpec((9, CONV_CH), const),
            pl.BlockSpec((1, CONV_CH), const),
            pl.BlockSpec((LANES, LANES), const),
            pl.BlockSpec((LANES, LANES), const),
        ],
        out_specs=[
            pl.BlockSpec((1, tm, CONV_CH), row),
            pl.BlockSpec((1, tm, D_Z), row),
            pl.BlockSpec((1, tm, LANES), row),
            pl.BlockSpec((1, ncg, NDH, CHUNK), lambda b, i: (b, i, 0, 0)),
            pl.BlockSpec((1, ncg, SSD_ROWS, SSM_GW), lambda b, i: (b, i, 0, 0)),
        ],
        out_shape=[
            jax.ShapeDtypeStruct((bsz, t, CONV_CH), F32),
            jax.ShapeDtypeStruct((bsz, t, D_Z), F32),
            jax.ShapeDtypeStruct((bsz, t, LANES), F32),
            jax.ShapeDtypeStruct((bsz, nc, NDH, CHUNK), F32),
            jax.ShapeDtypeStruct((bsz, nc, SSD_ROWS, SSM_GW), F32),
        ],
        compiler_params=_compiler_params(("parallel", "parallel")),
        name="in_proj_conv",
    )(x, x, x, g.reshape(1, d), shift.reshape(bsz, 1, d), scale.reshape(bsz, 1, d), wc, wz, ws, w9, conv_b,
      bias_rows, nega_rows)


def _write_gates(small, bias, nega, col_ref, grow_ref, arow_ref):
    tok = lax.broadcasted_iota(jnp.int32, (LANES, LANES), 0)
    out = lax.broadcasted_iota(jnp.int32, (LANES, LANES), 1)
    same = (tok // CHUNK) == (out // CHUNK)
    m_fwd = jnp.where(same & (tok <= out), 1.0, 0.0)
    m_bwd = jnp.where(same & (tok >= out), 1.0, 0.0)
    m_all = jnp.where(same, 1.0, 0.0)

    def dir_cumsum(v, heads):
        f = _dot_exact01(v, m_fwd)
        b = _dot_exact01(v, m_bwd)
        is_fwd = lax.broadcasted_iota(jnp.int32, v.shape, 0) < heads
        return jnp.where(is_fwd, f, b), _dot_exact01(v, m_all)

    for s in range(small.shape[0] // LANES):
        st = small[s * LANES:(s + 1) * LANES, :].T
        beta = jax.nn.sigmoid(st[0:NDH])
        sp = _softplus(st[NDH:2 * NDH + NSH] + bias[NDH:2 * NDH + NSH])
        logg = sp[0:NDH] * nega[NDH:2 * NDH]
        dt = sp[NDH:]
        a = dt * nega[2 * NDH:2 * NDH + NSH]
        g_cs, g_tot = dir_cumsum(logg, DN_HEADS)
        a_cs, a_tot = dir_cumsum(a, SSM_HEADS)
        rows = jnp.concatenate([
            beta, g_cs, jnp.exp(g_cs), jnp.exp(g_tot - g_cs), jnp.exp(g_tot),
            dt, a_cs, jnp.exp(a_cs), dt * jnp.exp(a_tot - a_cs), jnp.exp(a_tot),
            jnp.zeros((LANES - 5 * NDH - 5 * NSH, LANES), F32)], axis=0)
        col_ref[s * LANES:(s + 1) * LANES, :] = rows.T
        for half in range(LANES // CHUNK):
            c = s * (LANES // CHUNK) + half
            lo = half * CHUNK
            grow_ref[c] = g_cs[:, lo:lo + CHUNK]
            for q, v in enumerate((a_cs, dt)):
                for dg in range(2 * SSM_G):
                    r0 = dg * SSM_HPG
                    r = q * 2 * SSM_G + dg
                    arow_ref[c, r:r + 1, :] = jnp.concatenate(
                        [v[r0 + hh:r0 + hh + 1, lo:lo + CHUNK] for hh in range(SSM_HPG)], axis=1)


SCAN_SAMPLES_PER_STEP = 4
SCAN_TILE = 256


def _chunk_masks(rev):
    r = lax.broadcasted_iota(jnp.int32, (CHUNK, CHUNK), 0)
    c = lax.broadcasted_iota(jnp.int32, (CHUNK, CHUNK), 1)
    return ((r <= c), (r < c)) if rev else ((r >= c), (r > c))


def _unit_tri_inverse(ms):
    eye = (lax.broadcasted_iota(jnp.int32, (CHUNK, CHUNK), 0)
           == lax.broadcasted_iota(jnp.int32, (CHUNK, CHUNK), 1)).astype(F32)
    ps = [_dot(m, m) for m in ms]
    ts = [eye + m for m in ms]
    n_sq = int(np.log2(CHUNK)) - 1
    for _ in range(n_sq - 1):
        rs = [_dot(jnp.concatenate([t, p], axis=0), p) for t, p in zip(ts, ps)]
        ts = [t + r[:CHUNK] for t, r in zip(ts, rs)]
        ps = [r[CHUNK:] for r in rs]
    return [t + _dot(t, p) for t, p in zip(ts, ps)]


def _dn_kernel(qf_ref, kf_ref, vf_ref, cf_ref, rf_ref, qb_ref, kb_ref, vb_ref, cb_ref, rb_ref, s0_ref,
               of_ref, ob_ref, st_ref, s_scr, *, nct, nb):
    j = pl.program_id(1)

    @pl.when(j == 0)
    def _():
        s_scr[...] = s0_ref[...]

    masks = (_chunk_masks(False), _chunk_masks(True))
    dir_refs = ((qf_ref, kf_ref, vf_ref, cf_ref, rf_ref, of_ref),
                (qb_ref, kb_ref, vb_ref, cb_ref, rb_ref, ob_ref))

    def body(c, carry):
        chains = []
        for n in range(nb):
            for d in range(2):
                cc = c if d == 0 else nct - 1 - c
                r0 = pl.multiple_of(cc * CHUNK, CHUNK)
                ct = dir_refs[d][3][n, pl.ds(r0, CHUNK), :]
                for h in range(DN_HEADS):
                    chains.append((d, h, cc, r0, ct, n))

        def load(chain, which):
            d, h, _, r0, _, n = chain
            return dir_refs[d][which][n, pl.ds(r0, CHUNK), h * DN_DK:(h + 1) * DN_DK]

        def col(chain, off, rows=slice(None)):
            d, h, _, _, ct, _ = chain
            lane = off + d * DN_HEADS + h
            return ct[rows, lane:lane + 1]

        def decay_of(chain):
            d, h, cc, _, _, n = chain
            incl = masks[d][0]
            g_row = dir_refs[d][4][n, cc, d * DN_HEADS + h:d * DN_HEADS + h + 1, :]
            return jnp.where(incl, jnp.exp(jnp.where(incl, col(chain, COL_G) - g_row, 0.0)), 0.0)

        kbeta = [load(ch, 1) * col(ch, COL_BETA) for ch in chains]
        a = [_dot_nt(jnp.concatenate([kb, load(ch, 0)], axis=0), load(ch, 1)) for ch, kb in zip(chains, kbeta)]
        decay = [decay_of(ch) for ch in chains]
        neg_l = [jnp.where(masks[ch[0]][1], -(ai[:CHUNK] * dc), 0.0) for ch, ai, dc in zip(chains, a, decay)]
        qk = [ai[CHUNK:] * dc for ai, dc in zip(a, decay)]
        t_inv = _unit_tri_inverse(neg_l)
        uw = [_dot(ti, jnp.concatenate([load(ch, 2) * col(ch, COL_BETA), kb * col(ch, COL_EG)], axis=1))
              for ch, ti, kb in zip(chains, t_inv, kbeta)]
        s = [s_scr[ch[5], ch[0], ch[1]] for ch in chains]
        ws_qs = [_dot(jnp.concatenate([uwi[:, DN_DV:], load(ch, 0) * col(ch, COL_EG)], axis=0), si)
                 for ch, uwi, si in zip(chains, uw, s)]
        v_new = [uwi[:, :DN_DV] - wq[:CHUNK] for uwi, wq in zip(uw, ws_qs)]
        o = [wq[CHUNK:] + _dot(qki, vn) for wq, qki, vn in zip(ws_qs, qk, v_new)]
        s_new = [si * col(ch, COL_EGT, slice(0, 1)) + _dot_tn(load(ch, 1) * col(ch, COL_EGL), vn)
                 for ch, si, vn in zip(chains, s, v_new)]
        for ch, oi, sn in zip(chains, o, s_new):
            d, h, _, r0, _, n = ch
            dir_refs[d][5][n, pl.ds(r0, CHUNK), h * DN_DV:(h + 1) * DN_DV] = oi
            s_scr[n, d, h] = sn
        return carry

    lax.fori_loop(0, nct, body, 0)

    @pl.when(j == pl.num_programs(1) - 1)
    def _():
        st_ref[...] = s_scr[...]


def delta_scan(feat, gcol, grow, s0, tt, nb):
    bsz, t, _ = feat.shape
    nt = t // tt
    nct = tt // CHUNK
    fwd = lambda lane_blk: (lambda b, j: (b, j, lane_blk))
    bwd = lambda lane_blk: (lambda b, j: (b, nt - 1 - j, lane_blk))
    qkv_blk = (nb, tt, DN_QK)

    def side(im):
        return [pl.BlockSpec(qkv_blk, im(OFF_Q // DN_QK)), pl.BlockSpec(qkv_blk, im(OFF_K // DN_QK)),
                pl.BlockSpec(qkv_blk, im(OFF_V // DN_QK)), pl.BlockSpec((nb, tt, LANES), im(0))]

    row_spec = lambda rev: pl.BlockSpec((nb, nct, NDH, CHUNK),
                                        (lambda b, j: (b, nt - 1 - j, 0, 0)) if rev else (lambda b, j: (b, j, 0, 0)))
    state_spec = pl.BlockSpec((nb, 2, DN_HEADS, DN_DK, DN_DV), lambda b, j: (b, 0, 0, 0, 0))
    return pl.pallas_call(
        functools.partial(_dn_kernel, nct=nct, nb=nb),
        grid=(bsz // nb, nt),
        in_specs=side(fwd) + [row_spec(False)] + side(bwd) + [row_spec(True)] + [state_spec],
        out_specs=[pl.BlockSpec((nb, tt, DN_VW), fwd(0)), pl.BlockSpec((nb, tt, DN_VW), bwd(0)), state_spec],
        out_shape=[
            jax.ShapeDtypeStruct((bsz, t, DN_VW), F32),
            jax.ShapeDtypeStruct((bsz, t, DN_VW), F32),
            jax.ShapeDtypeStruct((bsz, 2, DN_HEADS, DN_DK, DN_DV), F32),
        ],
        scratch_shapes=[pltpu.VMEM((nb, 2, DN_HEADS, DN_DK, DN_DV), F32)],
        compiler_params=_compiler_params(("parallel", "arbitrary")),
        name="delta_scan",
    )(feat, feat, feat, gcol, grow, feat, feat, feat, gcol, grow, s0)


def _group_lane_select(cols):
    shape = (cols[0].shape[0], SSM_GW)
    head = lax.broadcasted_iota(jnp.int32, shape, 1) // SSM_P
    out = jnp.broadcast_to(cols[SSM_HPG - 1], shape)
    for hh in range(SSM_HPG - 2, -1, -1):
        out = jnp.where(head == hh, jnp.broadcast_to(cols[hh], shape), out)
    return out


def _ssd_kernel(xf_ref, bf_ref, cf_ref, gf_ref, rf_ref, xb_ref, bb_ref, cb_ref, gb_ref, rb_ref, h0_ref,
                yf_ref, yb_ref, ht_ref, h_scr, *, nct, nb):
    j = pl.program_id(1)

    @pl.when(j == 0)
    def _():
        h_scr[...] = h0_ref[...]

    row_i = lax.broadcasted_iota(jnp.int32, (CHUNK, SSM_GW), 0)
    col_j = lax.broadcasted_iota(jnp.int32, (CHUNK, SSM_GW), 1) % SSM_P
    incl_dir = (row_i >= col_j, row_i <= col_j)
    bd_rows = lax.broadcasted_iota(jnp.int32, (SSM_HPG * CHUNK, SSM_GW), 0) // CHUNK
    bd_cols = lax.broadcasted_iota(jnp.int32, (SSM_HPG * CHUNK, SSM_GW), 1) // SSM_P
    block_diag = bd_rows == bd_cols
    dir_refs = ((xf_ref, bf_ref, cf_ref, gf_ref, rf_ref, yf_ref),
                (xb_ref, bb_ref, cb_ref, gb_ref, rb_ref, yb_ref))

    def body(c, carry):
        chains = []
        for n in range(nb):
            for d in range(2):
                cc = c if d == 0 else nct - 1 - c
                r0 = pl.multiple_of(cc * CHUNK, CHUNK)
                ct = dir_refs[d][3][n, pl.ds(r0, CHUNK), :]
                for g in range(SSM_G):
                    chains.append((d, g, cc, r0, ct, n))

        def load(chain, which, width):
            d, g, _, r0, _, n = chain
            return dir_refs[d][which][n, pl.ds(r0, CHUNK), g * width:(g + 1) * width]

        def pick(chain, off, rows=slice(None)):
            d, g, _, _, ct, _ = chain
            h0 = off + d * SSM_HEADS + g * SSM_HPG
            return _group_lane_select([ct[rows, h0 + hh:h0 + hh + 1] for hh in range(SSM_HPG)])

        def decay_dt_of(chain):
            d, g, cc, _, _, n = chain
            incl = incl_dir[d]
            dg = d * SSM_G + g
            a_row = dir_refs[d][4][n, cc, dg:dg + 1, :]
            dt_row = dir_refs[d][4][n, cc, 2 * SSM_G + dg:2 * SSM_G + dg + 1, :]
            return jnp.where(incl, jnp.exp(jnp.where(incl, pick(chain, COL_A) - a_row, 0.0)) * dt_row, 0.0)

        cb4 = [_dot_nt(load(ch, 2, SSM_N), jnp.concatenate([load(ch, 1, SSM_N)] * SSM_HPG, axis=0)) for ch in chains]
        hg = [h_scr[ch[5], ch[0], ch[1]] for ch in chains]
        y_off = [_dot(load(ch, 2, SSM_N), h) for ch, h in zip(chains, hg)]
        h_in = [_dot_tn(load(ch, 1, SSM_N), load(ch, 0, SSM_GW) * pick(ch, COL_DTEAL)) for ch in chains]
        scores = [cb * decay_dt_of(ch) for ch, cb in zip(chains, cb4)]
        x_bd = [jnp.where(block_diag, jnp.concatenate([load(ch, 0, SSM_GW)] * SSM_HPG, axis=0), 0.0)
                for ch in chains]
        y_diag = [_dot(sc, xb) for sc, xb in zip(scores, x_bd)]
        for ch, yd, yo, h, hi in zip(chains, y_diag, y_off, hg, h_in):
            d, g, _, r0, _, n = ch
            dir_refs[d][5][n, pl.ds(r0, CHUNK), g * SSM_GW:(g + 1) * SSM_GW] = yd + yo * pick(ch, COL_EA)
            h_scr[n, d, g] = h * pick(ch, COL_EAT, slice(0, 1)) + hi
        return carry

    lax.fori_loop(0, nct, body, 0)

    @pl.when(j == pl.num_programs(1) - 1)
    def _():
        ht_ref[...] = h_scr[...]


def ssd_scan(feat, gcol, arow, h0, tt, nb):
    bsz, t, _ = feat.shape
    nt = t // tt
    nct = tt // CHUNK
    fwd = lambda lane_blk: (lambda b, j: (b, j, lane_blk))
    bwd = lambda lane_blk: (lambda b, j: (b, nt - 1 - j, lane_blk))

    def side(im):
        return [pl.BlockSpec((nb, tt, SSM_DI), im(OFF_X // SSM_DI)),
                pl.BlockSpec((nb, tt, SSM_BC), im(OFF_B // SSM_BC)),
                pl.BlockSpec((nb, tt, SSM_BC), im(OFF_C // SSM_BC)),
                pl.BlockSpec((nb, tt, LANES), im(0))]

    row_spec = lambda rev: pl.BlockSpec((nb, nct, SSD_ROWS, SSM_GW),
                                        (lambda b, j: (b, nt - 1 - j, 0, 0)) if rev else (lambda b, j: (b, j, 0, 0)))
    state_spec = pl.BlockSpec((nb, 2, SSM_G, SSM_N, SSM_GW), lambda b, j: (b, 0, 0, 0, 0))
    return pl.pallas_call(
        functools.partial(_ssd_kernel, nct=nct, nb=nb),
        grid=(bsz // nb, nt),
        in_specs=side(fwd) + [row_spec(False)] + side(bwd) + [row_spec(True)] + [state_spec],
        out_specs=[pl.BlockSpec((nb, tt, SSM_DI), fwd(0)), pl.BlockSpec((nb, tt, SSM_DI), bwd(0)), state_spec],
        out_shape=[
            jax.ShapeDtypeStruct((bsz, t, SSM_DI), F32),
            jax.ShapeDtypeStruct((bsz, t, SSM_DI), F32),
            jax.ShapeDtypeStruct((bsz, 2, SSM_G, SSM_N, SSM_GW), F32),
        ],
        scratch_shapes=[pltpu.VMEM((nb, 2, SSM_G, SSM_N, SSM_GW), F32)],
        compiler_params=_compiler_params(("parallel", "arbitrary")),
        name="ssd_scan",
    )(feat, feat, feat, gcol, arow, feat, feat, feat, gcol, arow, h0)


def _merge_out_kernel(of_ref, ob_ref, yf_ref, yb_ref, xs_ref, z_ref, dng_ref, dskip_ref, ssg_ref,
                      w_ref, g_ref, gate_ref, res_ref, g2_ref, shift2_ref, scale2_ref, wr_ref,
                      o_ref, hx_ref, aff_ref):
    o = of_ref[0] + ob_ref[0]
    z = z_ref[0]
    parts = []
    for h in range(DN_HEADS):
        lanes = slice(h * DN_DV, (h + 1) * DN_DV)
        parts.append(_rms(o[:, lanes]) * dng_ref[...] * _silu(z[:, lanes]))
    y = yf_ref[0] + yb_ref[0] + dskip_ref[...] * xs_ref[0]
    yz = y * _silu(z[:, DN_VW:])
    gw = SSM_DI // SSM_G
    for g in range(SSM_G):
        lanes = slice(g * gw, (g + 1) * gw)
        parts.append(_rms(yz[:, lanes]) * ssg_ref[:, lanes])
    mixed = jnp.concatenate(parts, axis=1).astype(BF16)
    m = jnp.dot(mixed, w_ref[...], preferred_element_type=F32)
    x_new = res_ref[0] + gate_ref[0] * (_rms(m) * g_ref[...])
    o_ref[0] = x_new
    for s in range(x_new.shape[0] // MOE_TILE):
        rows = slice(s * MOE_TILE, (s + 1) * MOE_TILE)
        hb, ext, aff = _route(x_new[rows], g2_ref[...], shift2_ref[0], scale2_ref[0], wr_ref[...])
        hx_ref[0, rows, :D_MODEL] = hb
        hx_ref[0, rows, D_MODEL:] = ext
        aff_ref[0, s] = aff


def merge_out(o_f, o_b, y_f, y_b, feat, z, dn_norm_g, dskip, ssm_norm_g, w_out, g, gate, res,
              g_moe, shift_moe, scale_moe, wr_t, tm):
    bsz, t, d = res.shape
    row = lambda b, i: (b, i, 0)
    const = lambda b, i: (0, 0)
    per_b = lambda b, i: (b, 0, 0)
    half = lambda: pl.BlockSpec((1, tm, DN_VW), row)
    tiles = tm // MOE_TILE
    return pl.pallas_call(
        _merge_out_kernel,
        grid=(bsz, t // tm),
        in_specs=[
            half(), half(), half(), half(),
            pl.BlockSpec((1, tm, SSM_DI), lambda b, i: (b, i, OFF_X // SSM_DI)),
            pl.BlockSpec((1, tm, D_Z), row),
            pl.BlockSpec((1, DN_DV), const),
            pl.BlockSpec((1, SSM_DI), const),
            pl.BlockSpec((1, SSM_DI), const),
            pl.BlockSpec((D_MIX, d), const),
            pl.BlockSpec((1, d), const),
            pl.BlockSpec((1, 1, d), per_b),
            pl.BlockSpec((1, tm, d), row),
            pl.BlockSpec((1, d), const),
            pl.BlockSpec((1, 1, d), per_b),
            pl.BlockSpec((1, 1, d), per_b),
            pl.BlockSpec((N_EXPERTS, d), const),
        ],
        out_specs=[
            pl.BlockSpec((1, tm, d), row),
            pl.BlockSpec((1, tm, D_EXT), row),
            pl.BlockSpec((1, tiles, N_EXPERTS, MOE_TILE), lambda b, i: (b, i, 0, 0)),
        ],
        out_shape=[
            jax.ShapeDtypeStruct((bsz, t, d), F32),
            jax.ShapeDtypeStruct((bsz, t, D_EXT), BF16),
            jax.ShapeDtypeStruct((bsz, t // MOE_TILE, N_EXPERTS, MOE_TILE), F32),
        ],
        compiler_params=_compiler_params(("parallel", "parallel")),
        name="merge_out",
    )(o_f, o_b, y_f, y_b, feat, z, dn_norm_g.reshape(1, DN_DV), dskip, ssm_norm_g.reshape(1, SSM_DI),
      w_out, g.reshape(1, d), gate.reshape(bsz, 1, d), res,
      g_moe.reshape(1, d), shift_moe.reshape(bsz, 1, d), scale_moe.reshape(bsz, 1, d), wr_t)


MOE_TILE = 256
SLOT_WINDOW = 64
SLOT_ALIGN = 16
CNT_LANES = 32
GATE_PIECES = 3
D_EXT = D_MODEL + LANES


def _route(x, g, shift, scale, wr_t):
    hb = ((_rms(x) * g) * (1.0 + scale) + shift).astype(BF16)
    logits = lax.dot_general(wr_t, hb, (((1,), (1,)), ((), ())), preferred_element_type=F32)
    ex = jnp.exp(logits - jnp.max(logits, axis=0, keepdims=True))
    aff = ex / jnp.sum(ex, axis=0, keepdims=True)
    pieces, rest = [], aff
    for _ in range(GATE_PIECES):
        p = rest.astype(BF16).astype(F32)
        pieces.append(p)
        rest = rest - p
    rows = jnp.concatenate(pieces + [jnp.zeros((LANES - GATE_PIECES * N_EXPERTS, MOE_TILE), F32)], axis=0)
    ext = jnp.concatenate([rows[:, m * LANES:(m + 1) * LANES].T for m in range(MOE_TILE // LANES)], axis=0)
    return hb, ext.astype(BF16), aff


def _lane_prefix(mask, upper):
    carry = jnp.zeros((mask.shape[0], 1), F32)
    blocks = []
    for m in range(mask.shape[1] // LANES):
        p = jnp.dot(mask[:, m * LANES:(m + 1) * LANES].astype(BF16), upper, preferred_element_type=F32) + carry
        carry = p[:, LANES - 1:LANES]
        blocks.append(p)
    return jnp.concatenate(blocks, axis=1)


def _select_kernel(aff_ref, pos_ref, posc_ref, cnt_ref, *, nt, cap):
    aff = jnp.concatenate([aff_ref[0, j] for j in range(nt)], axis=1)
    bits = pltpu.bitcast(aff, jnp.int32)

    def bisect(i, thr):
        cand = thr | lax.shift_left(jnp.int32(1), jnp.int32(30) - i)
        n_ge = jnp.sum(jnp.where(bits >= cand, 1.0, 0.0), axis=1, keepdims=True)
        return jnp.where(n_ge >= cap, cand, thr)

    thr = lax.fori_loop(0, 31, bisect, jnp.zeros((N_EXPERTS, 1), jnp.int32))
    above = bits > thr
    tied = bits == thr
    room = cap - jnp.sum(jnp.where(above, 1.0, 0.0), axis=1, keepdims=True)
    src = lax.broadcasted_iota(jnp.int32, (LANES, LANES), 0)
    dst = lax.broadcasted_iota(jnp.int32, (LANES, LANES), 1)
    upper = jnp.where(src <= dst, 1.0, 0.0).astype(BF16)
    tie_rank = _lane_prefix(jnp.where(tied, 1.0, 0.0), upper)
    sel = above | (tied & (tie_rank <= room))
    selc = jnp.where(sel, 1.0, 0.0)
    slot = jnp.where(sel, _lane_prefix(selc, upper) - 1.0, -1.0)
    lane = lax.broadcasted_iota(jnp.int32, (N_EXPERTS, LANES), 1)
    before = jnp.zeros((N_EXPERTS, 1), F32)
    counts = jnp.zeros((N_EXPERTS, LANES), F32)
    for j in range(nt):
        pos_ref[0, j] = slot[:, j * MOE_TILE:(j + 1) * MOE_TILE]
        before = before + jnp.sum(selc[:, j * MOE_TILE:(j + 1) * MOE_TILE], axis=1, keepdims=True)
        counts = jnp.where(lane == j + 1, before, counts)
    cnt_ref[0] = counts.astype(jnp.int32)
    pad = jnp.zeros((LANES - N_EXPERTS, LANES), F32)
    for m in range(slot.shape[1] // LANES):
        posc_ref[0, m * LANES:(m + 1) * LANES, :] = jnp.concatenate([slot[:, m * LANES:(m + 1) * LANES], pad], axis=0).T


def select_slots(aff, cap):
    bsz, nt, ne, tile = aff.shape
    blk = lambda b: (b, 0, 0, 0)
    return pl.pallas_call(
        functools.partial(_select_kernel, nt=nt, cap=cap),
        grid=(bsz,),
        in_specs=[pl.BlockSpec((1, nt, ne, tile), blk)],
        out_specs=[pl.BlockSpec((1, nt, ne, tile), blk),
                   pl.BlockSpec((1, nt * tile, LANES), lambda b: (b, 0, 0)),
                   pl.BlockSpec((1, ne, LANES), lambda b: (b, 0, 0))],
        out_shape=[jax.ShapeDtypeStruct((bsz, nt, ne, tile), F32),
                   jax.ShapeDtypeStruct((bsz, nt * tile, LANES), F32),
                   jax.ShapeDtypeStruct((bsz, ne, LANES), jnp.int32)],
        compiler_params=_compiler_params(("parallel",)),
        name="select_slots",
    )(aff)


def _window_start(cnt_ref, row, j):
    return pl.multiple_of((cnt_ref[row, j] // SLOT_ALIGN) * SLOT_ALIGN, SLOT_ALIGN)


def _extra_windows(cnt_ref, row, j, start):
    return (cnt_ref[row, j + 1] - start - 1) // SLOT_WINDOW


def _gather_kernel(cnt_ref, hx_ref, pos_ref, xs_ref):
    b = pl.program_id(0)
    j = pl.program_id(1)

    @pl.when(j == 0)
    def _():
        xs_ref[...] = jnp.zeros_like(xs_ref)

    hx = hx_ref[0]
    slot_iota = lax.broadcasted_iota(jnp.int32, (SLOT_WINDOW, MOE_TILE), 0).astype(F32)

    def one_hot(e, start):
        return jnp.where(pos_ref[0, 0, e:e + 1, :] - start.astype(F32) == slot_iota, 1.0, 0.0).astype(BF16)

    starts = [_window_start(cnt_ref, b * N_EXPERTS + e, j) for e in range(N_EXPERTS)]
    stacked = jnp.concatenate([one_hot(e, starts[e]) for e in range(N_EXPERTS)], axis=0)
    rows = jnp.dot(stacked, hx, preferred_element_type=F32)
    for e in range(N_EXPERTS):
        xs_ref[0, e, pl.ds(starts[e], SLOT_WINDOW), :] += rows[e * SLOT_WINDOW:(e + 1) * SLOT_WINDOW].astype(BF16)

    for e in range(N_EXPERTS):
        n_more = _extra_windows(cnt_ref, b * N_EXPERTS + e, j, starts[e])

        @pl.when(n_more > 0)
        def _(e=e, n_more=n_more):
            def more(i, carry):
                start = pl.multiple_of(starts[e] + i * SLOT_WINDOW, SLOT_ALIGN)
                r = jnp.dot(one_hot(e, start), hx, preferred_element_type=F32)
                xs_ref[0, e, pl.ds(start, SLOT_WINDOW), :] += r.astype(BF16)
                return carry
            lax.fori_loop(1, n_more + 1, more, 0)


def gather_tokens(cnt, hx, pos, cap):
    bsz, t, d_ext = hx.shape
    nt = t // MOE_TILE
    rows = cap + SLOT_WINDOW
    grid_spec = pltpu.PrefetchScalarGridSpec(
        num_scalar_prefetch=1,
        grid=(bsz, nt),
        in_specs=[
            pl.BlockSpec((1, MOE_TILE, d_ext), lambda b, j, c: (b, j, 0)),
            pl.BlockSpec((1, 1, N_EXPERTS, MOE_TILE), lambda b, j, c: (b, j, 0, 0)),
        ],
        out_specs=pl.BlockSpec((1, N_EXPERTS, rows, d_ext), lambda b, j, c: (b, 0, 0, 0),
                               pipeline_mode=pl.Buffered(1)),
    )
    return pl.pallas_call(
        _gather_kernel,
        grid_spec=grid_spec,
        out_shape=jax.ShapeDtypeStruct((bsz, N_EXPERTS, rows, d_ext), BF16),
        compiler_params=_compiler_params(("parallel", "arbitrary")),
        name="gather_tokens",
    )(cnt, hx, pos)


def _expert_ffn_kernel(xs_ref, wg_ref, wu_ref, wd_ref, o_ref):
    e = pl.program_id(0)
    d = wg_ref.shape[1]
    nbs = xs_ref.shape[0]
    cap = xs_ref.shape[2] - SLOT_WINDOW
    xs = jnp.concatenate([xs_ref[n, 0, :cap, :d] for n in range(nbs)], axis=0)
    ext = jnp.concatenate([xs_ref[n, 0, :cap, d:] for n in range(nbs)], axis=0).astype(F32)
    lane = lax.broadcasted_iota(jnp.int32, ext.shape, 1)
    mine = (lane % N_EXPERTS == e) & (lane < GATE_PIECES * N_EXPERTS)
    gate = jnp.sum(jnp.where(mine, ext, 0.0), axis=1, keepdims=True)
    a = jnp.dot(xs, wg_ref[0], preferred_element_type=F32)
    u = jnp.dot(xs, wu_ref[0], preferred_element_type=F32)
    hid = (_silu(a) * u).astype(BF16)
    out = (jnp.dot(hid, wd_ref[0], preferred_element_type=F32) * gate).astype(BF16)
    for n in range(nbs):
        o_ref[n, 0, :cap] = out[n * cap:(n + 1) * cap]
        o_ref[n, 0, cap:] = jnp.zeros((SLOT_WINDOW, d), BF16)


FFN_ROWS = 512


def expert_ffn(xs, wg, wu, wd):
    bsz, ne, rows, d_ext = xs.shape
    _, d, f = wg.shape
    nbs = min(bsz, max(1, FFN_ROWS // (rows - SLOT_WINDOW)))
    return pl.pallas_call(
        _expert_ffn_kernel,
        grid=(ne, bsz // nbs),
        in_specs=[
            pl.BlockSpec((nbs, 1, rows, d_ext), lambda e, b: (b, e, 0, 0)),
            pl.BlockSpec((1, d, f), lambda e, b: (e, 0, 0)),
            pl.BlockSpec((1, d, f), lambda e, b: (e, 0, 0)),
            pl.BlockSpec((1, f, d), lambda e, b: (e, 0, 0)),
        ],
        out_specs=pl.BlockSpec((nbs, 1, rows, d), lambda e, b: (b, e, 0, 0)),
        out_shape=jax.ShapeDtypeStruct((bsz, ne, rows, d), BF16),
        compiler_params=_compiler_params(("parallel", "parallel")),
        name="expert_ffn",
    )(xs, wg, wu, wd)


def _scatter_out_kernel(cnt_ref, og_ref, posc_ref, g_ref, gate_ref, res_ref, o_ref):
    b = pl.program_id(0)
    j = pl.program_id(1)
    posc = posc_ref[0]
    starts = [_window_start(cnt_ref, b * N_EXPERTS + e, j) for e in range(N_EXPERTS)]
    lane = lax.broadcasted_iota(jnp.int32, (MOE_TILE, LANES), 1)
    lane_f = lane.astype(F32)
    per_tile = LANES // SLOT_WINDOW
    tiles = []
    for m in range(N_EXPERTS // per_tile):
        rel = None
        for q in range(per_tile - 1, -1, -1):
            e = m * per_tile + q
            v = posc[:, e:e + 1] - starts[e].astype(F32) + float(q * SLOT_WINDOW)
            rel = v if rel is None else jnp.where(lane < (q + 1) * SLOT_WINDOW, v, rel)
        tiles.append(jnp.where(rel == lane_f, 1.0, 0.0).astype(BF16))
    sel_t = jnp.concatenate(tiles, axis=1)
    og = jnp.concatenate([og_ref[0, e, pl.ds(starts[e], SLOT_WINDOW), :] for e in range(N_EXPERTS)], axis=0)
    o_ref[0] = jnp.dot(sel_t, og, preferred_element_type=F32)

    win_lane = lax.broadcasted_iota(jnp.int32, (MOE_TILE, SLOT_WINDOW), 1).astype(F32)
    for e in range(N_EXPERTS):
        n_more = _extra_windows(cnt_ref, b * N_EXPERTS + e, j, starts[e])

        @pl.when(n_more > 0)
        def _(e=e, n_more=n_more):
            def more(i, carry):
                start = pl.multiple_of(starts[e] + i * SLOT_WINDOW, SLOT_ALIGN)
                p = jnp.where(posc[:, e:e + 1] - start.astype(F32) == win_lane, 1.0, 0.0).astype(BF16)
                o_ref[0] += jnp.dot(p, og_ref[0, e, pl.ds(start, SLOT_WINDOW), :], preferred_element_type=F32)
                return carry
            lax.fori_loop(1, n_more + 1, more, 0)

    o_ref[0] = res_ref[0] + gate_ref[0] * (_rms(o_ref[0]) * g_ref[...])


def scatter_out(cnt, og, posc, g, gate, res):
    bsz, t, d = res.shape
    nt = t // MOE_TILE
    _, ne, rows, _ = og.shape
    grid_spec = pltpu.PrefetchScalarGridSpec(
        num_scalar_prefetch=1,
        grid=(bsz, nt),
        in_specs=[
            pl.BlockSpec((1, ne, rows, d), lambda b, j, c: (b, 0, 0, 0), pipeline_mode=pl.Buffered(1)),
            pl.BlockSpec((1, MOE_TILE, LANES), lambda b, j, c: (b, j, 0)),
            pl.BlockSpec((1, d), lambda b, j, c: (0, 0)),
            pl.BlockSpec((1, 1, d), lambda b, j, c: (b, 0, 0)),
            pl.BlockSpec((1, MOE_TILE, d), lambda b, j, c: (b, j, 0)),
        ],
        out_specs=pl.BlockSpec((1, MOE_TILE, d), lambda b, j, c: (b, j, 0)),
    )
    return pl.pallas_call(
        _scatter_out_kernel,
        grid_spec=grid_spec,
        out_shape=jax.ShapeDtypeStruct((bsz, t, d), F32),
        compiler_params=_compiler_params(("parallel", "parallel")),
        name="scatter_out",
    )(cnt, og, posc, g.reshape(1, d), gate.reshape(bsz, 1, d), res)


def ec_moe_residual(x, hx, aff, wg, wu, wd, g_out, gate):
    bsz, t, _ = x.shape
    cap = EC_CAPACITY_FACTOR * t // N_EXPERTS
    pos, posc, counts = select_slots(aff, cap)
    cnt = counts[:, :, :CNT_LANES].reshape(bsz * N_EXPERTS, CNT_LANES)
    xs = gather_tokens(cnt, hx, pos, cap)
    og = expert_ffn(xs, wg, wu, wd)
    return scatter_out(cnt, og, posc, g_out, gate, x)


def _token_tile(t):
    return min(512, t)


def mixer_stream(x, g0, shift, scale, wc, wz, ws, w9, conv_b, bias_rows, nega_rows, dn_state, ssm_state, on_grid):
    t = x.shape[1]
    tm = _token_tile(t)
    feat, z, gcol, grow, arow = in_proj_conv(x, g0, shift, scale, wc, wz, ws, w9, conv_b, bias_rows, nega_rows,
                                             tm, on_grid)
    ts = min(SCAN_TILE, t)
    o_f, o_b, dn_state = delta_scan(feat, gcol, grow, dn_state, ts, SCAN_SAMPLES_PER_STEP)
    y_f, y_b, ssm_state = ssd_scan(feat, gcol, arow, ssm_state, ts, SCAN_SAMPLES_PER_STEP)
    return (o_f, o_b, y_f, y_b, feat, z), dn_state, ssm_state


def _gate_param_rows(dn_bias, dn_a_log, ssm_bias, ssm_a_log):
    zeros = jnp.zeros((NDH,), F32)
    bias = jnp.concatenate([zeros, dn_bias.reshape(-1), ssm_bias.reshape(-1)])
    nega = jnp.concatenate([zeros, -jnp.exp(dn_a_log.reshape(-1)), -jnp.exp(ssm_a_log.reshape(-1))])
    pad = LANES - bias.shape[0]
    expand = lambda v: jnp.broadcast_to(jnp.pad(v, (0, pad))[:, None], (LANES, LANES))
    return expand(bias), expand(nega)


def kernel(x, c, ctx, c_ctx, ada_w, ada_b, norm_g, w_in, conv_w, conv_b, dn_A_log, dn_dt_bias,
           dn_norm_g, ssm_A_log, ssm_dt_bias, ssm_D, ssm_norm_g, w_out, router_w,
           exp_w_gate, exp_w_up, exp_w_down):
    bsz = x.shape[0]
    s_lat = jax.nn.silu(c)
    s_ctx = jax.nn.silu(c_ctx)
    for l in range(DEPTH):
        last = l == DEPTH - 1
        mod_lat = jnp.split(s_lat @ ada_w[l] + ada_b[l], 6, axis=-1)
        mod_ctx_row = s_ctx @ ada_w[l] + ada_b[l]
        mod_ctx = [jnp.broadcast_to(m[None, :], (bsz, D_MODEL)) for m in jnp.split(mod_ctx_row, 6)]

        w_in_l = w_in[l]
        wc = w_in_l[:, :CONV_CH].astype(BF16)
        wz = w_in_l[:, CONV_CH:CONV_CH + D_Z].astype(BF16)
        ws = jnp.pad(w_in_l[:, CONV_CH + D_Z:], ((0, 0), (0, LANES - N_GATE_COLS))).astype(BF16)
        w9 = conv_w[l].reshape(CONV_CH, CONV_K * CONV_K).T
        cb = conv_b[l].reshape(1, CONV_CH)
        bias_rows, nega_rows = _gate_param_rows(dn_dt_bias[l], dn_A_log[l], ssm_dt_bias[l], ssm_A_log[l])
        dskip = jnp.repeat(ssm_D[l], SSM_P).reshape(1, SSM_DI)
        w_out_l = w_out[l].astype(BF16)
        wr_t = router_w[l].T.astype(BF16)
        moe_w = (exp_w_gate[l].astype(BF16), exp_w_up[l].astype(BF16), exp_w_down[l].astype(BF16), norm_g[l, 3])

        dn0 = jnp.zeros((bsz, 2, DN_HEADS, DN_DK, DN_DV), F32)
        ssm0 = jnp.zeros((bsz, 2, SSM_G, SSM_N, SSM_GW), F32)
        shared = (wc, wz, ws, w9, cb, bias_rows, nega_rows)
        mix_ctx, dn_c, ssm_c = mixer_stream(ctx, norm_g[l, 0], mod_ctx[0], mod_ctx[1], *shared, dn0, ssm0, False)
        mix_lat, _, _ = mixer_stream(x, norm_g[l, 0], mod_lat[0], mod_lat[1], *shared, dn_c, ssm_c, True)
        merge_w = (dn_norm_g[l], dskip, ssm_norm_g[l], w_out_l, norm_g[l, 1])
        x, hx, aff = merge_out(*mix_lat, *merge_w, mod_lat[2], x, norm_g[l, 2], mod_lat[3], mod_lat[4], wr_t,
                               _token_tile(x.shape[1]))
        x = ec_moe_residual(x, hx, aff, *moe_w, mod_lat[5])

        if not last:
            ctx, hx, aff = merge_out(*mix_ctx, *merge_w, mod_ctx[2], ctx, norm_g[l, 2], mod_ctx[3], mod_ctx[4], wr_t,
                                     _token_tile(ctx.shape[1]))
            ctx = ec_moe_residual(ctx, hx, aff, *moe_w, mod_ctx[5])
    return x
```

```python
import functools

import jax
import jax.numpy as jnp
import numpy as np
from jax import lax
from jax.experimental import pallas as pl
from jax.experimental.pallas import tpu as pltpu

D_MODEL = 1024
DEPTH = 4
GRID_W = 64
DN_HEADS = 4
DN_DK = 128
DN_DV = 128
SSM_HEADS = 8
SSM_P = 64
SSM_N = 128
SSM_G = 2
CHUNK = 64
CONV_K = 3
N_EXPERTS = 16
EC_CAPACITY_FACTOR = 2
D_EXPERT = 512
EPS = 1e-6

DN_QK = DN_HEADS * DN_DK
DN_VW = DN_HEADS * DN_DV
SSM_DI = SSM_HEADS * SSM_P
SSM_BC = SSM_G * SSM_N
SSM_HPG = SSM_HEADS // SSM_G
SSM_GW = SSM_HPG * SSM_P
D_MIX = DN_VW + SSM_DI
CONV_SPLITS = (DN_QK, DN_QK, DN_VW, SSM_DI, SSM_BC, SSM_BC)
CONV_CH = sum(CONV_SPLITS)
D_Z = DN_VW + SSM_DI
N_GATE_COLS = 2 * DN_HEADS + 2 * DN_HEADS + 2 * SSM_HEADS
D_IN_PROJ = CONV_CH + D_Z + N_GATE_COLS

LANES = 128
SUBLANES = 8
VMEM_LIMIT_BYTES = 56 * 1024 * 1024

OFF_Q, OFF_K, OFF_V = 0, DN_QK, 2 * DN_QK
OFF_X = 2 * DN_QK + DN_VW
OFF_B = OFF_X + SSM_DI
OFF_C = OFF_B + SSM_BC

NDH = 2 * DN_HEADS
NSH = 2 * SSM_HEADS
COL_BETA, COL_G, COL_EG, COL_EGL, COL_EGT = (i * NDH for i in range(5))
COL_DT, COL_A, COL_EA, COL_DTEAL, COL_EAT = (5 * NDH + i * NSH for i in range(5))

F32 = jnp.float32
BF16 = jnp.bfloat16


def _compiler_params(semantics):
    return pltpu.CompilerParams(dimension_semantics=semantics, vmem_limit_bytes=VMEM_LIMIT_BYTES)


def _rms(x):
    return x * lax.rsqrt(jnp.mean(x * x, axis=-1, keepdims=True) + EPS)


def _silu(x):
    return x * jax.nn.sigmoid(x)


def _softplus(x):
    return jnp.maximum(x, 0.0) + jnp.log(1.0 + jnp.exp(-jnp.abs(x)))


def _dot(a, b):
    return jnp.dot(a.astype(BF16), b.astype(BF16), preferred_element_type=F32)


def _dot_nt(a, b):
    return lax.dot_general(a.astype(BF16), b.astype(BF16), (((1,), (1,)), ((), ())),
                           preferred_element_type=F32)


def _dot_tn(a, b):
    return lax.dot_general(a.astype(BF16), b.astype(BF16), (((0,), (0,)), ((), ())),
                           preferred_element_type=F32)


def _dot_exact01(a, m01):
    a1 = a.astype(BF16)
    r1 = a - a1.astype(F32)
    a2 = r1.astype(BF16)
    a3 = (r1 - a2.astype(F32)).astype(BF16)
    m = m01.astype(BF16)
    out = jnp.dot(a3, m, preferred_element_type=F32)
    out = out + jnp.dot(a2, m, preferred_element_type=F32)
    return out + jnp.dot(a1, m, preferred_element_type=F32)


SSD_ROWS = 2 * 2 * SSM_G
HALO_BLOCK = 128
HALO = GRID_W + 16
CONV_LANES = 256


def _in_proj_conv_kernel(xp_ref, x_ref, xn_ref, g_ref, shift_ref, scale_ref, wc_ref, wz_ref, ws_ref, w9_ref, cb_ref,
                         gbias_ref, gnega_ref, of_ref, oz_ref, col_ref, grow_ref, arow_ref, *, on_grid):
    i = pl.program_id(1)
    n_i = pl.num_programs(1)
    tm = x_ref.shape[1]

    def modulated(x):
        h = _rms(x) * g_ref[...]
        return h * (1.0 + scale_ref[0]) + shift_ref[0]

    has_prev = jnp.where(i > 0, 1.0, 0.0)
    has_next = jnp.where(i < n_i - 1, 1.0, 0.0)
    h_main = modulated(x_ref[0]).astype(BF16)
    h_prev = (modulated(xp_ref[0, HALO_BLOCK - HALO:, :]) * has_prev).astype(BF16)
    h_next = (modulated(xn_ref[0, :HALO, :]) * has_next).astype(BF16)
    h_all = jnp.concatenate([h_prev, h_main, h_next], axis=0)

    oz_ref[0] = jnp.dot(h_main, wz_ref[...], preferred_element_type=F32)
    _write_gates(jnp.dot(h_main, ws_ref[...], preferred_element_type=F32), gbias_ref[...], gnega_ref[...],
                 col_ref.at[0], grow_ref.at[0], arow_ref.at[0])

    dys = (-1, 0, 1) if on_grid else (0,)
    col = lax.broadcasted_iota(jnp.int32, (tm, CONV_LANES), 0) % GRID_W
    n_blocks = CONV_CH // CONV_LANES

    def project(c):
        return jnp.dot(h_all, wc_ref[:, c * CONV_LANES:(c + 1) * CONV_LANES], preferred_element_type=F32)

    def conv(c, p):
        lanes = slice(c * CONV_LANES, (c + 1) * CONV_LANES)
        acc_c = acc_m = acc_p = None
        for dy in dys:
            lo = HALO + dy * GRID_W
            win = p[lo - SUBLANES:lo + tm + SUBLANES]
            wr = 3 * (dy + 1)
            tc = win[SUBLANES:SUBLANES + tm] * w9_ref[wr + 1:wr + 2, lanes]
            tl, tr = win * w9_ref[wr:wr + 1, lanes], win * w9_ref[wr + 2:wr + 3, lanes]
            acc_c = tc if acc_c is None else acc_c + tc
            acc_m = tl if acc_m is None else acc_m + tl
            acc_p = tr if acc_p is None else acc_p + tr
        acc_m = pltpu.roll(acc_m, 1, 0)[SUBLANES:SUBLANES + tm]
        acc_p = pltpu.roll(acc_p, tm + 2 * SUBLANES - 1, 0)[SUBLANES:SUBLANES + tm]
        if on_grid:
            acc_m = jnp.where(col != 0, acc_m, 0.0)
            acc_p = jnp.where(col != GRID_W - 1, acc_p, 0.0)
        u = _silu(acc_c + acc_m + acc_p + cb_ref[:, lanes])
        if c * CONV_LANES < 2 * DN_QK:
            scale = DN_DK ** -0.5 if c * CONV_LANES < DN_QK else 1.0
            heads = []
            for hh in range(CONV_LANES // DN_DK):
                uh = u[:, hh * DN_DK:(hh + 1) * DN_DK]
                heads.append(uh * (lax.rsqrt(jnp.sum(uh * uh, axis=-1, keepdims=True) + EPS) * scale))
            u = jnp.concatenate(heads, axis=1)
        of_ref[0, :, lanes] = u

    p = project(0)
    for c in range(n_blocks):
        p_next = project(c + 1) if c + 1 < n_blocks else None
        conv(c, p)
        p = p_next


def in_proj_conv(x, g, shift, scale, wc, wz, ws, w9, conv_b, bias_rows, nega_rows, tm, on_grid):
    bsz, t, d = x.shape
    nc = t // CHUNK
    ncg = tm // CHUNK
    per_tile = tm // HALO_BLOCK
    last_halo = t // HALO_BLOCK - 1
    row = lambda b, i: (b, i, 0)
    const = lambda b, i: (0, 0)
    per_b = lambda b, i: (b, 0, 0)
    return pl.pallas_call(
        functools.partial(_in_proj_conv_kernel, on_grid=on_grid),
        grid=(bsz, t // tm),
        in_specs=[
            pl.BlockSpec((1, HALO_BLOCK, d), lambda b, i: (b, jnp.maximum(i * per_tile - 1, 0), 0)),
            pl.BlockSpec((1, tm, d), row),
            pl.BlockSpec((1, HALO_BLOCK, d), lambda b, i: (b, jnp.minimum((i + 1) * per_tile, last_halo), 0)),
            pl.BlockSpec((1, d), const),
            pl.BlockSpec((1, 1, d), per_b),
            pl.BlockSpec((1, 1, d), per_b),
            pl.BlockSpec((d, CONV_CH), const),
            pl.BlockSpec((d, D_Z), const),
            pl.BlockSpec((d, LANES), const),
            pl.BlockSpec((9, CONV_CH), const),
            pl.BlockSpec((1, CONV_CH), const),
            pl.BlockSpec((LANES, LANES), const),
            pl.BlockSpec((LANES, LANES), const),
        ],
        out_specs=[
            pl.BlockSpec((1, tm, CONV_CH), row),
            pl.BlockSpec((1, tm, D_Z), row),
            pl.BlockSpec((1, tm, LANES), row),
            pl.BlockSpec((1, ncg, NDH, CHUNK), lambda b, i: (b, i, 0, 0)),
            pl.BlockSpec((1, ncg, SSD_ROWS, SSM_GW), lambda b, i: (b, i, 0, 0)),
        ],
        out_shape=[
            jax.ShapeDtypeStruct((bsz, t, CONV_CH), F32),
            jax.ShapeDtypeStruct((bsz, t, D_Z), F32),
            jax.ShapeDtypeStruct((bsz, t, LANES), F32),
            jax.ShapeDtypeStruct((bsz, nc, NDH, CHUNK), F32),
            jax.ShapeDtypeStruct((bsz, nc, SSD_ROWS, SSM_GW), F32),
        ],
        compiler_params=_compiler_params(("parallel", "parallel")),
        name="in_proj_conv",
    )(x, x, x, g.reshape(1, d), shift.reshape(bsz, 1, d), scale.reshape(bsz, 1, d), wc, wz, ws, w9, conv_b,
      bias_rows, nega_rows)


def _write_gates(small, bias, nega, col_ref, grow_ref, arow_ref):
    tok = lax.broadcasted_iota(jnp.int32, (LANES, LANES), 0)
    out = lax.broadcasted_iota(jnp.int32, (LANES, LANES), 1)
    same = (tok // CHUNK) == (out // CHUNK)
    m_fwd = jnp.where(same & (tok <= out), 1.0, 0.0)
    m_bwd = jnp.where(same & (tok >= out), 1.0, 0.0)
    m_all = jnp.where(same, 1.0, 0.0)

    def dir_cumsum(v, heads):
        f = _dot_exact01(v, m_fwd)
        b = _dot_exact01(v, m_bwd)
        is_fwd = lax.broadcasted_iota(jnp.int32, v.shape, 0) < heads
        return jnp.where(is_fwd, f, b), _dot_exact01(v, m_all)

    for s in range(small.shape[0] // LANES):
        st = small[s * LANES:(s + 1) * LANES, :].T
        beta = jax.nn.sigmoid(st[0:NDH])
        sp = _softplus(st[NDH:2 * NDH + NSH] + bias[NDH:2 * NDH + NSH])
        logg = sp[0:NDH] * nega[NDH:2 * NDH]
        dt = sp[NDH:]
        a = dt * nega[2 * NDH:2 * NDH + NSH]
        g_cs, g_tot = dir_cumsum(logg, DN_HEADS)
        a_cs, a_tot = dir_cumsum(a, SSM_HEADS)
        rows = jnp.concatenate([
            beta, g_cs, jnp.exp(g_cs), jnp.exp(g_tot - g_cs), jnp.exp(g_tot),
            dt, a_cs, jnp.exp(a_cs), dt * jnp.exp(a_tot - a_cs), jnp.exp(a_tot),
            jnp.zeros((LANES - 5 * NDH - 5 * NSH, LANES), F32)], axis=0)
        col_ref[s * LANES:(s + 1) * LANES, :] = rows.T
        for half in range(LANES // CHUNK):
            c = s * (LANES // CHUNK) + half
            lo = half * CHUNK
            grow_ref[c] = g_cs[:, lo:lo + CHUNK]
            for q, v in enumerate((a_cs, dt)):
                for dg in range(2 * SSM_G):
                    r0 = dg * SSM_HPG
                    r = q * 2 * SSM_G + dg
                    arow_ref[c, r:r + 1, :] = jnp.concatenate(
                        [v[r0 + hh:r0 + hh + 1, lo:lo + CHUNK] for hh in range(SSM_HPG)], axis=1)


SCAN_SAMPLES_PER_STEP = 4
SCAN_TILE = 256


def _chunk_masks(rev):
    r = lax.broadcasted_iota(jnp.int32, (CHUNK, CHUNK), 0)
    c = lax.broadcasted_iota(jnp.int32, (CHUNK, CHUNK), 1)
    return ((r <= c), (r < c)) if rev else ((r >= c), (r > c))


def _unit_tri_inverse(ms):
    eye = (lax.broadcasted_iota(jnp.int32, (CHUNK, CHUNK), 0)
           == lax.broadcasted_iota(jnp.int32, (CHUNK, CHUNK), 1)).astype(F32)
    ps = [_dot(m, m) for m in ms]
    ts = [eye + m for m in ms]
    n_sq = int(np.log2(CHUNK)) - 1
    for _ in range(n_sq - 1):
        rs = [_dot(jnp.concatenate([t, p], axis=0), p) for t, p in zip(ts, ps)]
        ts = [t + r[:CHUNK] for t, r in zip(ts, rs)]
        ps = [r[CHUNK:] for r in rs]
    return [t + _dot(t, p) for t, p in zip(ts, ps)]


def _dn_kernel(qf_ref, kf_ref, vf_ref, cf_ref, rf_ref, qb_ref, kb_ref, vb_ref, cb_ref, rb_ref, s0_ref,
               of_ref, ob_ref, st_ref, s_scr, *, nct, nb):
    j = pl.program_id(1)

    @pl.when(j == 0)
    def _():
        s_scr[...] = s0_ref[...]

    masks = (_chunk_masks(False), _chunk_masks(True))
    dir_refs = ((qf_ref, kf_ref, vf_ref, cf_ref, rf_ref, of_ref),
                (qb_ref, kb_ref, vb_ref, cb_ref, rb_ref, ob_ref))

    def body(c, carry):
        chains = []
        for n in range(nb):
            for d in range(2):
                cc = c if d == 0 else nct - 1 - c
                r0 = pl.multiple_of(cc * CHUNK, CHUNK)
                ct = dir_refs[d][3][n, pl.ds(r0, CHUNK), :]
                for h in range(DN_HEADS):
                    chains.append((d, h, cc, r0, ct, n))

        def load(chain, which):
            d, h, _, r0, _, n = chain
            return dir_refs[d][which][n, pl.ds(r0, CHUNK), h * DN_DK:(h + 1) * DN_DK]

        def col(chain, off, rows=slice(None)):
            d, h, _, _, ct, _ = chain
            lane = off + d * DN_HEADS + h
            return ct[rows, lane:lane + 1]

        def decay_of(chain):
            d, h, cc, _, _, n = chain
            incl = masks[d][0]
            g_row = dir_refs[d][4][n, cc, d * DN_HEADS + h:d * DN_HEADS + h + 1, :]
            return jnp.where(incl, jnp.exp(jnp.where(incl, col(chain, COL_G) - g_row, 0.0)), 0.0)

        kbeta = [load(ch, 1) * col(ch, COL_BETA) for ch in chains]
        a = [_dot_nt(jnp.concatenate([kb, load(ch, 0)], axis=0), load(ch, 1)) for ch, kb in zip(chains, kbeta)]
        decay = [decay_of(ch) for ch in chains]
        neg_l = [jnp.where(masks[ch[0]][1], -(ai[:CHUNK] * dc), 0.0) for ch, ai, dc in zip(chains, a, decay)]
        qk = [ai[CHUNK:] * dc for ai, dc in zip(a, decay)]
        t_inv = _unit_tri_inverse(neg_l)
        uw = [_dot(ti, jnp.concatenate([load(ch, 2) * col(ch, COL_BETA), kb * col(ch, COL_EG)], axis=1))
              for ch, ti, kb in zip(chains, t_inv, kbeta)]
        s = [s_scr[ch[5], ch[0], ch[1]] for ch in chains]
        ws_qs = [_dot(jnp.concatenate([uwi[:, DN_DV:], load(ch, 0) * col(ch, COL_EG)], axis=0), si)
                 for ch, uwi, si in zip(chains, uw, s)]
        v_new = [uwi[:, :DN_DV] - wq[:CHUNK] for uwi, wq in zip(uw, ws_qs)]
        o = [wq[CHUNK:] + _dot(qki, vn) for wq, qki, vn in zip(ws_qs, qk, v_new)]
        s_new = [si * col(ch, COL_EGT, slice(0, 1)) + _dot_tn(load(ch, 1) * col(ch, COL_EGL), vn)
                 for ch, si, vn in zip(chains, s, v_new)]
        for ch, oi, sn in zip(chains, o, s_new):
            d, h, _, r0, _, n = ch
            dir_refs[d][5][n, pl.ds(r0, CHUNK), h * DN_DV:(h + 1) * DN_DV] = oi
            s_scr[n, d, h] = sn
        return carry

    lax.fori_loop(0, nct, body, 0)

    @pl.when(j == pl.num_programs(1) - 1)
    def _():
        st_ref[...] = s_scr[...]


def delta_scan(feat, gcol, grow, s0, tt, nb):
    bsz, t, _ = feat.shape
    nt = t // tt
    nct = tt // CHUNK
    fwd = lambda lane_blk: (lambda b, j: (b, j, lane_blk))
    bwd = lambda lane_blk: (lambda b, j: (b, nt - 1 - j, lane_blk))
    qkv_blk = (nb, tt, DN_QK)

    def side(im):
        return [pl.BlockSpec(qkv_blk, im(OFF_Q // DN_QK)), pl.BlockSpec(qkv_blk, im(OFF_K // DN_QK)),
                pl.BlockSpec(qkv_blk, im(OFF_V // DN_QK)), pl.BlockSpec((nb, tt, LANES), im(0))]

    row_spec = lambda rev: pl.BlockSpec((nb, nct, NDH, CHUNK),
                                        (lambda b, j: (b, nt - 1 - j, 0, 0)) if rev else (lambda b, j: (b, j, 0, 0)))
    state_spec = pl.BlockSpec((nb, 2, DN_HEADS, DN_DK, DN_DV), lambda b, j: (b, 0, 0, 0, 0))
    return pl.pallas_call(
        functools.partial(_dn_kernel, nct=nct, nb=nb),
        grid=(bsz // nb, nt),
        in_specs=side(fwd) + [row_spec(False)] + side(bwd) + [row_spec(True)] + [state_spec],
        out_specs=[pl.BlockSpec((nb, tt, DN_VW), fwd(0)), pl.BlockSpec((nb, tt, DN_VW), bwd(0)), state_spec],
        out_shape=[
            jax.ShapeDtypeStruct((bsz, t, DN_VW), F32),
            jax.ShapeDtypeStruct((bsz, t, DN_VW), F32),
            jax.ShapeDtypeStruct((bsz, 2, DN_HEADS, DN_DK, DN_DV), F32),
        ],
        scratch_shapes=[pltpu.VMEM((nb, 2, DN_HEADS, DN_DK, DN_DV), F32)],
        compiler_params=_compiler_params(("parallel", "arbitrary")),
        name="delta_scan",
    )(feat, feat, feat, gcol, grow, feat, feat, feat, gcol, grow, s0)


def _group_lane_select(cols):
    shape = (cols[0].shape[0], SSM_GW)
    head = lax.broadcasted_iota(jnp.int32, shape, 1) // SSM_P
    out = jnp.broadcast_to(cols[SSM_HPG - 1], shape)
    for hh in range(SSM_HPG - 2, -1, -1):
        out = jnp.where(head == hh, jnp.broadcast_to(cols[hh], shape), out)
    return out


def _ssd_kernel(xf_ref, bf_ref, cf_ref, gf_ref, rf_ref, xb_ref, bb_ref, cb_ref, gb_ref, rb_ref, h0_ref,
                yf_ref, yb_ref, ht_ref, h_scr, *, nct, nb):
    j = pl.program_id(1)

    @pl.when(j == 0)
    def _():
        h_scr[...] = h0_ref[...]

    row_i = lax.broadcasted_iota(jnp.int32, (CHUNK, SSM_GW), 0)
    col_j = lax.broadcasted_iota(jnp.int32, (CHUNK, SSM_GW), 1) % SSM_P
    incl_dir = (row_i >= col_j, row_i <= col_j)
    bd_rows = lax.broadcasted_iota(jnp.int32, (SSM_HPG * CHUNK, SSM_GW), 0) // CHUNK
    bd_cols = lax.broadcasted_iota(jnp.int32, (SSM_HPG * CHUNK, SSM_GW), 1) // SSM_P
    block_diag = bd_rows == bd_cols
    dir_refs = ((xf_ref, bf_ref, cf_ref, gf_ref, rf_ref, yf_ref),
                (xb_ref, bb_ref, cb_ref, gb_ref, rb_ref, yb_ref))

    def body(c, carry):
        chains = []
        for n in range(nb):
            for d in range(2):
                cc = c if d == 0 else nct - 1 - c
                r0 = pl.multiple_of(cc * CHUNK, CHUNK)
                ct = dir_refs[d][3][n, pl.ds(r0, CHUNK), :]
                for g in range(SSM_G):
                    chains.append((d, g, cc, r0, ct, n))

        def load(chain, which, width):
            d, g, _, r0, _, n = chain
            return dir_refs[d][which][n, pl.ds(r0, CHUNK), g * width:(g + 1) * width]

        def pick(chain, off, rows=slice(None)):
            d, g, _, _, ct, _ = chain
            h0 = off + d * SSM_HEADS + g * SSM_HPG
            return _group_lane_select([ct[rows, h0 + hh:h0 + hh + 1] for hh in range(SSM_HPG)])

        def decay_dt_of(chain):
            d, g, cc, _, _, n = chain
            incl = incl_dir[d]
            dg = d * SSM_G + g
            a_row = dir_refs[d][4][n, cc, dg:dg + 1, :]
            dt_row = dir_refs[d][4][n, cc, 2 * SSM_G + dg:2 * SSM_G + dg + 1, :]
            return jnp.where(incl, jnp.exp(jnp.where(incl, pick(chain, COL_A) - a_row, 0.0)) * dt_row, 0.0)

        cb4 = [_dot_nt(load(ch, 2, SSM_N), jnp.concatenate([load(ch, 1, SSM_N)] * SSM_HPG, axis=0)) for ch in chains]
        hg = [h_scr[ch[5], ch[0], ch[1]] for ch in chains]
        y_off = [_dot(load(ch, 2, SSM_N), h) for ch, h in zip(chains, hg)]
        h_in = [_dot_tn(load(ch, 1, SSM_N), load(ch, 0, SSM_GW) * pick(ch, COL_DTEAL)) for ch in chains]
        scores = [cb * decay_dt_of(ch) for ch, cb in zip(chains, cb4)]
        x_bd = [jnp.where(block_diag, jnp.concatenate([load(ch, 0, SSM_GW)] * SSM_HPG, axis=0), 0.0)
                for ch in chains]
        y_diag = [_dot(sc, xb) for sc, xb in zip(scores, x_bd)]
        for ch, yd, yo, h, hi in zip(chains, y_diag, y_off, hg, h_in):
            d, g, _, r0, _, n = ch
            dir_refs[d][5][n, pl.ds(r0, CHUNK), g * SSM_GW:(g + 1) * SSM_GW] = yd + yo * pick(ch, COL_EA)
            h_scr[n, d, g] = h * pick(ch, COL_EAT, slice(0, 1)) + hi
        return carry

    lax.fori_loop(0, nct, body, 0)

    @pl.when(j == pl.num_programs(1) - 1)
    def _():
        ht_ref[...] = h_scr[...]


def ssd_scan(feat, gcol, arow, h0, tt, nb):
    bsz, t, _ = feat.shape
    nt = t // tt
    nct = tt // CHUNK
    fwd = lambda lane_blk: (lambda b, j: (b, j, lane_blk))
    bwd = lambda lane_blk: (lambda b, j: (b, nt - 1 - j, lane_blk))

    def side(im):
        return [pl.BlockSpec((nb, tt, SSM_DI), im(OFF_X // SSM_DI)),
                pl.BlockSpec((nb, tt, SSM_BC), im(OFF_B // SSM_BC)),
                pl.BlockSpec((nb, tt, SSM_BC), im(OFF_C // SSM_BC)),
                pl.BlockSpec((nb, tt, LANES), im(0))]

    row_spec = lambda rev: pl.BlockSpec((nb, nct, SSD_ROWS, SSM_GW),
                                        (lambda b, j: (b, nt - 1 - j, 0, 0)) if rev else (lambda b, j: (b, j, 0, 0)))
    state_spec = pl.BlockSpec((nb, 2, SSM_G, SSM_N, SSM_GW), lambda b, j: (b, 0, 0, 0, 0))
    return pl.pallas_call(
        functools.partial(_ssd_kernel, nct=nct, nb=nb),
        grid=(bsz // nb, nt),
        in_specs=side(fwd) + [row_spec(False)] + side(bwd) + [row_spec(True)] + [state_spec],
        out_specs=[pl.BlockSpec((nb, tt, SSM_DI), fwd(0)), pl.BlockSpec((nb, tt, SSM_DI), bwd(0)), state_spec],
        out_shape=[
            jax.ShapeDtypeStruct((bsz, t, SSM_DI), F32),
            jax.ShapeDtypeStruct((bsz, t, SSM_DI), F32),
            jax.ShapeDtypeStruct((bsz, 2, SSM_G, SSM_N, SSM_GW), F32),
        ],
        scratch_shapes=[pltpu.VMEM((nb, 2, SSM_G, SSM_N, SSM_GW), F32)],
        compiler_params=_compiler_params(("parallel", "arbitrary")),
        name="ssd_scan",
    )(feat, feat, feat, gcol, arow, feat, feat, feat, gcol, arow, h0)


def _merge_out_kernel(of_ref, ob_ref, yf_ref, yb_ref, xs_ref, z_ref, dng_ref, dskip_ref, ssg_ref,
                      w_ref, g_ref, gate_ref, res_ref, g2_ref, shift2_ref, scale2_ref, wr_ref,
                      o_ref, hx_ref, aff_ref):
    o = of_ref[0] + ob_ref[0]
    z = z_ref[0]
    parts = []
    for h in range(DN_HEADS):
        lanes = slice(h * DN_DV, (h + 1) * DN_DV)
        parts.append(_rms(o[:, lanes]) * dng_ref[...] * _silu(z[:, lanes]))
    y = yf_ref[0] + yb_ref[0] + dskip_ref[...] * xs_ref[0]
    yz = y * _silu(z[:, DN_VW:])
    gw = SSM_DI // SSM_G
    for g in range(SSM_G):
        lanes = slice(g * gw, (g + 1) * gw)
        parts.append(_rms(yz[:, lanes]) * ssg_ref[:, lanes])
    mixed = jnp.concatenate(parts, axis=1).astype(BF16)
    m = jnp.dot(mixed, w_ref[...], preferred_element_type=F32)
    x_new = res_ref[0] + gate_ref[0] * (_rms(m) * g_ref[...])
    o_ref[0] = x_new
    for s in range(x_new.shape[0] // MOE_TILE):
        rows = slice(s * MOE_TILE, (s + 1) * MOE_TILE)
        hb, ext, aff = _route(x_new[rows], g2_ref[...], shift2_ref[0], scale2_ref[0], wr_ref[...])
        hx_ref[0, rows, :D_MODEL] = hb
        hx_ref[0, rows, D_MODEL:] = ext
        aff_ref[0, s] = aff


def merge_out(o_f, o_b, y_f, y_b, feat, z, dn_norm_g, dskip, ssm_norm_g, w_out, g, gate, res,
              g_moe, shift_moe, scale_moe, wr_t, tm):
    bsz, t, d = res.shape
    row = lambda b, i: (b, i, 0)
    const = lambda b, i: (0, 0)
    per_b = lambda b, i: (b, 0, 0)
    half = lambda: pl.BlockSpec((1, tm, DN_VW), row)
    tiles = tm // MOE_TILE
    return pl.pallas_call(
        _merge_out_kernel,
        grid=(bsz, t // tm),
        in_specs=[
            half(), half(), half(), half(),
            pl.BlockSpec((1, tm, SSM_DI), lambda b, i: (b, i, OFF_X // SSM_DI)),
            pl.BlockSpec((1, tm, D_Z), row),
            pl.BlockSpec((1, DN_DV), const),
            pl.BlockSpec((1, SSM_DI), const),
            pl.BlockSpec((1, SSM_DI), const),
            pl.BlockSpec((D_MIX, d), const),
            pl.BlockSpec((1, d), const),
            pl.BlockSpec((1, 1, d), per_b),
            pl.BlockSpec((1, tm, d), row),
            pl.BlockSpec((1, d), const),
            pl.BlockSpec((1, 1, d), per_b),
            pl.BlockSpec((1, 1, d), per_b),
            pl.BlockSpec((N_EXPERTS, d), const),
        ],
        out_specs=[
            pl.BlockSpec((1, tm, d), row),
            pl.BlockSpec((1, tm, D_EXT), row),
            pl.BlockSpec((1, tiles, N_EXPERTS, MOE_TILE), lambda b, i: (b, i, 0, 0)),
        ],
        out_shape=[
            jax.ShapeDtypeStruct((bsz, t, d), F32),
            jax.ShapeDtypeStruct((bsz, t, D_EXT), BF16),
            jax.ShapeDtypeStruct((bsz, t // MOE_TILE, N_EXPERTS, MOE_TILE), F32),
        ],
        compiler_params=_compiler_params(("parallel", "parallel")),
        name="merge_out",
    )(o_f, o_b, y_f, y_b, feat, z, dn_norm_g.reshape(1, DN_DV), dskip, ssm_norm_g.reshape(1, SSM_DI),
      w_out, g.reshape(1, d), gate.reshape(bsz, 1, d), res,
      g_moe.reshape(1, d), shift_moe.reshape(bsz, 1, d), scale_moe.reshape(bsz, 1, d), wr_t)


MOE_TILE = 256
SLOT_WINDOW = 64
SLOT_ALIGN = 16
CNT_LANES = 32
GATE_PIECES = 3
D_EXT = D_MODEL + LANES


def _route(x, g, shift, scale, wr_t):
    hb = ((_rms(x) * g) * (1.0 + scale) + shift).astype(BF16)
    logits = lax.dot_general(wr_t, hb, (((1,), (1,)), ((), ())), preferred_element_type=F32)
    ex = jnp.exp(logits - jnp.max(logits, axis=0, keepdims=True))
    aff = ex / jnp.sum(ex, axis=0, keepdims=True)
    pieces, rest = [], aff
    for _ in range(GATE_PIECES):
        p = rest.astype(BF16).astype(F32)
        pieces.append(p)
        rest = rest - p
    rows = jnp.concatenate(pieces + [jnp.zeros((LANES - GATE_PIECES * N_EXPERTS, MOE_TILE), F32)], axis=0)
    ext = jnp.concatenate([rows[:, m * LANES:(m + 1) * LANES].T for m in range(MOE_TILE // LANES)], axis=0)
    return hb, ext.astype(BF16), aff


def _lane_prefix(mask, upper):
    carry = jnp.zeros((mask.shape[0], 1), F32)
    blocks = []
    for m in range(mask.shape[1] // LANES):
        p = jnp.dot(mask[:, m * LANES:(m + 1) * LANES].astype(BF16), upper, preferred_element_type=F32) + carry
        carry = p[:, LANES - 1:LANES]
        blocks.append(p)
    return jnp.concatenate(blocks, axis=1)


def _select_kernel(aff_ref, pos_ref, posc_ref, cnt_ref, *, nt, cap):
    nbs = aff_ref.shape[0]
    n_rows = nbs * N_EXPERTS
    aff = jnp.concatenate([jnp.concatenate([aff_ref[n, j] for j in range(nt)], axis=1) for n in range(nbs)],
                          axis=0)
    bits = pltpu.bitcast(aff, jnp.int32)

    def bisect(i, thr):
        cand = thr | lax.shift_left(jnp.int32(1), jnp.int32(30) - i)
        n_ge = jnp.sum(jnp.where(bits >= cand, 1.0, 0.0), axis=1, keepdims=True)
        return jnp.where(n_ge >= cap, cand, thr)

    thr = lax.fori_loop(0, 31, bisect, jnp.zeros((n_rows, 1), jnp.int32))
    above = bits > thr
    tied = bits == thr
    room = cap - jnp.sum(jnp.where(above, 1.0, 0.0), axis=1, keepdims=True)
    src = lax.broadcasted_iota(jnp.int32, (LANES, LANES), 0)
    dst = lax.broadcasted_iota(jnp.int32, (LANES, LANES), 1)
    upper = jnp.where(src <= dst, 1.0, 0.0).astype(BF16)
    tie_rank = _lane_prefix(jnp.where(tied, 1.0, 0.0), upper)
    sel = above | (tied & (tie_rank <= room))
    selc = jnp.where(sel, 1.0, 0.0)
    slot = jnp.where(sel, _lane_prefix(selc, upper) - 1.0, -1.0)
    lane = lax.broadcasted_iota(jnp.int32, (n_rows, LANES), 1)
    before = jnp.zeros((n_rows, 1), F32)
    counts = jnp.zeros((n_rows, LANES), F32)
    for j in range(nt):
        before = before + jnp.sum(selc[:, j * MOE_TILE:(j + 1) * MOE_TILE], axis=1, keepdims=True)
        counts = jnp.where(lane == j + 1, before, counts)
    counts = counts.astype(jnp.int32)
    pad = jnp.zeros((LANES - N_EXPERTS, LANES), F32)
    for n in range(nbs):
        mine = slot[n * N_EXPERTS:(n + 1) * N_EXPERTS]
        cnt_ref[n] = counts[n * N_EXPERTS:(n + 1) * N_EXPERTS]
        for j in range(nt):
            pos_ref[n, j] = mine[:, j * MOE_TILE:(j + 1) * MOE_TILE]
        for m in range(slot.shape[1] // LANES):
            posc_ref[n, m * LANES:(m + 1) * LANES, :] = jnp.concatenate([mine[:, m * LANES:(m + 1) * LANES], pad], axis=0).T


SELECT_ROWS = 64
SELECT_TOKENS = 4096


def select_slots(aff, cap):
    bsz, nt, ne, tile = aff.shape
    nbs = min(bsz, max(1, (SELECT_ROWS // ne) * SELECT_TOKENS // (nt * tile)))
    blk = lambda b: (b, 0, 0, 0)
    return pl.pallas_call(
        functools.partial(_select_kernel, nt=nt, cap=cap),
        grid=(bsz // nbs,),
        in_specs=[pl.BlockSpec((nbs, nt, ne, tile), blk)],
        out_specs=[pl.BlockSpec((nbs, nt, ne, tile), blk),
                   pl.BlockSpec((nbs, nt * tile, LANES), lambda b: (b, 0, 0)),
                   pl.BlockSpec((nbs, ne, LANES), lambda b: (b, 0, 0))],
        out_shape=[jax.ShapeDtypeStruct((bsz, nt, ne, tile), F32),
                   jax.ShapeDtypeStruct((bsz, nt * tile, LANES), F32),
                   jax.ShapeDtypeStruct((bsz, ne, LANES), jnp.int32)],
        compiler_params=_compiler_params(("parallel",)),
        name="select_slots",
    )(aff)


def _window_start(cnt_ref, row, j):
    return pl.multiple_of((cnt_ref[row, j] // SLOT_ALIGN) * SLOT_ALIGN, SLOT_ALIGN)


def _extra_windows(cnt_ref, row, j, start):
    return (cnt_ref[row, j + 1] - start - 1) // SLOT_WINDOW


def _gather_kernel(cnt_ref, hx_ref, pos_ref, xs_ref):
    b = pl.program_id(0)
    j = pl.program_id(1)

    @pl.when(j == 0)
    def _():
        xs_ref[...] = jnp.zeros_like(xs_ref)

    hx = hx_ref[0]
    slot_iota = lax.broadcasted_iota(jnp.int32, (SLOT_WINDOW, MOE_TILE), 0).astype(F32)

    def one_hot(e, start):
        return jnp.where(pos_ref[0, 0, e:e + 1, :] - start.astype(F32) == slot_iota, 1.0, 0.0).astype(BF16)

    starts = [_window_start(cnt_ref, b * N_EXPERTS + e, j) for e in range(N_EXPERTS)]
    stacked = jnp.concatenate([one_hot(e, starts[e]) for e in range(N_EXPERTS)], axis=0)
    rows = jnp.dot(stacked, hx, preferred_element_type=F32)
    for e in range(N_EXPERTS):
        xs_ref[0, e, pl.ds(starts[e], SLOT_WINDOW), :] += rows[e * SLOT_WINDOW:(e + 1) * SLOT_WINDOW].astype(BF16)

    for e in range(N_EXPERTS):
        n_more = _extra_windows(cnt_ref, b * N_EXPERTS + e, j, starts[e])

        @pl.when(n_more > 0)
        def _(e=e, n_more=n_more):
            def more(i, carry):
                start = pl.multiple_of(starts[e] + i * SLOT_WINDOW, SLOT_ALIGN)
                r = jnp.dot(one_hot(e, start), hx, preferred_element_type=F32)
                xs_ref[0, e, pl.ds(start, SLOT_WINDOW), :] += r.astype(BF16)
                return carry
            lax.fori_loop(1, n_more + 1, more, 0)


def gather_tokens(cnt, hx, pos, cap):
    bsz, t, d_ext = hx.shape
    nt = t // MOE_TILE
    rows = cap + SLOT_WINDOW
    grid_spec = pltpu.PrefetchScalarGridSpec(
        num_scalar_prefetch=1,
        grid=(bsz, nt),
        in_specs=[
            pl.BlockSpec((1, MOE_TILE, d_ext), lambda b, j, c: (b, j, 0)),
            pl.BlockSpec((1, 1, N_EXPERTS, MOE_TILE), lambda b, j, c: (b, j, 0, 0)),
        ],
        out_specs=pl.BlockSpec((1, N_EXPERTS, rows, d_ext), lambda b, j, c: (b, 0, 0, 0),
                               pipeline_mode=pl.Buffered(1)),
    )
    return pl.pallas_call(
        _gather_kernel,
        grid_spec=grid_spec,
        out_shape=jax.ShapeDtypeStruct((bsz, N_EXPERTS, rows, d_ext), BF16),
        compiler_params=_compiler_params(("parallel", "arbitrary")),
        name="gather_tokens",
    )(cnt, hx, pos)


def _expert_ffn_kernel(xs_ref, wg_ref, wu_ref, wd_ref, o_ref):
    e = pl.program_id(0)
    d = wg_ref.shape[1]
    nbs = xs_ref.shape[0]
    cap = xs_ref.shape[2] - SLOT_WINDOW
    xs = jnp.concatenate([xs_ref[n, 0, :cap, :d] for n in range(nbs)], axis=0)
    ext = jnp.concatenate([xs_ref[n, 0, :cap, d:] for n in range(nbs)], axis=0).astype(F32)
    lane = lax.broadcasted_iota(jnp.int32, ext.shape, 1)
    mine = (lane % N_EXPERTS == e) & (lane < GATE_PIECES * N_EXPERTS)
    gate = jnp.sum(jnp.where(mine, ext, 0.0), axis=1, keepdims=True)
    a = jnp.dot(xs, wg_ref[0], preferred_element_type=F32)
    u = jnp.dot(xs, wu_ref[0], preferred_element_type=F32)
    hid = (_silu(a) * u).astype(BF16)
    out = (jnp.dot(hid, wd_ref[0], preferred_element_type=F32) * gate).astype(BF16)
    for n in range(nbs):
        o_ref[n, 0, :cap] = out[n * cap:(n + 1) * cap]
        o_ref[n, 0, cap:] = jnp.zeros((SLOT_WINDOW, d), BF16)


FFN_ROWS = 512


def expert_ffn(xs, wg, wu, wd):
    bsz, ne, rows, d_ext = xs.shape
    _, d, f = wg.shape
    nbs = min(bsz, max(1, FFN_ROWS // (rows - SLOT_WINDOW)))
    return pl.pallas_call(
        _expert_ffn_kernel,
        grid=(ne, bsz // nbs),
        in_specs=[
            pl.BlockSpec((nbs, 1, rows, d_ext), lambda e, b: (b, e, 0, 0)),
            pl.BlockSpec((1, d, f), lambda e, b: (e, 0, 0)),
            pl.BlockSpec((1, d, f), lambda e, b: (e, 0, 0)),
            pl.BlockSpec((1, f, d), lambda e, b: (e, 0, 0)),
        ],
        out_specs=pl.BlockSpec((nbs, 1, rows, d), lambda e, b: (b, e, 0, 0)),
        out_shape=jax.ShapeDtypeStruct((bsz, ne, rows, d), BF16),
        compiler_params=_compiler_params(("parallel", "parallel")),
        name="expert_ffn",
    )(xs, wg, wu, wd)


def _scatter_out_kernel(cnt_ref, og_ref, posc_ref, g_ref, gate_ref, res_ref, o_ref):
    b = pl.program_id(0)
    j = pl.program_id(1)
    posc = posc_ref[0]
    starts = [_window_start(cnt_ref, b * N_EXPERTS + e, j) for e in range(N_EXPERTS)]
    lane = lax.broadcasted_iota(jnp.int32, (MOE_TILE, LANES), 1)
    lane_f = lane.astype(F32)
    per_tile = LANES // SLOT_WINDOW
    tiles = []
    for m in range(N_EXPERTS // per_tile):
        rel = None
        for q in range(per_tile - 1, -1, -1):
            e = m * per_tile + q
            v = posc[:, e:e + 1] - starts[e].astype(F32) + float(q * SLOT_WINDOW)
            rel = v if rel is None else jnp.where(lane < (q + 1) * SLOT_WINDOW, v, rel)
        tiles.append(jnp.where(rel == lane_f, 1.0, 0.0).astype(BF16))
    sel_t = jnp.concatenate(tiles, axis=1)
    og = jnp.concatenate([og_ref[0, e, pl.ds(starts[e], SLOT_WINDOW), :] for e in range(N_EXPERTS)], axis=0)
    o_ref[0] = jnp.dot(sel_t, og, preferred_element_type=F32)

    win_lane = lax.broadcasted_iota(jnp.int32, (MOE_TILE, SLOT_WINDOW), 1).astype(F32)
    for e in range(N_EXPERTS):
        n_more = _extra_windows(cnt_ref, b * N_EXPERTS + e, j, starts[e])

        @pl.when(n_more > 0)
        def _(e=e, n_more=n_more):
            def more(i, carry):
                start = pl.multiple_of(starts[e] + i * SLOT_WINDOW, SLOT_ALIGN)
                p = jnp.where(posc[:, e:e + 1] - start.astype(F32) == win_lane, 1.0, 0.0).astype(BF16)
                o_ref[0] += jnp.dot(p, og_ref[0, e, pl.ds(start, SLOT_WINDOW), :], preferred_element_type=F32)
                return carry
            lax.fori_loop(1, n_more + 1, more, 0)

    o_ref[0] = res_ref[0] + gate_ref[0] * (_rms(o_ref[0]) * g_ref[...])


def scatter_out(cnt, og, posc, g, gate, res):
    bsz, t, d = res.shape
    nt = t // MOE_TILE
    _, ne, rows, _ = og.shape
    grid_spec = pltpu.PrefetchScalarGridSpec(
        num_scalar_prefetch=1,
        grid=(bsz, nt),
        in_specs=[
            pl.BlockSpec((1, ne, rows, d), lambda b, j, c: (b, 0, 0, 0), pipeline_mode=pl.Buffered(1)),
            pl.BlockSpec((1, MOE_TILE, LANES), lambda b, j, c: (b, j, 0)),
            pl.BlockSpec((1, d), lambda b, j, c: (0, 0)),
            pl.BlockSpec((1, 1, d), lambda b, j, c: (b, 0, 0)),
            pl.BlockSpec((1, MOE_TILE, d), lambda b, j, c: (b, j, 0)),
        ],
        out_specs=pl.BlockSpec((1, MOE_TILE, d), lambda b, j, c: (b, j, 0)),
    )
    return pl.pallas_call(
        _scatter_out_kernel,
        grid_spec=grid_spec,
        out_shape=jax.ShapeDtypeStruct((bsz, t, d), F32),
        compiler_params=_compiler_params(("parallel", "parallel")),
        name="scatter_out",
    )(cnt, og, posc, g.reshape(1, d), gate.reshape(bsz, 1, d), res)


def ec_moe_residual(x, hx, aff, wg, wu, wd, g_out, gate):
    bsz, t, _ = x.shape
    cap = EC_CAPACITY_FACTOR * t // N_EXPERTS
    pos, posc, counts = select_slots(aff, cap)
    cnt = counts[:, :, :CNT_LANES].reshape(bsz * N_EXPERTS, CNT_LANES)
    xs = gather_tokens(cnt, hx, pos, cap)
    og = expert_ffn(xs, wg, wu, wd)
    return scatter_out(cnt, og, posc, g_out, gate, x)


def _token_tile(t):
    return min(512, t)


def mixer_stream(x, g0, shift, scale, wc, wz, ws, w9, conv_b, bias_rows, nega_rows, dn_state, ssm_state, on_grid):
    t = x.shape[1]
    tm = _token_tile(t)
    feat, z, gcol, grow, arow = in_proj_conv(x, g0, shift, scale, wc, wz, ws, w9, conv_b, bias_rows, nega_rows,
                                             tm, on_grid)
    ts = min(SCAN_TILE, t)
    o_f, o_b, dn_state = delta_scan(feat, gcol, grow, dn_state, ts, SCAN_SAMPLES_PER_STEP)
    y_f, y_b, ssm_state = ssd_scan(feat, gcol, arow, ssm_state, ts, SCAN_SAMPLES_PER_STEP)
    return (o_f, o_b, y_f, y_b, feat, z), dn_state, ssm_state


def _gate_param_rows(dn_bias, dn_a_log, ssm_bias, ssm_a_log):
    zeros = jnp.zeros((NDH,), F32)
    bias = jnp.concatenate([zeros, dn_bias.reshape(-1), ssm_bias.reshape(-1)])
    nega = jnp.concatenate([zeros, -jnp.exp(dn_a_log.reshape(-1)), -jnp.exp(ssm_a_log.reshape(-1))])
    pad = LANES - bias.shape[0]
    expand = lambda v: jnp.broadcast_to(jnp.pad(v, (0, pad))[:, None], (LANES, LANES))
    return expand(bias), expand(nega)


def kernel(x, c, ctx, c_ctx, ada_w, ada_b, norm_g, w_in, conv_w, conv_b, dn_A_log, dn_dt_bias,
           dn_norm_g, ssm_A_log, ssm_dt_bias, ssm_D, ssm_norm_g, w_out, router_w,
           exp_w_gate, exp_w_up, exp_w_down):
    bsz = x.shape[0]
    s_lat = jax.nn.silu(c)
    s_ctx = jax.nn.silu(c_ctx)
    for l in range(DEPTH):
        last = l == DEPTH - 1
        mod_lat = jnp.split(s_lat @ ada_w[l] + ada_b[l], 6, axis=-1)
        mod_ctx_row = s_ctx @ ada_w[l] + ada_b[l]
        mod_ctx = [jnp.broadcast_to(m[None, :], (bsz, D_MODEL)) for m in jnp.split(mod_ctx_row, 6)]

        w_in_l = w_in[l]
        wc = w_in_l[:, :CONV_CH].astype(BF16)
        wz = w_in_l[:, CONV_CH:CONV_CH + D_Z].astype(BF16)
        ws = jnp.pad(w_in_l[:, CONV_CH + D_Z:], ((0, 0), (0, LANES - N_GATE_COLS))).astype(BF16)
        w9 = conv_w[l].reshape(CONV_CH, CONV_K * CONV_K).T
        cb = conv_b[l].reshape(1, CONV_CH)
        bias_rows, nega_rows = _gate_param_rows(dn_dt_bias[l], dn_A_log[l], ssm_dt_bias[l], ssm_A_log[l])
        dskip = jnp.repeat(ssm_D[l], SSM_P).reshape(1, SSM_DI)
        w_out_l = w_out[l].astype(BF16)
        wr_t = router_w[l].T.astype(BF16)
        moe_w = (exp_w_gate[l].astype(BF16), exp_w_up[l].astype(BF16), exp_w_down[l].astype(BF16), norm_g[l, 3])

        dn0 = jnp.zeros((bsz, 2, DN_HEADS, DN_DK, DN_DV), F32)
        ssm0 = jnp.zeros((bsz, 2, SSM_G, SSM_N, SSM_GW), F32)
        shared = (wc, wz, ws, w9, cb, bias_rows, nega_rows)
        mix_ctx, dn_c, ssm_c = mixer_stream(ctx, norm_g[l, 0], mod_ctx[0], mod_ctx[1], *shared, dn0, ssm0, False)
        mix_lat, _, _ = mixer_stream(x, norm_g[l, 0], mod_lat[0], mod_lat[1], *shared, dn_c, ssm_c, True)
        merge_w = (dn_norm_g[l], dskip, ssm_norm_g[l], w_out_l, norm_g[l, 1])
        x, hx, aff = merge_out(*mix_lat, *merge_w, mod_lat[2], x, norm_g[l, 2], mod_lat[3], mod_lat[4], wr_t,
                               _token_tile(x.shape[1]))
        x = ec_moe_residual(x, hx, aff, *moe_w, mod_lat[5])

        if not last:
            ctx, hx, aff = merge_out(*mix_ctx, *merge_w, mod_ctx[2], ctx, norm_g[l, 2], mod_ctx[3], mod_ctx[4], wr_t,
                                     _token_tile(ctx.shape[1]))
            ctx = ec_moe_residual(ctx, hx, aff, *moe_w, mod_ctx[5])
    return x
```

```python
import functools

import jax
import jax.numpy as jnp
import numpy as np
from jax import lax
from jax.experimental import pallas as pl
from jax.experimental.pallas import tpu as pltpu

D_MODEL = 1024
DEPTH = 4
GRID_W = 64
DN_HEADS = 4
DN_DK = 128
DN_DV = 128
SSM_HEADS = 8
SSM_P = 64
SSM_N = 128
SSM_G = 2
CHUNK = 64
CONV_K = 3
N_EXPERTS = 16
EC_CAPACITY_FACTOR = 2
D_EXPERT = 512
EPS = 1e-6

DN_QK = DN_HEADS * DN_DK
DN_VW = DN_HEADS * DN_DV
SSM_DI = SSM_HEADS * SSM_P
SSM_BC = SSM_G * SSM_N
SSM_HPG = SSM_HEADS // SSM_G
SSM_GW = SSM_HPG * SSM_P
D_MIX = DN_VW + SSM_DI
CONV_SPLITS = (DN_QK, DN_QK, DN_VW, SSM_DI, SSM_BC, SSM_BC)
CONV_CH = sum(CONV_SPLITS)
D_Z = DN_VW + SSM_DI
N_GATE_COLS = 2 * DN_HEADS + 2 * DN_HEADS + 2 * SSM_HEADS
D_IN_PROJ = CONV_CH + D_Z + N_GATE_COLS

LANES = 128
SUBLANES = 8
VMEM_LIMIT_BYTES = 56 * 1024 * 1024

OFF_Q, OFF_K, OFF_V = 0, DN_QK, 2 * DN_QK
OFF_X = 2 * DN_QK + DN_VW
OFF_B = OFF_X + SSM_DI
OFF_C = OFF_B + SSM_BC

NDH = 2 * DN_HEADS
NSH = 2 * SSM_HEADS
COL_BETA, COL_G, COL_EG, COL_EGL, COL_EGT = (i * NDH for i in range(5))
COL_DT, COL_A, COL_EA, COL_DTEAL, COL_EAT = (5 * NDH + i * NSH for i in range(5))

F32 = jnp.float32
BF16 = jnp.bfloat16


def _compiler_params(semantics):
    return pltpu.CompilerParams(dimension_semantics=semantics, vmem_limit_bytes=VMEM_LIMIT_BYTES)


def _rms(x):
    return x * lax.rsqrt(jnp.mean(x * x, axis=-1, keepdims=True) + EPS)


def _silu(x):
    return x * jax.nn.sigmoid(x)


def _softplus(x):
    return jnp.maximum(x, 0.0) + jnp.log(1.0 + jnp.exp(-jnp.abs(x)))


def _dot(a, b):
    return jnp.dot(a.astype(BF16), b.astype(BF16), preferred_element_type=F32)


def _dot_nt(a, b):
    return lax.dot_general(a.astype(BF16), b.astype(BF16), (((1,), (1,)), ((), ())),
                           preferred_element_type=F32)


def _dot_tn(a, b):
    return lax.dot_general(a.astype(BF16), b.astype(BF16), (((0,), (0,)), ((), ())),
                           preferred_element_type=F32)


def _dot_exact01(a, m01):
    a1 = a.astype(BF16)
    r1 = a - a1.astype(F32)
    a2 = r1.astype(BF16)
    a3 = (r1 - a2.astype(F32)).astype(BF16)
    m = m01.astype(BF16)
    out = jnp.dot(a3, m, preferred_element_type=F32)
    out = out + jnp.dot(a2, m, preferred_element_type=F32)
    return out + jnp.dot(a1, m, preferred_element_type=F32)


SSD_ROWS = 2 * 2 * SSM_G
HALO_BLOCK = 128
HALO = GRID_W + 16
CONV_LANES = 256


def _in_proj_conv_kernel(xp_ref, x_ref, xn_ref, g_ref, shift_ref, scale_ref, wc_ref, wz_ref, ws_ref, w9_ref, cb_ref,
                         gbias_ref, gnega_ref, of_ref, oz_ref, col_ref, grow_ref, arow_ref, *, on_grid):
    i = pl.program_id(1)
    n_i = pl.num_programs(1)
    tm = x_ref.shape[1]

    def modulated(x):
        h = _rms(x) * g_ref[...]
        return h * (1.0 + scale_ref[0]) + shift_ref[0]

    has_prev = jnp.where(i > 0, 1.0, 0.0)
    has_next = jnp.where(i < n_i - 1, 1.0, 0.0)
    h_main = modulated(x_ref[0]).astype(BF16)
    h_prev = (modulated(xp_ref[0, HALO_BLOCK - HALO:, :]) * has_prev).astype(BF16)
    h_next = (modulated(xn_ref[0, :HALO, :]) * has_next).astype(BF16)
    h_all = jnp.concatenate([h_prev, h_main, h_next], axis=0)

    oz_ref[0] = jnp.dot(h_main, wz_ref[...], preferred_element_type=F32)
    _write_gates(jnp.dot(h_main, ws_ref[...], preferred_element_type=F32), gbias_ref[...], gnega_ref[...],
                 col_ref.at[0], grow_ref.at[0], arow_ref.at[0])

    dys = (-1, 0, 1) if on_grid else (0,)
    col = lax.broadcasted_iota(jnp.int32, (tm, CONV_LANES), 0) % GRID_W
    n_blocks = CONV_CH // CONV_LANES

    def project(c):
        return jnp.dot(h_all, wc_ref[:, c * CONV_LANES:(c + 1) * CONV_LANES], preferred_element_type=F32)

    def conv(c, p):
        lanes = slice(c * CONV_LANES, (c + 1) * CONV_LANES)
        acc_c = acc_m = acc_p = None
        for dy in dys:
            lo = HALO + dy * GRID_W
            win = p[lo - SUBLANES:lo + tm + SUBLANES]
            wr = 3 * (dy + 1)
            tc = win[SUBLANES:SUBLANES + tm] * w9_ref[wr + 1:wr + 2, lanes]
            tl, tr = win * w9_ref[wr:wr + 1, lanes], win * w9_ref[wr + 2:wr + 3, lanes]
            acc_c = tc if acc_c is None else acc_c + tc
            acc_m = tl if acc_m is None else acc_m + tl
            acc_p = tr if acc_p is None else acc_p + tr
        acc_m = pltpu.roll(acc_m, 1, 0)[SUBLANES:SUBLANES + tm]
        acc_p = pltpu.roll(acc_p, tm + 2 * SUBLANES - 1, 0)[SUBLANES:SUBLANES + tm]
        if on_grid:
            acc_m = jnp.where(col != 0, acc_m, 0.0)
            acc_p = jnp.where(col != GRID_W - 1, acc_p, 0.0)
        u = _silu(acc_c + acc_m + acc_p + cb_ref[:, lanes])
        if c * CONV_LANES < 2 * DN_QK:
            scale = DN_DK ** -0.5 if c * CONV_LANES < DN_QK else 1.0
            heads = []
            for hh in range(CONV_LANES // DN_DK):
                uh = u[:, hh * DN_DK:(hh + 1) * DN_DK]
                heads.append(uh * (lax.rsqrt(jnp.sum(uh * uh, axis=-1, keepdims=True) + EPS) * scale))
            u = jnp.concatenate(heads, axis=1)
        of_ref[0, :, lanes] = u

    p = project(0)
    for c in range(n_blocks):
        p_next = project(c + 1) if c + 1 < n_blocks else None
        conv(c, p)
        p = p_next


def in_proj_conv(x, g, shift, scale, wc, wz, ws, w9, conv_b, bias_rows, nega_rows, tm, on_grid):
    bsz, t, d = x.shape
    nc = t // CHUNK
    ncg = tm // CHUNK
    per_tile = tm // HALO_BLOCK
    last_halo = t // HALO_BLOCK - 1
    row = lambda b, i: (b, i, 0)
    const = lambda b, i: (0, 0)
    per_b = lambda b, i: (b, 0, 0)
    return pl.pallas_call(
        functools.partial(_in_proj_conv_kernel, on_grid=on_grid),
        grid=(bsz, t // tm),
        in_specs=[
            pl.BlockSpec((1, HALO_BLOCK, d), lambda b, i: (b, jnp.maximum(i * per_tile - 1, 0), 0)),
            pl.BlockSpec((1, tm, d), row),
            pl.BlockSpec((1, HALO_BLOCK, d), lambda b, i: (b, jnp.minimum((i + 1) * per_tile, last_halo), 0)),
            pl.BlockSpec((1, d), const),
            pl.BlockSpec((1, 1, d), per_b),
            pl.BlockSpec((1, 1, d), per_b),
            pl.BlockSpec((d, CONV_CH), const),
            pl.BlockSpec((d, D_Z), const),
            pl.BlockSpec((d, LANES), const),
            pl.BlockSpec((9, CONV_CH), const),
            pl.BlockSpec((1, CONV_CH), const),
            pl.BlockSpec((LANES, LANES), const),
            pl.BlockSpec((LANES, LANES), const),
        ],
        out_specs=[
            pl.BlockSpec((1, tm, CONV_CH), row),
            pl.BlockSpec((1, tm, D_Z), row),
            pl.BlockSpec((1, tm, LANES), row),
            pl.BlockSpec((1, ncg, NDH, CHUNK), lambda b, i: (b, i, 0, 0)),
            pl.BlockSpec((1, ncg, SSD_ROWS, SSM_GW), lambda b, i: (b, i, 0, 0)),
        ],
        out_shape=[
            jax.ShapeDtypeStruct((bsz, t, CONV_CH), F32),
            jax.ShapeDtypeStruct((bsz, t, D_Z), F32),
            jax.ShapeDtypeStruct((bsz, t, LANES), F32),
            jax.ShapeDtypeStruct((bsz, nc, NDH, CHUNK), F32),
            jax.ShapeDtypeStruct((bsz, nc, SSD_ROWS, SSM_GW), F32),
        ],
        compiler_params=_compiler_params(("parallel", "parallel")),
        name="in_proj_conv",
    )(x, x, x, g.reshape(1, d), shift.reshape(bsz, 1, d), scale.reshape(bsz, 1, d), wc, wz, ws, w9, conv_b,
      bias_rows, nega_rows)


def _write_gates(small, bias, nega, col_ref, grow_ref, arow_ref):
    tok = lax.broadcasted_iota(jnp.int32, (LANES, LANES), 0)
    out = lax.broadcasted_iota(jnp.int32, (LANES, LANES), 1)
    same = (tok // CHUNK) == (out // CHUNK)
    m_fwd = jnp.where(same & (tok <= out), 1.0, 0.0)
    m_bwd = jnp.where(same & (tok >= out), 1.0, 0.0)
    m_all = jnp.where(same, 1.0, 0.0)

    def dir_cumsum(v, heads):
        f = _dot_exact01(v, m_fwd)
        b = _dot_exact01(v, m_bwd)
        is_fwd = lax.broadcasted_iota(jnp.int32, v.shape, 0) < heads
        return jnp.where(is_fwd, f, b), _dot_exact01(v, m_all)

    for s in range(small.shape[0] // LANES):
        st = small[s * LANES:(s + 1) * LANES, :].T
        beta = jax.nn.sigmoid(st[0:NDH])
        sp = _softplus(st[NDH:2 * NDH + NSH] + bias[NDH:2 * NDH + NSH])
        logg = sp[0:NDH] * nega[NDH:2 * NDH]
        dt = sp[NDH:]
        a = dt * nega[2 * NDH:2 * NDH + NSH]
        g_cs, g_tot = dir_cumsum(logg, DN_HEADS)
        a_cs, a_tot = dir_cumsum(a, SSM_HEADS)
        rows = jnp.concatenate([
            beta, g_cs, jnp.exp(g_cs), jnp.exp(g_tot - g_cs), jnp.exp(g_tot),
            dt, a_cs, jnp.exp(a_cs), dt * jnp.exp(a_tot - a_cs), jnp.exp(a_tot),
            jnp.zeros((LANES - 5 * NDH - 5 * NSH, LANES), F32)], axis=0)
        col_ref[s * LANES:(s + 1) * LANES, :] = rows.T
        for half in range(LANES // CHUNK):
            c = s * (LANES // CHUNK) + half
            lo = half * CHUNK
            grow_ref[c] = g_cs[:, lo:lo + CHUNK]
            for q, v in enumerate((a_cs, dt)):
                for dg in range(2 * SSM_G):
                    r0 = dg * SSM_HPG
                    r = q * 2 * SSM_G + dg
                    arow_ref[c, r:r + 1, :] = jnp.concatenate(
                        [v[r0 + hh:r0 + hh + 1, lo:lo + CHUNK] for hh in range(SSM_HPG)], axis=1)


SCAN_SAMPLES_PER_STEP = 4
SCAN_TILE = 256


def _chunk_masks(rev):
    r = lax.broadcasted_iota(jnp.int32, (CHUNK, CHUNK), 0)
    c = lax.broadcasted_iota(jnp.int32, (CHUNK, CHUNK), 1)
    return ((r <= c), (r < c)) if rev else ((r >= c), (r > c))


def _unit_tri_inverse(ms):
    eye = (lax.broadcasted_iota(jnp.int32, (CHUNK, CHUNK), 0)
           == lax.broadcasted_iota(jnp.int32, (CHUNK, CHUNK), 1)).astype(F32)
    ps = [_dot(m, m) for m in ms]
    ts = [eye + m for m in ms]
    n_sq = int(np.log2(CHUNK)) - 1
    for _ in range(n_sq - 1):
        rs = [_dot(jnp.concatenate([t, p], axis=0), p) for t, p in zip(ts, ps)]
        ts = [t + r[:CHUNK] for t, r in zip(ts, rs)]
        ps = [r[CHUNK:] for r in rs]
    return [t + _dot(t, p) for t, p in zip(ts, ps)]


def _dn_kernel(qf_ref, kf_ref, vf_ref, cf_ref, rf_ref, qb_ref, kb_ref, vb_ref, cb_ref, rb_ref, s0_ref,
               of_ref, ob_ref, st_ref, s_scr, *, nct, nb):
    j = pl.program_id(1)

    @pl.when(j == 0)
    def _():
        s_scr[...] = s0_ref[...]

    masks = (_chunk_masks(False), _chunk_masks(True))
    dir_refs = ((qf_ref, kf_ref, vf_ref, cf_ref, rf_ref, of_ref),
                (qb_ref, kb_ref, vb_ref, cb_ref, rb_ref, ob_ref))

    def body(c, carry):
        chains = []
        for n in range(nb):
            for d in range(2):
                cc = c if d == 0 else nct - 1 - c
                r0 = pl.multiple_of(cc * CHUNK, CHUNK)
                ct = dir_refs[d][3][n, pl.ds(r0, CHUNK), :]
                for h in range(DN_HEADS):
                    chains.append((d, h, cc, r0, ct, n))

        def load(chain, which):
            d, h, _, r0, _, n = chain
            return dir_refs[d][which][n, pl.ds(r0, CHUNK), h * DN_DK:(h + 1) * DN_DK]

        def col(chain, off, rows=slice(None)):
            d, h, _, _, ct, _ = chain
            lane = off + d * DN_HEADS + h
            return ct[rows, lane:lane + 1]

        def decay_of(chain):
            d, h, cc, _, _, n = chain
            incl = masks[d][0]
            g_row = dir_refs[d][4][n, cc, d * DN_HEADS + h:d * DN_HEADS + h + 1, :]
            return jnp.where(incl, jnp.exp(jnp.where(incl, col(chain, COL_G) - g_row, 0.0)), 0.0)

        kbeta = [load(ch, 1) * col(ch, COL_BETA) for ch in chains]
        a = [_dot_nt(jnp.concatenate([kb, load(ch, 0)], axis=0), load(ch, 1)) for ch, kb in zip(chains, kbeta)]
        decay = [decay_of(ch) for ch in chains]
        neg_l = [jnp.where(masks[ch[0]][1], -(ai[:CHUNK] * dc), 0.0) for ch, ai, dc in zip(chains, a, decay)]
        qk = [ai[CHUNK:] * dc for ai, dc in zip(a, decay)]
        t_inv = _unit_tri_inverse(neg_l)
        uw = [_dot(ti, jnp.concatenate([load(ch, 2) * col(ch, COL_BETA), kb * col(ch, COL_EG)], axis=1))
              for ch, ti, kb in zip(chains, t_inv, kbeta)]
        s = [s_scr[ch[5], ch[0], ch[1]] for ch in chains]
        ws_qs = [_dot(jnp.concatenate([uwi[:, DN_DV:], load(ch, 0) * col(ch, COL_EG)], axis=0), si)
                 for ch, uwi, si in zip(chains, uw, s)]
        v_new = [uwi[:, :DN_DV] - wq[:CHUNK] for uwi, wq in zip(uw, ws_qs)]
        o = [wq[CHUNK:] + _dot(qki, vn) for wq, qki, vn in zip(ws_qs, qk, v_new)]
        s_new = [si * col(ch, COL_EGT, slice(0, 1)) + _dot_tn(load(ch, 1) * col(ch, COL_EGL), vn)
                 for ch, si, vn in zip(chains, s, v_new)]
        for ch, oi, sn in zip(chains, o, s_new):
            d, h, _, r0, _, n = ch
            dir_refs[d][5][n, pl.ds(r0, CHUNK), h * DN_DV:(h + 1) * DN_DV] = oi
            s_scr[n, d, h] = sn
        return carry

    lax.fori_loop(0, nct, body, 0)

    @pl.when(j == pl.num_programs(1) - 1)
    def _():
        st_ref[...] = s_scr[...]


def delta_scan(feat, gcol, grow, s0, tt, nb):
    bsz, t, _ = feat.shape
    nt = t // tt
    nct = tt // CHUNK
    fwd = lambda lane_blk: (lambda b, j: (b, j, lane_blk))
    bwd = lambda lane_blk: (lambda b, j: (b, nt - 1 - j, lane_blk))
    qkv_blk = (nb, tt, DN_QK)

    def side(im):
        return [pl.BlockSpec(qkv_blk, im(OFF_Q // DN_QK)), pl.BlockSpec(qkv_blk, im(OFF_K // DN_QK)),
                pl.BlockSpec(qkv_blk, im(OFF_V // DN_QK)), pl.BlockSpec((nb, tt, LANES), im(0))]

    row_spec = lambda rev: pl.BlockSpec((nb, nct, NDH, CHUNK),
                                        (lambda b, j: (b, nt - 1 - j, 0, 0)) if rev else (lambda b, j: (b, j, 0, 0)))
    state_spec = pl.BlockSpec((nb, 2, DN_HEADS, DN_DK, DN_DV), lambda b, j: (b, 0, 0, 0, 0))
    return pl.pallas_call(
        functools.partial(_dn_kernel, nct=nct, nb=nb),
        grid=(bsz // nb, nt),
        in_specs=side(fwd) + [row_spec(False)] + side(bwd) + [row_spec(True)] + [state_spec],
        out_specs=[pl.BlockSpec((nb, tt, DN_VW), fwd(0)), pl.BlockSpec((nb, tt, DN_VW), bwd(0)), state_spec],
        out_shape=[
            jax.ShapeDtypeStruct((bsz, t, DN_VW), F32),
            jax.ShapeDtypeStruct((bsz, t, DN_VW), F32),
            jax.ShapeDtypeStruct((bsz, 2, DN_HEADS, DN_DK, DN_DV), F32),
        ],
        scratch_shapes=[pltpu.VMEM((nb, 2, DN_HEADS, DN_DK, DN_DV), F32)],
        compiler_params=_compiler_params(("parallel", "arbitrary")),
        name="delta_scan",
    )(feat, feat, feat, gcol, grow, feat, feat, feat, gcol, grow, s0)


def _group_lane_select(cols):
    shape = (cols[0].shape[0], SSM_GW)
    head = lax.broadcasted_iota(jnp.int32, shape, 1) // SSM_P
    out = jnp.broadcast_to(cols[SSM_HPG - 1], shape)
    for hh in range(SSM_HPG - 2, -1, -1):
        out = jnp.where(head == hh, jnp.broadcast_to(cols[hh], shape), out)
    return out


def _ssd_kernel(xf_ref, bf_ref, cf_ref, gf_ref, rf_ref, xb_ref, bb_ref, cb_ref, gb_ref, rb_ref, h0_ref,
                yf_ref, yb_ref, ht_ref, h_scr, *, nct, nb):
    j = pl.program_id(1)

    @pl.when(j == 0)
    def _():
        h_scr[...] = h0_ref[...]

    row_i = lax.broadcasted_iota(jnp.int32, (CHUNK, SSM_GW), 0)
    col_j = lax.broadcasted_iota(jnp.int32, (CHUNK, SSM_GW), 1) % SSM_P
    incl_dir = (row_i >= col_j, row_i <= col_j)
    bd_rows = lax.broadcasted_iota(jnp.int32, (SSM_HPG * CHUNK, SSM_GW), 0) // CHUNK
    bd_cols = lax.broadcasted_iota(jnp.int32, (SSM_HPG * CHUNK, SSM_GW), 1) // SSM_P
    block_diag = bd_rows == bd_cols
    dir_refs = ((xf_ref, bf_ref, cf_ref, gf_ref, rf_ref, yf_ref),
                (xb_ref, bb_ref, cb_ref, gb_ref, rb_ref, yb_ref))

    def body(c, carry):
        chains = []
        for n in range(nb):
            for d in range(2):
                cc = c if d == 0 else nct - 1 - c
                r0 = pl.multiple_of(cc * CHUNK, CHUNK)
                ct = dir_refs[d][3][n, pl.ds(r0, CHUNK), :]
                for g in range(SSM_G):
                    chains.append((d, g, cc, r0, ct, n))

        def load(chain, which, width):
            d, g, _, r0, _, n = chain
            return dir_refs[d][which][n, pl.ds(r0, CHUNK), g * width:(g + 1) * width]

        def pick(chain, off, rows=slice(None)):
            d, g, _, _, ct, _ = chain
            h0 = off + d * SSM_HEADS + g * SSM_HPG
            return _group_lane_select([ct[rows, h0 + hh:h0 + hh + 1] for hh in range(SSM_HPG)])

        def decay_dt_of(chain):
            d, g, cc, _, _, n = chain
            incl = incl_dir[d]
            dg = d * SSM_G + g
            a_row = dir_refs[d][4][n, cc, dg:dg + 1, :]
            dt_row = dir_refs[d][4][n, cc, 2 * SSM_G + dg:2 * SSM_G + dg + 1, :]
            return jnp.where(incl, jnp.exp(jnp.where(incl, pick(chain, COL_A) - a_row, 0.0)) * dt_row, 0.0)

        cb4 = [_dot_nt(load(ch, 2, SSM_N), jnp.concatenate([load(ch, 1, SSM_N)] * SSM_HPG, axis=0)) for ch in chains]
        hg = [h_scr[ch[5], ch[0], ch[1]] for ch in chains]
        y_off = [_dot(load(ch, 2, SSM_N), h) for ch, h in zip(chains, hg)]
        h_in = [_dot_tn(load(ch, 1, SSM_N), load(ch, 0, SSM_GW) * pick(ch, COL_DTEAL)) for ch in chains]
        scores = [cb * decay_dt_of(ch) for ch, cb in zip(chains, cb4)]
        x_bd = [jnp.where(block_diag, jnp.concatenate([load(ch, 0, SSM_GW)] * SSM_HPG, axis=0), 0.0)
                for ch in chains]
        y_diag = [_dot(sc, xb) for sc, xb in zip(scores, x_bd)]
        for ch, yd, yo, h, hi in zip(chains, y_diag, y_off, hg, h_in):
            d, g, _, r0, _, n = ch
            dir_refs[d][5][n, pl.ds(r0, CHUNK), g * SSM_GW:(g + 1) * SSM_GW] = yd + yo * pick(ch, COL_EA)
            h_scr[n, d, g] = h * pick(ch, COL_EAT, slice(0, 1)) + hi
        return carry

    lax.fori_loop(0, nct, body, 0)

    @pl.when(j == pl.num_programs(1) - 1)
    def _():
        ht_ref[...] = h_scr[...]


def ssd_scan(feat, gcol, arow, h0, tt, nb):
    bsz, t, _ = feat.shape
    nt = t // tt
    nct = tt // CHUNK
    fwd = lambda lane_blk: (lambda b, j: (b, j, lane_blk))
    bwd = lambda lane_blk: (lambda b, j: (b, nt - 1 - j, lane_blk))

    def side(im):
        return [pl.BlockSpec((nb, tt, SSM_DI), im(OFF_X // SSM_DI)),
                pl.BlockSpec((nb, tt, SSM_BC), im(OFF_B // SSM_BC)),
                pl.BlockSpec((nb, tt, SSM_BC), im(OFF_C // SSM_BC)),
                pl.BlockSpec((nb, tt, LANES), im(0))]

    row_spec = lambda rev: pl.BlockSpec((nb, nct, SSD_ROWS, SSM_GW),
                                        (lambda b, j: (b, nt - 1 - j, 0, 0)) if rev else (lambda b, j: (b, j, 0, 0)))
    state_spec = pl.BlockSpec((nb, 2, SSM_G, SSM_N, SSM_GW), lambda b, j: (b, 0, 0, 0, 0))
    return pl.pallas_call(
        functools.partial(_ssd_kernel, nct=nct, nb=nb),
        grid=(bsz // nb, nt),
        in_specs=side(fwd) + [row_spec(False)] + side(bwd) + [row_spec(True)] + [state_spec],
        out_specs=[pl.BlockSpec((nb, tt, SSM_DI), fwd(0)), pl.BlockSpec((nb, tt, SSM_DI), bwd(0)), state_spec],
        out_shape=[
            jax.ShapeDtypeStruct((bsz, t, SSM_DI), F32),
            jax.ShapeDtypeStruct((bsz, t, SSM_DI), F32),
            jax.ShapeDtypeStruct((bsz, 2, SSM_G, SSM_N, SSM_GW), F32),
        ],
        scratch_shapes=[pltpu.VMEM((nb, 2, SSM_G, SSM_N, SSM_GW), F32)],
        compiler_params=_compiler_params(("parallel", "arbitrary")),
        name="ssd_scan",
    )(feat, feat, feat, gcol, arow, feat, feat, feat, gcol, arow, h0)


def _merge_out_kernel(of_ref, ob_ref, yf_ref, yb_ref, xs_ref, z_ref, dng_ref, dskip_ref, ssg_ref,
                      w_ref, g_ref, gate_ref, res_ref, g2_ref, shift2_ref, scale2_ref, wr_ref,
                      o_ref, hx_ref, aff_ref):
    o = of_ref[0] + ob_ref[0]
    z = z_ref[0]
    parts = []
    for h in range(DN_HEADS):
        lanes = slice(h * DN_DV, (h + 1) * DN_DV)
        parts.append(_rms(o[:, lanes]) * dng_ref[...] * _silu(z[:, lanes]))
    y = yf_ref[0] + yb_ref[0] + dskip_ref[...] * xs_ref[0]
    yz = y * _silu(z[:, DN_VW:])
    gw = SSM_DI // SSM_G
    for g in range(SSM_G):
        lanes = slice(g * gw, (g + 1) * gw)
        parts.append(_rms(yz[:, lanes]) * ssg_ref[:, lanes])
    mixed = jnp.concatenate(parts, axis=1).astype(BF16)
    m = jnp.dot(mixed, w_ref[...], preferred_element_type=F32)
    x_new = res_ref[0] + gate_ref[0] * (_rms(m) * g_ref[...])
    o_ref[0] = x_new
    for s in range(x_new.shape[0] // MOE_TILE):
        rows = slice(s * MOE_TILE, (s + 1) * MOE_TILE)
        hb, ext, aff = _route(x_new[rows], g2_ref[...], shift2_ref[0], scale2_ref[0], wr_ref[...])
        hx_ref[0, rows, :D_MODEL] = hb
        hx_ref[0, rows, D_MODEL:] = ext
        aff_ref[0, s] = aff


def merge_out(o_f, o_b, y_f, y_b, feat, z, dn_norm_g, dskip, ssm_norm_g, w_out, g, gate, res,
              g_moe, shift_moe, scale_moe, wr_t, tm):
    bsz, t, d = res.shape
    row = lambda b, i: (b, i, 0)
    const = lambda b, i: (0, 0)
    per_b = lambda b, i: (b, 0, 0)
    half = lambda: pl.BlockSpec((1, tm, DN_VW), row)
    tiles = tm // MOE_TILE
    return pl.pallas_call(
        _merge_out_kernel,
        grid=(bsz, t // tm),
        in_specs=[
            half(), half(), half(), half(),
            pl.BlockSpec((1, tm, SSM_DI), lambda b, i: (b, i, OFF_X // SSM_DI)),
            pl.BlockSpec((1, tm, D_Z), row),
            pl.BlockSpec((1, DN_DV), const),
            pl.BlockSpec((1, SSM_DI), const),
            pl.BlockSpec((1, SSM_DI), const),
            pl.BlockSpec((D_MIX, d), const),
            pl.BlockSpec((1, d), const),
            pl.BlockSpec((1, 1, d), per_b),
            pl.BlockSpec((1, tm, d), row),
            pl.BlockSpec((1, d), const),
            pl.BlockSpec((1, 1, d), per_b),
            pl.BlockSpec((1, 1, d), per_b),
            pl.BlockSpec((N_EXPERTS, d), const),
        ],
        out_specs=[
            pl.BlockSpec((1, tm, d), row),
            pl.BlockSpec((1, tm, D_EXT), row),
            pl.BlockSpec((1, tiles, N_EXPERTS, MOE_TILE), lambda b, i: (b, i, 0, 0)),
        ],
        out_shape=[
            jax.ShapeDtypeStruct((bsz, t, d), F32),
            jax.ShapeDtypeStruct((bsz, t, D_EXT), BF16),
            jax.ShapeDtypeStruct((bsz, t // MOE_TILE, N_EXPERTS, MOE_TILE), F32),
        ],
        compiler_params=_compiler_params(("parallel", "parallel")),
        name="merge_out",
    )(o_f, o_b, y_f, y_b, feat, z, dn_norm_g.reshape(1, DN_DV), dskip, ssm_norm_g.reshape(1, SSM_DI),
      w_out, g.reshape(1, d), gate.reshape(bsz, 1, d), res,
      g_moe.reshape(1, d), shift_moe.reshape(bsz, 1, d), scale_moe.reshape(bsz, 1, d), wr_t)


MOE_TILE = 256
SLOT_WINDOW = 64
SLOT_ALIGN = 16
CNT_LANES = 32
GATE_PIECES = 3
D_EXT = D_MODEL + LANES


def _route(x, g, shift, scale, wr_t):
    hb = ((_rms(x) * g) * (1.0 + scale) + shift).astype(BF16)
    logits = lax.dot_general(wr_t, hb, (((1,), (1,)), ((), ())), preferred_element_type=F32)
    ex = jnp.exp(logits - jnp.max(logits, axis=0, keepdims=True))
    aff = ex / jnp.sum(ex, axis=0, keepdims=True)
    pieces, rest = [], aff
    for _ in range(GATE_PIECES):
        p = rest.astype(BF16).astype(F32)
        pieces.append(p)
        rest = rest - p
    rows = jnp.concatenate(pieces + [jnp.zeros((LANES - GATE_PIECES * N_EXPERTS, MOE_TILE), F32)], axis=0)
    ext = jnp.concatenate([rows[:, m * LANES:(m + 1) * LANES].T for m in range(MOE_TILE // LANES)], axis=0)
    return hb, ext.astype(BF16), aff


def _lane_prefix(mask, upper):
    carry = jnp.zeros((mask.shape[0], 1), F32)
    blocks = []
    for m in range(mask.shape[1] // LANES):
        p = jnp.dot(mask[:, m * LANES:(m + 1) * LANES].astype(BF16), upper, preferred_element_type=F32) + carry
        carry = p[:, LANES - 1:LANES]
        blocks.append(p)
    return jnp.concatenate(blocks, axis=1)


def _select_kernel(aff_ref, pos_ref, posc_ref, cnt_ref, *, nt, cap):
    nbs = aff_ref.shape[0]
    n_rows = nbs * N_EXPERTS
    aff = jnp.concatenate([jnp.concatenate([aff_ref[n, j] for j in range(nt)], axis=1) for n in range(nbs)],
                          axis=0)
    bits = pltpu.bitcast(aff, jnp.int32)

    def bisect(i, thr):
        cand = thr | lax.shift_left(jnp.int32(1), jnp.int32(30) - i)
        n_ge = jnp.sum(jnp.where(bits >= cand, 1.0, 0.0), axis=1, keepdims=True)
        return jnp.where(n_ge >= cap, cand, thr)

    thr = lax.fori_loop(0, 31, bisect, jnp.zeros((n_rows, 1), jnp.int32))
    above = bits > thr
    tied = bits == thr
    room = cap - jnp.sum(jnp.where(above, 1.0, 0.0), axis=1, keepdims=True)
    src = lax.broadcasted_iota(jnp.int32, (LANES, LANES), 0)
    dst = lax.broadcasted_iota(jnp.int32, (LANES, LANES), 1)
    upper = jnp.where(src <= dst, 1.0, 0.0).astype(BF16)
    tie_rank = _lane_prefix(jnp.where(tied, 1.0, 0.0), upper)
    sel = above | (tied & (tie_rank <= room))
    selc = jnp.where(sel, 1.0, 0.0)
    slot = jnp.where(sel, _lane_prefix(selc, upper) - 1.0, -1.0)
    lane = lax.broadcasted_iota(jnp.int32, (n_rows, LANES), 1)
    before = jnp.zeros((n_rows, 1), F32)
    counts = jnp.zeros((n_rows, LANES), F32)
    for j in range(nt):
        before = before + jnp.sum(selc[:, j * MOE_TILE:(j + 1) * MOE_TILE], axis=1, keepdims=True)
        counts = jnp.where(lane == j + 1, before, counts)
    counts = counts.astype(jnp.int32)
    pad = jnp.zeros((LANES - N_EXPERTS, LANES), F32)
    for n in range(nbs):
        mine = slot[n * N_EXPERTS:(n + 1) * N_EXPERTS]
        cnt_ref[n] = counts[n * N_EXPERTS:(n + 1) * N_EXPERTS]
        for j in range(nt):
            pos_ref[n, j] = mine[:, j * MOE_TILE:(j + 1) * MOE_TILE]
        for m in range(slot.shape[1] // LANES):
            posc_ref[n, m * LANES:(m + 1) * LANES, :] = jnp.concatenate([mine[:, m * LANES:(m + 1) * LANES], pad], axis=0).T


SELECT_ROWS = 64
SELECT_TOKENS = 4096


def select_slots(aff, cap):
    bsz, nt, ne, tile = aff.shape
    nbs = min(bsz, max(1, (SELECT_ROWS // ne) * SELECT_TOKENS // (nt * tile)))
    blk = lambda b: (b, 0, 0, 0)
    return pl.pallas_call(
        functools.partial(_select_kernel, nt=nt, cap=cap),
        grid=(bsz // nbs,),
        in_specs=[pl.BlockSpec((nbs, nt, ne, tile), blk)],
        out_specs=[pl.BlockSpec((nbs, nt, ne, tile), blk),
                   pl.BlockSpec((nbs, nt * tile, LANES), lambda b: (b, 0, 0)),
                   pl.BlockSpec((nbs, ne, LANES), lambda b: (b, 0, 0))],
        out_shape=[jax.ShapeDtypeStruct((bsz, nt, ne, tile), F32),
                   jax.ShapeDtypeStruct((bsz, nt * tile, LANES), F32),
                   jax.ShapeDtypeStruct((bsz, ne, LANES), jnp.int32)],
        compiler_params=_compiler_params(("parallel",)),
        name="select_slots",
    )(aff)


def _window_start(cnt_ref, row, j):
    return pl.multiple_of((cnt_ref[row, j] // SLOT_ALIGN) * SLOT_ALIGN, SLOT_ALIGN)


def _extra_windows(cnt_ref, row, j, start):
    return (cnt_ref[row, j + 1] - start - 1) // SLOT_WINDOW


def _gather_kernel(cnt_ref, hx_ref, pos_ref, xs_ref):
    b = pl.program_id(0)
    j = pl.program_id(1)

    @pl.when(j == 0)
    def _():
        xs_ref[...] = jnp.zeros_like(xs_ref)

    hx = hx_ref[0]
    slot_iota = lax.broadcasted_iota(jnp.int32, (SLOT_WINDOW, MOE_TILE), 0).astype(F32)

    def one_hot(e, start):
        return jnp.where(pos_ref[0, 0, e:e + 1, :] - start.astype(F32) == slot_iota, 1.0, 0.0).astype(BF16)

    starts = [_window_start(cnt_ref, b * N_EXPERTS + e, j) for e in range(N_EXPERTS)]
    stacked = jnp.concatenate([one_hot(e, starts[e]) for e in range(N_EXPERTS)], axis=0)
    rows = jnp.dot(stacked, hx, preferred_element_type=F32)
    for e in range(N_EXPERTS):
        xs_ref[0, e, pl.ds(starts[e], SLOT_WINDOW), :] += rows[e * SLOT_WINDOW:(e + 1) * SLOT_WINDOW].astype(BF16)

    for e in range(N_EXPERTS):
        n_more = _extra_windows(cnt_ref, b * N_EXPERTS + e, j, starts[e])

        @pl.when(n_more > 0)
        def _(e=e, n_more=n_more):
            def more(i, carry):
                start = pl.multiple_of(starts[e] + i * SLOT_WINDOW, SLOT_ALIGN)
                r = jnp.dot(one_hot(e, start), hx, preferred_element_type=F32)
                xs_ref[0, e, pl.ds(start, SLOT_WINDOW), :] += r.astype(BF16)
                return carry
            lax.fori_loop(1, n_more + 1, more, 0)


def gather_tokens(cnt, hx, pos, cap):
    bsz, t, d_ext = hx.shape
    nt = t // MOE_TILE
    rows = cap + SLOT_WINDOW
    grid_spec = pltpu.PrefetchScalarGridSpec(
        num_scalar_prefetch=1,
        grid=(bsz, nt),
        in_specs=[
            pl.BlockSpec((1, MOE_TILE, d_ext), lambda b, j, c: (b, j, 0)),
            pl.BlockSpec((1, 1, N_EXPERTS, MOE_TILE), lambda b, j, c: (b, j, 0, 0)),
        ],
        out_specs=pl.BlockSpec((1, N_EXPERTS, rows, d_ext), lambda b, j, c: (b, 0, 0, 0)),
    )
    return pl.pallas_call(
        _gather_kernel,
        grid_spec=grid_spec,
        out_shape=jax.ShapeDtypeStruct((bsz, N_EXPERTS, rows, d_ext), BF16),
        compiler_params=_compiler_params(("parallel", "arbitrary")),
        name="gather_tokens",
    )(cnt, hx, pos)


def _expert_ffn_kernel(xs_ref, wg_ref, wu_ref, wd_ref, o_ref):
    e = pl.program_id(0)
    d = wg_ref.shape[1]
    nbs = xs_ref.shape[0]
    cap = xs_ref.shape[2] - SLOT_WINDOW
    xs = jnp.concatenate([xs_ref[n, 0, :cap, :d] for n in range(nbs)], axis=0)
    ext = jnp.concatenate([xs_ref[n, 0, :cap, d:] for n in range(nbs)], axis=0).astype(F32)
    lane = lax.broadcasted_iota(jnp.int32, ext.shape, 1)
    mine = (lane % N_EXPERTS == e) & (lane < GATE_PIECES * N_EXPERTS)
    gate = jnp.sum(jnp.where(mine, ext, 0.0), axis=1, keepdims=True)
    a = jnp.dot(xs, wg_ref[0], preferred_element_type=F32)
    u = jnp.dot(xs, wu_ref[0], preferred_element_type=F32)
    hid = (_silu(a) * u).astype(BF16)
    out = (jnp.dot(hid, wd_ref[0], preferred_element_type=F32) * gate).astype(BF16)
    for n in range(nbs):
        o_ref[n, 0, :cap] = out[n * cap:(n + 1) * cap]
        o_ref[n, 0, cap:] = jnp.zeros((SLOT_WINDOW, d), BF16)


FFN_ROWS = 512


def expert_ffn(xs, wg, wu, wd):
    bsz, ne, rows, d_ext = xs.shape
    _, d, f = wg.shape
    nbs = min(bsz, max(1, FFN_ROWS // (rows - SLOT_WINDOW)))
    return pl.pallas_call(
        _expert_ffn_kernel,
        grid=(ne, bsz // nbs),
        in_specs=[
            pl.BlockSpec((nbs, 1, rows, d_ext), lambda e, b: (b, e, 0, 0)),
            pl.BlockSpec((1, d, f), lambda e, b: (e, 0, 0)),
            pl.BlockSpec((1, d, f), lambda e, b: (e, 0, 0)),
            pl.BlockSpec((1, f, d), lambda e, b: (e, 0, 0)),
        ],
        out_specs=pl.BlockSpec((nbs, 1, rows, d), lambda e, b: (b, e, 0, 0)),
        out_shape=jax.ShapeDtypeStruct((bsz, ne, rows, d), BF16),
        compiler_params=_compiler_params(("parallel", "parallel")),
        name="expert_ffn",
    )(xs, wg, wu, wd)


def _scatter_out_kernel(cnt_ref, og_ref, posc_ref, g_ref, gate_ref, res_ref, o_ref):
    b = pl.program_id(0)
    j = pl.program_id(1)
    posc = posc_ref[0]
    starts = [_window_start(cnt_ref, b * N_EXPERTS + e, j) for e in range(N_EXPERTS)]
    lane = lax.broadcasted_iota(jnp.int32, (MOE_TILE, LANES), 1)
    lane_f = lane.astype(F32)
    per_tile = LANES // SLOT_WINDOW
    tiles = []
    for m in range(N_EXPERTS // per_tile):
        rel = None
        for q in range(per_tile - 1, -1, -1):
            e = m * per_tile + q
            v = posc[:, e:e + 1] - starts[e].astype(F32) + float(q * SLOT_WINDOW)
            rel = v if rel is None else jnp.where(lane < (q + 1) * SLOT_WINDOW, v, rel)
        tiles.append(jnp.where(rel == lane_f, 1.0, 0.0).astype(BF16))
    sel_t = jnp.concatenate(tiles, axis=1)
    og = jnp.concatenate([og_ref[0, e, pl.ds(starts[e], SLOT_WINDOW), :] for e in range(N_EXPERTS)], axis=0)
    o_ref[0] = jnp.dot(sel_t, og, preferred_element_type=F32)

    win_lane = lax.broadcasted_iota(jnp.int32, (MOE_TILE, SLOT_WINDOW), 1).astype(F32)
    for e in range(N_EXPERTS):
        n_more = _extra_windows(cnt_ref, b * N_EXPERTS + e, j, starts[e])

        @pl.when(n_more > 0)
        def _(e=e, n_more=n_more):
            def more(i, carry):
                start = pl.multiple_of(starts[e] + i * SLOT_WINDOW, SLOT_ALIGN)
                p = jnp.where(posc[:, e:e + 1] - start.astype(F32) == win_lane, 1.0, 0.0).astype(BF16)
                o_ref[0] += jnp.dot(p, og_ref[0, e, pl.ds(start, SLOT_WINDOW), :], preferred_element_type=F32)
                return carry
            lax.fori_loop(1, n_more + 1, more, 0)

    o_ref[0] = res_ref[0] + gate_ref[0] * (_rms(o_ref[0]) * g_ref[...])


def scatter_out(cnt, og, posc, g, gate, res):
    bsz, t, d = res.shape
    nt = t // MOE_TILE
    _, ne, rows, _ = og.shape
    grid_spec = pltpu.PrefetchScalarGridSpec(
        num_scalar_prefetch=1,
        grid=(bsz, nt),
        in_specs=[
            pl.BlockSpec((1, ne, rows, d), lambda b, j, c: (b, 0, 0, 0)),
            pl.BlockSpec((1, MOE_TILE, LANES), lambda b, j, c: (b, j, 0)),
            pl.BlockSpec((1, d), lambda b, j, c: (0, 0)),
            pl.BlockSpec((1, 1, d), lambda b, j, c: (b, 0, 0)),
            pl.BlockSpec((1, MOE_TILE, d), lambda b, j, c: (b, j, 0)),
        ],
        out_specs=pl.BlockSpec((1, MOE_TILE, d), lambda b, j, c: (b, j, 0)),
    )
    return pl.pallas_call(
        _scatter_out_kernel,
        grid_spec=grid_spec,
        out_shape=jax.ShapeDtypeStruct((bsz, t, d), F32),
        compiler_params=_compiler_params(("parallel", "parallel")),
        name="scatter_out",
    )(cnt, og, posc, g.reshape(1, d), gate.reshape(bsz, 1, d), res)


def ec_moe_residual(x, hx, aff, wg, wu, wd, g_out, gate):
    bsz, t, _ = x.shape
    cap = EC_CAPACITY_FACTOR * t // N_EXPERTS
    pos, posc, counts = select_slots(aff, cap)
    cnt = counts[:, :, :CNT_LANES].reshape(bsz * N_EXPERTS, CNT_LANES)
    xs = gather_tokens(cnt, hx, pos, cap)
    og = expert_ffn(xs, wg, wu, wd)
    return scatter_out(cnt, og, posc, g_out, gate, x)


def _token_tile(t):
    return min(512, t)


def mixer_stream(x, g0, shift, scale, wc, wz, ws, w9, conv_b, bias_rows, nega_rows, dn_state, ssm_state, on_grid):
    t = x.shape[1]
    tm = _token_tile(t)
    feat, z, gcol, grow, arow = in_proj_conv(x, g0, shift, scale, wc, wz, ws, w9, conv_b, bias_rows, nega_rows,
                                             tm, on_grid)
    ts = min(SCAN_TILE, t)
    o_f, o_b, dn_state = delta_scan(feat, gcol, grow, dn_state, ts, SCAN_SAMPLES_PER_STEP)
    y_f, y_b, ssm_state = ssd_scan(feat, gcol, arow, ssm_state, ts, SCAN_SAMPLES_PER_STEP)
    return (o_f, o_b, y_f, y_b, feat, z), dn_state, ssm_state


def _gate_param_rows(dn_bias, dn_a_log, ssm_bias, ssm_a_log):
    zeros = jnp.zeros((NDH,), F32)
    bias = jnp.concatenate([zeros, dn_bias.reshape(-1), ssm_bias.reshape(-1)])
    nega = jnp.concatenate([zeros, -jnp.exp(dn_a_log.reshape(-1)), -jnp.exp(ssm_a_log.reshape(-1))])
    pad = LANES - bias.shape[0]
    expand = lambda v: jnp.broadcast_to(jnp.pad(v, (0, pad))[:, None], (LANES, LANES))
    return expand(bias), expand(nega)


def kernel(x, c, ctx, c_ctx, ada_w, ada_b, norm_g, w_in, conv_w, conv_b, dn_A_log, dn_dt_bias,
           dn_norm_g, ssm_A_log, ssm_dt_bias, ssm_D, ssm_norm_g, w_out, router_w,
           exp_w_gate, exp_w_up, exp_w_down):
    bsz = x.shape[0]
    s_lat = jax.nn.silu(c)
    s_ctx = jax.nn.silu(c_ctx)
    for l in range(DEPTH):
        last = l == DEPTH - 1
        mod_lat = jnp.split(s_lat @ ada_w[l] + ada_b[l], 6, axis=-1)
        mod_ctx_row = s_ctx @ ada_w[l] + ada_b[l]
        mod_ctx = [jnp.broadcast_to(m[None, :], (bsz, D_MODEL)) for m in jnp.split(mod_ctx_row, 6)]

        w_in_l = w_in[l]
        wc = w_in_l[:, :CONV_CH].astype(BF16)
        wz = w_in_l[:, CONV_CH:CONV_CH + D_Z].astype(BF16)
        ws = jnp.pad(w_in_l[:, CONV_CH + D_Z:], ((0, 0), (0, LANES - N_GATE_COLS))).astype(BF16)
        w9 = conv_w[l].reshape(CONV_CH, CONV_K * CONV_K).T
        cb = conv_b[l].reshape(1, CONV_CH)
        bias_rows, nega_rows = _gate_param_rows(dn_dt_bias[l], dn_A_log[l], ssm_dt_bias[l], ssm_A_log[l])
        dskip = jnp.repeat(ssm_D[l], SSM_P).reshape(1, SSM_DI)
        w_out_l = w_out[l].astype(BF16)
        wr_t = router_w[l].T.astype(BF16)
        moe_w = (exp_w_gate[l].astype(BF16), exp_w_up[l].astype(BF16), exp_w_down[l].astype(BF16), norm_g[l, 3])

        dn0 = jnp.zeros((bsz, 2, DN_HEADS, DN_DK, DN_DV), F32)
        ssm0 = jnp.zeros((bsz, 2, SSM_G, SSM_N, SSM_GW), F32)
        shared = (wc, wz, ws, w9, cb, bias_rows, nega_rows)
        mix_ctx, dn_c, ssm_c = mixer_stream(ctx, norm_g[l, 0], mod_ctx[0], mod_ctx[1], *shared, dn0, ssm0, False)
        mix_lat, _, _ = mixer_stream(x, norm_g[l, 0], mod_lat[0], mod_lat[1], *shared, dn_c, ssm_c, True)
        merge_w = (dn_norm_g[l], dskip, ssm_norm_g[l], w_out_l, norm_g[l, 1])
        x, hx, aff = merge_out(*mix_lat, *merge_w, mod_lat[2], x, norm_g[l, 2], mod_lat[3], mod_lat[4], wr_t,
                               _token_tile(x.shape[1]))
        x = ec_moe_residual(x, hx, aff, *moe_w, mod_lat[5])

        if not last:
            ctx, hx, aff = merge_out(*mix_ctx, *merge_w, mod_ctx[2], ctx, norm_g[l, 2], mod_ctx[3], mod_ctx[4], wr_t,
                                     _token_tile(ctx.shape[1]))
            ctx = ec_moe_residual(ctx, hx, aff, *moe_w, mod_ctx[5])
    return x
```

```python
import functools

import jax
import jax.numpy as jnp
import numpy as np
from jax import lax
from jax.experimental import pallas as pl
from jax.experimental.pallas import tpu as pltpu

D_MODEL = 1024
DEPTH = 4
GRID_W = 64
DN_HEADS = 4
DN_DK = 128
DN_DV = 128
SSM_HEADS = 8
SSM_P = 64
SSM_N = 128
SSM_G = 2
CHUNK = 64
CONV_K = 3
N_EXPERTS = 16
EC_CAPACITY_FACTOR = 2
D_EXPERT = 512
EPS = 1e-6

DN_QK = DN_HEADS * DN_DK
DN_VW = DN_HEADS * DN_DV
SSM_DI = SSM_HEADS * SSM_P
SSM_BC = SSM_G * SSM_N
SSM_HPG = SSM_HEADS // SSM_G
SSM_GW = SSM_HPG * SSM_P
D_MIX = DN_VW + SSM_DI
CONV_SPLITS = (DN_QK, DN_QK, DN_VW, SSM_DI, SSM_BC, SSM_BC)
CONV_CH = sum(CONV_SPLITS)
D_Z = DN_VW + SSM_DI
N_GATE_COLS = 2 * DN_HEADS + 2 * DN_HEADS + 2 * SSM_HEADS
D_IN_PROJ = CONV_CH + D_Z + N_GATE_COLS

LANES = 128
SUBLANES = 8
VMEM_LIMIT_BYTES = 56 * 1024 * 1024

OFF_Q, OFF_K, OFF_V = 0, DN_QK, 2 * DN_QK
OFF_X = 2 * DN_QK + DN_VW
OFF_B = OFF_X + SSM_DI
OFF_C = OFF_B + SSM_BC

NDH = 2 * DN_HEADS
NSH = 2 * SSM_HEADS
COL_BETA, COL_G, COL_EG, COL_EGL, COL_EGT = (i * NDH for i in range(5))
COL_DT, COL_A, COL_EA, COL_DTEAL, COL_EAT = (5 * NDH + i * NSH for i in range(5))

F32 = jnp.float32
BF16 = jnp.bfloat16


def _compiler_params(semantics):
    return pltpu.CompilerParams(dimension_semantics=semantics, vmem_limit_bytes=VMEM_LIMIT_BYTES)


def _rms(x):
    return x * lax.rsqrt(jnp.mean(x * x, axis=-1, keepdims=True) + EPS)


def _silu(x):
    return x * jax.nn.sigmoid(x)


def _softplus(x):
    return jnp.maximum(x, 0.0) + jnp.log(1.0 + jnp.exp(-jnp.abs(x)))


def _dot(a, b):
    return jnp.dot(a.astype(BF16), b.astype(BF16), preferred_element_type=F32)


def _dot_nt(a, b):
    return lax.dot_general(a.astype(BF16), b.astype(BF16), (((1,), (1,)), ((), ())),
                           preferred_element_type=F32)


def _dot_tn(a, b):
    return lax.dot_general(a.astype(BF16), b.astype(BF16), (((0,), (0,)), ((), ())),
                           preferred_element_type=F32)


def _dot_exact01(a, m01):
    a1 = a.astype(BF16)
    r1 = a - a1.astype(F32)
    a2 = r1.astype(BF16)
    a3 = (r1 - a2.astype(F32)).astype(BF16)
    m = m01.astype(BF16)
    out = jnp.dot(a3, m, preferred_element_type=F32)
    out = out + jnp.dot(a2, m, preferred_element_type=F32)
    return out + jnp.dot(a1, m, preferred_element_type=F32)


SSD_ROWS = 2 * 2 * SSM_G
HALO_BLOCK = 128
HALO = GRID_W + 16
CONV_LANES = 256


def _in_proj_conv_kernel(xp_ref, x_ref, xn_ref, g_ref, shift_ref, scale_ref, wc_ref, wz_ref, ws_ref, w9_ref, cb_ref,
                         gbias_ref, gnega_ref, of_ref, oz_ref, col_ref, grow_ref, arow_ref, *, on_grid):
    i = pl.program_id(1)
    n_i = pl.num_programs(1)
    tm = x_ref.shape[1]

    def modulated(x):
        h = _rms(x) * g_ref[...]
        return h * (1.0 + scale_ref[0]) + shift_ref[0]

    has_prev = jnp.where(i > 0, 1.0, 0.0)
    has_next = jnp.where(i < n_i - 1, 1.0, 0.0)
    h_main = modulated(x_ref[0]).astype(BF16)
    h_prev = (modulated(xp_ref[0, HALO_BLOCK - HALO:, :]) * has_prev).astype(BF16)
    h_next = (modulated(xn_ref[0, :HALO, :]) * has_next).astype(BF16)
    h_all = jnp.concatenate([h_prev, h_main, h_next], axis=0)

    dys = (-1, 0, 1) if on_grid else (0,)
    col = lax.broadcasted_iota(jnp.int32, (tm, CONV_LANES), 0) % GRID_W
    n_blocks = CONV_CH // CONV_LANES

    def project(c):
        return jnp.dot(h_all, wc_ref[:, c * CONV_LANES:(c + 1) * CONV_LANES], preferred_element_type=F32)

    def conv(c, p):
        lanes = slice(c * CONV_LANES, (c + 1) * CONV_LANES)
        acc_c = acc_m = acc_p = None
        for dy in dys:
            lo = HALO + dy * GRID_W
            win = p[lo - SUBLANES:lo + tm + SUBLANES]
            wr = 3 * (dy + 1)
            tc = win[SUBLANES:SUBLANES + tm] * w9_ref[wr + 1:wr + 2, lanes]
            tl, tr = win * w9_ref[wr:wr + 1, lanes], win * w9_ref[wr + 2:wr + 3, lanes]
            acc_c = tc if acc_c is None else acc_c + tc
            acc_m = tl if acc_m is None else acc_m + tl
            acc_p = tr if acc_p is None else acc_p + tr
        acc_m = pltpu.roll(acc_m, 1, 0)[SUBLANES:SUBLANES + tm]
        acc_p = pltpu.roll(acc_p, tm + 2 * SUBLANES - 1, 0)[SUBLANES:SUBLANES + tm]
        if on_grid:
            acc_m = jnp.where(col != 0, acc_m, 0.0)
            acc_p = jnp.where(col != GRID_W - 1, acc_p, 0.0)
        u = _silu(acc_c + acc_m + acc_p + cb_ref[:, lanes])
        if c * CONV_LANES < 2 * DN_QK:
            scale = DN_DK ** -0.5 if c * CONV_LANES < DN_QK else 1.0
            heads = []
            for hh in range(CONV_LANES // DN_DK):
                uh = u[:, hh * DN_DK:(hh + 1) * DN_DK]
                heads.append(uh * (lax.rsqrt(jnp.sum(uh * uh, axis=-1, keepdims=True) + EPS) * scale))
            u = jnp.concatenate(heads, axis=1)
        of_ref[0, :, lanes] = u

    z_blocks = D_Z // CONV_LANES
    p = project(0)
    for c in range(n_blocks):
        p_next = project(c + 1) if c + 1 < n_blocks else None
        if c < z_blocks:
            lanes = slice(c * CONV_LANES, (c + 1) * CONV_LANES)
            oz_ref[0, :, lanes] = jnp.dot(h_main, wz_ref[:, lanes], preferred_element_type=F32)
        elif c == z_blocks:
            _write_gates(jnp.dot(h_main, ws_ref[...], preferred_element_type=F32), gbias_ref[...], gnega_ref[...],
                         col_ref.at[0], grow_ref.at[0], arow_ref.at[0])
        conv(c, p)
        p = p_next


def in_proj_conv(x, g, shift, scale, wc, wz, ws, w9, conv_b, bias_rows, nega_rows, tm, on_grid):
    bsz, t, d = x.shape
    nc = t // CHUNK
    ncg = tm // CHUNK
    per_tile = tm // HALO_BLOCK
    last_halo = t // HALO_BLOCK - 1
    row = lambda b, i: (b, i, 0)
    const = lambda b, i: (0, 0)
    per_b = lambda b, i: (b, 0, 0)
    return pl.pallas_call(
        functools.partial(_in_proj_conv_kernel, on_grid=on_grid),
        grid=(bsz, t // tm),
        in_specs=[
            pl.BlockSpec((1, HALO_BLOCK, d), lambda b, i: (b, jnp.maximum(i * per_tile - 1, 0), 0)),
            pl.BlockSpec((1, tm, d), row),
            pl.BlockSpec((1, HALO_BLOCK, d), lambda b, i: (b, jnp.minimum((i + 1) * per_tile, last_halo), 0)),
            pl.BlockSpec((1, d), const),
            pl.BlockSpec((1, 1, d), per_b),
            pl.BlockSpec((1, 1, d), per_b),
            pl.BlockSpec((d, CONV_CH), const),
            pl.BlockSpec((d, D_Z), const),
            pl.BlockSpec((d, LANES), const),
            pl.BlockSpec((9, CONV_CH), const),
            pl.BlockSpec((1, CONV_CH), const),
            pl.BlockSpec((LANES, LANES), const),
            pl.BlockSpec((LANES, LANES), const),
        ],
        out_specs=[
            pl.BlockSpec((1, tm, CONV_CH), row),
            pl.BlockSpec((1, tm, D_Z), row),
            pl.BlockSpec((1, tm, LANES), row),
            pl.BlockSpec((1, ncg, NDH, CHUNK), lambda b, i: (b, i, 0, 0)),
            pl.BlockSpec((1, ncg, SSD_ROWS, SSM_GW), lambda b, i: (b, i, 0, 0)),
        ],
        out_shape=[
            jax.ShapeDtypeStruct((bsz, t, CONV_CH), F32),
            jax.ShapeDtypeStruct((bsz, t, D_Z), F32),
            jax.ShapeDtypeStruct((bsz, t, LANES), F32),
            jax.ShapeDtypeStruct((bsz, nc, NDH, CHUNK), F32),
            jax.ShapeDtypeStruct((bsz, nc, SSD_ROWS, SSM_GW), F32),
        ],
        compiler_params=_compiler_params(("parallel", "parallel")),
        name="in_proj_conv",
    )(x, x, x, g.reshape(1, d), shift.reshape(bsz, 1, d), scale.reshape(bsz, 1, d), wc, wz, ws, w9, conv_b,
      bias_rows, nega_rows)


def _write_gates(small, bias, nega, col_ref, grow_ref, arow_ref):
    tok = lax.broadcasted_iota(jnp.int32, (LANES, LANES), 0)
    out = lax.broadcasted_iota(jnp.int32, (LANES, LANES), 1)
    same = (tok // CHUNK) == (out // CHUNK)
    m_fwd = jnp.where(same & (tok <= out), 1.0, 0.0)
    m_bwd = jnp.where(same & (tok >= out), 1.0, 0.0)
    m_all = jnp.where(same, 1.0, 0.0)

    def dir_cumsum(v, heads):
        f = _dot_exact01(v, m_fwd)
        b = _dot_exact01(v, m_bwd)
        is_fwd = lax.broadcasted_iota(jnp.int32, v.shape, 0) < heads
        return jnp.where(is_fwd, f, b), _dot_exact01(v, m_all)

    for s in range(small.shape[0] // LANES):
        st = small[s * LANES:(s + 1) * LANES, :].T
        beta = jax.nn.sigmoid(st[0:NDH])
        sp = _softplus(st[NDH:2 * NDH + NSH] + bias[NDH:2 * NDH + NSH])
        logg = sp[0:NDH] * nega[NDH:2 * NDH]
        dt = sp[NDH:]
        a = dt * nega[2 * NDH:2 * NDH + NSH]
        g_cs, g_tot = dir_cumsum(logg, DN_HEADS)
        a_cs, a_tot = dir_cumsum(a, SSM_HEADS)
        rows = jnp.concatenate([
            beta, g_cs, jnp.exp(g_cs), jnp.exp(g_tot - g_cs), jnp.exp(g_tot),
            dt, a_cs, jnp.exp(a_cs), dt * jnp.exp(a_tot - a_cs), jnp.exp(a_tot),
            jnp.zeros((LANES - 5 * NDH - 5 * NSH, LANES), F32)], axis=0)
        col_ref[s * LANES:(s + 1) * LANES, :] = rows.T
        for half in range(LANES // CHUNK):
            c = s * (LANES // CHUNK) + half
            lo = half * CHUNK
            grow_ref[c] = g_cs[:, lo:lo + CHUNK]
            for q, v in enumerate((a_cs, dt)):
                for dg in range(2 * SSM_G):
                    r0 = dg * SSM_HPG
                    r = q * 2 * SSM_G + dg
                    arow_ref[c, r:r + 1, :] = jnp.concatenate(
                        [v[r0 + hh:r0 + hh + 1, lo:lo + CHUNK] for hh in range(SSM_HPG)], axis=1)


SCAN_SAMPLES_PER_STEP = 4
SCAN_TILE = 256


def _chunk_masks(rev):
    r = lax.broadcasted_iota(jnp.int32, (CHUNK, CHUNK), 0)
    c = lax.broadcasted_iota(jnp.int32, (CHUNK, CHUNK), 1)
    return ((r <= c), (r < c)) if rev else ((r >= c), (r > c))


def _unit_tri_inverse(ms):
    eye = (lax.broadcasted_iota(jnp.int32, (CHUNK, CHUNK), 0)
           == lax.broadcasted_iota(jnp.int32, (CHUNK, CHUNK), 1)).astype(F32)
    ps = [_dot(m, m) for m in ms]
    ts = [eye + m for m in ms]
    n_sq = int(np.log2(CHUNK)) - 1
    for _ in range(n_sq - 1):
        rs = [_dot(jnp.concatenate([t, p], axis=0), p) for t, p in zip(ts, ps)]
        ts = [t + r[:CHUNK] for t, r in zip(ts, rs)]
        ps = [r[CHUNK:] for r in rs]
    return [t + _dot(t, p) for t, p in zip(ts, ps)]


def _dn_kernel(qf_ref, kf_ref, vf_ref, cf_ref, rf_ref, qb_ref, kb_ref, vb_ref, cb_ref, rb_ref, s0_ref,
               of_ref, ob_ref, st_ref, s_scr, *, nct, nb):
    j = pl.program_id(1)

    @pl.when(j == 0)
    def _():
        s_scr[...] = s0_ref[...]

    masks = (_chunk_masks(False), _chunk_masks(True))
    dir_refs = ((qf_ref, kf_ref, vf_ref, cf_ref, rf_ref, of_ref),
                (qb_ref, kb_ref, vb_ref, cb_ref, rb_ref, ob_ref))

    def body(c, carry):
        chains = []
        for n in range(nb):
            for d in range(2):
                cc = c if d == 0 else nct - 1 - c
                r0 = pl.multiple_of(cc * CHUNK, CHUNK)
                ct = dir_refs[d][3][n, pl.ds(r0, CHUNK), :]
                for h in range(DN_HEADS):
                    chains.append((d, h, cc, r0, ct, n))

        def load(chain, which):
            d, h, _, r0, _, n = chain
            return dir_refs[d][which][n, pl.ds(r0, CHUNK), h * DN_DK:(h + 1) * DN_DK]

        def col(chain, off, rows=slice(None)):
            d, h, _, _, ct, _ = chain
            lane = off + d * DN_HEADS + h
            return ct[rows, lane:lane + 1]

        def decay_of(chain):
            d, h, cc, _, _, n = chain
            incl = masks[d][0]
            g_row = dir_refs[d][4][n, cc, d * DN_HEADS + h:d * DN_HEADS + h + 1, :]
            return jnp.where(incl, jnp.exp(jnp.where(incl, col(chain, COL_G) - g_row, 0.0)), 0.0)

        kbeta = [load(ch, 1) * col(ch, COL_BETA) for ch in chains]
        a = [_dot_nt(jnp.concatenate([kb, load(ch, 0)], axis=0), load(ch, 1)) for ch, kb in zip(chains, kbeta)]
        decay = [decay_of(ch) for ch in chains]
        neg_l = [jnp.where(masks[ch[0]][1], -(ai[:CHUNK] * dc), 0.0) for ch, ai, dc in zip(chains, a, decay)]
        qk = [ai[CHUNK:] * dc for ai, dc in zip(a, decay)]
        t_inv = _unit_tri_inverse(neg_l)
        uw = [_dot(ti, jnp.concatenate([load(ch, 2) * col(ch, COL_BETA), kb * col(ch, COL_EG)], axis=1))
              for ch, ti, kb in zip(chains, t_inv, kbeta)]
        s = [s_scr[ch[5], ch[0], ch[1]] for ch in chains]
        ws_qs = [_dot(jnp.concatenate([uwi[:, DN_DV:], load(ch, 0) * col(ch, COL_EG)], axis=0), si)
                 for ch, uwi, si in zip(chains, uw, s)]
        v_new = [uwi[:, :DN_DV] - wq[:CHUNK] for uwi, wq in zip(uw, ws_qs)]
        o = [wq[CHUNK:] + _dot(qki, vn) for wq, qki, vn in zip(ws_qs, qk, v_new)]
        s_new = [si * col(ch, COL_EGT, slice(0, 1)) + _dot_tn(load(ch, 1) * col(ch, COL_EGL), vn)
                 for ch, si, vn in zip(chains, s, v_new)]
        for ch, oi, sn in zip(chains, o, s_new):
            d, h, _, r0, _, n = ch
            dir_refs[d][5][n, pl.ds(r0, CHUNK), h * DN_DV:(h + 1) * DN_DV] = oi
            s_scr[n, d, h] = sn
        return carry

    lax.fori_loop(0, nct, body, 0)

    @pl.when(j == pl.num_programs(1) - 1)
    def _():
        st_ref[...] = s_scr[...]


def delta_scan(feat, gcol, grow, s0, tt, nb):
    bsz, t, _ = feat.shape
    nt = t // tt
    nct = tt // CHUNK
    fwd = lambda lane_blk: (lambda b, j: (b, j, lane_blk))
    bwd = lambda lane_blk: (lambda b, j: (b, nt - 1 - j, lane_blk))
    qkv_blk = (nb, tt, DN_QK)

    def side(im):
        return [pl.BlockSpec(qkv_blk, im(OFF_Q // DN_QK)), pl.BlockSpec(qkv_blk, im(OFF_K // DN_QK)),
                pl.BlockSpec(qkv_blk, im(OFF_V // DN_QK)), pl.BlockSpec((nb, tt, LANES), im(0))]

    row_spec = lambda rev: pl.BlockSpec((nb, nct, NDH, CHUNK),
                                        (lambda b, j: (b, nt - 1 - j, 0, 0)) if rev else (lambda b, j: (b, j, 0, 0)))
    state_spec = pl.BlockSpec((nb, 2, DN_HEADS, DN_DK, DN_DV), lambda b, j: (b, 0, 0, 0, 0))
    return pl.pallas_call(
        functools.partial(_dn_kernel, nct=nct, nb=nb),
        grid=(bsz // nb, nt),
        in_specs=side(fwd) + [row_spec(False)] + side(bwd) + [row_spec(True)] + [state_spec],
        out_specs=[pl.BlockSpec((nb, tt, DN_VW), fwd(0)), pl.BlockSpec((nb, tt, DN_VW), bwd(0)), state_spec],
        out_shape=[
            jax.ShapeDtypeStruct((bsz, t, DN_VW), F32),
            jax.ShapeDtypeStruct((bsz, t, DN_VW), F32),
            jax.ShapeDtypeStruct((bsz, 2, DN_HEADS, DN_DK, DN_DV), F32),
        ],
        scratch_shapes=[pltpu.VMEM((nb, 2, DN_HEADS, DN_DK, DN_DV), F32)],
        compiler_params=_compiler_params(("parallel", "arbitrary")),
        name="delta_scan",
    )(feat, feat, feat, gcol, grow, feat, feat, feat, gcol, grow, s0)


def _group_lane_select(cols):
    shape = (cols[0].shape[0], SSM_GW)
    head = lax.broadcasted_iota(jnp.int32, shape, 1) // SSM_P
    out = jnp.broadcast_to(cols[SSM_HPG - 1], shape)
    for hh in range(SSM_HPG - 2, -1, -1):
        out = jnp.where(head == hh, jnp.broadcast_to(cols[hh], shape), out)
    return out


def _ssd_kernel(xf_ref, bf_ref, cf_ref, gf_ref, rf_ref, xb_ref, bb_ref, cb_ref, gb_ref, rb_ref, h0_ref,
                yf_ref, yb_ref, ht_ref, h_scr, *, nct, nb):
    j = pl.program_id(1)

    @pl.when(j == 0)
    def _():
        h_scr[...] = h0_ref[...]

    row_i = lax.broadcasted_iota(jnp.int32, (CHUNK, SSM_GW), 0)
    col_j = lax.broadcasted_iota(jnp.int32, (CHUNK, SSM_GW), 1) % SSM_P
    incl_dir = (row_i >= col_j, row_i <= col_j)
    bd_rows = lax.broadcasted_iota(jnp.int32, (SSM_HPG * CHUNK, SSM_GW), 0) // CHUNK
    bd_cols = lax.broadcasted_iota(jnp.int32, (SSM_HPG * CHUNK, SSM_GW), 1) // SSM_P
    block_diag = bd_rows == bd_cols
    dir_refs = ((xf_ref, bf_ref, cf_ref, gf_ref, rf_ref, yf_ref),
                (xb_ref, bb_ref, cb_ref, gb_ref, rb_ref, yb_ref))

    def body(c, carry):
        chains = []
        for n in range(nb):
            for d in range(2):
                cc = c if d == 0 else nct - 1 - c
                r0 = pl.multiple_of(cc * CHUNK, CHUNK)
                ct = dir_refs[d][3][n, pl.ds(r0, CHUNK), :]
                for g in range(SSM_G):
                    chains.append((d, g, cc, r0, ct, n))

        def load(chain, which, width):
            d, g, _, r0, _, n = chain
            return dir_refs[d][which][n, pl.ds(r0, CHUNK), g * width:(g + 1) * width]

        def pick(chain, off, rows=slice(None)):
            d, g, _, _, ct, _ = chain
            h0 = off + d * SSM_HEADS + g * SSM_HPG
            return _group_lane_select([ct[rows, h0 + hh:h0 + hh + 1] for hh in range(SSM_HPG)])

        def decay_dt_of(chain):
            d, g, cc, _, _, n = chain
            incl = incl_dir[d]
            dg = d * SSM_G + g
            a_row = dir_refs[d][4][n, cc, dg:dg + 1, :]
            dt_row = dir_refs[d][4][n, cc, 2 * SSM_G + dg:2 * SSM_G + dg + 1, :]
            return jnp.where(incl, jnp.exp(jnp.where(incl, pick(chain, COL_A) - a_row, 0.0)) * dt_row, 0.0)

        cb4 = [_dot_nt(load(ch, 2, SSM_N), jnp.concatenate([load(ch, 1, SSM_N)] * SSM_HPG, axis=0)) for ch in chains]
        hg = [h_scr[ch[5], ch[0], ch[1]] for ch in chains]
        y_off = [_dot(load(ch, 2, SSM_N), h) for ch, h in zip(chains, hg)]
        h_in = [_dot_tn(load(ch, 1, SSM_N), load(ch, 0, SSM_GW) * pick(ch, COL_DTEAL)) for ch in chains]
        scores = [cb * decay_dt_of(ch) for ch, cb in zip(chains, cb4)]
        x_bd = [jnp.where(block_diag, jnp.concatenate([load(ch, 0, SSM_GW)] * SSM_HPG, axis=0), 0.0)
                for ch in chains]
        y_diag = [_dot(sc, xb) for sc, xb in zip(scores, x_bd)]
        for ch, yd, yo, h, hi in zip(chains, y_diag, y_off, hg, h_in):
            d, g, _, r0, _, n = ch
            dir_refs[d][5][n, pl.ds(r0, CHUNK), g * SSM_GW:(g + 1) * SSM_GW] = yd + yo * pick(ch, COL_EA)
            h_scr[n, d, g] = h * pick(ch, COL_EAT, slice(0, 1)) + hi
        return carry

    lax.fori_loop(0, nct, body, 0)

    @pl.when(j == pl.num_programs(1) - 1)
    def _():
        ht_ref[...] = h_scr[...]


def ssd_scan(feat, gcol, arow, h0, tt, nb):
    bsz, t, _ = feat.shape
    nt = t // tt
    nct = tt // CHUNK
    fwd = lambda lane_blk: (lambda b, j: (b, j, lane_blk))
    bwd = lambda lane_blk: (lambda b, j: (b, nt - 1 - j, lane_blk))

    def side(im):
        return [pl.BlockSpec((nb, tt, SSM_DI), im(OFF_X // SSM_DI)),
                pl.BlockSpec((nb, tt, SSM_BC), im(OFF_B // SSM_BC)),
                pl.BlockSpec((nb, tt, SSM_BC), im(OFF_C // SSM_BC)),
                pl.BlockSpec((nb, tt, LANES), im(0))]

    row_spec = lambda rev: pl.BlockSpec((nb, nct, SSD_ROWS, SSM_GW),
                                        (lambda b, j: (b, nt - 1 - j, 0, 0)) if rev else (lambda b, j: (b, j, 0, 0)))
    state_spec = pl.BlockSpec((nb, 2, SSM_G, SSM_N, SSM_GW), lambda b, j: (b, 0, 0, 0, 0))
    return pl.pallas_call(
        functools.partial(_ssd_kernel, nct=nct, nb=nb),
        grid=(bsz // nb, nt),
        in_specs=side(fwd) + [row_spec(False)] + side(bwd) + [row_spec(True)] + [state_spec],
        out_specs=[pl.BlockSpec((nb, tt, SSM_DI), fwd(0)), pl.BlockSpec((nb, tt, SSM_DI), bwd(0)), state_spec],
        out_shape=[
            jax.ShapeDtypeStruct((bsz, t, SSM_DI), F32),
            jax.ShapeDtypeStruct((bsz, t, SSM_DI), F32),
            jax.ShapeDtypeStruct((bsz, 2, SSM_G, SSM_N, SSM_GW), F32),
        ],
        scratch_shapes=[pltpu.VMEM((nb, 2, SSM_G, SSM_N, SSM_GW), F32)],
        compiler_params=_compiler_params(("parallel", "arbitrary")),
        name="ssd_scan",
    )(feat, feat, feat, gcol, arow, feat, feat, feat, gcol, arow, h0)


def _merge_out_kernel(of_ref, ob_ref, yf_ref, yb_ref, xs_ref, z_ref, dng_ref, dskip_ref, ssg_ref,
                      w_ref, g_ref, gate_ref, res_ref, g2_ref, shift2_ref, scale2_ref, wr_ref,
                      o_ref, hx_ref, aff_ref):
    o = of_ref[0] + ob_ref[0]
    z = z_ref[0]
    parts = []
    for h in range(DN_HEADS):
        lanes = slice(h * DN_DV, (h + 1) * DN_DV)
        parts.append(_rms(o[:, lanes]) * dng_ref[...] * _silu(z[:, lanes]))
    y = yf_ref[0] + yb_ref[0] + dskip_ref[...] * xs_ref[0]
    yz = y * _silu(z[:, DN_VW:])
    gw = SSM_DI // SSM_G
    for g in range(SSM_G):
        lanes = slice(g * gw, (g + 1) * gw)
        parts.append(_rms(yz[:, lanes]) * ssg_ref[:, lanes])
    mixed = jnp.concatenate(parts, axis=1).astype(BF16)
    m = jnp.dot(mixed, w_ref[...], preferred_element_type=F32)
    x_new = res_ref[0] + gate_ref[0] * (_rms(m) * g_ref[...])
    o_ref[0] = x_new
    for s in range(x_new.shape[0] // MOE_TILE):
        rows = slice(s * MOE_TILE, (s + 1) * MOE_TILE)
        hb, ext, aff = _route(x_new[rows], g2_ref[...], shift2_ref[0], scale2_ref[0], wr_ref[...])
        hx_ref[0, rows, :D_MODEL] = hb
        hx_ref[0, rows, D_MODEL:] = ext
        aff_ref[0, s] = aff


def merge_out(o_f, o_b, y_f, y_b, feat, z, dn_norm_g, dskip, ssm_norm_g, w_out, g, gate, res,
              g_moe, shift_moe, scale_moe, wr_t, tm):
    bsz, t, d = res.shape
    row = lambda b, i: (b, i, 0)
    const = lambda b, i: (0, 0)
    per_b = lambda b, i: (b, 0, 0)
    half = lambda: pl.BlockSpec((1, tm, DN_VW), row)
    tiles = tm // MOE_TILE
    return pl.pallas_call(
        _merge_out_kernel,
        grid=(bsz, t // tm),
        in_specs=[
            half(), half(), half(), half(),
            pl.BlockSpec((1, tm, SSM_DI), lambda b, i: (b, i, OFF_X // SSM_DI)),
            pl.BlockSpec((1, tm, D_Z), row),
            pl.BlockSpec((1, DN_DV), const),
            pl.BlockSpec((1, SSM_DI), const),
            pl.BlockSpec((1, SSM_DI), const),
            pl.BlockSpec((D_MIX, d), const),
            pl.BlockSpec((1, d), const),
            pl.BlockSpec((1, 1, d), per_b),
            pl.BlockSpec((1, tm, d), row),
            pl.BlockSpec((1, d), const),
            pl.BlockSpec((1, 1, d), per_b),
            pl.BlockSpec((1, 1, d), per_b),
            pl.BlockSpec((N_EXPERTS, d), const),
        ],
        out_specs=[
            pl.BlockSpec((1, tm, d), row),
            pl.BlockSpec((1, tm, D_EXT), row),
            pl.BlockSpec((1, tiles, N_EXPERTS, MOE_TILE), lambda b, i: (b, i, 0, 0)),
        ],
        out_shape=[
            jax.ShapeDtypeStruct((bsz, t, d), F32),
            jax.ShapeDtypeStruct((bsz, t, D_EXT), BF16),
            jax.ShapeDtypeStruct((bsz, t // MOE_TILE, N_EXPERTS, MOE_TILE), F32),
        ],
        compiler_params=_compiler_params(("parallel", "parallel")),
        name="merge_out",
    )(o_f, o_b, y_f, y_b, feat, z, dn_norm_g.reshape(1, DN_DV), dskip, ssm_norm_g.reshape(1, SSM_DI),
      w_out, g.reshape(1, d), gate.reshape(bsz, 1, d), res,
      g_moe.reshape(1, d), shift_moe.reshape(bsz, 1, d), scale_moe.reshape(bsz, 1, d), wr_t)


MOE_TILE = 256
SLOT_WINDOW = 64
SLOT_ALIGN = 16
CNT_LANES = 32
GATE_PIECES = 3
D_EXT = D_MODEL + LANES


def _route(x, g, shift, scale, wr_t):
    hb = ((_rms(x) * g) * (1.0 + scale) + shift).astype(BF16)
    logits = lax.dot_general(wr_t, hb, (((1,), (1,)), ((), ())), preferred_element_type=F32)
    ex = jnp.exp(logits - jnp.max(logits, axis=0, keepdims=True))
    aff = ex / jnp.sum(ex, axis=0, keepdims=True)
    pieces, rest = [], aff
    for _ in range(GATE_PIECES):
        p = rest.astype(BF16).astype(F32)
        pieces.append(p)
        rest = rest - p
    rows = jnp.concatenate(pieces + [jnp.zeros((LANES - GATE_PIECES * N_EXPERTS, MOE_TILE), F32)], axis=0)
    ext = jnp.concatenate([rows[:, m * LANES:(m + 1) * LANES].T for m in range(MOE_TILE // LANES)], axis=0)
    return hb, ext.astype(BF16), aff


def _lane_prefix(mask, upper):
    carry = jnp.zeros((mask.shape[0], 1), F32)
    blocks = []
    for m in range(mask.shape[1] // LANES):
        p = jnp.dot(mask[:, m * LANES:(m + 1) * LANES].astype(BF16), upper, preferred_element_type=F32) + carry
        carry = p[:, LANES - 1:LANES]
        blocks.append(p)
    return jnp.concatenate(blocks, axis=1)


def _select_kernel(aff_ref, pos_ref, posc_ref, cnt_ref, *, nt, cap):
    nbs = aff_ref.shape[0]
    n_rows = nbs * N_EXPERTS
    aff = jnp.concatenate([jnp.concatenate([aff_ref[n, j] for j in range(nt)], axis=1) for n in range(nbs)],
                          axis=0)
    bits = pltpu.bitcast(aff, jnp.int32)

    def bisect(i, thr):
        cand = thr | lax.shift_left(jnp.int32(1), jnp.int32(30) - i)
        n_ge = jnp.sum(jnp.where(bits >= cand, 1.0, 0.0), axis=1, keepdims=True)
        return jnp.where(n_ge >= cap, cand, thr)

    thr = lax.fori_loop(0, 31, bisect, jnp.zeros((n_rows, 1), jnp.int32))
    above = bits > thr
    tied = bits == thr
    room = cap - jnp.sum(jnp.where(above, 1.0, 0.0), axis=1, keepdims=True)
    src = lax.broadcasted_iota(jnp.int32, (LANES, LANES), 0)
    dst = lax.broadcasted_iota(jnp.int32, (LANES, LANES), 1)
    upper = jnp.where(src <= dst, 1.0, 0.0).astype(BF16)
    tie_rank = _lane_prefix(jnp.where(tied, 1.0, 0.0), upper)
    sel = above | (tied & (tie_rank <= room))
    selc = jnp.where(sel, 1.0, 0.0)
    slot = jnp.where(sel, _lane_prefix(selc, upper) - 1.0, -1.0)
    lane = lax.broadcasted_iota(jnp.int32, (n_rows, LANES), 1)
    before = jnp.zeros((n_rows, 1), F32)
    counts = jnp.zeros((n_rows, LANES), F32)
    for j in range(nt):
        before = before + jnp.sum(selc[:, j * MOE_TILE:(j + 1) * MOE_TILE], axis=1, keepdims=True)
        counts = jnp.where(lane == j + 1, before, counts)
    counts = counts.astype(jnp.int32)
    pad = jnp.zeros((LANES - N_EXPERTS, LANES), F32)
    for n in range(nbs):
        mine = slot[n * N_EXPERTS:(n + 1) * N_EXPERTS]
        cnt_ref[n] = counts[n * N_EXPERTS:(n + 1) * N_EXPERTS]
        for j in range(nt):
            pos_ref[n, j] = mine[:, j * MOE_TILE:(j + 1) * MOE_TILE]
        for m in range(slot.shape[1] // LANES):
            posc_ref[n, m * LANES:(m + 1) * LANES, :] = jnp.concatenate([mine[:, m * LANES:(m + 1) * LANES], pad], axis=0).T


SELECT_ROWS = 64
SELECT_TOKENS = 4096


def select_slots(aff, cap):
    bsz, nt, ne, tile = aff.shape
    nbs = min(bsz, max(1, (SELECT_ROWS // ne) * SELECT_TOKENS // (nt * tile)))
    blk = lambda b: (b, 0, 0, 0)
    return pl.pallas_call(
        functools.partial(_select_kernel, nt=nt, cap=cap),
        grid=(bsz // nbs,),
        in_specs=[pl.BlockSpec((nbs, nt, ne, tile), blk)],
        out_specs=[pl.BlockSpec((nbs, nt, ne, tile), blk),
                   pl.BlockSpec((nbs, nt * tile, LANES), lambda b: (b, 0, 0)),
                   pl.BlockSpec((nbs, ne, LANES), lambda b: (b, 0, 0))],
        out_shape=[jax.ShapeDtypeStruct((bsz, nt, ne, tile), F32),
                   jax.ShapeDtypeStruct((bsz, nt * tile, LANES), F32),
                   jax.ShapeDtypeStruct((bsz, ne, LANES), jnp.int32)],
        compiler_params=_compiler_params(("parallel",)),
        name="select_slots",
    )(aff)


def _window_start(cnt_ref, row, j):
    return pl.multiple_of((cnt_ref[row, j] // SLOT_ALIGN) * SLOT_ALIGN, SLOT_ALIGN)


def _extra_windows(cnt_ref, row, j, start):
    return (cnt_ref[row, j + 1] - start - 1) // SLOT_WINDOW


def _gather_kernel(cnt_ref, hx_ref, pos_ref, xs_ref):
    b = pl.program_id(0)
    j = pl.program_id(1)

    @pl.when(j == 0)
    def _():
        xs_ref[...] = jnp.zeros_like(xs_ref)

    hx = hx_ref[0]
    slot_iota = lax.broadcasted_iota(jnp.int32, (SLOT_WINDOW, MOE_TILE), 0).astype(F32)

    def one_hot(e, start):
        return jnp.where(pos_ref[0, 0, e:e + 1, :] - start.astype(F32) == slot_iota, 1.0, 0.0).astype(BF16)

    starts = [_window_start(cnt_ref, b * N_EXPERTS + e, j) for e in range(N_EXPERTS)]
    stacked = jnp.concatenate([one_hot(e, starts[e]) for e in range(N_EXPERTS)], axis=0)
    rows = jnp.dot(stacked, hx, preferred_element_type=F32)
    for e in range(N_EXPERTS):
        xs_ref[0, e, pl.ds(starts[e], SLOT_WINDOW), :] += rows[e * SLOT_WINDOW:(e + 1) * SLOT_WINDOW].astype(BF16)

    for e in range(N_EXPERTS):
        n_more = _extra_windows(cnt_ref, b * N_EXPERTS + e, j, starts[e])

        @pl.when(n_more > 0)
        def _(e=e, n_more=n_more):
            def more(i, carry):
                start = pl.multiple_of(starts[e] + i * SLOT_WINDOW, SLOT_ALIGN)
                r = jnp.dot(one_hot(e, start), hx, preferred_element_type=F32)
                xs_ref[0, e, pl.ds(start, SLOT_WINDOW), :] += r.astype(BF16)
                return carry
            lax.fori_loop(1, n_more + 1, more, 0)


def gather_tokens(cnt, hx, pos, cap):
    bsz, t, d_ext = hx.shape
    nt = t // MOE_TILE
    rows = cap + SLOT_WINDOW
    grid_spec = pltpu.PrefetchScalarGridSpec(
        num_scalar_prefetch=1,
        grid=(bsz, nt),
        in_specs=[
            pl.BlockSpec((1, MOE_TILE, d_ext), lambda b, j, c: (b, j, 0)),
            pl.BlockSpec((1, 1, N_EXPERTS, MOE_TILE), lambda b, j, c: (b, j, 0, 0)),
        ],
        out_specs=pl.BlockSpec((1, N_EXPERTS, rows, d_ext), lambda b, j, c: (b, 0, 0, 0)),
    )
    return pl.pallas_call(
        _gather_kernel,
        grid_spec=grid_spec,
        out_shape=jax.ShapeDtypeStruct((bsz, N_EXPERTS, rows, d_ext), BF16),
        compiler_params=_compiler_params(("parallel", "arbitrary")),
        name="gather_tokens",
    )(cnt, hx, pos)


def _expert_ffn_kernel(xs_ref, wg_ref, wu_ref, wd_ref, o_ref):
    e = pl.program_id(0)
    d = wg_ref.shape[1]
    nbs = xs_ref.shape[0]
    cap = xs_ref.shape[2] - SLOT_WINDOW
    xs = jnp.concatenate([xs_ref[n, 0, :cap, :d] for n in range(nbs)], axis=0)
    ext = jnp.concatenate([xs_ref[n, 0, :cap, d:] for n in range(nbs)], axis=0).astype(F32)
    lane = lax.broadcasted_iota(jnp.int32, ext.shape, 1)
    mine = (lane % N_EXPERTS == e) & (lane < GATE_PIECES * N_EXPERTS)
    gate = jnp.sum(jnp.where(mine, ext, 0.0), axis=1, keepdims=True)
    a = jnp.dot(xs, wg_ref[0], preferred_element_type=F32)
    u = jnp.dot(xs, wu_ref[0], preferred_element_type=F32)
    hid = (_silu(a) * u).astype(BF16)
    out = (jnp.dot(hid, wd_ref[0], preferred_element_type=F32) * gate).astype(BF16)
    for n in range(nbs):
        o_ref[n, 0, :cap] = out[n * cap:(n + 1) * cap]
        o_ref[n, 0, cap:] = jnp.zeros((SLOT_WINDOW, d), BF16)


FFN_ROWS = 512


def expert_ffn(xs, wg, wu, wd):
    bsz, ne, rows, d_ext = xs.shape
    _, d, f = wg.shape
    nbs = min(bsz, max(1, FFN_ROWS // (rows - SLOT_WINDOW)))
    return pl.pallas_call(
        _expert_ffn_kernel,
        grid=(ne, bsz // nbs),
        in_specs=[
            pl.BlockSpec((nbs, 1, rows, d_ext), lambda e, b: (b, e, 0, 0)),
            pl.BlockSpec((1, d, f), lambda e, b: (e, 0, 0)),
            pl.BlockSpec((1, d, f), lambda e, b: (e, 0, 0)),
            pl.BlockSpec((1, f, d), lambda e, b: (e, 0, 0)),
        ],
        out_specs=pl.BlockSpec((nbs, 1, rows, d), lambda e, b: (b, e, 0, 0)),
        out_shape=jax.ShapeDtypeStruct((bsz, ne, rows, d), BF16),
        compiler_params=_compiler_params(("parallel", "parallel")),
        name="expert_ffn",
    )(xs, wg, wu, wd)


def _scatter_out_kernel(cnt_ref, og_ref, posc_ref, g_ref, gate_ref, res_ref, o_ref):
    b = pl.program_id(0)
    j = pl.program_id(1)
    posc = posc_ref[0]
    starts = [_window_start(cnt_ref, b * N_EXPERTS + e, j) for e in range(N_EXPERTS)]
    lane = lax.broadcasted_iota(jnp.int32, (MOE_TILE, LANES), 1)
    lane_f = lane.astype(F32)
    per_tile = LANES // SLOT_WINDOW
    tiles = []
    for m in range(N_EXPERTS // per_tile):
        rel = None
        for q in range(per_tile - 1, -1, -1):
            e = m * per_tile + q
            v = posc[:, e:e + 1] - starts[e].astype(F32) + float(q * SLOT_WINDOW)
            rel = v if rel is None else jnp.where(lane < (q + 1) * SLOT_WINDOW, v, rel)
        tiles.append(jnp.where(rel == lane_f, 1.0, 0.0).astype(BF16))
    sel_t = jnp.concatenate(tiles, axis=1)
    og = jnp.concatenate([og_ref[0, e, pl.ds(starts[e], SLOT_WINDOW), :] for e in range(N_EXPERTS)], axis=0)
    o_ref[0] = jnp.dot(sel_t, og, preferred_element_type=F32)

    win_lane = lax.broadcasted_iota(jnp.int32, (MOE_TILE, SLOT_WINDOW), 1).astype(F32)
    for e in range(N_EXPERTS):
        n_more = _extra_windows(cnt_ref, b * N_EXPERTS + e, j, starts[e])

        @pl.when(n_more > 0)
        def _(e=e, n_more=n_more):
            def more(i, carry):
                start = pl.multiple_of(starts[e] + i * SLOT_WINDOW, SLOT_ALIGN)
                p = jnp.where(posc[:, e:e + 1] - start.astype(F32) == win_lane, 1.0, 0.0).astype(BF16)
                o_ref[0] += jnp.dot(p, og_ref[0, e, pl.ds(start, SLOT_WINDOW), :], preferred_element_type=F32)
                return carry
            lax.fori_loop(1, n_more + 1, more, 0)

    o_ref[0] = res_ref[0] + gate_ref[0] * (_rms(o_ref[0]) * g_ref[...])


def scatter_out(cnt, og, posc, g, gate, res):
    bsz, t, d = res.shape
    nt = t // MOE_TILE
    _, ne, rows, _ = og.shape
    grid_spec = pltpu.PrefetchScalarGridSpec(
        num_scalar_prefetch=1,
        grid=(bsz, nt),
        in_specs=[
            pl.BlockSpec((1, ne, rows, d), lambda b, j, c: (b, 0, 0, 0)),
            pl.BlockSpec((1, MOE_TILE, LANES), lambda b, j, c: (b, j, 0)),
            pl.BlockSpec((1, d), lambda b, j, c: (0, 0)),
            pl.BlockSpec((1, 1, d), lambda b, j, c: (b, 0, 0)),
            pl.BlockSpec((1, MOE_TILE, d), lambda b, j, c: (b, j, 0)),
        ],
        out_specs=pl.BlockSpec((1, MOE_TILE, d), lambda b, j, c: (b, j, 0)),
    )
    return pl.pallas_call(
        _scatter_out_kernel,
        grid_spec=grid_spec,
        out_shape=jax.ShapeDtypeStruct((bsz, t, d), F32),
        compiler_params=_compiler_params(("parallel", "parallel")),
        name="scatter_out",
    )(cnt, og, posc, g.reshape(1, d), gate.reshape(bsz, 1, d), res)


def ec_moe_residual(x, hx, aff, wg, wu, wd, g_out, gate):
    bsz, t, _ = x.shape
    cap = EC_CAPACITY_FACTOR * t // N_EXPERTS
    pos, posc, counts = select_slots(aff, cap)
    cnt = counts[:, :, :CNT_LANES].reshape(bsz * N_EXPERTS, CNT_LANES)
    xs = gather_tokens(cnt, hx, pos, cap)
    og = expert_ffn(xs, wg, wu, wd)
    return scatter_out(cnt, og, posc, g_out, gate, x)


def _token_tile(t):
    return min(512, t)


def mixer_stream(x, g0, shift, scale, wc, wz, ws, w9, conv_b, bias_rows, nega_rows, dn_state, ssm_state, on_grid):
    t = x.shape[1]
    tm = _token_tile(t)
    feat, z, gcol, grow, arow = in_proj_conv(x, g0, shift, scale, wc, wz, ws, w9, conv_b, bias_rows, nega_rows,
                                             tm, on_grid)
    ts = min(SCAN_TILE, t)
    o_f, o_b, dn_state = delta_scan(feat, gcol, grow, dn_state, ts, SCAN_SAMPLES_PER_STEP)
    y_f, y_b, ssm_state = ssd_scan(feat, gcol, arow, ssm_state, ts, SCAN_SAMPLES_PER_STEP)
    return (o_f, o_b, y_f, y_b, feat, z), dn_state, ssm_state


def _gate_param_rows(dn_bias, dn_a_log, ssm_bias, ssm_a_log):
    zeros = jnp.zeros((NDH,), F32)
    bias = jnp.concatenate([zeros, dn_bias.reshape(-1), ssm_bias.reshape(-1)])
    nega = jnp.concatenate([zeros, -jnp.exp(dn_a_log.reshape(-1)), -jnp.exp(ssm_a_log.reshape(-1))])
    pad = LANES - bias.shape[0]
    expand = lambda v: jnp.broadcast_to(jnp.pad(v, (0, pad))[:, None], (LANES, LANES))
    return expand(bias), expand(nega)


def kernel(x, c, ctx, c_ctx, ada_w, ada_b, norm_g, w_in, conv_w, conv_b, dn_A_log, dn_dt_bias,
           dn_norm_g, ssm_A_log, ssm_dt_bias, ssm_D, ssm_norm_g, w_out, router_w,
           exp_w_gate, exp_w_up, exp_w_down):
    bsz = x.shape[0]
    s_lat = jax.nn.silu(c)
    s_ctx = jax.nn.silu(c_ctx)
    for l in range(DEPTH):
        last = l == DEPTH - 1
        mod_lat = jnp.split(s_lat @ ada_w[l] + ada_b[l], 6, axis=-1)
        mod_ctx_row = s_ctx @ ada_w[l] + ada_b[l]
        mod_ctx = [jnp.broadcast_to(m[None, :], (bsz, D_MODEL)) for m in jnp.split(mod_ctx_row, 6)]

        w_in_l = w_in[l]
        wc = w_in_l[:, :CONV_CH].astype(BF16)
        wz = w_in_l[:, CONV_CH:CONV_CH + D_Z].astype(BF16)
        ws = jnp.pad(w_in_l[:, CONV_CH + D_Z:], ((0, 0), (0, LANES - N_GATE_COLS))).astype(BF16)
        w9 = conv_w[l].reshape(CONV_CH, CONV_K * CONV_K).T
        cb = conv_b[l].reshape(1, CONV_CH)
        bias_rows, nega_rows = _gate_param_rows(dn_dt_bias[l], dn_A_log[l], ssm_dt_bias[l], ssm_A_log[l])
        dskip = jnp.repeat(ssm_D[l], SSM_P).reshape(1, SSM_DI)
        w_out_l = w_out[l].astype(BF16)
        wr_t = router_w[l].T.astype(BF16)
        moe_w = (exp_w_gate[l].astype(BF16), exp_w_up[l].astype(BF16), exp_w_down[l].astype(BF16), norm_g[l, 3])

        dn0 = jnp.zeros((bsz, 2, DN_HEADS, DN_DK, DN_DV), F32)
        ssm0 = jnp.zeros((bsz, 2, SSM_G, SSM_N, SSM_GW), F32)
        shared = (wc, wz, ws, w9, cb, bias_rows, nega_rows)
        mix_ctx, dn_c, ssm_c = mixer_stream(ctx, norm_g[l, 0], mod_ctx[0], mod_ctx[1], *shared, dn0, ssm0, False)
        mix_lat, _, _ = mixer_stream(x, norm_g[l, 0], mod_lat[0], mod_lat[1], *shared, dn_c, ssm_c, True)
        merge_w = (dn_norm_g[l], dskip, ssm_norm_g[l], w_out_l, norm_g[l, 1])
        x, hx, aff = merge_out(*mix_lat, *merge_w, mod_lat[2], x, norm_g[l, 2], mod_lat[3], mod_lat[4], wr_t,
                               _token_tile(x.shape[1]))
        x = ec_moe_residual(x, hx, aff, *moe_w, mod_lat[5])

        if not last:
            ctx, hx, aff = merge_out(*mix_ctx, *merge_w, mod_ctx[2], ctx, norm_g[l, 2], mod_ctx[3], mod_ctx[4], wr_t,
                                     _token_tile(ctx.shape[1]))
            ctx = ec_moe_residual(ctx, hx, aff, *moe_w, mod_ctx[5])
    return x
```
